```python
import functools
import jax, jax.numpy as jnp
from jax import lax
import numpy as np

D_MODEL = 1024
BATCH = 16
SEQ = 2048
DEPTH = 2
DEC_BATCH = 32
DEC_SEQ = 4
PAST_LEN = 16384
PAGE_SIZE = 128

HEAD_DIM = 64
A_HEADS = 8
A_DIM = A_HEADS * HEAD_DIM
LORA_W = 64
LORA_A = 64
LORA_G = 128
B_DIM = 512
CONV_W = 3
C_HEADS = 8
C_KV_HEADS = 4
C_DIM = C_HEADS * HEAD_DIM
C_KV_DIM = C_KV_HEADS * HEAD_DIM
IDX_HEADS = 8
IDX_DIM = 64
TOPK_MAX = 256
ROT_DIM = HEAD_DIM // 4
ROPE_THETA = 500000.0
D_FF = 2816
PLE_DIM = 256
N_BRANCH = 3
RMS_EPS = 1e-6
GN_EPS = 64e-5
Q_BLOCK = 128

RWKV_COLS = 3 * A_DIM + LORA_W + LORA_A + LORA_G
CONV_COLS = 3 * B_DIM
ATTN_COLS = C_DIM + 2 * C_KV_DIM + IDX_HEADS * IDX_DIM + IDX_DIM + IDX_HEADS
GATE_COLS = N_BRANCH * D_MODEL
IN_COLS = RWKV_COLS + CONV_COLS + ATTN_COLS + GATE_COLS

kernel_name = 'hybrid_rwkv7_shortconv_dsa_decoder_step'


def rmsnorm(x, g):
    xf = x.astype(jnp.float32)
    y = xf * lax.rsqrt(jnp.mean(xf * xf, axis=-1, keepdims=True) + RMS_EPS)
    return (y * g.astype(jnp.float32)).astype(x.dtype)


def swiglu(h, wg, wu, wd):
    return (jax.nn.silu(h @ wg) * (h @ wu)) @ wd


def rope_partial(x, pos):
    half = ROT_DIM // 2
    inv = ROPE_THETA ** (-jnp.arange(half, dtype=jnp.float32) / half)
    ang = pos.astype(jnp.float32)[:, None] * inv[None, :]
    cos = jnp.cos(ang)[:, None, :]
    sin = jnp.sin(ang)[:, None, :]
    xr = x[..., :ROT_DIM].astype(jnp.float32)
    x1, x2 = xr[..., :half], xr[..., half:]
    rot = jnp.concatenate([x1 * cos - x2 * sin, x2 * cos + x1 * sin], axis=-1)
    return jnp.concatenate([rot.astype(x.dtype), x[..., ROT_DIM:]], axis=-1)


def rwkv7_mixer(u, shift0, wkv0, lp):
    B, T, _ = u.shape
    u_prev = jnp.concatenate([shift0[:, None, :].astype(u.dtype), u[:, :-1]], axis=1)
    us = u + (u_prev - u) * lp['rwkv_mu']
    o1 = A_DIM; o2 = 2 * A_DIM; o3 = 3 * A_DIM
    r, k, v, xw, xa, xg = jnp.split(us, [o1, o2, o3, o3 + LORA_W, o3 + LORA_W + LORA_A], axis=-1)
    w_log = -jax.nn.softplus(-(lp['rwkv_w0'] + jnp.tanh(xw) @ lp['rwkv_w2'])) - 0.5
    decay = jnp.exp(-jnp.exp(w_log.astype(jnp.float32)))
    a = jax.nn.sigmoid(lp['rwkv_a0'] + xa @ lp['rwkv_a2'])
    g = jax.nn.sigmoid(xg) @ lp['rwkv_g2']
    heads = lambda t: t.reshape(B, T, A_HEADS, HEAD_DIM).astype(jnp.float32)
    kk = heads(k * lp['rwkv_kk'])
    kk = kk / jnp.maximum(jnp.sqrt(jnp.sum(kk * kk, axis=-1, keepdims=True)), 1e-12)
    k = k * (1.0 + (a - 1.0) * lp['rwkv_ka'])
    rh, kh, vh, ah, wh = heads(r), heads(k), heads(v), heads(a), heads(decay)
    a_vec = -kk
    b_vec = kk * ah

    def step(S, inp):
        r_t, w_t, k_t, v_t, a_t, b_t = inp
        sa = jnp.einsum('bhij,bhj->bhi', S, a_t)
        S = S * w_t[:, :, None, :] + sa[..., :, None] * b_t[..., None, :] + v_t[..., :, None] * k_t[..., None, :]
        return S, jnp.einsum('bhij,bhj->bhi', S, r_t)

    tm = lambda t: jnp.moveaxis(t, 1, 0)
    S_fin, o = lax.scan(step, wkv0.astype(jnp.float32), (tm(rh), tm(wh), tm(kh), tm(vh), tm(a_vec), tm(b_vec)))
    o = jnp.moveaxis(o, 0, 1)
    mean = jnp.mean(o, axis=-1, keepdims=True)
    var = jnp.mean(jnp.square(o - mean), axis=-1, keepdims=True)
    on = ((o - mean) * lax.rsqrt(var + GN_EPS)).reshape(B, T, A_DIM) * lp['rwkv_lnx_w'] + lp['rwkv_lnx_b']
    bonus = jnp.sum(rh * kh * lp['rwkv_rk'].astype(jnp.float32), axis=-1, keepdims=True) * vh
    out = (on + bonus.reshape(B, T, A_DIM)) * g
    return out.astype(u.dtype), u[:, -1], S_fin.astype(wkv0.dtype)


def shortconv_mixer(u, conv0, w_conv):
    T = u.shape[1]
    bg, cg, xin = jnp.split(u, [B_DIM, 2 * B_DIM], axis=-1)
    z = cg * xin
    zp = jnp.concatenate([conv0.astype(z.dtype), z], axis=1)
    y = zp[:, 0:T] * w_conv[0]
    for j in range(1, CONV_W):
        y = y + zp[:, j:j + T] * w_conv[j]
    return bg * y, zp[:, -(CONV_W - 1):]


def attn_project(u, pos, lp):
    B, T, _ = u.shape
    offs = np.cumsum([C_DIM, C_KV_DIM, C_KV_DIM, IDX_HEADS * IDX_DIM, IDX_DIM]).tolist()
    q, k, v, qi, ki, wi = jnp.split(u, offs, axis=-1)
    q = rope_partial(rmsnorm(q.reshape(B, T, C_HEADS, HEAD_DIM), lp['q_norm']), pos)
    k = rope_partial(rmsnorm(k.reshape(B, T, C_KV_HEADS, HEAD_DIM), lp['k_norm']), pos)
    v = v.reshape(B, T, C_KV_HEADS, HEAD_DIM)
    qi = rope_partial(qi.reshape(B, T, IDX_HEADS, IDX_DIM), pos)
    ki = rope_partial(rmsnorm(ki, lp['idx_k_norm'])[:, :, None, :], pos)[:, :, 0, :]
    return q, k, v, qi, ki, wi


def index_scores(qi, wi, ki):
    s = jnp.einsum('bthd,bsd->bths', qi, ki).astype(jnp.float32) * (IDX_DIM ** -0.5)
    return jnp.einsum('bths,bth->bts', jax.nn.relu(s), wi.astype(jnp.float32) * (IDX_HEADS ** -0.5))


def sparse_attend(q, k_sel, v_sel, valid):
    B, Tq = q.shape[:2]
    qg = q.reshape(B, Tq, C_KV_HEADS, C_HEADS // C_KV_HEADS, HEAD_DIM)
    s = jnp.einsum('bqgrd,bqkgd->bqgrk', qg, k_sel).astype(jnp.float32) * (HEAD_DIM ** -0.5)
    s = jnp.where(valid[:, :, None, None, :], s, -jnp.inf)
    p = jax.nn.softmax(s, axis=-1).astype(v_sel.dtype)
    o = jnp.einsum('bqgrk,bqkgd->bqgrd', p, v_sel)
    return o.reshape(B, Tq, C_DIM)


def prompt_sparse_attention(q, k, v, qi, ki, wi):
    B, S = q.shape[:2]
    topk = min(TOPK_MAX, S // 4)
    key_pos = jnp.arange(S)
    gather = jax.vmap(lambda kb, ib: kb[ib])

    def block(bi):
        start = bi * Q_BLOCK
        qb = lax.dynamic_slice_in_dim(q, start, Q_BLOCK, axis=1)
        qib = lax.dynamic_slice_in_dim(qi, start, Q_BLOCK, axis=1)
        wib = lax.dynamic_slice_in_dim(wi, start, Q_BLOCK, axis=1)
        qpos = start + jnp.arange(Q_BLOCK)
        sc = index_scores(qib, wib, ki)
        sc = jnp.where((key_pos[None, :] <= qpos[:, None])[None], sc, -jnp.inf)
        _, idx = lax.top_k(sc, topk)
        valid = idx <= qpos[None, :, None]
        return sparse_attend(qb, gather(k, idx), gather(v, idx), valid)

    out = lax.map(block, jnp.arange(S // Q_BLOCK))
    return jnp.moveaxis(out, 0, 1).reshape(B, S, C_DIM)


def sample_sparse_attention(q, k, v, qi, ki, wi, *, cache_k, cache_v, cache_kidx, page_table, layer):
    DB, T = q.shape[:2]
    n_pages = page_table.shape[1]
    page = cache_k.shape[2]
    past = n_pages * page
    L = past + T
    topk = min(TOPK_MAX, L // 4)
    ki_past = cache_kidx[layer, page_table].reshape(DB, past, IDX_DIM)
    ki_all = jnp.concatenate([ki_past.astype(ki.dtype), ki], axis=1)
    sc = index_scores(qi, wi, ki_all)
    qpos = past + jnp.arange(T)
    kpos = jnp.arange(L)
    sc = jnp.where((kpos[None, :] <= qpos[:, None])[None], sc, -jnp.inf)
    _, idx = lax.top_k(sc, topk)
    valid = idx <= qpos[None, :, None]
    in_past = idx < past
    pidx = jnp.minimum(idx, past - 1)
    phys_page = jnp.take_along_axis(page_table, (pidx // page).reshape(DB, -1), axis=1).reshape(DB, T, topk)
    phys = phys_page * page + pidx % page
    depth = cache_k.shape[0]
    ck = cache_k.reshape(depth, -1, C_KV_HEADS, HEAD_DIM)
    cv = cache_v.reshape(depth, -1, C_KV_HEADS, HEAD_DIM)
    nidx = jnp.clip(idx - past, 0, T - 1)
    gather = jax.vmap(lambda kb, ib: kb[ib])
    sel = in_past[..., None, None]
    k_sel = jnp.where(sel, ck[layer, phys].astype(k.dtype), gather(k, nidx))
    v_sel = jnp.where(sel, cv[layer, phys].astype(v.dtype), gather(v, nidx))
    return sparse_attend(q, k_sel, v_sel, valid)


def trunk_layer(x, p, pos, lp, shift0, conv0, wkv0, attn_fn):
    B, T, _ = x.shape
    h = rmsnorm(x, lp['ffn1_norm'])
    x = x + 0.5 * swiglu(h, lp['ffn1_wg'], lp['ffn1_wu'], lp['ffn1_wd'])
    h = rmsnorm(x, lp['mix_norm'])
    u = h @ lp['w_in']
    u_a, u_b, u_c, u_g = jnp.split(u, [RWKV_COLS, RWKV_COLS + CONV_COLS, RWKV_COLS + CONV_COLS + ATTN_COLS], axis=-1)
    o_a, shift1, wkv1 = rwkv7_mixer(u_a, shift0, wkv0, lp)
    o_b, conv1 = shortconv_mixer(u_b, conv0, lp['conv_w'])
    q, k, v, qi, ki, wi = attn_project(u_c, pos, lp)
    o_c = attn_fn(q, k, v, qi, ki, wi)
    gates = jax.nn.sigmoid(u_g).reshape(B, T, N_BRANCH, D_MODEL)
    merged = (gates[:, :, 0] * (o_a @ lp['w_pa']) + gates[:, :, 1] * (o_b @ lp['w_pb'])
              + gates[:, :, 2] * (o_c @ lp['w_pc']))
    x = x + merged @ lp['w_out']
    h = rmsnorm(x, lp['ffn2_norm'])
    x = x + 0.5 * swiglu(h, lp['ffn2_wg'], lp['ffn2_wu'], lp['ffn2_wd'])
    gate = jax.nn.sigmoid(rmsnorm(x, lp['ple_norm']) @ lp['ple_gate'])
    x = x + gate * (p.astype(x.dtype) @ lp['ple_proj'])
    return x, (k, v, ki, wkv1, shift1, conv1)


def setup_inputs(seed: int = 0) -> dict:
    key = jax.random.key(seed)
    keys = jax.random.split(key, 64)
    ctr = [0]

    def nxt():
        ctr[0] += 1
        return keys[ctr[0] - 1]

    def nrm(shape, scale=1.0):
        return jax.random.normal(nxt(), shape, jnp.float32) * scale

    def gain(shape):
        return 1.0 + 0.05 * nrm(shape)

    n_pages = PAST_LEN // PAGE_SIZE
    n_used = DEC_BATCH * n_pages
    n_phys = n_used + n_used // 4
    perm = jax.random.permutation(nxt(), n_phys)
    page_table = perm[:n_used].reshape(DEC_BATCH, n_pages).astype(jnp.int32)
    Ly = DEPTH
    return {
        'x_prompt': nrm((BATCH, SEQ, D_MODEL)),
        'x_sample': nrm((DEC_BATCH, DEC_SEQ, D_MODEL)),
        'cache_k': nrm((Ly, n_phys, PAGE_SIZE, C_KV_HEADS, HEAD_DIM)),
        'cache_v': nrm((Ly, n_phys, PAGE_SIZE, C_KV_HEADS, HEAD_DIM)),
        'cache_kidx': nrm((Ly, n_phys, PAGE_SIZE, IDX_DIM)),
        'state_wkv': nrm((Ly, DEC_BATCH, A_HEADS, HEAD_DIM, HEAD_DIM), 0.5),
        'state_shift': nrm((Ly, DEC_BATCH, RWKV_COLS)),
        'state_conv': nrm((Ly, DEC_BATCH, CONV_W - 1, B_DIM)),
        'page_table': page_table,
        'p_prompt': nrm((Ly, BATCH, SEQ, PLE_DIM)),
        'p_sample': nrm((Ly, DEC_BATCH, DEC_SEQ, PLE_DIM)),
        'ffn1_norm': gain((Ly, D_MODEL)),
        'ffn1_wg': nrm((Ly, D_MODEL, D_FF), D_MODEL ** -0.5),
        'ffn1_wu': nrm((Ly, D_MODEL, D_FF), D_MODEL ** -0.5),
        'ffn1_wd': nrm((Ly, D_FF, D_MODEL), D_FF ** -0.5),
        'mix_norm': gain((Ly, D_MODEL)),
        'w_in': nrm((Ly, D_MODEL, IN_COLS), D_MODEL ** -0.5),
        'rwkv_mu': jax.random.uniform(nxt(), (Ly, RWKV_COLS), jnp.float32),
        'rwkv_w0': nrm((Ly, A_DIM), 0.5),
        'rwkv_w2': nrm((Ly, LORA_W, A_DIM), 0.5 * LORA_W ** -0.5),
        'rwkv_a0': nrm((Ly, A_DIM), 0.1),
        'rwkv_a2': nrm((Ly, LORA_A, A_DIM), 0.5 * LORA_A ** -0.5),
        'rwkv_g2': nrm((Ly, LORA_G, A_DIM), LORA_G ** -0.5),
        'rwkv_kk': 0.85 + 0.05 * nrm((Ly, A_DIM)),
        'rwkv_ka': gain((Ly, A_DIM)),
        'rwkv_rk': nrm((Ly, A_HEADS, HEAD_DIM), 0.1),
        'rwkv_lnx_w': gain((Ly, A_DIM)),
        'rwkv_lnx_b': nrm((Ly, A_DIM), 0.01),
        'conv_w': nrm((Ly, CONV_W, B_DIM), CONV_W ** -0.5),
        'q_norm': gain((Ly, HEAD_DIM)),
        'k_norm': gain((Ly, HEAD_DIM)),
        'idx_k_norm': gain((Ly, IDX_DIM)),
        'w_pa': nrm((Ly, A_DIM, D_MODEL), A_DIM ** -0.5),
        'w_pb': nrm((Ly, B_DIM, D_MODEL), B_DIM ** -0.5),
        'w_pc': nrm((Ly, C_DIM, D_MODEL), C_DIM ** -0.5),
        'w_out': nrm((Ly, D_MODEL, D_MODEL), D_MODEL ** -0.5),
        'ffn2_norm': gain((Ly, D_MODEL)),
        'ffn2_wg': nrm((Ly, D_MODEL, D_FF), D_MODEL ** -0.5),
        'ffn2_wu': nrm((Ly, D_MODEL, D_FF), D_MODEL ** -0.5),
        'ffn2_wd': nrm((Ly, D_FF, D_MODEL), D_FF ** -0.5),
        'ple_norm': gain((Ly, D_MODEL)),
        'ple_gate': nrm((Ly, D_MODEL, D_MODEL), D_MODEL ** -0.5),
        'ple_proj': nrm((Ly, PLE_DIM, D_MODEL), PLE_DIM ** -0.5),
    }


def reference(x_prompt, x_sample, cache_k, cache_v, cache_kidx, state_wkv, state_shift, state_conv,
              page_table, p_prompt, p_sample,
              ffn1_norm, ffn1_wg, ffn1_wu, ffn1_wd, mix_norm, w_in,
              rwkv_mu, rwkv_w0, rwkv_w2, rwkv_a0, rwkv_a2, rwkv_g2, rwkv_kk, rwkv_ka, rwkv_rk,
              rwkv_lnx_w, rwkv_lnx_b, conv_w, q_norm, k_norm, idx_k_norm,
              w_pa, w_pb, w_pc, w_out, ffn2_norm, ffn2_wg, ffn2_wu, ffn2_wd,
              ple_norm, ple_gate, ple_proj):
    B, S, _ = x_prompt.shape
    DB, T, _ = x_sample.shape
    past = page_table.shape[1] * cache_k.shape[2]
    pos_p = jnp.arange(S, dtype=jnp.int32)
    pos_s = past + jnp.arange(T, dtype=jnp.int32)
    shift_zero = jnp.zeros((B, RWKV_COLS), x_prompt.dtype)
    conv_zero = jnp.zeros((B, CONV_W - 1, B_DIM), x_prompt.dtype)
    wkv_zero = jnp.zeros((B, A_HEADS, HEAD_DIM, HEAD_DIM), jnp.float32)
    xp, xs = x_prompt, x_sample
    outs_p, outs_s = [], []
    for i in range(DEPTH):
        lp = dict(ffn1_norm=ffn1_norm[i], ffn1_wg=ffn1_wg[i], ffn1_wu=ffn1_wu[i], ffn1_wd=ffn1_wd[i],
                  mix_norm=mix_norm[i], w_in=w_in[i], rwkv_mu=rwkv_mu[i], rwkv_w0=rwkv_w0[i],
                  rwkv_w2=rwkv_w2[i], rwkv_a0=rwkv_a0[i], rwkv_a2=rwkv_a2[i], rwkv_g2=rwkv_g2[i],
                  rwkv_kk=rwkv_kk[i], rwkv_ka=rwkv_ka[i], rwkv_rk=rwkv_rk[i], rwkv_lnx_w=rwkv_lnx_w[i],
                  rwkv_lnx_b=rwkv_lnx_b[i], conv_w=conv_w[i], q_norm=q_norm[i], k_norm=k_norm[i],
                  idx_k_norm=idx_k_norm[i], w_pa=w_pa[i], w_pb=w_pb[i], w_pc=w_pc[i], w_out=w_out[i],
                  ffn2_norm=ffn2_norm[i], ffn2_wg=ffn2_wg[i], ffn2_wu=ffn2_wu[i], ffn2_wd=ffn2_wd[i],
                  ple_norm=ple_norm[i], ple_gate=ple_gate[i], ple_proj=ple_proj[i])
        xp, st_p = trunk_layer(xp, p_prompt[i], pos_p, lp, shift_zero, conv_zero, wkv_zero,
                               prompt_sparse_attention)
        attn_s = functools.partial(sample_sparse_attention, cache_k=cache_k, cache_v=cache_v,
                                   cache_kidx=cache_kidx, page_table=page_table, layer=i)
        xs, st_s = trunk_layer(xs, p_sample[i], pos_s, lp, state_shift[i], state_conv[i], state_wkv[i], attn_s)
        outs_p.append(st_p)
        outs_s.append(st_s)
    k_p, v_p, kidx_p, wkv_p, shift_p, conv_p = [jnp.stack(z) for z in zip(*outs_p)]
    k_s, v_s, kidx_s, wkv_s, shift_s, conv_s = [jnp.stack(z) for z in zip(*outs_s)]
    return (xp, xs, k_p, v_p, kidx_p, wkv_p, shift_p, conv_p, k_s, v_s, kidx_s, wkv_s, shift_s, conv_s)
```

```python
import functools

import jax
import jax.numpy as jnp
import numpy as np
from jax import lax
from jax.experimental import pallas as pl
from jax.experimental.pallas import tpu as pltpu

F32 = jnp.float32
BF16 = jnp.bfloat16
I32 = jnp.int32

HEAD_DIM = 64
A_HEADS = 8
A_DIM = A_HEADS * HEAD_DIM
LORA_W = 64
LORA_A = 64
LORA_G = 128
B_DIM = 512
CONV_W = 3
C_HEADS = 8
C_KV_HEADS = 4
C_DIM = C_HEADS * HEAD_DIM
C_KV_DIM = C_KV_HEADS * HEAD_DIM
IDX_HEADS = 8
IDX_DIM = 64
TOPK_MAX = 256
ROT_DIM = HEAD_DIM // 4
ROT_HALF = ROT_DIM // 2
ROPE_THETA = 500000.0
N_BRANCH = 3
RMS_EPS = 1e-6
GN_EPS = 64e-5
RWKV_COLS = 3 * A_DIM + LORA_W + LORA_A + LORA_G
CONV_COLS = 3 * B_DIM
ATTN_COLS = C_DIM + 2 * C_KV_DIM + IDX_HEADS * IDX_DIM + IDX_DIM + IDX_HEADS
ATTN_COLS_PAD = 1664

LANES = 128
INT_MIN = -2 ** 31
NEG_BIG = -1e30
VMEM_LIMIT = 56 * 1024 * 1024
PAGES_PER_STEP = 16


def _cparams(n_axes):
    return pltpu.CompilerParams(dimension_semantics=("arbitrary",) * n_axes,
                                vmem_limit_bytes=VMEM_LIMIT)


def _const_spec(shape):
    nd = len(shape)
    return pl.BlockSpec(shape, lambda *_: (0,) * nd, pipeline_mode=pl.Buffered(1))


def _tile(m, pref):
    t = min(m, pref)
    while m % t:
        t -= 8
    return t


def _rms(x, g):
    ms = jnp.mean(x * x, axis=-1, keepdims=True)
    return x * lax.rsqrt(ms + RMS_EPS) * g


def _dot(a, b):
    return jnp.dot(a, b, preferred_element_type=F32)


def _dot_nt(a, b):
    return lax.dot_general(a, b, (((1,), (1,)), ((), ())), preferred_element_type=F32)


def _segsum(x, bd):
    hi = x.astype(BF16)
    lo = (x - hi.astype(F32)).astype(BF16)
    return _dot(hi, bd) + _dot(lo, bd)


def _rope(x, cos, sa, sb):
    n = x.shape[-1]
    return x * cos + pltpu.roll(x, n - ROT_HALF, 1) * sa + pltpu.roll(x, ROT_HALF, 1) * sb


def _rows_before(x, k, fills, *, seg_len, first_chunk_rows=None):
    rows = x.shape[0]
    y = pltpu.roll(x, k, 0)
    ridx = lax.broadcasted_iota(I32, (rows, 1), 0)
    t = ridx if seg_len is None else ridx % seg_len
    for r in range(k):
        y = jnp.where(t == r, fills[k - 1 - r], y)
    return y


def _softplus(y):
    return jnp.maximum(y, 0.0) + jnp.log(1.0 + jnp.exp(-jnp.abs(y)))


def _ffn_kernel(*refs, chunks, ple):
    if ple:
        x_ref, g_ref, wg_ref, wu_ref, wd_ref, p_ref, pn_ref, pg_ref, pp_ref, o_ref = refs
    else:
        x_ref, g_ref, wg_ref, wu_ref, wd_ref, o_ref = refs
    x = x_ref[...]
    h = _rms(x, g_ref[...]).astype(BF16)
    acc = None
    for lo, hi in chunks:
        gt = _dot(h, wg_ref[:, lo:hi])
        ut = _dot(h, wu_ref[:, lo:hi])
        act = (gt * jax.nn.sigmoid(gt) * ut).astype(BF16)
        d = _dot(act, wd_ref[lo:hi, :])
        acc = d if acc is None else acc + d
    y = x + 0.5 * acc
    if ple:
        hg = _rms(y, pn_ref[...]).astype(BF16)
        gate = jax.nn.sigmoid(_dot(hg, pg_ref[...]))
        y = y + gate * _dot(p_ref[...].astype(BF16), pp_ref[...])
    o_ref[...] = y


def _ffn(x, g, wg, wu, wd, ple=None):
    m, d = x.shape
    f = wg.shape[1]
    tm = _tile(m, 512)
    step = 1024
    chunks = tuple((lo, min(lo + step, f)) for lo in range(0, f, step))
    row = lambda i: (i, 0)
    in_specs = [pl.BlockSpec((tm, d), row), _const_spec((1, d)), _const_spec((d, f)),
                _const_spec((d, f)), _const_spec((f, d))]
    args = [x, g.reshape(1, d), wg, wu, wd]
    if ple is not None:
        p, pn, pg, pp = ple
        in_specs += [pl.BlockSpec((tm, p.shape[1]), row), _const_spec((1, d)),
                     _const_spec(pg.shape), _const_spec(pp.shape)]
        args += [p, pn.reshape(1, d), pg, pp]
    return pl.pallas_call(
        functools.partial(_ffn_kernel, chunks=chunks, ple=ple is not None),
        grid=(m // tm,), in_specs=in_specs, out_specs=pl.BlockSpec((tm, d), row),
        out_shape=jax.ShapeDtypeStruct((m, d), F32), compiler_params=_cparams(1),
        name="ffn_ple" if ple is not None else "ffn")(*args)


def _rwkv_pre_kernel(*refs, seg_len):
    (x_ref, g_ref, wa_ref, mu_ref, w0_ref, a0_ref, wwa_ref, g2_ref, kkw_ref, ka_ref,
     rk_ref, bd_ref) = refs[:12]
    rest = refs[12:]
    if seg_len is None:
        r_o, w_o, k_o, v_o, na_o, nb_o, g_o, bonus_o, shift_o, carry_ref = rest
    else:
        init_ref, r_o, w_o, k_o, v_o, na_o, nb_o, g_o, bonus_o, shift_o = rest
    x = x_ref[...]
    tm = x.shape[0]
    h = _rms(x, g_ref[...]).astype(BF16)
    u = _dot(h, wa_ref[...])
    if seg_len is None:
        @pl.when(pl.program_id(1) == 0)
        def _():
            carry_ref[...] = jnp.zeros_like(carry_ref)
        u_prev = _rows_before(u, 1, [carry_ref[0:1, :]], seg_len=None)
        carry_ref[0:1, :] = u[tm - 1:tm, :]
        shift_o[0] = u[tm - 1:tm, :]
    else:
        u_prev = _rows_before(u, 1, [init_ref[...]], seg_len=seg_len)
        shift_o[...] = u
    us = u + (u_prev - u) * mu_ref[...]
    r = us[:, 0:A_DIM]
    k = us[:, A_DIM:2 * A_DIM]
    v = us[:, 2 * A_DIM:3 * A_DIM]
    o3 = 3 * A_DIM
    xwa = us[:, o3:o3 + LORA_W + LORA_A]
    lane = lax.broadcasted_iota(I32, xwa.shape, 1)
    xwa = jnp.where(lane < LORA_W, jnp.tanh(xwa), xwa)
    lo = _dot(xwa.astype(BF16), wwa_ref[...])
    w_log = -_softplus(-(w0_ref[...] + lo[:, :A_DIM])) - 0.5
    decay = jnp.exp(-jnp.exp(w_log))
    a = jax.nn.sigmoid(a0_ref[...] + lo[:, A_DIM:])
    xg = us[:, o3 + LORA_W + LORA_A:]
    g = _dot(jax.nn.sigmoid(xg).astype(BF16), g2_ref[...])
    bd = bd_ref[...]
    kk = k * kkw_ref[...]
    kk = kk / jnp.maximum(jnp.sqrt(_segsum(kk * kk, bd)), 1e-12)
    k2 = k * (1.0 + (a - 1.0) * ka_ref[...])
    r_o[...] = r
    w_o[...] = decay
    k_o[...] = k2
    v_o[...] = v
    na_o[...] = -kk
    nb_o[...] = kk * a
    g_o[...] = g
    bonus_o[...] = _segsum(r * k2 * rk_ref[...], bd) * v


def _rwkv_pre(x, nb, seq, lw, init_rows=None):
    m, d = x.shape
    consts = [lw["mix_norm"], lw["w_a"], lw["rwkv_mu"], lw["rwkv_w0"], lw["rwkv_a0"], lw["w_wa"],
              lw["rwkv_g2"], lw["rwkv_kk"], lw["rwkv_ka"], lw["rwkv_rk"], lw["bd"]]
    const_specs = [_const_spec(c.shape) for c in consts]
    outs = [jax.ShapeDtypeStruct((m, A_DIM), F32)] * 8
    if init_rows is None:
        tm = _tile(seq, 512)
        nc = seq // tm
        row = lambda b, c: (b * nc + c, 0)
        grid = (nb, nc)
        in_specs = [pl.BlockSpec((tm, d), row)] + const_specs
        out_specs = [pl.BlockSpec((tm, A_DIM), row)] * 8 + [pl.BlockSpec((1, 1, RWKV_COLS), lambda b, c: (b, 0, 0))]
        outs = outs + [jax.ShapeDtypeStruct((nb, 1, RWKV_COLS), F32)]
        scratch = [pltpu.VMEM((8, RWKV_COLS), F32)]
        args = [x] + consts
        seg_len = None
    else:
        tm = m
        row = lambda i: (0, 0)
        grid = (1,)
        in_specs = [pl.BlockSpec((tm, d), row)] + const_specs + [pl.BlockSpec((tm, RWKV_COLS), row)]
        out_specs = [pl.BlockSpec((tm, A_DIM), row)] * 8 + [pl.BlockSpec((tm, RWKV_COLS), row)]
        outs = outs + [jax.ShapeDtypeStruct((m, RWKV_COLS), F32)]
        scratch = []
        args = [x] + consts + [init_rows]
        seg_len = seq
    return pl.pallas_call(
        functools.partial(_rwkv_pre_kernel, seg_len=seg_len), grid=grid, in_specs=in_specs,
        out_specs=out_specs, out_shape=outs, scratch_shapes=scratch,
        compiler_params=_cparams(len(grid)), name="rwkv_pre")(*args)


def _scan_kernel(r_ref, w_ref, k_ref, v_ref, a_ref, b_ref, s0_ref, o_ref, st_ref, s_ref, *, tc):
    c = pl.program_id(1)

    @pl.when(c == 0)
    def _():
        s_ref[...] = s0_ref[...]

    def step(t, carry):
        sa = None
        for j in range(HEAD_DIM):
            term = s_ref[j] * a_ref[t, j:j + 1, :]
            sa = term if sa is None else sa + term
        vt = v_ref[t]
        o = None
        for j in range(HEAD_DIM):
            sn = (s_ref[j] * w_ref[t, j:j + 1, :] + sa * b_ref[t, j:j + 1, :]
                  + vt * k_ref[t, j:j + 1, :])
            s_ref[j] = sn
            term = sn * r_ref[t, j:j + 1, :]
            o = term if o is None else o + term
        o_ref[t] = o
        return carry

    lax.fori_loop(0, tc, step, 0)

    @pl.when(c == pl.num_programs(1) - 1)
    def _():
        st_ref[...] = s_ref[...]


def _scan(r, w, k, v, a, b, s0):
    t_len, hd, nbh = r.shape
    ln = min(LANES, nbh)
    tc = _tile(t_len, 32) if t_len % 8 == 0 else t_len
    grid = (nbh // ln, t_len // tc)
    seq_spec = pl.BlockSpec((tc, hd, ln), lambda l, c: (c, 0, l))
    st_spec = pl.BlockSpec((hd, hd, ln), lambda l, c: (0, 0, l))
    return pl.pallas_call(
        functools.partial(_scan_kernel, tc=tc), grid=grid,
        in_specs=[seq_spec] * 6 + [st_spec], out_specs=[seq_spec, st_spec],
        out_shape=[jax.ShapeDtypeStruct((t_len, hd, nbh), F32),
                   jax.ShapeDtypeStruct((hd, hd, nbh), F32)],
        scratch_shapes=[pltpu.VMEM((hd, hd, ln), F32)],
        compiler_params=_cparams(2), name="rwkv_scan")(r, w, k, v, a, b, s0)


def _conv_kernel(*refs, seg_len):
    x_ref, g_ref, wb_ref, cw_ref = refs[:4]
    rest = refs[4:]
    if seg_len is None:
        o_ref, st_ref, carry_ref = rest
    else:
        i0_ref, i1_ref, o_ref, st_ref = rest
    x = x_ref[...]
    tm = x.shape[0]
    h = _rms(x, g_ref[...]).astype(BF16)
    u = _dot(h, wb_ref[...])
    bg = u[:, :B_DIM]
    z = u[:, B_DIM:2 * B_DIM] * u[:, 2 * B_DIM:]
    if seg_len is None:
        @pl.when(pl.program_id(1) == 0)
        def _():
            carry_ref[...] = jnp.zeros_like(carry_ref)
        hist = [carry_ref[1:2, :], carry_ref[0:1, :]]
        z1 = _rows_before(z, 1, hist[:1], seg_len=None)
        z2 = _rows_before(z, 2, hist, seg_len=None)
        carry_ref[0:2, :] = z[tm - 2:tm, :]
        st_ref[0] = z[tm - 2:tm, :]
    else:
        hist = [i1_ref[...], i0_ref[...]]
        z1 = _rows_before(z, 1, hist[:1], seg_len=seg_len)
        z2 = _rows_before(z, 2, hist, seg_len=seg_len)
        st_ref[...] = z
    cw = cw_ref[...]
    y = z2 * cw[0:1, :] + z1 * cw[1:2, :] + z * cw[2:3, :]
    o_ref[...] = bg * y


def _conv(x, nb, seq, lw, init=None):
    m, d = x.shape
    assert seq >= CONV_W - 1
    consts = [lw["mix_norm"], lw["w_b"], lw["conv_w"]]
    const_specs = [_const_spec(c.shape) for c in consts]
    if init is None:
        tm = _tile(seq, 512)
        nc = seq // tm
        row = lambda b, c: (b * nc + c, 0)
        grid = (nb, nc)
        in_specs = [pl.BlockSpec((tm, d), row)] + const_specs
        out_specs = [pl.BlockSpec((tm, B_DIM), row),
                     pl.BlockSpec((1, CONV_W - 1, B_DIM), lambda b, c: (b, 0, 0))]
        outs = [jax.ShapeDtypeStruct((m, B_DIM), F32), jax.ShapeDtypeStruct((nb, CONV_W - 1, B_DIM), F32)]
        scratch = [pltpu.VMEM((8, B_DIM), F32)]
        args = [x] + consts
        seg_len = None
    else:
        tm = m
        row = lambda i: (0, 0)
        grid = (1,)
        in_specs = [pl.BlockSpec((tm, d), row)] + const_specs + [pl.BlockSpec((tm, B_DIM), row)] * 2
        out_specs = [pl.BlockSpec((tm, B_DIM), row)] * 2
        outs = [jax.ShapeDtypeStruct((m, B_DIM), F32)] * 2
        scratch = []
        args = [x] + consts + list(init)
        seg_len = seq
    return pl.pallas_call(
        functools.partial(_conv_kernel, seg_len=seg_len), grid=grid, in_specs=in_specs,
        out_specs=out_specs, out_shape=outs, scratch_shapes=scratch,
        compiler_params=_cparams(len(grid)), name="shortconv")(*args)


def _attn_proj_kernel(x_ref, g_ref, wc_ref, qn_ref, kn_ref, in_ref, bd_ref, cos_ref, sa_ref, sb_ref,
                      q_o, k_o, v_o, qi_o, kw_o, ki_o):
    x = x_ref[...]
    h = _rms(x, g_ref[...]).astype(BF16)
    u = _dot(h, wc_ref[...])
    cos, sa, sb = cos_ref[...], sa_ref[...], sb_ref[...]
    bd = bd_ref[...]
    inv_hd = 1.0 / HEAD_DIM
    q = u[:, :C_DIM]
    q = q * lax.rsqrt(_segsum(q * q, bd) * inv_hd + RMS_EPS) * qn_ref[...]
    q_o[...] = _rope(q, cos, sa, sb)
    k = u[:, C_DIM:C_DIM + C_KV_DIM]
    k = k * lax.rsqrt(_segsum(k * k, bd[:C_KV_DIM, :C_KV_DIM]) * inv_hd + RMS_EPS) * kn_ref[...]
    k_o[...] = _rope(k, cos[:, :C_KV_DIM], sa[:, :C_KV_DIM], sb[:, :C_KV_DIM])
    o = C_DIM + C_KV_DIM
    v_o[...] = u[:, o:o + C_KV_DIM]
    o += C_KV_DIM
    qi_o[...] = _rope(u[:, o:o + IDX_HEADS * IDX_DIM], cos, sa, sb)
    o += IDX_HEADS * IDX_DIM
    kw = u[:, o:o + LANES]
    lane = lax.broadcasted_iota(I32, kw.shape, 1)
    is_ki = lane < IDX_DIM
    ms = jnp.sum(jnp.where(is_ki, kw * kw, 0.0), axis=-1, keepdims=True) * (1.0 / IDX_DIM)
    kin = _rope(kw * lax.rsqrt(ms + RMS_EPS) * in_ref[...], cos[:, :LANES], sa[:, :LANES], sb[:, :LANES])
    kw = jnp.where(is_ki, kin, kw)
    kw_o[...] = kw
    ki_o[...] = kw[:, :IDX_DIM]


def _attn_proj(x, nb, seq, lw, tabs, tiled_tabs):
    m, d = x.shape
    consts = [lw["mix_norm"], lw["w_c"], lw["q_norm"], lw["k_norm"], lw["idx_k_norm"], lw["bd"]]
    const_specs = [_const_spec(c.shape) for c in consts]
    if tiled_tabs:
        tm, nc = m, 1
        grid = (1, 1)
    else:
        tm = _tile(seq, 512)
        nc = seq // tm
        grid = (nb, nc)
    row = lambda b, c: (b * nc + c, 0)
    tab = lambda b, c: (c, 0)
    widths = [C_DIM, C_KV_DIM, C_KV_DIM, IDX_HEADS * IDX_DIM, LANES, IDX_DIM]
    return pl.pallas_call(
        _attn_proj_kernel, grid=grid,
        in_specs=[pl.BlockSpec((tm, d), row)] + const_specs + [pl.BlockSpec((tm, C_DIM), tab)] * 3,
        out_specs=[pl.BlockSpec((tm, w), row) for w in widths],
        out_shape=[jax.ShapeDtypeStruct((m, w), F32) for w in widths],
        compiler_params=_cparams(2), name="attn_proj")(x, *consts, *tabs)


def _score_keys(score):
    bits = pltpu.bitcast(score, I32)
    return jnp.where(bits < 0, bits ^ 0x7FFFFFFF, bits)


def _lane_total(acc):
    return jnp.broadcast_to(jnp.sum(acc, axis=-1, keepdims=True), acc.shape)


def _select_topk(count_fn, topk, pos_bits, rows):
    zeros = jnp.zeros((rows, LANES), I32)

    def vbit(it, acc):
        cand = acc | jnp.left_shift(jnp.int32(1), 31 - it)
        cmp = cand ^ INT_MIN
        cnt = count_fn(lambda kk, pos: jnp.where(kk >= cmp, 1, 0))
        return jnp.where(cnt >= topk, cand, acc)

    thr = lax.fori_loop(0, 32, vbit, zeros) ^ INT_MIN
    need = topk - count_fn(lambda kk, pos: jnp.where(kk > thr, 1, 0))

    def pbit(it, acc):
        cand = acc | jnp.left_shift(jnp.int32(1), pos_bits - 1 - it)
        cnt = count_fn(lambda kk, pos: jnp.where(kk == thr, jnp.where(pos < cand, 1, 0), 0))
        return jnp.where(cnt < need, cand, acc)

    last_pos = lax.fori_loop(0, pos_bits, pbit, zeros)
    return thr, last_pos


def _select_bias(kk, pos, thr, last_pos):
    tie = jnp.where(pos <= last_pos, 0.0, NEG_BIG)
    bias = jnp.where(kk == thr, tie, jnp.where(kk > thr, 0.0, NEG_BIG))
    return jnp.where(kk == INT_MIN, NEG_BIG, bias)


def _pattn_kernel(qi_ref, kwq_ref, q_ref, kwk_ref, k_ref, v_ref, o_ref, keys_ref, bias_ref,
                  *, tq, topk, pos_bits):
    qb = pl.program_id(1)
    nch = qb + 1
    nsl = tq // LANES
    wq = kwq_ref[:, IDX_DIM:IDX_DIM + IDX_HEADS] * ((IDX_DIM ** -0.5) * (IDX_HEADS ** -0.5))
    qi = qi_ref[...].astype(BF16)
    row_pos = qb * tq + lax.broadcasted_iota(I32, (tq, tq), 0)
    col_in = lax.broadcasted_iota(I32, (tq, tq), 1)

    def idx_body(kc, carry):
        start = pl.multiple_of(kc * tq, tq)
        ks = kwk_ref[pl.ds(start, tq), :][:, :IDX_DIM].astype(BF16)
        acc = jnp.zeros((tq, tq), F32)
        for hh in range(IDX_HEADS):
            s = _dot_nt(qi[:, hh * IDX_DIM:(hh + 1) * IDX_DIM], ks)
            acc = acc + jnp.maximum(s, 0.0) * wq[:, hh:hh + 1]
        kk = jnp.where(kc * tq + col_in <= row_pos, _score_keys(acc), INT_MIN)
        keys_ref[kc] = kk
        return carry

    lax.fori_loop(0, nch, idx_body, 0)

    lane = lax.broadcasted_iota(I32, (tq, LANES), 1)

    def count_fn(pred):
        def body(kc, acc):
            kk = keys_ref[kc]
            for j in range(nsl):
                pos = kc * tq + j * LANES + lane
                acc = acc + pred(kk[:, j * LANES:(j + 1) * LANES], pos)
            return acc
        return _lane_total(lax.fori_loop(0, nch, body, jnp.zeros((tq, LANES), I32)))

    thr, last_pos = _select_topk(count_fn, topk, pos_bits, tq)

    def bias_body(kc, carry):
        kk = keys_ref[kc]
        parts = []
        for j in range(nsl):
            pos = kc * tq + j * LANES + lane
            parts.append(_select_bias(kk[:, j * LANES:(j + 1) * LANES], pos, thr, last_pos))
        bias_ref[kc] = jnp.concatenate(parts, axis=1)
        return carry

    lax.fori_loop(0, nch, bias_body, 0)

    rep = C_HEADS // C_KV_HEADS
    scale = HEAD_DIM ** -0.5
    for g in range(C_KV_HEADS):
        qg = jnp.concatenate(
            [q_ref[:, (g * rep + r) * HEAD_DIM:(g * rep + r + 1) * HEAD_DIM] for r in range(rep)],
            axis=0)
        qg = (qg * scale).astype(BF16)

        def att_body(kc, carry, g=g, qg=qg):
            m_run, l_run, acc = carry
            start = pl.multiple_of(kc * tq, tq)
            kch = k_ref[pl.ds(start, tq), :][:, g * HEAD_DIM:(g + 1) * HEAD_DIM].astype(BF16)
            vch = v_ref[pl.ds(start, tq), :][:, g * HEAD_DIM:(g + 1) * HEAD_DIM].astype(BF16)
            bias = bias_ref[kc]
            s = _dot_nt(qg, kch) + jnp.concatenate([bias] * rep, axis=0)
            m_new = jnp.maximum(m_run, jnp.max(s, axis=-1, keepdims=True))
            p = jnp.exp(s - m_new)
            alpha = jnp.exp(m_run - m_new)
            l_new = alpha * l_run + jnp.sum(p, axis=-1, keepdims=True)
            acc = alpha * acc + _dot(p.astype(BF16), vch)
            return m_new, l_new, acc

        init = (jnp.full((rep * tq, 1), NEG_BIG, F32), jnp.zeros((rep * tq, 1), F32),
                jnp.zeros((rep * tq, HEAD_DIM), F32))
        _, l_fin, acc = lax.fori_loop(0, nch, att_body, init)
        out = acc / l_fin
        for r in range(rep):
            hq = g * rep + r
            o_ref[:, hq * HEAD_DIM:(hq + 1) * HEAD_DIM] = out[r * tq:(r + 1) * tq, :]


def _prompt_attention(q, k, v, qi, kw, nb, seq):
    topk = min(TOPK_MAX, seq // 4)
    tq = _tile(seq, 256)
    assert tq % LANES == 0
    nq = seq // tq
    pos_bits = max(1, int(seq - 1).bit_length())
    qrow = lambda b, c: (b * nq + c, 0)
    krow = lambda b, c: (b, 0)
    return pl.pallas_call(
        functools.partial(_pattn_kernel, tq=tq, topk=topk, pos_bits=pos_bits),
        grid=(nb, nq),
        in_specs=[pl.BlockSpec((tq, IDX_HEADS * IDX_DIM), qrow), pl.BlockSpec((tq, LANES), qrow),
                  pl.BlockSpec((tq, C_DIM), qrow), pl.BlockSpec((seq, LANES), krow),
                  pl.BlockSpec((seq, C_KV_DIM), krow), pl.BlockSpec((seq, C_KV_DIM), krow)],
        out_specs=pl.BlockSpec((tq, C_DIM), qrow),
        out_shape=jax.ShapeDtypeStruct((nb * seq, C_DIM), F32),
        scratch_shapes=[pltpu.VMEM((nq, tq, tq), I32), pltpu.VMEM((nq, tq, tq), F32)],
        compiler_params=_cparams(2), name="prompt_attn")(qi, kw, q, kw, k, v)


SROWS = 8


def _sidx_kernel(pt_ref, qh_ref, wh_ref, kn_ref, *rest, npg, page):
    pages = rest[:npg]
    past_o, new_o = rest[npg:]
    qh = qh_ref[0].astype(BF16)
    wh = wh_ref[0] * ((IDX_DIM ** -0.5) * (IDX_HEADS ** -0.5))

    def scores(keys):
        s = jnp.maximum(_dot_nt(qh, keys.astype(BF16)), 0.0) * wh
        tot = s[0:SROWS]
        for hh in range(1, IDX_HEADS):
            tot = tot + s[hh * SROWS:(hh + 1) * SROWS]
        return tot

    for i in range(npg):
        past_o[0, :, i * page:(i + 1) * page] = scores(pages[i][0])

    @pl.when(pl.program_id(1) == 0)
    def _():
        new_o[0] = scores(kn_ref[0])


def _sample_index_scores(cache_kidx, page_table, layer, qh, wh, ki_new):
    depth, n_phys, page, _ = cache_kidx.shape
    db, n_pages = page_table.shape
    npg = PAGES_PER_STEP if n_pages % PAGES_PER_STEP == 0 else 1
    cache = cache_kidx.reshape(depth * n_phys, page, IDX_DIM)
    base = layer * n_phys

    def page_spec(i):
        return pl.BlockSpec((1, page, IDX_DIM), lambda b, s, pt: (base + pt[b, s * npg + i], 0, 0))

    per_b = lambda b, s, pt: (b, 0, 0)
    grid_spec = pltpu.PrefetchScalarGridSpec(
        num_scalar_prefetch=1, grid=(db, n_pages // npg),
        in_specs=[pl.BlockSpec((1, IDX_HEADS * SROWS, IDX_DIM), per_b),
                  pl.BlockSpec((1, IDX_HEADS * SROWS, 1), per_b),
                  pl.BlockSpec((1, LANES, IDX_DIM), per_b)] + [page_spec(i) for i in range(npg)],
        out_specs=[pl.BlockSpec((1, SROWS, npg * page), lambda b, s, pt: (b, 0, s)),
                   pl.BlockSpec((1, SROWS, LANES), per_b)])
    return pl.pallas_call(
        functools.partial(_sidx_kernel, npg=npg, page=page), grid_spec=grid_spec,
        out_shape=[jax.ShapeDtypeStruct((db, SROWS, n_pages * page), F32),
                   jax.ShapeDtypeStruct((db, SROWS, LANES), F32)],
        compiler_params=_cparams(2), name="sample_index")(page_table, qh, wh, ki_new, *([cache] * npg))


def _ssel_kernel(past_ref, new_ref, bpast_o, bnew_o, keys_ref, *, topk, pos_bits, n_new):
    rows, past = past_ref.shape
    nch = past // LANES
    lane = lax.broadcasted_iota(I32, (rows, LANES), 1)
    t_row = lax.broadcasted_iota(I32, (rows, LANES), 0) % SROWS

    def fill(c, carry):
        start = pl.multiple_of(c * LANES, LANES)
        keys_ref[c] = _score_keys(past_ref[:, pl.ds(start, LANES)])
        return carry

    lax.fori_loop(0, nch, fill, 0)
    new_ok = lane <= jnp.minimum(t_row, n_new - 1)
    keys_ref[nch] = jnp.where(new_ok, _score_keys(new_ref[...]), INT_MIN)

    def count_fn(pred):
        def body(c, acc):
            return acc + pred(keys_ref[c], c * LANES + lane)
        return _lane_total(lax.fori_loop(0, nch + 1, body, jnp.zeros((rows, LANES), I32)))

    thr, last_pos = _select_topk(count_fn, topk, pos_bits, rows)

    def emit(c, carry):
        start = pl.multiple_of(c * LANES, LANES)
        bpast_o[:, pl.ds(start, LANES)] = _select_bias(keys_ref[c], c * LANES + lane, thr, last_pos)
        return carry

    lax.fori_loop(0, nch, emit, 0)
    bnew_o[...] = _select_bias(keys_ref[nch], nch * LANES + lane, thr, last_pos)


def _sample_select(sc_past, sc_new, n_new):
    rows, past = sc_past.shape
    topk = min(TOPK_MAX, (past + n_new) // 4)
    tr = _tile(rows, 64)
    pos_bits = int(past + LANES - 1).bit_length()
    row = lambda i: (i, 0)
    return pl.pallas_call(
        functools.partial(_ssel_kernel, topk=topk, pos_bits=pos_bits, n_new=n_new),
        grid=(rows // tr,),
        in_specs=[pl.BlockSpec((tr, past), row), pl.BlockSpec((tr, LANES), row)],
        out_specs=[pl.BlockSpec((tr, past), row), pl.BlockSpec((tr, LANES), row)],
        out_shape=[jax.ShapeDtypeStruct((rows, past), F32), jax.ShapeDtypeStruct((rows, LANES), F32)],
        scratch_shapes=[pltpu.VMEM((past // LANES + 1, tr, LANES), I32)],
        compiler_params=_cparams(1), name="sample_select")(sc_past, sc_new)


def _satt_kernel(pt_ref, qbd_ref, bpast_ref, bnew_ref, kn_ref, vn_ref, *rest, npg, page):
    kpages = rest[:npg]
    vpages = rest[npg:2 * npg]
    o_ref, m_ref, l_ref, acc_ref = rest[2 * npg:]
    s_id = pl.program_id(1)
    nrow = qbd_ref.shape[1]
    reps = nrow // SROWS

    @pl.when(s_id == 0)
    def _():
        m_ref[...] = jnp.full_like(m_ref, NEG_BIG)
        l_ref[...] = jnp.zeros_like(l_ref)
        acc_ref[...] = jnp.zeros_like(acc_ref)

    qbd = qbd_ref[0].astype(BF16)

    def update(keys, vals, bias):
        s = _dot_nt(qbd, keys.astype(BF16)) + jnp.concatenate([bias] * reps, axis=0)
        m_old = m_ref[...]
        m_new = jnp.maximum(m_old, jnp.max(s, axis=-1, keepdims=True))
        p = jnp.exp(s - m_new)
        alpha = jnp.exp(m_old - m_new)
        l_ref[...] = alpha * l_ref[...] + jnp.sum(p, axis=-1, keepdims=True)
        acc_ref[...] = alpha * acc_ref[...] + _dot(p.astype(BF16), vals.astype(BF16))
        m_ref[...] = m_new

    keys = jnp.concatenate([kp[0] for kp in kpages], axis=0)
    vals = jnp.concatenate([vp[0] for vp in vpages], axis=0)
    update(keys, vals, bpast_ref[0])

    @pl.when(s_id == pl.num_programs(1) - 1)
    def _():
        update(kn_ref[0], vn_ref[0], bnew_ref[0])
        o_ref[0] = acc_ref[...] / l_ref[...]


def _sample_attend(cache_k, cache_v, page_table, layer, qbd, bias_past, bias_new, k_new, v_new):
    depth, n_phys, page = cache_k.shape[:3]
    db, n_pages = page_table.shape
    npg = PAGES_PER_STEP if n_pages % PAGES_PER_STEP == 0 else 1
    ck = cache_k.reshape(depth * n_phys, page, C_KV_DIM)
    cv = cache_v.reshape(depth * n_phys, page, C_KV_DIM)
    base = layer * n_phys
    nrow = qbd.shape[1]

    def page_spec(i):
        return pl.BlockSpec((1, page, C_KV_DIM), lambda b, s, pt: (base + pt[b, s * npg + i], 0, 0))

    per_b = lambda b, s, pt: (b, 0, 0)
    grid_spec = pltpu.PrefetchScalarGridSpec(
        num_scalar_prefetch=1, grid=(db, n_pages // npg),
        in_specs=[pl.BlockSpec((1, nrow, C_KV_DIM), per_b),
                  pl.BlockSpec((1, SROWS, npg * page), lambda b, s, pt: (b, 0, s)),
                  pl.BlockSpec((1, SROWS, LANES), per_b),
                  pl.BlockSpec((1, LANES, C_KV_DIM), per_b),
                  pl.BlockSpec((1, LANES, C_KV_DIM), per_b)]
                 + [page_spec(i) for i in range(npg)] * 2,
        out_specs=pl.BlockSpec((1, nrow, C_KV_DIM), per_b),
        scratch_shapes=[pltpu.VMEM((nrow, 1), F32), pltpu.VMEM((nrow, 1), F32),
                        pltpu.VMEM((nrow, C_KV_DIM), F32)])
    return pl.pallas_call(
        functools.partial(_satt_kernel, npg=npg, page=page), grid_spec=grid_spec,
        out_shape=jax.ShapeDtypeStruct((db, nrow, C_KV_DIM), F32),
        compiler_params=_cparams(2), name="sample_attn")(
            page_table, qbd, bias_past, bias_new, k_new, v_new, *([ck] * npg), *([cv] * npg))


def _merge_kernel(x_ref, g_ref, wg_ref, os_ref, bon_ref, gg_ref, lw_ref, lb_ref, bd_ref,
                  ob_ref, oc_ref, wpa_ref, wpb_ref, wpc_ref, wo_ref, o_ref):
    x = x_ref[...]
    d = x.shape[1]
    h = _rms(x, g_ref[...]).astype(BF16)
    gates = jax.nn.sigmoid(_dot(h, wg_ref[...]))
    bd = bd_ref[...]
    o = os_ref[...]
    mean = _segsum(o, bd) * (1.0 / HEAD_DIM)
    cen = o - mean
    var = _segsum(cen * cen, bd) * (1.0 / HEAD_DIM)
    on = cen * lax.rsqrt(var + GN_EPS) * lw_ref[...] + lb_ref[...]
    oa = ((on + bon_ref[...]) * gg_ref[...]).astype(BF16)
    merged = (gates[:, :d] * _dot(oa, wpa_ref[...])
              + gates[:, d:2 * d] * _dot(ob_ref[...].astype(BF16), wpb_ref[...])
              + gates[:, 2 * d:] * _dot(oc_ref[...].astype(BF16), wpc_ref[...]))
    o_ref[...] = x + _dot(merged.astype(BF16), wo_ref[...])


def _merge(x, o_scan, bonus, g, o_b, o_c, lw):
    m, d = x.shape
    tm = _tile(m, 512)
    row = lambda i: (i, 0)
    tok = lambda w: pl.BlockSpec((tm, w), row)
    consts_a = [lw["mix_norm"], lw["w_g"]]
    consts_b = [lw["rwkv_lnx_w"], lw["rwkv_lnx_b"], lw["bd"]]
    consts_c = [lw["w_pa"], lw["w_pb"], lw["w_pc"], lw["w_out"]]
    in_specs = ([tok(d)] + [_const_spec(c.shape) for c in consts_a] + [tok(A_DIM)] * 3
                + [_const_spec(c.shape) for c in consts_b] + [tok(B_DIM), tok(C_DIM)]
                + [_const_spec(c.shape) for c in consts_c])
    return pl.pallas_call(
        _merge_kernel, grid=(m // tm,), in_specs=in_specs, out_specs=tok(d),
        out_shape=jax.ShapeDtypeStruct((m, d), F32), compiler_params=_cparams(1), name="merge")(
            x, *consts_a, o_scan, bonus, g, *consts_b, o_b, o_c, *consts_c)


def _rope_tables(pos):
    inv = ROPE_THETA ** (-jnp.arange(ROT_HALF, dtype=F32) / ROT_HALF)
    ang = pos.astype(F32)[:, None] * inv[None, :]
    c, s = jnp.cos(ang), jnp.sin(ang)
    t = pos.shape[0]
    pad = jnp.zeros((t, HEAD_DIM - ROT_DIM), F32)
    zer = jnp.zeros((t, ROT_HALF), F32)
    cos = jnp.concatenate([c, c, pad + 1.0], axis=1)
    sa = jnp.concatenate([-s, zer, pad], axis=1)
    sb = jnp.concatenate([zer, s, pad], axis=1)
    return tuple(jnp.tile(z, (1, C_HEADS)) for z in (cos, sa, sb))


def _layer_weights(i, p):
    d = p["w_in"].shape[1]
    w_in = p["w_in"][i]
    o1 = RWKV_COLS
    o2 = o1 + CONV_COLS
    o3 = o2 + ATTN_COLS
    row = lambda v: v.reshape(1, -1)
    tile_h = lambda v, n: jnp.tile(v, n).reshape(1, -1)
    z = jnp.zeros((LORA_W, A_DIM), F32)
    w_wa = jnp.concatenate([jnp.concatenate([p["rwkv_w2"][i], z], axis=1),
                            jnp.concatenate([z, p["rwkv_a2"][i]], axis=1)], axis=0)
    head = np.arange(A_DIM) // HEAD_DIM
    bd = jnp.asarray(head[:, None] == head[None, :], BF16)
    idx_norm = jnp.concatenate([p["idx_k_norm"][i], jnp.zeros((LANES - IDX_DIM,), F32)])
    bf = lambda w: w.astype(BF16)
    return dict(
        ffn1_norm=p["ffn1_norm"][i], ffn1_wg=bf(p["ffn1_wg"][i]), ffn1_wu=bf(p["ffn1_wu"][i]),
        ffn1_wd=bf(p["ffn1_wd"][i]),
        ffn2_norm=p["ffn2_norm"][i], ffn2_wg=bf(p["ffn2_wg"][i]), ffn2_wu=bf(p["ffn2_wu"][i]),
        ffn2_wd=bf(p["ffn2_wd"][i]),
        mix_norm=row(p["mix_norm"][i]),
        w_a=bf(w_in[:, :o1]), w_b=bf(w_in[:, o1:o2]),
        w_c=bf(jnp.pad(w_in[:, o2:o3], ((0, 0), (0, ATTN_COLS_PAD - ATTN_COLS)))),
        w_g=bf(w_in[:, o3:]),
        rwkv_mu=row(p["rwkv_mu"][i]), rwkv_w0=row(p["rwkv_w0"][i]), rwkv_a0=row(p["rwkv_a0"][i]),
        w_wa=bf(w_wa), rwkv_g2=bf(p["rwkv_g2"][i]), rwkv_kk=row(p["rwkv_kk"][i]),
        rwkv_ka=row(p["rwkv_ka"][i]), rwkv_rk=row(p["rwkv_rk"][i]),
        rwkv_lnx_w=row(p["rwkv_lnx_w"][i]), rwkv_lnx_b=row(p["rwkv_lnx_b"][i]),
        conv_w=p["conv_w"][i], bd=bd,
        q_norm=tile_h(p["q_norm"][i], C_HEADS), k_norm=tile_h(p["k_norm"][i], C_KV_HEADS),
        idx_k_norm=row(idx_norm),
        w_pa=bf(p["w_pa"][i]), w_pb=bf(p["w_pb"][i]), w_pc=bf(p["w_pc"][i]), w_out=bf(p["w_out"][i]),
        ple_norm=p["ple_norm"][i], ple_gate=bf(p["ple_gate"][i]), ple_proj=bf(p["ple_proj"][i]),
    )


def _to_scan(x, nb, seq):
    return x.reshape(nb, seq, A_HEADS, HEAD_DIM).transpose(1, 3, 0, 2).reshape(seq, HEAD_DIM, nb * A_HEADS)


def _from_scan(o, nb, seq):
    return o.reshape(seq, HEAD_DIM, nb, A_HEADS).transpose(2, 0, 3, 1).reshape(nb * seq, A_DIM)


def _rwkv_branch(x, nb, seq, lw, wkv0, shift_rows):
    r, w, k, v, na, nb_, g, bonus, shift_o = _rwkv_pre(x, nb, seq, lw, shift_rows)
    s0 = wkv0.transpose(3, 2, 0, 1).reshape(HEAD_DIM, HEAD_DIM, nb * A_HEADS)
    ts = lambda z: _to_scan(z, nb, seq)
    o, s_fin = _scan(ts(r), ts(w), ts(k), ts(v), ts(na), ts(nb_), s0)
    wkv1 = s_fin.reshape(HEAD_DIM, HEAD_DIM, nb, A_HEADS).transpose(2, 3, 1, 0)
    if shift_rows is None:
        shift1 = shift_o.reshape(nb, RWKV_COLS)
    else:
        shift1 = shift_o.reshape(nb, seq, RWKV_COLS)[:, -1]
    return _from_scan(o, nb, seq), bonus, g, shift1, wkv1


def _layer_tail(x1, o_scan, bonus, g, o_b, o_c, p_emb, lw):
    x2 = _merge(x1, o_scan, bonus, g, o_b, o_c, lw)
    return _ffn(x2, lw["ffn2_norm"], lw["ffn2_wg"], lw["ffn2_wu"], lw["ffn2_wd"],
                ple=(p_emb, lw["ple_norm"], lw["ple_gate"], lw["ple_proj"]))


def _prompt_layer(x, p_emb, nb, seq, lw, tabs):
    x1 = _ffn(x, lw["ffn1_norm"], lw["ffn1_wg"], lw["ffn1_wu"], lw["ffn1_wd"])
    wkv0 = jnp.zeros((nb, A_HEADS, HEAD_DIM, HEAD_DIM), F32)
    o_scan, bonus, g, shift1, wkv1 = _rwkv_branch(x1, nb, seq, lw, wkv0, None)
    o_b, conv1 = _conv(x1, nb, seq, lw)
    q, k, v, qi, kw, ki = _attn_proj(x1, nb, seq, lw, tabs, False)
    o_c = _prompt_attention(q, k, v, qi, kw, nb, seq)
    x4 = _layer_tail(x1, o_scan, bonus, g, o_b, o_c, p_emb, lw)
    st = (k.reshape(nb, seq, C_KV_HEADS, HEAD_DIM), v.reshape(nb, seq, C_KV_HEADS, HEAD_DIM),
          ki.reshape(nb, seq, IDX_DIM), wkv1, shift1, conv1)
    return x4, st


def _sample_layer(x, p_emb, nb, seq, lw, tabs, layer, cache_k, cache_v, cache_kidx, page_table,
                  wkv0, shift0, conv0):
    assert seq <= SROWS
    x1 = _ffn(x, lw["ffn1_norm"], lw["ffn1_wg"], lw["ffn1_wu"], lw["ffn1_wd"])
    rep_rows = lambda z: jnp.repeat(z, seq, axis=0)
    o_scan, bonus, g, shift1, wkv1 = _rwkv_branch(x1, nb, seq, lw, wkv0, rep_rows(shift0))
    o_b, z_all = _conv(x1, nb, seq, lw, init=(rep_rows(conv0[:, 0]), rep_rows(conv0[:, 1])))
    conv1 = z_all.reshape(nb, seq, B_DIM)[:, seq - (CONV_W - 1):]
    q, k, v, qi, kw, ki = _attn_proj(x1, nb, seq, lw, tabs, True)

    pad_t = lambda z: jnp.pad(z, ((0, 0), (0, SROWS - seq)) + ((0, 0),) * (z.ndim - 2))
    qh = pad_t(qi.reshape(nb, seq, IDX_HEADS, IDX_DIM)).transpose(0, 2, 1, 3)
    qh = qh.reshape(nb, IDX_HEADS * SROWS, IDX_DIM)
    wi = kw[:, IDX_DIM:IDX_DIM + IDX_HEADS].reshape(nb, seq, IDX_HEADS)
    wh = pad_t(wi).transpose(0, 2, 1).reshape(nb, IDX_HEADS * SROWS, 1)
    pad_keys = lambda z: jnp.pad(z.reshape(nb, seq, -1), ((0, 0), (0, LANES - seq), (0, 0)))
    sc_past, sc_new = _sample_index_scores(cache_kidx, page_table, layer, qh, wh, pad_keys(ki))
    past = sc_past.shape[-1]
    b_past, b_new = _sample_select(sc_past.reshape(nb * SROWS, past), sc_new.reshape(nb * SROWS, LANES), seq)

    rep = C_HEADS // C_KV_HEADS
    qg = pad_t(q.reshape(nb, seq, C_KV_HEADS, rep, HEAD_DIM)).transpose(0, 2, 3, 1, 4)
    qg = qg * (HEAD_DIM ** -0.5)
    eye = jnp.eye(C_KV_HEADS, dtype=F32)
    qbd = jnp.einsum("bgrtd,gh->bgrthd", qg, eye).reshape(nb, C_HEADS * SROWS, C_KV_DIM)
    o = _sample_attend(cache_k, cache_v, page_table, layer, qbd,
                       b_past.reshape(nb, SROWS, past), b_new.reshape(nb, SROWS, LANES),
                       pad_keys(k), pad_keys(v))
    o = o.reshape(nb, C_KV_HEADS, rep, SROWS, C_KV_HEADS, HEAD_DIM)
    o = jnp.einsum("bgrthd,gh->bgrtd", o, eye)[:, :, :, :seq]
    o_c = o.transpose(0, 3, 1, 2, 4).reshape(nb * seq, C_DIM)

    x4 = _layer_tail(x1, o_scan, bonus, g, o_b, o_c, p_emb, lw)
    st = (k.reshape(nb, seq, C_KV_HEADS, HEAD_DIM), v.reshape(nb, seq, C_KV_HEADS, HEAD_DIM),
          ki.reshape(nb, seq, IDX_DIM), wkv1, shift1, conv1)
    return x4, st


def kernel(x_prompt, x_sample, cache_k, cache_v, cache_kidx, state_wkv, state_shift, state_conv, page_table, p_prompt, p_sample, ffn1_norm, ffn1_wg, ffn1_wu, ffn1_wd, mix_norm, w_in, rwkv_mu, rwkv_w0, rwkv_w2, rwkv_a0, rwkv_a2, rwkv_g2, rwkv_kk, rwkv_ka, rwkv_rk, rwkv_lnx_w, rwkv_lnx_b, conv_w, q_norm, k_norm, idx_k_norm, w_pa, w_pb, w_pc, w_out, ffn2_norm, ffn2_wg, ffn2_wu, ffn2_wd, ple_norm, ple_gate, ple_proj):
    params = dict(ffn1_norm=ffn1_norm, ffn1_wg=ffn1_wg, ffn1_wu=ffn1_wu, ffn1_wd=ffn1_wd,
                  mix_norm=mix_norm, w_in=w_in, rwkv_mu=rwkv_mu, rwkv_w0=rwkv_w0, rwkv_w2=rwkv_w2,
                  rwkv_a0=rwkv_a0, rwkv_a2=rwkv_a2, rwkv_g2=rwkv_g2, rwkv_kk=rwkv_kk, rwkv_ka=rwkv_ka,
                  rwkv_rk=rwkv_rk.reshape(rwkv_rk.shape[0], -1), rwkv_lnx_w=rwkv_lnx_w,
                  rwkv_lnx_b=rwkv_lnx_b, conv_w=conv_w, q_norm=q_norm, k_norm=k_norm,
                  idx_k_norm=idx_k_norm, w_pa=w_pa, w_pb=w_pb, w_pc=w_pc, w_out=w_out,
                  ffn2_norm=ffn2_norm, ffn2_wg=ffn2_wg, ffn2_wu=ffn2_wu, ffn2_wd=ffn2_wd,
                  ple_norm=ple_norm, ple_gate=ple_gate, ple_proj=ple_proj)
    nb, seq, d = x_prompt.shape
    db, dseq, _ = x_sample.shape
    depth = w_in.shape[0]
    past = page_table.shape[1] * cache_k.shape[2]
    tabs_p = _rope_tables(jnp.arange(seq, dtype=jnp.int32))
    tabs_s = tuple(jnp.tile(z, (db, 1)) for z in _rope_tables(past + jnp.arange(dseq, dtype=jnp.int32)))
    xp = x_prompt.reshape(nb * seq, d)
    xs = x_sample.reshape(db * dseq, d)
    outs_p, outs_s = [], []
    for i in range(depth):
        lw = _layer_weights(i, params)
        xp, st_p = _prompt_layer(xp, p_prompt[i].reshape(nb * seq, -1), nb, seq, lw, tabs_p)
        xs, st_s = _sample_layer(xs, p_sample[i].reshape(db * dseq, -1), db, dseq, lw, tabs_s, i,
                                 cache_k, cache_v, cache_kidx, page_table,
                                 state_wkv[i], state_shift[i], state_conv[i])
        outs_p.append(st_p)
        outs_s.append(st_s)
    k_p, v_p, kidx_p, wkv_p, shift_p, conv_p = [jnp.stack(z) for z in zip(*outs_p)]
    k_s, v_s, kidx_s, wkv_s, shift_s, conv_s = [jnp.stack(z) for z in zip(*outs_s)]
    return (xp.reshape(nb, seq, d), xs.reshape(db, dseq, d), k_p, v_p, kidx_p, wkv_p, shift_p, conv_p,
            k_s, v_s, kidx_s, wkv_s, shift_s, conv_s)
```

```python
import functools

import jax
import jax.numpy as jnp
import numpy as np
from jax import lax
from jax.experimental import pallas as pl
from jax.experimental.pallas import tpu as pltpu

F32 = jnp.float32
BF16 = jnp.bfloat16
I32 = jnp.int32

HEAD_DIM = 64
A_HEADS = 8
A_DIM = A_HEADS * HEAD_DIM
LORA_W = 64
LORA_A = 64
LORA_G = 128
B_DIM = 512
CONV_W = 3
C_HEADS = 8
C_KV_HEADS = 4
C_DIM = C_HEADS * HEAD_DIM
C_KV_DIM = C_KV_HEADS * HEAD_DIM
IDX_HEADS = 8
IDX_DIM = 64
TOPK_MAX = 256
ROT_DIM = HEAD_DIM // 4
ROT_HALF = ROT_DIM // 2
ROPE_THETA = 500000.0
N_BRANCH = 3
RMS_EPS = 1e-6
GN_EPS = 64e-5
RWKV_COLS = 3 * A_DIM + LORA_W + LORA_A + LORA_G
CONV_COLS = 3 * B_DIM
ATTN_COLS = C_DIM + 2 * C_KV_DIM + IDX_HEADS * IDX_DIM + IDX_DIM + IDX_HEADS
ATTN_COLS_PAD = 1664

LANES = 128
INT_MIN = -2 ** 31
NEG_BIG = -1e30
VMEM_LIMIT = 56 * 1024 * 1024
PAGES_PER_STEP = 16


def _cparams(n_axes):
    return pltpu.CompilerParams(dimension_semantics=("arbitrary",) * n_axes,
                                vmem_limit_bytes=VMEM_LIMIT)


def _const_spec(shape):
    nd = len(shape)
    return pl.BlockSpec(shape, lambda *_: (0,) * nd, pipeline_mode=pl.Buffered(1))


def _tile(m, pref):
    t = min(m, pref)
    while m % t:
        t -= 8
    return t


def _rms(x, g):
    ms = jnp.mean(x * x, axis=-1, keepdims=True)
    return x * lax.rsqrt(ms + RMS_EPS) * g


def _dot(a, b):
    return jnp.dot(a, b, preferred_element_type=F32)


def _dot_nt(a, b):
    return lax.dot_general(a, b, (((1,), (1,)), ((), ())), preferred_element_type=F32)


def _segsum(x, bd):
    hi = x.astype(BF16)
    lo = (x - hi.astype(F32)).astype(BF16)
    return _dot(hi, bd) + _dot(lo, bd)


def _rope(x, cos, sa, sb):
    n = x.shape[-1]
    return x * cos + pltpu.roll(x, n - ROT_HALF, 1) * sa + pltpu.roll(x, ROT_HALF, 1) * sb


def _rows_before(x, k, fills, *, seg_len, first_chunk_rows=None):
    rows = x.shape[0]
    y = pltpu.roll(x, k, 0)
    ridx = lax.broadcasted_iota(I32, (rows, 1), 0)
    t = ridx if seg_len is None else ridx % seg_len
    for r in range(k):
        y = jnp.where(t == r, fills[k - 1 - r], y)
    return y


def _softplus(y):
    return jnp.maximum(y, 0.0) + jnp.log(1.0 + jnp.exp(-jnp.abs(y)))


def _ffn_kernel(*refs, chunks, ple):
    if ple:
        x_ref, g_ref, wg_ref, wu_ref, wd_ref, p_ref, pn_ref, pg_ref, pp_ref, o_ref = refs
    else:
        x_ref, g_ref, wg_ref, wu_ref, wd_ref, o_ref = refs
    x = x_ref[...]
    h = _rms(x, g_ref[...]).astype(BF16)
    acc = None
    for lo, hi in chunks:
        gt = _dot(h, wg_ref[:, lo:hi])
        ut = _dot(h, wu_ref[:, lo:hi])
        act = (gt * jax.nn.sigmoid(gt) * ut).astype(BF16)
        d = _dot(act, wd_ref[lo:hi, :])
        acc = d if acc is None else acc + d
    y = x + 0.5 * acc
    if ple:
        hg = _rms(y, pn_ref[...]).astype(BF16)
        gate = jax.nn.sigmoid(_dot(hg, pg_ref[...]))
        y = y + gate * _dot(p_ref[...].astype(BF16), pp_ref[...])
    o_ref[...] = y


def _ffn(x, g, wg, wu, wd, ple=None):
    m, d = x.shape
    f = wg.shape[1]
    tm = _tile(m, 512)
    step = 1024
    chunks = tuple((lo, min(lo + step, f)) for lo in range(0, f, step))
    row = lambda i: (i, 0)
    in_specs = [pl.BlockSpec((tm, d), row), _const_spec((1, d)), _const_spec((d, f)),
                _const_spec((d, f)), _const_spec((f, d))]
    args = [x, g.reshape(1, d), wg, wu, wd]
    if ple is not None:
        (p, layer), pn, pg, pp = ple
        off = layer * (m // tm)
        in_specs += [pl.BlockSpec((tm, p.shape[1]), lambda i: (i + off, 0)), _const_spec((1, d)),
                     _const_spec(pg.shape), _const_spec(pp.shape)]
        args += [p, pn.reshape(1, d), pg, pp]
    return pl.pallas_call(
        functools.partial(_ffn_kernel, chunks=chunks, ple=ple is not None),
        grid=(m // tm,), in_specs=in_specs, out_specs=pl.BlockSpec((tm, d), row),
        out_shape=jax.ShapeDtypeStruct((m, d), F32), compiler_params=_cparams(1),
        name="ffn_ple" if ple is not None else "ffn")(*args)


def _rwkv_pre_kernel(*refs, seg_len):
    (x_ref, g_ref, wa_ref, mu_ref, w0_ref, a0_ref, wwa_ref, g2_ref, kkw_ref, ka_ref,
     rk_ref, bd_ref) = refs[:12]
    rest = refs[12:]
    if seg_len is None:
        r_o, w_o, k_o, v_o, na_o, nb_o, g_o, bonus_o, shift_o, carry_ref = rest
    else:
        init_ref, r_o, w_o, k_o, v_o, na_o, nb_o, g_o, bonus_o, shift_o = rest
    x = x_ref[...]
    tm = x.shape[0]
    h = _rms(x, g_ref[...]).astype(BF16)
    u = _dot(h, wa_ref[...])
    if seg_len is None:
        @pl.when(pl.program_id(1) == 0)
        def _():
            carry_ref[...] = jnp.zeros_like(carry_ref)
        u_prev = _rows_before(u, 1, [carry_ref[0:1, :]], seg_len=None)
        carry_ref[0:1, :] = u[tm - 1:tm, :]
        shift_o[0] = u[tm - 1:tm, :]
    else:
        u_prev = _rows_before(u, 1, [init_ref[...]], seg_len=seg_len)
        shift_o[...] = u
    us = u + (u_prev - u) * mu_ref[...]
    r = us[:, 0:A_DIM]
    k = us[:, A_DIM:2 * A_DIM]
    v = us[:, 2 * A_DIM:3 * A_DIM]
    o3 = 3 * A_DIM
    xwa = us[:, o3:o3 + LORA_W + LORA_A]
    lane = lax.broadcasted_iota(I32, xwa.shape, 1)
    xwa = jnp.where(lane < LORA_W, jnp.tanh(xwa), xwa)
    lo = _dot(xwa.astype(BF16), wwa_ref[...])
    w_log = -_softplus(-(w0_ref[...] + lo[:, :A_DIM])) - 0.5
    decay = jnp.exp(-jnp.exp(w_log))
    a = jax.nn.sigmoid(a0_ref[...] + lo[:, A_DIM:])
    xg = us[:, o3 + LORA_W + LORA_A:]
    g = _dot(jax.nn.sigmoid(xg).astype(BF16), g2_ref[...])
    bd = bd_ref[...]
    kk = k * kkw_ref[...]
    kk = kk / jnp.maximum(jnp.sqrt(_segsum(kk * kk, bd)), 1e-12)
    k2 = k * (1.0 + (a - 1.0) * ka_ref[...])
    r_o[...] = r
    w_o[...] = decay
    k_o[...] = k2
    v_o[...] = v
    na_o[...] = -kk
    nb_o[...] = kk * a
    g_o[...] = g
    bonus_o[...] = _segsum(r * k2 * rk_ref[...], bd) * v


def _rwkv_pre(x, nb, seq, lw, init_rows=None):
    m, d = x.shape
    consts = [lw["mix_norm"], lw["w_a"], lw["rwkv_mu"], lw["rwkv_w0"], lw["rwkv_a0"], lw["w_wa"],
              lw["rwkv_g2"], lw["rwkv_kk"], lw["rwkv_ka"], lw["rwkv_rk"], lw["bd"]]
    const_specs = [_const_spec(c.shape) for c in consts]
    outs = [jax.ShapeDtypeStruct((m, A_DIM), F32)] * 8
    if init_rows is None:
        tm = _tile(seq, 512)
        nc = seq // tm
        row = lambda b, c: (b * nc + c, 0)
        grid = (nb, nc)
        in_specs = [pl.BlockSpec((tm, d), row)] + const_specs
        out_specs = [pl.BlockSpec((tm, A_DIM), row)] * 8 + [pl.BlockSpec((1, 1, RWKV_COLS), lambda b, c: (b, 0, 0))]
        outs = outs + [jax.ShapeDtypeStruct((nb, 1, RWKV_COLS), F32)]
        scratch = [pltpu.VMEM((8, RWKV_COLS), F32)]
        args = [x] + consts
        seg_len = None
    else:
        tm = m
        row = lambda i: (0, 0)
        grid = (1,)
        in_specs = [pl.BlockSpec((tm, d), row)] + const_specs + [pl.BlockSpec((tm, RWKV_COLS), row)]
        out_specs = [pl.BlockSpec((tm, A_DIM), row)] * 8 + [pl.BlockSpec((tm, RWKV_COLS), row)]
        outs = outs + [jax.ShapeDtypeStruct((m, RWKV_COLS), F32)]
        scratch = []
        args = [x] + consts + [init_rows]
        seg_len = seq
    return pl.pallas_call(
        functools.partial(_rwkv_pre_kernel, seg_len=seg_len), grid=grid, in_specs=in_specs,
        out_specs=out_specs, out_shape=outs, scratch_shapes=scratch,
        compiler_params=_cparams(len(grid)), name="rwkv_pre")(*args)


def _scan_kernel(r_ref, w_ref, k_ref, v_ref, a_ref, b_ref, s0_ref, o_ref, st_ref, s_ref, *, tc):
    c = pl.program_id(1)

    @pl.when(c == 0)
    def _():
        s_ref[...] = s0_ref[...]

    def step(t, carry):
        sa = None
        for j in range(HEAD_DIM):
            term = s_ref[j] * a_ref[t, j:j + 1, :]
            sa = term if sa is None else sa + term
        vt = v_ref[t]
        o = None
        for j in range(HEAD_DIM):
            sn = (s_ref[j] * w_ref[t, j:j + 1, :] + sa * b_ref[t, j:j + 1, :]
                  + vt * k_ref[t, j:j + 1, :])
            s_ref[j] = sn
            term = sn * r_ref[t, j:j + 1, :]
            o = term if o is None else o + term
        o_ref[t] = o
        return carry

    lax.fori_loop(0, tc, step, 0)

    @pl.when(c == pl.num_programs(1) - 1)
    def _():
        st_ref[...] = s_ref[...]


def _scan(r, w, k, v, a, b, s0):
    t_len, hd, nbh = r.shape
    ln = min(LANES, nbh)
    tc = _tile(t_len, 32) if t_len % 8 == 0 else t_len
    grid = (nbh // ln, t_len // tc)
    seq_spec = pl.BlockSpec((tc, hd, ln), lambda l, c: (c, 0, l))
    st_spec = pl.BlockSpec((hd, hd, ln), lambda l, c: (0, 0, l))
    return pl.pallas_call(
        functools.partial(_scan_kernel, tc=tc), grid=grid,
        in_specs=[seq_spec] * 6 + [st_spec], out_specs=[seq_spec, st_spec],
        out_shape=[jax.ShapeDtypeStruct((t_len, hd, nbh), F32),
                   jax.ShapeDtypeStruct((hd, hd, nbh), F32)],
        scratch_shapes=[pltpu.VMEM((hd, hd, ln), F32)],
        compiler_params=_cparams(2), name="rwkv_scan")(r, w, k, v, a, b, s0)


def _conv_kernel(*refs, seg_len):
    x_ref, g_ref, wb_ref, cw_ref = refs[:4]
    rest = refs[4:]
    if seg_len is None:
        o_ref, st_ref, carry_ref = rest
    else:
        i0_ref, i1_ref, o_ref, st_ref = rest
    x = x_ref[...]
    tm = x.shape[0]
    h = _rms(x, g_ref[...]).astype(BF16)
    u = _dot(h, wb_ref[...])
    bg = u[:, :B_DIM]
    z = u[:, B_DIM:2 * B_DIM] * u[:, 2 * B_DIM:]
    if seg_len is None:
        @pl.when(pl.program_id(1) == 0)
        def _():
            carry_ref[...] = jnp.zeros_like(carry_ref)
        hist = [carry_ref[1:2, :], carry_ref[0:1, :]]
        z1 = _rows_before(z, 1, hist[:1], seg_len=None)
        z2 = _rows_before(z, 2, hist, seg_len=None)
        carry_ref[0:2, :] = z[tm - 2:tm, :]
        st_ref[0] = z[tm - 2:tm, :]
    else:
        hist = [i1_ref[...], i0_ref[...]]
        z1 = _rows_before(z, 1, hist[:1], seg_len=seg_len)
        z2 = _rows_before(z, 2, hist, seg_len=seg_len)
        st_ref[...] = z
    cw = cw_ref[...]
    y = z2 * cw[0:1, :] + z1 * cw[1:2, :] + z * cw[2:3, :]
    o_ref[...] = bg * y


def _conv(x, nb, seq, lw, init=None):
    m, d = x.shape
    assert seq >= CONV_W - 1
    consts = [lw["mix_norm"], lw["w_b"], lw["conv_w"]]
    const_specs = [_const_spec(c.shape) for c in consts]
    if init is None:
        tm = _tile(seq, 512)
        nc = seq // tm
        row = lambda b, c: (b * nc + c, 0)
        grid = (nb, nc)
        in_specs = [pl.BlockSpec((tm, d), row)] + const_specs
        out_specs = [pl.BlockSpec((tm, B_DIM), row),
                     pl.BlockSpec((1, CONV_W - 1, B_DIM), lambda b, c: (b, 0, 0))]
        outs = [jax.ShapeDtypeStruct((m, B_DIM), F32), jax.ShapeDtypeStruct((nb, CONV_W - 1, B_DIM), F32)]
        scratch = [pltpu.VMEM((8, B_DIM), F32)]
        args = [x] + consts
        seg_len = None
    else:
        tm = m
        row = lambda i: (0, 0)
        grid = (1,)
        in_specs = [pl.BlockSpec((tm, d), row)] + const_specs + [pl.BlockSpec((tm, B_DIM), row)] * 2
        out_specs = [pl.BlockSpec((tm, B_DIM), row)] * 2
        outs = [jax.ShapeDtypeStruct((m, B_DIM), F32)] * 2
        scratch = []
        args = [x] + consts + list(init)
        seg_len = seq
    return pl.pallas_call(
        functools.partial(_conv_kernel, seg_len=seg_len), grid=grid, in_specs=in_specs,
        out_specs=out_specs, out_shape=outs, scratch_shapes=scratch,
        compiler_params=_cparams(len(grid)), name="shortconv")(*args)


def _attn_proj_kernel(x_ref, g_ref, wc_ref, qn_ref, kn_ref, in_ref, bd_ref, cos_ref, sa_ref, sb_ref,
                      *outs, transposed):
    x = x_ref[...]
    h = _rms(x, g_ref[...]).astype(BF16)
    u = _dot(h, wc_ref[...])
    cos, sa, sb = cos_ref[...], sa_ref[...], sb_ref[...]
    bd = bd_ref[...]
    inv_hd = 1.0 / HEAD_DIM
    q = u[:, :C_DIM]
    q = q * lax.rsqrt(_segsum(q * q, bd) * inv_hd + RMS_EPS) * qn_ref[...]
    q = _rope(q, cos, sa, sb)
    k = u[:, C_DIM:C_DIM + C_KV_DIM]
    k = k * lax.rsqrt(_segsum(k * k, bd[:C_KV_DIM, :C_KV_DIM]) * inv_hd + RMS_EPS) * kn_ref[...]
    k = _rope(k, cos[:, :C_KV_DIM], sa[:, :C_KV_DIM], sb[:, :C_KV_DIM])
    o = C_DIM + C_KV_DIM
    v = u[:, o:o + C_KV_DIM]
    o += C_KV_DIM
    qi = _rope(u[:, o:o + IDX_HEADS * IDX_DIM], cos, sa, sb)
    o += IDX_HEADS * IDX_DIM
    kw = u[:, o:o + LANES]
    lane = lax.broadcasted_iota(I32, kw.shape, 1)
    is_ki = lane < IDX_DIM
    ms = jnp.sum(jnp.where(is_ki, kw * kw, 0.0), axis=-1, keepdims=True) * (1.0 / IDX_DIM)
    kin = _rope(kw * lax.rsqrt(ms + RMS_EPS) * in_ref[...], cos[:, :LANES], sa[:, :LANES], sb[:, :LANES])
    kw = jnp.where(is_ki, kin, kw)
    if transposed:
        q_o, k_o, qi_o, kw_o, kt_o, vt_o, kit_o = outs
        kt_o[0] = jnp.transpose(k)
        vt_o[0] = jnp.transpose(v)
        kit_o[0] = jnp.transpose(kw)[:IDX_DIM, :]
    else:
        q_o, k_o, qi_o, kw_o, v_o, ki_o = outs
        v_o[...] = v
        ki_o[...] = kw[:, :IDX_DIM]
    q_o[...] = q
    k_o[...] = k
    qi_o[...] = qi
    kw_o[...] = kw


def _attn_proj(x, nb, seq, lw, tabs, sample):
    m, d = x.shape
    consts = [lw["mix_norm"], lw["w_c"], lw["q_norm"], lw["k_norm"], lw["idx_k_norm"], lw["bd"]]
    const_specs = [_const_spec(c.shape) for c in consts]
    if sample:
        tm, nc = m, 1
        grid = (1, 1)
    else:
        tm = _tile(seq, 512)
        nc = seq // tm
        grid = (nb, nc)
    row = lambda b, c: (b * nc + c, 0)
    tab = lambda b, c: (c, 0)
    widths = [C_DIM, C_KV_DIM, IDX_HEADS * IDX_DIM, LANES]
    out_specs = [pl.BlockSpec((tm, w), row) for w in widths]
    out_shape = [jax.ShapeDtypeStruct((m, w), F32) for w in widths]
    if sample:
        for w in (C_KV_DIM, IDX_DIM):
            out_specs.append(pl.BlockSpec((tm, w), row))
            out_shape.append(jax.ShapeDtypeStruct((m, w), F32))
    else:
        for w in (C_KV_DIM, C_KV_DIM, IDX_DIM):
            out_specs.append(pl.BlockSpec((1, w, tm), lambda b, c: (b, 0, c)))
            out_shape.append(jax.ShapeDtypeStruct((nb, w, seq), F32))
    return pl.pallas_call(
        functools.partial(_attn_proj_kernel, transposed=not sample), grid=grid,
        in_specs=[pl.BlockSpec((tm, d), row)] + const_specs + [pl.BlockSpec((tm, C_DIM), tab)] * 3,
        out_specs=out_specs, out_shape=out_shape,
        compiler_params=_cparams(2), name="attn_proj")(x, *consts, *tabs)


def _score_keys(score):
    bits = pltpu.bitcast(score, I32)
    return jnp.where(bits < 0, bits ^ 0x7FFFFFFF, bits)


def _lane_total(acc):
    return jnp.broadcast_to(jnp.sum(acc, axis=-1, keepdims=True), acc.shape)


def _tree(parts, op):
    while len(parts) > 1:
        nxt = [op(parts[i], parts[i + 1]) for i in range(0, len(parts) - 1, 2)]
        parts = nxt + (parts[-1:] if len(parts) % 2 else [])
    return parts[0]


def _fold8(x, op):
    return _tree([x[i * 8:(i + 1) * 8] for i in range(x.shape[0] // 8)], op)


def _select_topk(count_fn, topk, pos_bits, shape, lp_ref):
    zeros = jnp.zeros(shape, I32)

    def vbit(it, acc):
        cand = acc | jnp.left_shift(jnp.int32(1), 31 - it)
        cmp = cand ^ INT_MIN
        cnt = count_fn(lambda kk, pos: jnp.where(kk >= cmp, 1, 0))
        return jnp.where(cnt >= topk, cand, acc)

    thr = lax.fori_loop(0, 32, vbit, zeros) ^ INT_MIN
    n_ge = count_fn(lambda kk, pos: jnp.where(kk >= thr, 1, 0))
    excess = jnp.where(thr == INT_MIN, 0, n_ge - topk)
    lp_ref[...] = jnp.full(shape, 2 ** 31 - 1, I32)

    @pl.when(jnp.max(excess) > 0)
    def _():
        need = topk - count_fn(lambda kk, pos: jnp.where(kk > thr, 1, 0))

        def pbit(it, acc):
            cand = acc | jnp.left_shift(jnp.int32(1), pos_bits - 1 - it)
            cnt = count_fn(lambda kk, pos: jnp.where(kk == thr, jnp.where(pos < cand, 1, 0), 0))
            return jnp.where(cnt < need, cand, acc)

        lp_ref[...] = lax.fori_loop(0, pos_bits, pbit, zeros)

    return thr, lp_ref[...]


def _select_bias(kk, pos, thr, last_pos):
    tie = jnp.where(pos <= last_pos, 0.0, NEG_BIG)
    bias = jnp.where(kk == thr, tie, jnp.where(kk > thr, 0.0, NEG_BIG))
    return jnp.where(kk == INT_MIN, NEG_BIG, bias)


def _pattn_kernel(qi_ref, kwq_ref, q_ref, kwk_ref, k_ref, vt_ref, o_ref,
                  keys_ref, bias_ref, qip_ref, qg_ref, lp_ref, acc_ref, *, tq, topk, pos_bits):
    qb = pl.program_id(1)
    nch = qb + 1
    nslab = tq // 8
    shape8 = (8, tq)
    sub8 = lax.broadcasted_iota(I32, shape8, 0)
    lane_q = lax.broadcasted_iota(I32, (tq, LANES), 1)
    key_in = lax.broadcasted_iota(I32, (tq, tq), 0)
    qry_in = lax.broadcasted_iota(I32, (tq, tq), 1)

    wt = jnp.transpose(kwq_ref[...]) * ((IDX_DIM ** -0.5) * (IDX_HEADS ** -0.5))
    for hh in range(IDX_HEADS):
        slab = qi_ref[:, (hh // 2) * LANES:(hh // 2 + 1) * LANES]
        if hh % 2:
            slab = pltpu.roll(slab, IDX_DIM, 1)
        qip_ref[hh] = jnp.where(lane_q < IDX_DIM, slab, 0.0).astype(BF16)

    def idx_body(kc, carry):
        start = pl.multiple_of(kc * tq, tq)
        ks = kwk_ref[pl.ds(start, tq), :].astype(BF16)
        acc = None
        for hh in range(IDX_HEADS):
            s = _dot_nt(ks, qip_ref[hh])
            term = jnp.maximum(s, 0.0) * wt[IDX_DIM + hh:IDX_DIM + hh + 1, :]
            acc = term if acc is None else acc + term
        kk = jnp.where(acc == 0.0, 0, _score_keys(acc))
        keys_ref[kc] = jnp.where(kc * tq + key_in <= qb * tq + qry_in, kk, INT_MIN)
        return carry

    lax.fori_loop(0, nch, idx_body, 0)

    def count_fn(ind):
        def body(kc, acc):
            base = kc * tq
            parts = [ind(keys_ref[kc, i * 8:(i + 1) * 8, :], base + i * 8 + sub8) for i in range(nslab)]
            return acc + _tree(parts, jnp.add)
        acc = lax.fori_loop(0, nch, body, jnp.zeros(shape8, I32))
        return jnp.broadcast_to(jnp.sum(acc, axis=0, keepdims=True), shape8)

    thr, last_pos = _select_topk(count_fn, topk, pos_bits, shape8, lp_ref)

    scale = (HEAD_DIM ** -0.5) * 1.4426950408889634
    for g in range(C_KV_HEADS):
        qslab = q_ref[:, g * LANES:(g + 1) * LANES] * scale
        rolled = pltpu.roll(qslab, HEAD_DIM, 1)
        koff = g % 2
        in_half = (lane_q >= HEAD_DIM) if koff else (lane_q < HEAD_DIM)
        qg_ref[g] = jnp.concatenate(
            [jnp.where(in_half, qslab if r == koff else rolled, 0.0) for r in range(2)],
            axis=0).astype(BF16)

    def scores(kc, g, bias2):
        start = pl.multiple_of(kc * tq, tq)
        kcol = (g // 2) * LANES
        kch = k_ref[pl.ds(start, tq), kcol:kcol + LANES].astype(BF16)
        return _dot_nt(kch, qg_ref[g]) + bias2

    def max_body(kc, carry):
        base = kc * tq
        for i in range(nslab):
            bias_ref[kc, i * 8:(i + 1) * 8, :] = _select_bias(
                keys_ref[kc, i * 8:(i + 1) * 8, :], base + i * 8 + sub8, thr, last_pos)
        bias = bias_ref[kc]
        bias2 = jnp.concatenate([bias, bias], axis=1)
        return tuple(jnp.maximum(carry[g], _fold8(scores(kc, g, bias2), jnp.maximum))
                     for g in range(C_KV_HEADS))

    m8 = lax.fori_loop(0, nch, max_body,
                       tuple(jnp.full((8, 2 * tq), NEG_BIG, F32) for _ in range(C_KV_HEADS)))
    m_row = [jnp.max(m, axis=0, keepdims=True) for m in m8]

    acc_ref[...] = jnp.zeros_like(acc_ref)

    def sum_body(kc, carry):
        start = pl.multiple_of(kc * tq, tq)
        bias = bias_ref[kc]
        bias2 = jnp.concatenate([bias, bias], axis=1)
        new = []
        for g in range(C_KV_HEADS):
            p = jnp.exp2(scores(kc, g, bias2) - m_row[g])
            vt = vt_ref[0, g * HEAD_DIM:(g + 1) * HEAD_DIM, pl.ds(start, tq)].astype(BF16)
            acc_ref[g] += _dot(vt, p.astype(BF16))
            new.append(carry[g] + _fold8(p, jnp.add))
        return tuple(new)

    l8 = lax.fori_loop(0, nch, sum_body,
                       tuple(jnp.zeros((8, 2 * tq), F32) for _ in range(C_KV_HEADS)))
    for g in range(C_KV_HEADS):
        out_t = acc_ref[g] / jnp.sum(l8[g], axis=0, keepdims=True)
        o_ref[:, g * LANES:(g + 1) * LANES] = jnp.transpose(
            jnp.concatenate([out_t[:, :tq], out_t[:, tq:]], axis=0))


def _prompt_attention(q, k, vt, qi, kw, nb, seq):
    assert C_HEADS == 2 * C_KV_HEADS
    topk = min(TOPK_MAX, seq // 4)
    tq = _tile(seq, 256)
    assert tq % LANES == 0
    nq = seq // tq
    pos_bits = max(1, int(seq - 1).bit_length())
    qrow = lambda b, c: (b * nq + c, 0)
    krow = lambda b, c: (b, 0)
    return pl.pallas_call(
        functools.partial(_pattn_kernel, tq=tq, topk=topk, pos_bits=pos_bits),
        grid=(nb, nq),
        in_specs=[pl.BlockSpec((tq, IDX_HEADS * IDX_DIM), qrow), pl.BlockSpec((tq, LANES), qrow),
                  pl.BlockSpec((tq, C_DIM), qrow), pl.BlockSpec((seq, LANES), krow),
                  pl.BlockSpec((seq, C_KV_DIM), krow),
                  pl.BlockSpec((1, C_KV_DIM, seq), lambda b, c: (b, 0, 0))],
        out_specs=pl.BlockSpec((tq, C_DIM), qrow),
        out_shape=jax.ShapeDtypeStruct((nb * seq, C_DIM), F32),
        scratch_shapes=[pltpu.VMEM((nq, tq, tq), I32), pltpu.VMEM((nq, tq, tq), F32),
                        pltpu.VMEM((IDX_HEADS, tq, LANES), BF16),
                        pltpu.VMEM((C_KV_HEADS, 2 * tq, LANES), BF16), pltpu.VMEM((8, tq), I32),
                        pltpu.VMEM((C_KV_HEADS, HEAD_DIM, 2 * tq), F32)],
        compiler_params=_cparams(2), name="prompt_attn")(qi, kw, q, kw, k, vt)


SROWS = 8


def _sidx_kernel(pt_ref, qh_ref, wh_ref, kn_ref, *rest, npg, page):
    pages = rest[:npg]
    past_o, new_o = rest[npg:]
    qh = qh_ref[0].astype(BF16)
    wh = wh_ref[0] * ((IDX_DIM ** -0.5) * (IDX_HEADS ** -0.5))

    def head_sum(s):
        s = jnp.maximum(s, 0.0) * wh
        return _tree([s[hh * SROWS:(hh + 1) * SROWS] for hh in range(IDX_HEADS)], jnp.add)

    keys_t = jnp.concatenate([pg[0] for pg in pages], axis=1).astype(BF16)
    past_o[0] = head_sum(_dot(qh, keys_t))

    @pl.when(pl.program_id(1) == 0)
    def _():
        new_o[0] = head_sum(_dot_nt(qh, kn_ref[0].astype(BF16)))


def _sample_index_scores(cache_kidx, page_table, layer, qh, wh, ki_new):
    depth, n_phys, page, _ = cache_kidx.shape
    db, n_pages = page_table.shape
    npg = PAGES_PER_STEP if n_pages % PAGES_PER_STEP == 0 else 1
    cache = cache_kidx.transpose(0, 1, 3, 2).reshape(depth * n_phys, IDX_DIM, page)
    base = layer * n_phys

    def page_spec(i):
        return pl.BlockSpec((1, IDX_DIM, page), lambda b, s, pt: (base + pt[b, s * npg + i], 0, 0))

    per_b = lambda b, s, pt: (b, 0, 0)
    grid_spec = pltpu.PrefetchScalarGridSpec(
        num_scalar_prefetch=1, grid=(db, n_pages // npg),
        in_specs=[pl.BlockSpec((1, IDX_HEADS * SROWS, IDX_DIM), per_b),
                  pl.BlockSpec((1, IDX_HEADS * SROWS, 1), per_b),
                  pl.BlockSpec((1, LANES, IDX_DIM), per_b)] + [page_spec(i) for i in range(npg)],
        out_specs=[pl.BlockSpec((1, SROWS, npg * page), lambda b, s, pt: (b, 0, s)),
                   pl.BlockSpec((1, SROWS, LANES), per_b)])
    return pl.pallas_call(
        functools.partial(_sidx_kernel, npg=npg, page=page), grid_spec=grid_spec,
        out_shape=[jax.ShapeDtypeStruct((db, SROWS, n_pages * page), F32),
                   jax.ShapeDtypeStruct((db, SROWS, LANES), F32)],
        compiler_params=_cparams(2), name="sample_index")(page_table, qh, wh, ki_new, *([cache] * npg))


def _ssel_kernel(past_ref, new_ref, bpast_o, bnew_o, keys_ref, lp_ref, *, topk, pos_bits, n_new):
    rows, past = past_ref.shape
    nch = past // LANES
    lane = lax.broadcasted_iota(I32, (rows, LANES), 1)
    t_row = lax.broadcasted_iota(I32, (rows, LANES), 0) % SROWS

    def fill(c, carry):
        start = pl.multiple_of(c * LANES, LANES)
        keys_ref[c] = _score_keys(past_ref[:, pl.ds(start, LANES)])
        return carry

    lax.fori_loop(0, nch, fill, 0)
    new_ok = lane <= jnp.minimum(t_row, n_new - 1)
    keys_ref[nch] = jnp.where(new_ok, _score_keys(new_ref[...]), INT_MIN)

    def count_fn(pred):
        def body(c, acc):
            return acc + pred(keys_ref[c], c * LANES + lane)
        return _lane_total(lax.fori_loop(0, nch + 1, body, jnp.zeros((rows, LANES), I32)))

    thr, last_pos = _select_topk(count_fn, topk, pos_bits, (rows, LANES), lp_ref)

    def emit(c, carry):
        start = pl.multiple_of(c * LANES, LANES)
        bpast_o[:, pl.ds(start, LANES)] = _select_bias(keys_ref[c], c * LANES + lane, thr, last_pos)
        return carry

    lax.fori_loop(0, nch, emit, 0)
    bnew_o[...] = _select_bias(keys_ref[nch], nch * LANES + lane, thr, last_pos)


def _sample_select(sc_past, sc_new, n_new):
    rows, past = sc_past.shape
    topk = min(TOPK_MAX, (past + n_new) // 4)
    tr = _tile(rows, 64)
    pos_bits = int(past + LANES - 1).bit_length()
    row = lambda i: (i, 0)
    return pl.pallas_call(
        functools.partial(_ssel_kernel, topk=topk, pos_bits=pos_bits, n_new=n_new),
        grid=(rows // tr,),
        in_specs=[pl.BlockSpec((tr, past), row), pl.BlockSpec((tr, LANES), row)],
        out_specs=[pl.BlockSpec((tr, past), row), pl.BlockSpec((tr, LANES), row)],
        out_shape=[jax.ShapeDtypeStruct((rows, past), F32), jax.ShapeDtypeStruct((rows, LANES), F32)],
        scratch_shapes=[pltpu.VMEM((past // LANES + 1, tr, LANES), I32), pltpu.VMEM((tr, LANES), I32)],
        compiler_params=_cparams(1), name="sample_select")(sc_past, sc_new)


def _satt_kernel(pt_ref, qbd_ref, bpast_ref, bnew_ref, kn_ref, vn_ref, *rest, npg, page):
    kpages = rest[:npg]
    vpages = rest[npg:2 * npg]
    o_ref, m_ref, l_ref, acc_ref = rest[2 * npg:]
    s_id = pl.program_id(1)
    nrow = qbd_ref.shape[1]
    reps = nrow // SROWS

    @pl.when(s_id == 0)
    def _():
        m_ref[...] = jnp.full_like(m_ref, NEG_BIG)
        l_ref[...] = jnp.zeros_like(l_ref)
        acc_ref[...] = jnp.zeros_like(acc_ref)

    qbd = qbd_ref[0].astype(BF16)

    def update(s, bias, pv):
        s = s + jnp.concatenate([bias] * reps, axis=0)
        m_old = m_ref[...]
        m_new = jnp.maximum(m_old, jnp.max(s, axis=-1, keepdims=True))
        p = jnp.exp(s - m_new)
        alpha = jnp.exp(m_old - m_new)
        l_ref[...] = alpha * l_ref[...] + jnp.sum(p, axis=-1, keepdims=True)
        acc_ref[...] = alpha * acc_ref[...] + pv(p.astype(BF16))
        m_ref[...] = m_new

    keys_t = jnp.concatenate([kp[0] for kp in kpages], axis=1).astype(BF16)
    vals_t = jnp.concatenate([vp[0] for vp in vpages], axis=1).astype(BF16)
    update(_dot(qbd, keys_t), bpast_ref[0], lambda p: _dot_nt(p, vals_t))

    @pl.when(s_id == pl.num_programs(1) - 1)
    def _():
        vn = vn_ref[0].astype(BF16)
        update(_dot_nt(qbd, kn_ref[0].astype(BF16)), bnew_ref[0], lambda p: _dot(p, vn))
        o_ref[0] = acc_ref[...] / l_ref[...]


def _sample_attend(cache_k, cache_v, page_table, layer, qbd, bias_past, bias_new, k_new, v_new):
    depth, n_phys, page = cache_k.shape[:3]
    db, n_pages = page_table.shape
    npg = PAGES_PER_STEP if n_pages % PAGES_PER_STEP == 0 else 1
    ck = cache_k.transpose(0, 1, 3, 4, 2).reshape(depth * n_phys, C_KV_DIM, page)
    cv = cache_v.transpose(0, 1, 3, 4, 2).reshape(depth * n_phys, C_KV_DIM, page)
    base = layer * n_phys
    nrow = qbd.shape[1]

    def page_spec(i):
        return pl.BlockSpec((1, C_KV_DIM, page), lambda b, s, pt: (base + pt[b, s * npg + i], 0, 0))

    per_b = lambda b, s, pt: (b, 0, 0)
    grid_spec = pltpu.PrefetchScalarGridSpec(
        num_scalar_prefetch=1, grid=(db, n_pages // npg),
        in_specs=[pl.BlockSpec((1, nrow, C_KV_DIM), per_b),
                  pl.BlockSpec((1, SROWS, npg * page), lambda b, s, pt: (b, 0, s)),
                  pl.BlockSpec((1, SROWS, LANES), per_b),
                  pl.BlockSpec((1, LANES, C_KV_DIM), per_b),
                  pl.BlockSpec((1, LANES, C_KV_DIM), per_b)]
                 + [page_spec(i) for i in range(npg)] * 2,
        out_specs=pl.BlockSpec((1, nrow, C_KV_DIM), per_b),
        scratch_shapes=[pltpu.VMEM((nrow, 1), F32), pltpu.VMEM((nrow, 1), F32),
                        pltpu.VMEM((nrow, C_KV_DIM), F32)])
    return pl.pallas_call(
        functools.partial(_satt_kernel, npg=npg, page=page), grid_spec=grid_spec,
        out_shape=jax.ShapeDtypeStruct((db, nrow, C_KV_DIM), F32),
        compiler_params=_cparams(2), name="sample_attn")(
            page_table, qbd, bias_past, bias_new, k_new, v_new, *([ck] * npg), *([cv] * npg))


def _merge_kernel(x_ref, g_ref, wg_ref, os_ref, bon_ref, gg_ref, lw_ref, lb_ref, bd_ref,
                  ob_ref, oc_ref, wpa_ref, wpb_ref, wpc_ref, wo_ref, o_ref):
    x = x_ref[...]
    d = x.shape[1]
    h = _rms(x, g_ref[...]).astype(BF16)
    gates = jax.nn.sigmoid(_dot(h, wg_ref[...]))
    bd = bd_ref[...]
    o = os_ref[...]
    mean = _segsum(o, bd) * (1.0 / HEAD_DIM)
    cen = o - mean
    var = _segsum(cen * cen, bd) * (1.0 / HEAD_DIM)
    on = cen * lax.rsqrt(var + GN_EPS) * lw_ref[...] + lb_ref[...]
    oa = ((on + bon_ref[...]) * gg_ref[...]).astype(BF16)
    merged = (gates[:, :d] * _dot(oa, wpa_ref[...])
              + gates[:, d:2 * d] * _dot(ob_ref[...].astype(BF16), wpb_ref[...])
              + gates[:, 2 * d:] * _dot(oc_ref[...].astype(BF16), wpc_ref[...]))
    o_ref[...] = x + _dot(merged.astype(BF16), wo_ref[...])


def _merge(x, o_scan, bonus, g, o_b, o_c, lw):
    m, d = x.shape
    tm = _tile(m, 512)
    row = lambda i: (i, 0)
    tok = lambda w: pl.BlockSpec((tm, w), row)
    consts_a = [lw["mix_norm"], lw["w_g"]]
    consts_b = [lw["rwkv_lnx_w"], lw["rwkv_lnx_b"], lw["bd"]]
    consts_c = [lw["w_pa"], lw["w_pb"], lw["w_pc"], lw["w_out"]]
    in_specs = ([tok(d)] + [_const_spec(c.shape) for c in consts_a] + [tok(A_DIM)] * 3
                + [_const_spec(c.shape) for c in consts_b] + [tok(B_DIM), tok(C_DIM)]
                + [_const_spec(c.shape) for c in consts_c])
    return pl.pallas_call(
        _merge_kernel, grid=(m // tm,), in_specs=in_specs, out_specs=tok(d),
        out_shape=jax.ShapeDtypeStruct((m, d), F32), compiler_params=_cparams(1), name="merge")(
            x, *consts_a, o_scan, bonus, g, *consts_b, o_b, o_c, *consts_c)


def _rope_tables(pos):
    inv = ROPE_THETA ** (-jnp.arange(ROT_HALF, dtype=F32) / ROT_HALF)
    ang = pos.astype(F32)[:, None] * inv[None, :]
    c, s = jnp.cos(ang), jnp.sin(ang)
    t = pos.shape[0]
    pad = jnp.zeros((t, HEAD_DIM - ROT_DIM), F32)
    zer = jnp.zeros((t, ROT_HALF), F32)
    cos = jnp.concatenate([c, c, pad + 1.0], axis=1)
    sa = jnp.concatenate([-s, zer, pad], axis=1)
    sb = jnp.concatenate([zer, s, pad], axis=1)
    return tuple(jnp.tile(z, (1, C_HEADS)) for z in (cos, sa, sb))


def _layer_weights(i, p):
    d = p["w_in"].shape[1]
    w_in = p["w_in"][i]
    o1 = RWKV_COLS
    o2 = o1 + CONV_COLS
    o3 = o2 + ATTN_COLS
    row = lambda v: v.reshape(1, -1)
    tile_h = lambda v, n: jnp.tile(v, n).reshape(1, -1)
    z = jnp.zeros((LORA_W, A_DIM), F32)
    w_wa = jnp.concatenate([jnp.concatenate([p["rwkv_w2"][i], z], axis=1),
                            jnp.concatenate([z, p["rwkv_a2"][i]], axis=1)], axis=0)
    head = np.arange(A_DIM) // HEAD_DIM
    bd = jnp.asarray(head[:, None] == head[None, :], BF16)
    idx_norm = jnp.concatenate([p["idx_k_norm"][i], jnp.zeros((LANES - IDX_DIM,), F32)])
    bf = lambda w: w.astype(BF16)
    return dict(
        ffn1_norm=p["ffn1_norm"][i], ffn1_wg=bf(p["ffn1_wg"][i]), ffn1_wu=bf(p["ffn1_wu"][i]),
        ffn1_wd=bf(p["ffn1_wd"][i]),
        ffn2_norm=p["ffn2_norm"][i], ffn2_wg=bf(p["ffn2_wg"][i]), ffn2_wu=bf(p["ffn2_wu"][i]),
        ffn2_wd=bf(p["ffn2_wd"][i]),
        mix_norm=row(p["mix_norm"][i]),
        w_a=bf(w_in[:, :o1]), w_b=bf(w_in[:, o1:o2]),
        w_c=bf(jnp.pad(w_in[:, o2:o3], ((0, 0), (0, ATTN_COLS_PAD - ATTN_COLS)))),
        w_g=bf(w_in[:, o3:]),
        rwkv_mu=row(p["rwkv_mu"][i]), rwkv_w0=row(p["rwkv_w0"][i]), rwkv_a0=row(p["rwkv_a0"][i]),
        w_wa=bf(w_wa), rwkv_g2=bf(p["rwkv_g2"][i]), rwkv_kk=row(p["rwkv_kk"][i]),
        rwkv_ka=row(p["rwkv_ka"][i]), rwkv_rk=row(p["rwkv_rk"][i]),
        rwkv_lnx_w=row(p["rwkv_lnx_w"][i]), rwkv_lnx_b=row(p["rwkv_lnx_b"][i]),
        conv_w=p["conv_w"][i], bd=bd,
        q_norm=tile_h(p["q_norm"][i], C_HEADS), k_norm=tile_h(p["k_norm"][i], C_KV_HEADS),
        idx_k_norm=row(idx_norm),
        w_pa=bf(p["w_pa"][i]), w_pb=bf(p["w_pb"][i]), w_pc=bf(p["w_pc"][i]), w_out=bf(p["w_out"][i]),
        ple_norm=p["ple_norm"][i], ple_gate=bf(p["ple_gate"][i]), ple_proj=bf(p["ple_proj"][i]),
    )


def _to_scan(x, nb, seq):
    return x.reshape(nb, seq, A_HEADS, HEAD_DIM).transpose(1, 3, 0, 2).reshape(seq, HEAD_DIM, nb * A_HEADS)


def _from_scan(o, nb, seq):
    return o.reshape(seq, HEAD_DIM, nb, A_HEADS).transpose(2, 0, 3, 1).reshape(nb * seq, A_DIM)


def _rwkv_branch(x, nb, seq, lw, wkv0, shift_rows):
    r, w, k, v, na, nb_, g, bonus, shift_o = _rwkv_pre(x, nb, seq, lw, shift_rows)
    s0 = wkv0.transpose(3, 2, 0, 1).reshape(HEAD_DIM, HEAD_DIM, nb * A_HEADS)
    ts = lambda z: _to_scan(z, nb, seq)
    o, s_fin = _scan(ts(r), ts(w), ts(k), ts(v), ts(na), ts(nb_), s0)
    wkv1 = s_fin.reshape(HEAD_DIM, HEAD_DIM, nb, A_HEADS).transpose(2, 3, 1, 0)
    if shift_rows is None:
        shift1 = shift_o.reshape(nb, RWKV_COLS)
    else:
        shift1 = shift_o.reshape(nb, seq, RWKV_COLS)[:, -1]
    return _from_scan(o, nb, seq), bonus, g, shift1, wkv1


def _layer_tail(x1, o_scan, bonus, g, o_b, o_c, p_emb, lw):
    x2 = _merge(x1, o_scan, bonus, g, o_b, o_c, lw)
    return _ffn(x2, lw["ffn2_norm"], lw["ffn2_wg"], lw["ffn2_wu"], lw["ffn2_wd"],
                ple=(p_emb, lw["ple_norm"], lw["ple_gate"], lw["ple_proj"]))


def _prompt_layer(x, p_emb, nb, seq, lw, tabs):
    x1 = _ffn(x, lw["ffn1_norm"], lw["ffn1_wg"], lw["ffn1_wu"], lw["ffn1_wd"])
    wkv0 = jnp.zeros((nb, A_HEADS, HEAD_DIM, HEAD_DIM), F32)
    o_scan, bonus, g, shift1, wkv1 = _rwkv_branch(x1, nb, seq, lw, wkv0, None)
    o_b, conv1 = _conv(x1, nb, seq, lw)
    q, k, qi, kw, kt, vt, kit = _attn_proj(x1, nb, seq, lw, tabs, False)
    o_c = _prompt_attention(q, k, vt, qi, kw, nb, seq)
    x4 = _layer_tail(x1, o_scan, bonus, g, o_b, o_c, p_emb, lw)
    heads = lambda z: z.reshape(nb, C_KV_HEADS, HEAD_DIM, seq).transpose(0, 3, 1, 2)
    st = (heads(kt), heads(vt), kit.transpose(0, 2, 1), wkv1, shift1, conv1)
    return x4, st


def _sample_layer(x, p_emb, nb, seq, lw, tabs, layer, cache_k, cache_v, cache_kidx, page_table,
                  wkv0, shift0, conv0):
    assert seq <= SROWS
    x1 = _ffn(x, lw["ffn1_norm"], lw["ffn1_wg"], lw["ffn1_wu"], lw["ffn1_wd"])
    rep_rows = lambda z: jnp.repeat(z, seq, axis=0)
    o_scan, bonus, g, shift1, wkv1 = _rwkv_branch(x1, nb, seq, lw, wkv0, rep_rows(shift0))
    o_b, z_all = _conv(x1, nb, seq, lw, init=(rep_rows(conv0[:, 0]), rep_rows(conv0[:, 1])))
    conv1 = z_all.reshape(nb, seq, B_DIM)[:, seq - (CONV_W - 1):]
    q, k, qi, kw, v, ki = _attn_proj(x1, nb, seq, lw, tabs, True)

    pad_t = lambda z: jnp.pad(z, ((0, 0), (0, SROWS - seq)) + ((0, 0),) * (z.ndim - 2))
    qh = pad_t(qi.reshape(nb, seq, IDX_HEADS, IDX_DIM)).transpose(0, 2, 1, 3)
    qh = qh.reshape(nb, IDX_HEADS * SROWS, IDX_DIM)
    wi = kw[:, IDX_DIM:IDX_DIM + IDX_HEADS].reshape(nb, seq, IDX_HEADS)
    wh = pad_t(wi).transpose(0, 2, 1).reshape(nb, IDX_HEADS * SROWS, 1)
    pad_keys = lambda z: jnp.pad(z.reshape(nb, seq, -1), ((0, 0), (0, LANES - seq), (0, 0)))
    sc_past, sc_new = _sample_index_scores(cache_kidx, page_table, layer, qh, wh, pad_keys(ki))
    past = sc_past.shape[-1]
    b_past, b_new = _sample_select(sc_past.reshape(nb * SROWS, past), sc_new.reshape(nb * SROWS, LANES), seq)

    rep = C_HEADS // C_KV_HEADS
    qg = pad_t(q.reshape(nb, seq, C_KV_HEADS, rep, HEAD_DIM)).transpose(0, 2, 3, 1, 4)
    qg = qg * (HEAD_DIM ** -0.5)
    eye = jnp.eye(C_KV_HEADS, dtype=F32)
    qbd = jnp.einsum("bgrtd,gh->bgrthd", qg, eye).reshape(nb, C_HEADS * SROWS, C_KV_DIM)
    o = _sample_attend(cache_k, cache_v, page_table, layer, qbd,
                       b_past.reshape(nb, SROWS, past), b_new.reshape(nb, SROWS, LANES),
                       pad_keys(k), pad_keys(v))
    o = o.reshape(nb, C_KV_HEADS, rep, SROWS, C_KV_HEADS, HEAD_DIM)
    o = jnp.einsum("bgrthd,gh->bgrtd", o, eye)[:, :, :, :seq]
    o_c = o.transpose(0, 3, 1, 2, 4).reshape(nb * seq, C_DIM)

    x4 = _layer_tail(x1, o_scan, bonus, g, o_b, o_c, p_emb, lw)
    st = (k.reshape(nb, seq, C_KV_HEADS, HEAD_DIM), v.reshape(nb, seq, C_KV_HEADS, HEAD_DIM),
          ki.reshape(nb, seq, IDX_DIM), wkv1, shift1, conv1)
    return x4, st


def kernel(x_prompt, x_sample, cache_k, cache_v, cache_kidx, state_wkv, state_shift, state_conv, page_table, p_prompt, p_sample, ffn1_norm, ffn1_wg, ffn1_wu, ffn1_wd, mix_norm, w_in, rwkv_mu, rwkv_w0, rwkv_w2, rwkv_a0, rwkv_a2, rwkv_g2, rwkv_kk, rwkv_ka, rwkv_rk, rwkv_lnx_w, rwkv_lnx_b, conv_w, q_norm, k_norm, idx_k_norm, w_pa, w_pb, w_pc, w_out, ffn2_norm, ffn2_wg, ffn2_wu, ffn2_wd, ple_norm, ple_gate, ple_proj):
    params = dict(ffn1_norm=ffn1_norm, ffn1_wg=ffn1_wg, ffn1_wu=ffn1_wu, ffn1_wd=ffn1_wd,
                  mix_norm=mix_norm, w_in=w_in, rwkv_mu=rwkv_mu, rwkv_w0=rwkv_w0, rwkv_w2=rwkv_w2,
                  rwkv_a0=rwkv_a0, rwkv_a2=rwkv_a2, rwkv_g2=rwkv_g2, rwkv_kk=rwkv_kk, rwkv_ka=rwkv_ka,
                  rwkv_rk=rwkv_rk.reshape(rwkv_rk.shape[0], -1), rwkv_lnx_w=rwkv_lnx_w,
                  rwkv_lnx_b=rwkv_lnx_b, conv_w=conv_w, q_norm=q_norm, k_norm=k_norm,
                  idx_k_norm=idx_k_norm, w_pa=w_pa, w_pb=w_pb, w_pc=w_pc, w_out=w_out,
                  ffn2_norm=ffn2_norm, ffn2_wg=ffn2_wg, ffn2_wu=ffn2_wu, ffn2_wd=ffn2_wd,
                  ple_norm=ple_norm, ple_gate=ple_gate, ple_proj=ple_proj)
    nb, seq, d = x_prompt.shape
    db, dseq, _ = x_sample.shape
    depth = w_in.shape[0]
    past = page_table.shape[1] * cache_k.shape[2]
    tabs_p = _rope_tables(jnp.arange(seq, dtype=jnp.int32))
    tabs_s = tuple(jnp.tile(z, (db, 1)) for z in _rope_tables(past + jnp.arange(dseq, dtype=jnp.int32)))
    xp = x_prompt.reshape(nb * seq, d)
    xs = x_sample.reshape(db * dseq, d)
    outs_p, outs_s = [], []
    for i in range(depth):
        lw = _layer_weights(i, params)
        xp, st_p = _prompt_layer(xp, (p_prompt.reshape(depth * nb * seq, -1), i), nb, seq, lw, tabs_p)
        xs, st_s = _sample_layer(xs, (p_sample.reshape(depth * db * dseq, -1), i), db, dseq, lw, tabs_s, i,
                                 cache_k, cache_v, cache_kidx, page_table,
                                 state_wkv[i], state_shift[i], state_conv[i])
        outs_p.append(st_p)
        outs_s.append(st_s)
    k_p, v_p, kidx_p, wkv_p, shift_p, conv_p = [jnp.stack(z) for z in zip(*outs_p)]
    k_s, v_s, kidx_s, wkv_s, shift_s, conv_s = [jnp.stack(z) for z in zip(*outs_s)]
    return (xp.reshape(nb, seq, d), xs.reshape(db, dseq, d), k_p, v_p, kidx_p, wkv_p, shift_p, conv_p,
            k_s, v_s, kidx_s, wkv_s, shift_s, conv_s)
```

```python
import functools

import jax
import jax.numpy as jnp
import numpy as np
from jax import lax
from jax.experimental import pallas as pl
from jax.experimental.pallas import tpu as pltpu

F32 = jnp.float32
BF16 = jnp.bfloat16
I32 = jnp.int32

HEAD_DIM = 64
A_HEADS = 8
A_DIM = A_HEADS * HEAD_DIM
LORA_W = 64
LORA_A = 64
LORA_G = 128
B_DIM = 512
CONV_W = 3
C_HEADS = 8
C_KV_HEADS = 4
C_DIM = C_HEADS * HEAD_DIM
C_KV_DIM = C_KV_HEADS * HEAD_DIM
IDX_HEADS = 8
IDX_DIM = 64
TOPK_MAX = 256
ROT_DIM = HEAD_DIM // 4
ROT_HALF = ROT_DIM // 2
ROPE_THETA = 500000.0
N_BRANCH = 3
RMS_EPS = 1e-6
GN_EPS = 64e-5
RWKV_COLS = 3 * A_DIM + LORA_W + LORA_A + LORA_G
CONV_COLS = 3 * B_DIM
ATTN_COLS = C_DIM + 2 * C_KV_DIM + IDX_HEADS * IDX_DIM + IDX_DIM + IDX_HEADS
ATTN_COLS_PAD = 1664

LANES = 128
INT_MIN = -2 ** 31
NEG_BIG = -1e30
VMEM_LIMIT = 56 * 1024 * 1024
PAGES_PER_STEP = 32


def _pages_per_step(n_pages):
    npg = min(PAGES_PER_STEP, n_pages)
    while n_pages % npg:
        npg -= 1
    return npg


def _cparams(n_axes):
    return pltpu.CompilerParams(dimension_semantics=("arbitrary",) * n_axes,
                                vmem_limit_bytes=VMEM_LIMIT)


def _const_spec(shape):
    nd = len(shape)
    return pl.BlockSpec(shape, lambda *_: (0,) * nd, pipeline_mode=pl.Buffered(1))


def _tile(m, pref):
    t = min(m, pref)
    while m % t:
        t -= 8
    return t


def _rms(x, g):
    ms = jnp.mean(x * x, axis=-1, keepdims=True)
    return x * lax.rsqrt(ms + RMS_EPS) * g


def _dot(a, b):
    return jnp.dot(a, b, preferred_element_type=F32)


def _dot_nt(a, b):
    return lax.dot_general(a, b, (((1,), (1,)), ((), ())), preferred_element_type=F32)


def _segsum(x, bd):
    hi = x.astype(BF16)
    lo = (x - hi.astype(F32)).astype(BF16)
    return _dot(hi, bd) + _dot(lo, bd)


def _rope(x, cos, sa, sb):
    n = x.shape[-1]
    return x * cos + pltpu.roll(x, n - ROT_HALF, 1) * sa + pltpu.roll(x, ROT_HALF, 1) * sb


def _rows_before(x, k, fills, *, seg_len, first_chunk_rows=None):
    rows = x.shape[0]
    y = pltpu.roll(x, k, 0)
    ridx = lax.broadcasted_iota(I32, (rows, 1), 0)
    t = ridx if seg_len is None else ridx % seg_len
    for r in range(k):
        y = jnp.where(t == r, fills[k - 1 - r], y)
    return y


def _softplus(y):
    return jnp.maximum(y, 0.0) + jnp.log(1.0 + jnp.exp(-jnp.abs(y)))


def _ffn_kernel(*refs, chunks, ple):
    if ple:
        x_ref, g_ref, wg_ref, wu_ref, wd_ref, p_ref, pn_ref, pg_ref, pp_ref, o_ref = refs
    else:
        x_ref, g_ref, wg_ref, wu_ref, wd_ref, o_ref = refs
    x = x_ref[...]
    h = _rms(x, g_ref[...]).astype(BF16)
    acc = None
    for lo, hi in chunks:
        gt = _dot(h, wg_ref[:, lo:hi])
        ut = _dot(h, wu_ref[:, lo:hi])
        act = (gt * jax.nn.sigmoid(gt) * ut).astype(BF16)
        d = _dot(act, wd_ref[lo:hi, :])
        acc = d if acc is None else acc + d
    y = x + 0.5 * acc
    if ple:
        hg = _rms(y, pn_ref[...]).astype(BF16)
        gate = jax.nn.sigmoid(_dot(hg, pg_ref[...]))
        y = y + gate * _dot(p_ref[...].astype(BF16), pp_ref[...])
    o_ref[...] = y


def _ffn(x, g, wg, wu, wd, ple=None):
    m, d = x.shape
    f = wg.shape[1]
    tm = _tile(m, 512)
    step = 1024
    chunks = tuple((lo, min(lo + step, f)) for lo in range(0, f, step))
    row = lambda i: (i, 0)
    in_specs = [pl.BlockSpec((tm, d), row), _const_spec((1, d)), _const_spec((d, f)),
                _const_spec((d, f)), _const_spec((f, d))]
    args = [x, g.reshape(1, d), wg, wu, wd]
    if ple is not None:
        (p, layer), pn, pg, pp = ple
        off = layer * (m // tm)
        in_specs += [pl.BlockSpec((tm, p.shape[1]), lambda i: (i + off, 0)), _const_spec((1, d)),
                     _const_spec(pg.shape), _const_spec(pp.shape)]
        args += [p, pn.reshape(1, d), pg, pp]
    return pl.pallas_call(
        functools.partial(_ffn_kernel, chunks=chunks, ple=ple is not None),
        grid=(m // tm,), in_specs=in_specs, out_specs=pl.BlockSpec((tm, d), row),
        out_shape=jax.ShapeDtypeStruct((m, d), F32), compiler_params=_cparams(1),
        name="ffn_ple" if ple is not None else "ffn")(*args)


def _rwkv_pre_kernel(*refs, seg_len):
    (x_ref, g_ref, wa_ref, mu_ref, w0_ref, a0_ref, wwa_ref, g2_ref, kkw_ref, ka_ref,
     rk_ref, bd_ref) = refs[:12]
    rest = refs[12:]
    if seg_len is None:
        r_o, w_o, k_o, v_o, na_o, nb_o, g_o, bonus_o, shift_o, carry_ref = rest
    else:
        init_ref, r_o, w_o, k_o, v_o, na_o, nb_o, g_o, bonus_o, shift_o = rest
    x = x_ref[...]
    tm = x.shape[0]
    h = _rms(x, g_ref[...]).astype(BF16)
    u = _dot(h, wa_ref[...])
    if seg_len is None:
        @pl.when(pl.program_id(1) == 0)
        def _():
            carry_ref[...] = jnp.zeros_like(carry_ref)
        u_prev = _rows_before(u, 1, [carry_ref[0:1, :]], seg_len=None)
        carry_ref[0:1, :] = u[tm - 1:tm, :]
        shift_o[0] = u[tm - 1:tm, :]
    else:
        u_prev = _rows_before(u, 1, [init_ref[...]], seg_len=seg_len)
        shift_o[...] = u
    us = u + (u_prev - u) * mu_ref[...]
    r = us[:, 0:A_DIM]
    k = us[:, A_DIM:2 * A_DIM]
    v = us[:, 2 * A_DIM:3 * A_DIM]
    o3 = 3 * A_DIM
    xwa = us[:, o3:o3 + LORA_W + LORA_A]
    lane = lax.broadcasted_iota(I32, xwa.shape, 1)
    xwa = jnp.where(lane < LORA_W, jnp.tanh(xwa), xwa)
    lo = _dot(xwa.astype(BF16), wwa_ref[...])
    w_log = -_softplus(-(w0_ref[...] + lo[:, :A_DIM])) - 0.5
    decay = jnp.exp(-jnp.exp(w_log))
    a = jax.nn.sigmoid(a0_ref[...] + lo[:, A_DIM:])
    xg = us[:, o3 + LORA_W + LORA_A:]
    g = _dot(jax.nn.sigmoid(xg).astype(BF16), g2_ref[...])
    bd = bd_ref[...]
    kk = k * kkw_ref[...]
    kk = kk / jnp.maximum(jnp.sqrt(_segsum(kk * kk, bd)), 1e-12)
    k2 = k * (1.0 + (a - 1.0) * ka_ref[...])
    r_o[...] = r
    w_o[...] = decay
    k_o[...] = k2
    v_o[...] = v
    na_o[...] = -kk
    nb_o[...] = kk * a
    g_o[...] = g
    bonus_o[...] = _segsum(r * k2 * rk_ref[...], bd) * v


def _rwkv_pre(x, nb, seq, lw, init_rows=None):
    m, d = x.shape
    consts = [lw["mix_norm"], lw["w_a"], lw["rwkv_mu"], lw["rwkv_w0"], lw["rwkv_a0"], lw["w_wa"],
              lw["rwkv_g2"], lw["rwkv_kk"], lw["rwkv_ka"], lw["rwkv_rk"], lw["bd"]]
    const_specs = [_const_spec(c.shape) for c in consts]
    outs = [jax.ShapeDtypeStruct((m, A_DIM), F32)] * 8
    if init_rows is None:
        tm = _tile(seq, 512)
        nc = seq // tm
        row = lambda b, c: (b * nc + c, 0)
        grid = (nb, nc)
        in_specs = [pl.BlockSpec((tm, d), row)] + const_specs
        out_specs = ([pl.BlockSpec((tm, A_DIM), lambda b, c: (c, b))] * 6 + [pl.BlockSpec((tm, A_DIM), row)] * 2
                     + [pl.BlockSpec((1, 1, RWKV_COLS), lambda b, c: (b, 0, 0))])
        outs = ([jax.ShapeDtypeStruct((seq, nb * A_DIM), F32)] * 6 + outs[6:]
                + [jax.ShapeDtypeStruct((nb, 1, RWKV_COLS), F32)])
        scratch = [pltpu.VMEM((8, RWKV_COLS), F32)]
        args = [x] + consts
        seg_len = None
    else:
        tm = m
        row = lambda i: (0, 0)
        grid = (1,)
        in_specs = [pl.BlockSpec((tm, d), row)] + const_specs + [pl.BlockSpec((tm, RWKV_COLS), row)]
        out_specs = [pl.BlockSpec((tm, A_DIM), row)] * 8 + [pl.BlockSpec((tm, RWKV_COLS), row)]
        outs = outs + [jax.ShapeDtypeStruct((m, RWKV_COLS), F32)]
        scratch = []
        args = [x] + consts + [init_rows]
        seg_len = seq
    return pl.pallas_call(
        functools.partial(_rwkv_pre_kernel, seg_len=seg_len), grid=grid, in_specs=in_specs,
        out_specs=out_specs, out_shape=outs, scratch_shapes=scratch,
        compiler_params=_cparams(len(grid)), name="rwkv_pre")(*args)


def _scan_kernel(r_ref, w_ref, k_ref, v_ref, a_ref, b_ref, s0_ref, o_ref, st_ref, s_ref, *, tc):
    c = pl.program_id(1)

    @pl.when(c == 0)
    def _():
        s_ref[...] = s0_ref[...]

    def step(t, carry):
        sa = None
        for j in range(HEAD_DIM):
            term = s_ref[j] * a_ref[t, j:j + 1, :]
            sa = term if sa is None else sa + term
        vt = v_ref[t]
        o = None
        for j in range(HEAD_DIM):
            sn = (s_ref[j] * w_ref[t, j:j + 1, :] + sa * b_ref[t, j:j + 1, :]
                  + vt * k_ref[t, j:j + 1, :])
            s_ref[j] = sn
            term = sn * r_ref[t, j:j + 1, :]
            o = term if o is None else o + term
        o_ref[t] = o
        return carry

    lax.fori_loop(0, tc, step, 0)

    @pl.when(c == pl.num_programs(1) - 1)
    def _():
        st_ref[...] = s_ref[...]


def _scan(r, w, k, v, a, b, s0):
    t_len, hd, nbh = r.shape
    ln = min(LANES, nbh)
    tc = _tile(t_len, 32) if t_len % 8 == 0 else t_len
    grid = (nbh // ln, t_len // tc)
    seq_spec = pl.BlockSpec((tc, hd, ln), lambda l, c: (c, 0, l))
    st_spec = pl.BlockSpec((hd, hd, ln), lambda l, c: (0, 0, l))
    return pl.pallas_call(
        functools.partial(_scan_kernel, tc=tc), grid=grid,
        in_specs=[seq_spec] * 6 + [st_spec], out_specs=[seq_spec, st_spec],
        out_shape=[jax.ShapeDtypeStruct((t_len, hd, nbh), F32),
                   jax.ShapeDtypeStruct((hd, hd, nbh), F32)],
        scratch_shapes=[pltpu.VMEM((hd, hd, ln), F32)],
        compiler_params=_cparams(2), name="rwkv_scan")(r, w, k, v, a, b, s0)


def _conv_kernel(*refs, seg_len):
    x_ref, g_ref, wb_ref, cw_ref = refs[:4]
    rest = refs[4:]
    if seg_len is None:
        o_ref, st_ref, carry_ref = rest
    else:
        i0_ref, i1_ref, o_ref, st_ref = rest
    x = x_ref[...]
    tm = x.shape[0]
    h = _rms(x, g_ref[...]).astype(BF16)
    u = _dot(h, wb_ref[...])
    bg = u[:, :B_DIM]
    z = u[:, B_DIM:2 * B_DIM] * u[:, 2 * B_DIM:]
    if seg_len is None:
        @pl.when(pl.program_id(1) == 0)
        def _():
            carry_ref[...] = jnp.zeros_like(carry_ref)
        hist = [carry_ref[1:2, :], carry_ref[0:1, :]]
        z1 = _rows_before(z, 1, hist[:1], seg_len=None)
        z2 = _rows_before(z, 2, hist, seg_len=None)
        carry_ref[0:2, :] = z[tm - 2:tm, :]
        st_ref[0] = z[tm - 2:tm, :]
    else:
        hist = [i1_ref[...], i0_ref[...]]
        z1 = _rows_before(z, 1, hist[:1], seg_len=seg_len)
        z2 = _rows_before(z, 2, hist, seg_len=seg_len)
        st_ref[...] = z
    cw = cw_ref[...]
    y = z2 * cw[0:1, :] + z1 * cw[1:2, :] + z * cw[2:3, :]
    o_ref[...] = bg * y


def _conv(x, nb, seq, lw, init=None):
    m, d = x.shape
    assert seq >= CONV_W - 1
    consts = [lw["mix_norm"], lw["w_b"], lw["conv_w"]]
    const_specs = [_const_spec(c.shape) for c in consts]
    if init is None:
        tm = _tile(seq, 512)
        nc = seq // tm
        row = lambda b, c: (b * nc + c, 0)
        grid = (nb, nc)
        in_specs = [pl.BlockSpec((tm, d), row)] + const_specs
        out_specs = [pl.BlockSpec((tm, B_DIM), row),
                     pl.BlockSpec((1, CONV_W - 1, B_DIM), lambda b, c: (b, 0, 0))]
        outs = [jax.ShapeDtypeStruct((m, B_DIM), F32), jax.ShapeDtypeStruct((nb, CONV_W - 1, B_DIM), F32)]
        scratch = [pltpu.VMEM((8, B_DIM), F32)]
        args = [x] + consts
        seg_len = None
    else:
        tm = m
        row = lambda i: (0, 0)
        grid = (1,)
        in_specs = [pl.BlockSpec((tm, d), row)] + const_specs + [pl.BlockSpec((tm, B_DIM), row)] * 2
        out_specs = [pl.BlockSpec((tm, B_DIM), row)] * 2
        outs = [jax.ShapeDtypeStruct((m, B_DIM), F32)] * 2
        scratch = []
        args = [x] + consts + list(init)
        seg_len = seq
    return pl.pallas_call(
        functools.partial(_conv_kernel, seg_len=seg_len), grid=grid, in_specs=in_specs,
        out_specs=out_specs, out_shape=outs, scratch_shapes=scratch,
        compiler_params=_cparams(len(grid)), name="shortconv")(*args)


def _attn_proj_kernel(x_ref, g_ref, wc_ref, qn_ref, kn_ref, in_ref, bd_ref, cos_ref, sa_ref, sb_ref,
                      *outs, transposed):
    x = x_ref[...]
    h = _rms(x, g_ref[...]).astype(BF16)
    u = _dot(h, wc_ref[...])
    cos, sa, sb = cos_ref[...], sa_ref[...], sb_ref[...]
    bd = bd_ref[...]
    inv_hd = 1.0 / HEAD_DIM
    q = u[:, :C_DIM]
    q = q * lax.rsqrt(_segsum(q * q, bd) * inv_hd + RMS_EPS) * qn_ref[...]
    q = _rope(q, cos, sa, sb)
    k = u[:, C_DIM:C_DIM + C_KV_DIM]
    k = k * lax.rsqrt(_segsum(k * k, bd[:C_KV_DIM, :C_KV_DIM]) * inv_hd + RMS_EPS) * kn_ref[...]
    k = _rope(k, cos[:, :C_KV_DIM], sa[:, :C_KV_DIM], sb[:, :C_KV_DIM])
    o = C_DIM + C_KV_DIM
    v = u[:, o:o + C_KV_DIM]
    o += C_KV_DIM
    qi = _rope(u[:, o:o + IDX_HEADS * IDX_DIM], cos, sa, sb)
    o += IDX_HEADS * IDX_DIM
    kw = u[:, o:o + LANES]
    lane = lax.broadcasted_iota(I32, kw.shape, 1)
    is_ki = lane < IDX_DIM
    ms = jnp.sum(jnp.where(is_ki, kw * kw, 0.0), axis=-1, keepdims=True) * (1.0 / IDX_DIM)
    kin = _rope(kw * lax.rsqrt(ms + RMS_EPS) * in_ref[...], cos[:, :LANES], sa[:, :LANES], sb[:, :LANES])
    kw = jnp.where(is_ki, kin, kw)
    if transposed:
        q_o, k_o, qi_o, kw_o, kt_o, vt_o, kit_o = outs
        kt_o[0] = jnp.transpose(k)
        vt_o[0] = jnp.transpose(v)
        kit_o[0] = jnp.transpose(kw)[:IDX_DIM, :]
    else:
        q_o, k_o, qi_o, kw_o, v_o, ki_o = outs
        v_o[...] = v
        ki_o[...] = kw[:, :IDX_DIM]
    q_o[...] = q
    k_o[...] = k
    qi_o[...] = qi
    kw_o[...] = kw


def _attn_proj(x, nb, seq, lw, tabs, sample):
    m, d = x.shape
    consts = [lw["mix_norm"], lw["w_c"], lw["q_norm"], lw["k_norm"], lw["idx_k_norm"], lw["bd"]]
    const_specs = [_const_spec(c.shape) for c in consts]
    if sample:
        tm, nc = m, 1
        grid = (1, 1)
    else:
        tm = _tile(seq, 512)
        nc = seq // tm
        grid = (nb, nc)
    row = lambda b, c: (b * nc + c, 0)
    tab = lambda b, c: (c, 0)
    widths = [C_DIM, C_KV_DIM, IDX_HEADS * IDX_DIM, LANES]
    out_specs = [pl.BlockSpec((tm, w), row) for w in widths]
    out_shape = [jax.ShapeDtypeStruct((m, w), F32) for w in widths]
    if sample:
        for w in (C_KV_DIM, IDX_DIM):
            out_specs.append(pl.BlockSpec((tm, w), row))
            out_shape.append(jax.ShapeDtypeStruct((m, w), F32))
    else:
        for w in (C_KV_DIM, C_KV_DIM, IDX_DIM):
            out_specs.append(pl.BlockSpec((1, w, tm), lambda b, c: (b, 0, c)))
            out_shape.append(jax.ShapeDtypeStruct((nb, w, seq), F32))
    return pl.pallas_call(
        functools.partial(_attn_proj_kernel, transposed=not sample), grid=grid,
        in_specs=[pl.BlockSpec((tm, d), row)] + const_specs + [pl.BlockSpec((tm, C_DIM), tab)] * 3,
        out_specs=out_specs, out_shape=out_shape,
        compiler_params=_cparams(2), name="attn_proj")(x, *consts, *tabs)


def _score_keys(score):
    bits = pltpu.bitcast(score, I32)
    return jnp.where(bits < 0, bits ^ 0x7FFFFFFF, bits)


def _lane_total(acc):
    return jnp.broadcast_to(jnp.sum(acc, axis=-1, keepdims=True), acc.shape)


def _tree(parts, op):
    while len(parts) > 1:
        nxt = [op(parts[i], parts[i + 1]) for i in range(0, len(parts) - 1, 2)]
        parts = nxt + (parts[-1:] if len(parts) % 2 else [])
    return parts[0]


def _fold8(x, op):
    return _tree([x[i * 8:(i + 1) * 8] for i in range(x.shape[0] // 8)], op)


def _select_topk(count_fn, topk, pos_bits, shape, lp_ref):
    zeros = jnp.zeros(shape, I32)

    def vbit(it, acc):
        cand = acc | jnp.left_shift(jnp.int32(1), 31 - it)
        cmp = cand ^ INT_MIN
        cnt = count_fn(lambda kk, pos: jnp.where(kk >= cmp, 1, 0))
        return jnp.where(cnt >= topk, cand, acc)

    thr = lax.fori_loop(0, 32, vbit, zeros) ^ INT_MIN
    n_ge = count_fn(lambda kk, pos: jnp.where(kk >= thr, 1, 0))
    excess = jnp.where(thr == INT_MIN, 0, n_ge - topk)
    lp_ref[...] = jnp.full(shape, 2 ** 31 - 1, I32)

    @pl.when(jnp.max(excess) > 0)
    def _():
        need = topk - count_fn(lambda kk, pos: jnp.where(kk > thr, 1, 0))

        def pbit(it, acc):
            cand = acc | jnp.left_shift(jnp.int32(1), pos_bits - 1 - it)
            cnt = count_fn(lambda kk, pos: jnp.where(kk == thr, jnp.where(pos < cand, 1, 0), 0))
            return jnp.where(cnt < need, cand, acc)

        lp_ref[...] = lax.fori_loop(0, pos_bits, pbit, zeros)

    return thr, lp_ref[...]


def _select_bias(kk, pos, thr, last_pos):
    tie = jnp.where(pos <= last_pos, 0.0, NEG_BIG)
    bias = jnp.where(kk == thr, tie, jnp.where(kk > thr, 0.0, NEG_BIG))
    return jnp.where(kk == INT_MIN, NEG_BIG, bias)


def _pattn_kernel(qi_ref, kwq_ref, q_ref, kwk_ref, k_ref, vt_ref, o_ref,
                  keys_ref, s_ref, qip_ref, qg_ref, lp_ref, acc_ref, *, tq, topk, pos_bits):
    qb = pl.program_id(1)
    nch = qb + 1
    nslab = tq // 8
    shape8 = (8, tq)
    sub8 = lax.broadcasted_iota(I32, shape8, 0)
    lane_q = lax.broadcasted_iota(I32, (tq, LANES), 1)
    key_in = lax.broadcasted_iota(I32, (tq, tq), 0)
    qry_in = lax.broadcasted_iota(I32, (tq, tq), 1)

    wt = jnp.transpose(kwq_ref[...]) * ((IDX_DIM ** -0.5) * (IDX_HEADS ** -0.5))
    for hh in range(IDX_HEADS):
        slab = qi_ref[:, (hh // 2) * LANES:(hh // 2 + 1) * LANES]
        if hh % 2:
            slab = pltpu.roll(slab, IDX_DIM, 1)
        qip_ref[hh] = jnp.where(lane_q < IDX_DIM, slab, 0.0).astype(BF16)

    def idx_body(kc, carry):
        start = pl.multiple_of(kc * tq, tq)
        ks = kwk_ref[pl.ds(start, tq), :].astype(BF16)
        acc = None
        for hh in range(IDX_HEADS):
            s = _dot_nt(ks, qip_ref[hh])
            term = jnp.maximum(s, 0.0) * wt[IDX_DIM + hh:IDX_DIM + hh + 1, :]
            acc = term if acc is None else acc + term
        kk = jnp.where(acc == 0.0, 0, _score_keys(acc))
        keys_ref[kc] = jnp.where(kc * tq + key_in <= qb * tq + qry_in, kk, INT_MIN)
        return carry

    lax.fori_loop(0, nch, idx_body, 0)

    def count_fn(ind):
        def body(kc, acc):
            base = kc * tq
            parts = [ind(keys_ref[kc, i * 8:(i + 1) * 8, :], base + i * 8 + sub8) for i in range(nslab)]
            return acc + _tree(parts, jnp.add)
        acc = lax.fori_loop(0, nch, body, jnp.zeros(shape8, I32))
        return jnp.broadcast_to(jnp.sum(acc, axis=0, keepdims=True), shape8)

    thr, last_pos = _select_topk(count_fn, topk, pos_bits, shape8, lp_ref)

    scale = (HEAD_DIM ** -0.5) * 1.4426950408889634
    for g in range(C_KV_HEADS):
        qslab = q_ref[:, g * LANES:(g + 1) * LANES] * scale
        rolled = pltpu.roll(qslab, HEAD_DIM, 1)
        koff = g % 2
        in_half = (lane_q >= HEAD_DIM) if koff else (lane_q < HEAD_DIM)
        qg_ref[g] = jnp.concatenate(
            [jnp.where(in_half, qslab if r == koff else rolled, 0.0) for r in range(2)],
            axis=0).astype(BF16)

    def max_body(kc, carry):
        start = pl.multiple_of(kc * tq, tq)
        base = kc * tq
        bias = jnp.concatenate(
            [_select_bias(keys_ref[kc, i * 8:(i + 1) * 8, :], base + i * 8 + sub8, thr, last_pos)
             for i in range(nslab)], axis=0)
        bias2 = jnp.concatenate([bias, bias], axis=1)
        new = []
        for g in range(C_KV_HEADS):
            kcol = (g // 2) * LANES
            kch = k_ref[pl.ds(start, tq), kcol:kcol + LANES].astype(BF16)
            s = _dot_nt(kch, qg_ref[g]) + bias2
            s_ref[kc, :, g * 2 * tq:(g + 1) * 2 * tq] = s
            new.append(jnp.maximum(carry[g], _fold8(s, jnp.maximum)))
        return tuple(new)

    m8 = lax.fori_loop(0, nch, max_body,
                       tuple(jnp.full((8, 2 * tq), NEG_BIG, F32) for _ in range(C_KV_HEADS)))
    m_row = [jnp.max(m, axis=0, keepdims=True) for m in m8]

    acc_ref[...] = jnp.zeros_like(acc_ref)

    def sum_body(kc, carry):
        start = pl.multiple_of(kc * tq, tq)
        new = []
        for g in range(C_KV_HEADS):
            p = jnp.exp2(s_ref[kc, :, g * 2 * tq:(g + 1) * 2 * tq] - m_row[g])
            vt = vt_ref[0, g * HEAD_DIM:(g + 1) * HEAD_DIM, pl.ds(start, tq)].astype(BF16)
            acc_ref[g] += _dot(vt, p.astype(BF16))
            new.append(carry[g] + _fold8(p, jnp.add))
        return tuple(new)

    l8 = lax.fori_loop(0, nch, sum_body,
                       tuple(jnp.zeros((8, 2 * tq), F32) for _ in range(C_KV_HEADS)))
    for g in range(C_KV_HEADS):
        out_t = acc_ref[g] / jnp.sum(l8[g], axis=0, keepdims=True)
        o_ref[:, g * LANES:(g + 1) * LANES] = jnp.transpose(
            jnp.concatenate([out_t[:, :tq], out_t[:, tq:]], axis=0))


def _prompt_attention(q, k, vt, qi, kw, nb, seq):
    assert C_HEADS == 2 * C_KV_HEADS
    topk = min(TOPK_MAX, seq // 4)
    tq = _tile(seq, 256)
    assert tq % LANES == 0
    nq = seq // tq
    pos_bits = max(1, int(seq - 1).bit_length())
    qrow = lambda b, c: (b * nq + c, 0)
    krow = lambda b, c: (b, 0)
    return pl.pallas_call(
        functools.partial(_pattn_kernel, tq=tq, topk=topk, pos_bits=pos_bits),
        grid=(nb, nq),
        in_specs=[pl.BlockSpec((tq, IDX_HEADS * IDX_DIM), qrow), pl.BlockSpec((tq, LANES), qrow),
                  pl.BlockSpec((tq, C_DIM), qrow), pl.BlockSpec((seq, LANES), krow),
                  pl.BlockSpec((seq, C_KV_DIM), krow),
                  pl.BlockSpec((1, C_KV_DIM, seq), lambda b, c: (b, 0, 0))],
        out_specs=pl.BlockSpec((tq, C_DIM), qrow),
        out_shape=jax.ShapeDtypeStruct((nb * seq, C_DIM), F32),
        scratch_shapes=[pltpu.VMEM((nq, tq, tq), I32),
                        pltpu.VMEM((nq, tq, C_KV_HEADS * 2 * tq), F32),
                        pltpu.VMEM((IDX_HEADS, tq, LANES), BF16),
                        pltpu.VMEM((C_KV_HEADS, 2 * tq, LANES), BF16), pltpu.VMEM((8, tq), I32),
                        pltpu.VMEM((C_KV_HEADS, HEAD_DIM, 2 * tq), F32)],
        compiler_params=_cparams(2), name="prompt_attn")(qi, kw, q, kw, k, vt)


SROWS = 8


def _sidx_kernel(pt_ref, qh_ref, wh_ref, kn_ref, *rest, npg, page):
    pages = rest[:npg]
    past_o, new_o = rest[npg:]
    qh = qh_ref[0].astype(BF16)
    wh = wh_ref[0] * ((IDX_DIM ** -0.5) * (IDX_HEADS ** -0.5))

    def head_sum(s):
        s = jnp.maximum(s, 0.0) * wh
        return _tree([s[hh * SROWS:(hh + 1) * SROWS] for hh in range(IDX_HEADS)], jnp.add)

    keys_t = jnp.concatenate([pg[0] for pg in pages], axis=1).astype(BF16)
    past_o[0] = head_sum(_dot(qh, keys_t))

    @pl.when(pl.program_id(1) == 0)
    def _():
        new_o[0] = head_sum(_dot_nt(qh, kn_ref[0].astype(BF16)))


def _sample_index_scores(cache_kidx, page_table, layer, qh, wh, ki_new):
    depth, n_phys, page, _ = cache_kidx.shape
    db, n_pages = page_table.shape
    npg = _pages_per_step(n_pages)
    cache = cache_kidx.transpose(0, 1, 3, 2).reshape(depth * n_phys, IDX_DIM, page)
    base = layer * n_phys

    def page_spec(i):
        return pl.BlockSpec((1, IDX_DIM, page), lambda b, s, pt: (base + pt[b, s * npg + i], 0, 0))

    per_b = lambda b, s, pt: (b, 0, 0)
    grid_spec = pltpu.PrefetchScalarGridSpec(
        num_scalar_prefetch=1, grid=(db, n_pages // npg),
        in_specs=[pl.BlockSpec((1, IDX_HEADS * SROWS, IDX_DIM), per_b),
                  pl.BlockSpec((1, IDX_HEADS * SROWS, 1), per_b),
                  pl.BlockSpec((1, LANES, IDX_DIM), per_b)] + [page_spec(i) for i in range(npg)],
        out_specs=[pl.BlockSpec((1, SROWS, npg * page), lambda b, s, pt: (b, 0, s)),
                   pl.BlockSpec((1, SROWS, LANES), per_b)])
    return pl.pallas_call(
        functools.partial(_sidx_kernel, npg=npg, page=page), grid_spec=grid_spec,
        out_shape=[jax.ShapeDtypeStruct((db, SROWS, n_pages * page), F32),
                   jax.ShapeDtypeStruct((db, SROWS, LANES), F32)],
        compiler_params=_cparams(2), name="sample_index")(page_table, qh, wh, ki_new, *([cache] * npg))


def _ssel_kernel(past_ref, new_ref, bpast_o, bnew_o, keys_ref, lp_ref, *, topk, pos_bits, n_new):
    rows, past = past_ref.shape
    nch = past // LANES
    lane = lax.broadcasted_iota(I32, (rows, LANES), 1)
    t_row = lax.broadcasted_iota(I32, (rows, LANES), 0) % SROWS

    def fill(c, carry):
        start = pl.multiple_of(c * LANES, LANES)
        keys_ref[c] = _score_keys(past_ref[:, pl.ds(start, LANES)])
        return carry

    lax.fori_loop(0, nch, fill, 0)
    new_ok = lane <= jnp.minimum(t_row, n_new - 1)
    keys_ref[nch] = jnp.where(new_ok, _score_keys(new_ref[...]), INT_MIN)

    def count_fn(pred):
        def body(c, acc):
            return acc + pred(keys_ref[c], c * LANES + lane)
        return _lane_total(lax.fori_loop(0, nch + 1, body, jnp.zeros((rows, LANES), I32)))

    thr, last_pos = _select_topk(count_fn, topk, pos_bits, (rows, LANES), lp_ref)

    def emit(c, carry):
        start = pl.multiple_of(c * LANES, LANES)
        bpast_o[:, pl.ds(start, LANES)] = _select_bias(keys_ref[c], c * LANES + lane, thr, last_pos)
        return carry

    lax.fori_loop(0, nch, emit, 0)
    bnew_o[...] = _select_bias(keys_ref[nch], nch * LANES + lane, thr, last_pos)


def _sample_select(sc_past, sc_new, n_new):
    rows, past = sc_past.shape
    topk = min(TOPK_MAX, (past + n_new) // 4)
    tr = _tile(rows, 64)
    pos_bits = int(past + LANES - 1).bit_length()
    row = lambda i: (i, 0)
    return pl.pallas_call(
        functools.partial(_ssel_kernel, topk=topk, pos_bits=pos_bits, n_new=n_new),
        grid=(rows // tr,),
        in_specs=[pl.BlockSpec((tr, past), row), pl.BlockSpec((tr, LANES), row)],
        out_specs=[pl.BlockSpec((tr, past), row), pl.BlockSpec((tr, LANES), row)],
        out_shape=[jax.ShapeDtypeStruct((rows, past), F32), jax.ShapeDtypeStruct((rows, LANES), F32)],
        scratch_shapes=[pltpu.VMEM((past // LANES + 1, tr, LANES), I32), pltpu.VMEM((tr, LANES), I32)],
        compiler_params=_cparams(1), name="sample_select")(sc_past, sc_new)


def _satt_kernel(pt_ref, qbd_ref, bpast_ref, bnew_ref, kn_ref, vn_ref, *rest, npg, page):
    kpages = rest[:npg]
    vpages = rest[npg:2 * npg]
    o_ref, m_ref, l_ref, acc_ref = rest[2 * npg:]
    s_id = pl.program_id(1)
    nrow = qbd_ref.shape[1]
    reps = nrow // SROWS

    @pl.when(s_id == 0)
    def _():
        m_ref[...] = jnp.full_like(m_ref, NEG_BIG)
        l_ref[...] = jnp.zeros_like(l_ref)
        acc_ref[...] = jnp.zeros_like(acc_ref)

    qbd = qbd_ref[0].astype(BF16)

    def update(s, bias, pv):
        s = s + jnp.concatenate([bias] * reps, axis=0)
        m_old = m_ref[...]
        m_new = jnp.maximum(m_old, jnp.max(s, axis=-1, keepdims=True))
        p = jnp.exp(s - m_new)
        alpha = jnp.exp(m_old - m_new)
        l_ref[...] = alpha * l_ref[...] + jnp.sum(p, axis=-1, keepdims=True)
        acc_ref[...] = alpha * acc_ref[...] + pv(p.astype(BF16))
        m_ref[...] = m_new

    keys_t = jnp.concatenate([kp[0] for kp in kpages], axis=1).astype(BF16)
    vals_t = jnp.concatenate([vp[0] for vp in vpages], axis=1).astype(BF16)
    update(_dot(qbd, keys_t), bpast_ref[0], lambda p: _dot_nt(p, vals_t))

    @pl.when(s_id == pl.num_programs(1) - 1)
    def _():
        vn = vn_ref[0].astype(BF16)
        update(_dot_nt(qbd, kn_ref[0].astype(BF16)), bnew_ref[0], lambda p: _dot(p, vn))
        o_ref[0] = acc_ref[...] / l_ref[...]


def _sample_attend(cache_k, cache_v, page_table, layer, qbd, bias_past, bias_new, k_new, v_new):
    depth, n_phys, page = cache_k.shape[:3]
    db, n_pages = page_table.shape
    npg = _pages_per_step(n_pages)
    ck = cache_k.transpose(0, 1, 3, 4, 2).reshape(depth * n_phys, C_KV_DIM, page)
    cv = cache_v.transpose(0, 1, 3, 4, 2).reshape(depth * n_phys, C_KV_DIM, page)
    base = layer * n_phys
    nrow = qbd.shape[1]

    def page_spec(i):
        return pl.BlockSpec((1, C_KV_DIM, page), lambda b, s, pt: (base + pt[b, s * npg + i], 0, 0))

    per_b = lambda b, s, pt: (b, 0, 0)
    grid_spec = pltpu.PrefetchScalarGridSpec(
        num_scalar_prefetch=1, grid=(db, n_pages // npg),
        in_specs=[pl.BlockSpec((1, nrow, C_KV_DIM), per_b),
                  pl.BlockSpec((1, SROWS, npg * page), lambda b, s, pt: (b, 0, s)),
                  pl.BlockSpec((1, SROWS, LANES), per_b),
                  pl.BlockSpec((1, LANES, C_KV_DIM), per_b),
                  pl.BlockSpec((1, LANES, C_KV_DIM), per_b)]
                 + [page_spec(i) for i in range(npg)] * 2,
        out_specs=pl.BlockSpec((1, nrow, C_KV_DIM), per_b),
        scratch_shapes=[pltpu.VMEM((nrow, 1), F32), pltpu.VMEM((nrow, 1), F32),
                        pltpu.VMEM((nrow, C_KV_DIM), F32)])
    return pl.pallas_call(
        functools.partial(_satt_kernel, npg=npg, page=page), grid_spec=grid_spec,
        out_shape=jax.ShapeDtypeStruct((db, nrow, C_KV_DIM), F32),
        compiler_params=_cparams(2), name="sample_attn")(
            page_table, qbd, bias_past, bias_new, k_new, v_new, *([ck] * npg), *([cv] * npg))


def _merge_kernel(x_ref, g_ref, wg_ref, os_ref, bon_ref, gg_ref, lw_ref, lb_ref, bd_ref,
                  ob_ref, oc_ref, wpa_ref, wpb_ref, wpc_ref, wo_ref, o_ref):
    x = x_ref[...]
    d = x.shape[1]
    h = _rms(x, g_ref[...]).astype(BF16)
    gates = jax.nn.sigmoid(_dot(h, wg_ref[...]))
    bd = bd_ref[...]
    o = os_ref[...]
    mean = _segsum(o, bd) * (1.0 / HEAD_DIM)
    cen = o - mean
    var = _segsum(cen * cen, bd) * (1.0 / HEAD_DIM)
    on = cen * lax.rsqrt(var + GN_EPS) * lw_ref[...] + lb_ref[...]
    oa = ((on + bon_ref[...]) * gg_ref[...]).astype(BF16)
    merged = (gates[:, :d] * _dot(oa, wpa_ref[...])
              + gates[:, d:2 * d] * _dot(ob_ref[...].astype(BF16), wpb_ref[...])
              + gates[:, 2 * d:] * _dot(oc_ref[...].astype(BF16), wpc_ref[...]))
    o_ref[...] = x + _dot(merged.astype(BF16), wo_ref[...])


def _merge(x, o_scan, bonus, g, o_b, o_c, lw, nb, seq):
    m, d = x.shape
    tm = _tile(seq, 512)
    nc = seq // tm
    row = lambda b, c: (b * nc + c, 0)
    tok = lambda w: pl.BlockSpec((tm, w), row)
    consts_a = [lw["mix_norm"], lw["w_g"]]
    consts_b = [lw["rwkv_lnx_w"], lw["rwkv_lnx_b"], lw["bd"]]
    consts_c = [lw["w_pa"], lw["w_pb"], lw["w_pc"], lw["w_out"]]
    in_specs = ([tok(d)] + [_const_spec(c.shape) for c in consts_a]
                + [pl.BlockSpec((tm, A_DIM), lambda b, c: (c, b))] + [tok(A_DIM)] * 2
                + [_const_spec(c.shape) for c in consts_b] + [tok(B_DIM), tok(C_DIM)]
                + [_const_spec(c.shape) for c in consts_c])
    return pl.pallas_call(
        _merge_kernel, grid=(nb, nc), in_specs=in_specs, out_specs=tok(d),
        out_shape=jax.ShapeDtypeStruct((m, d), F32), compiler_params=_cparams(2), name="merge")(
            x, *consts_a, o_scan, bonus, g, *consts_b, o_b, o_c, *consts_c)


def _rope_tables(pos):
    inv = ROPE_THETA ** (-jnp.arange(ROT_HALF, dtype=F32) / ROT_HALF)
    ang = pos.astype(F32)[:, None] * inv[None, :]
    c, s = jnp.cos(ang), jnp.sin(ang)
    t = pos.shape[0]
    pad = jnp.zeros((t, HEAD_DIM - ROT_DIM), F32)
    zer = jnp.zeros((t, ROT_HALF), F32)
    cos = jnp.concatenate([c, c, pad + 1.0], axis=1)
    sa = jnp.concatenate([-s, zer, pad], axis=1)
    sb = jnp.concatenate([zer, s, pad], axis=1)
    return tuple(jnp.tile(z, (1, C_HEADS)) for z in (cos, sa, sb))


def _layer_weights(i, p):
    d = p["w_in"].shape[1]
    w_in = p["w_in"][i]
    o1 = RWKV_COLS
    o2 = o1 + CONV_COLS
    o3 = o2 + ATTN_COLS
    row = lambda v: v.reshape(1, -1)
    tile_h = lambda v, n: jnp.tile(v, n).reshape(1, -1)
    z = jnp.zeros((LORA_W, A_DIM), F32)
    w_wa = jnp.concatenate([jnp.concatenate([p["rwkv_w2"][i], z], axis=1),
                            jnp.concatenate([z, p["rwkv_a2"][i]], axis=1)], axis=0)
    head = np.arange(A_DIM) // HEAD_DIM
    bd = jnp.asarray(head[:, None] == head[None, :], BF16)
    idx_norm = jnp.concatenate([p["idx_k_norm"][i], jnp.zeros((LANES - IDX_DIM,), F32)])
    bf = lambda w: w.astype(BF16)
    return dict(
        ffn1_norm=p["ffn1_norm"][i], ffn1_wg=bf(p["ffn1_wg"][i]), ffn1_wu=bf(p["ffn1_wu"][i]),
        ffn1_wd=bf(p["ffn1_wd"][i]),
        ffn2_norm=p["ffn2_norm"][i], ffn2_wg=bf(p["ffn2_wg"][i]), ffn2_wu=bf(p["ffn2_wu"][i]),
        ffn2_wd=bf(p["ffn2_wd"][i]),
        mix_norm=row(p["mix_norm"][i]),
        w_a=bf(w_in[:, :o1]), w_b=bf(w_in[:, o1:o2]),
        w_c=bf(jnp.pad(w_in[:, o2:o3], ((0, 0), (0, ATTN_COLS_PAD - ATTN_COLS)))),
        w_g=bf(w_in[:, o3:]),
        rwkv_mu=row(p["rwkv_mu"][i]), rwkv_w0=row(p["rwkv_w0"][i]), rwkv_a0=row(p["rwkv_a0"][i]),
        w_wa=bf(w_wa), rwkv_g2=bf(p["rwkv_g2"][i]), rwkv_kk=row(p["rwkv_kk"][i]),
        rwkv_ka=row(p["rwkv_ka"][i]), rwkv_rk=row(p["rwkv_rk"][i]),
        rwkv_lnx_w=row(p["rwkv_lnx_w"][i]), rwkv_lnx_b=row(p["rwkv_lnx_b"][i]),
        conv_w=p["conv_w"][i], bd=bd,
        q_norm=tile_h(p["q_norm"][i], C_HEADS), k_norm=tile_h(p["k_norm"][i], C_KV_HEADS),
        idx_k_norm=row(idx_norm),
        w_pa=bf(p["w_pa"][i]), w_pb=bf(p["w_pb"][i]), w_pc=bf(p["w_pc"][i]), w_out=bf(p["w_out"][i]),
        ple_norm=p["ple_norm"][i], ple_gate=bf(p["ple_gate"][i]), ple_proj=bf(p["ple_proj"][i]),
    )


def _rwkv_branch(x, nb, seq, lw, wkv0, shift_rows):
    r, w, k, v, na, nb_, g, bonus, shift_o = _rwkv_pre(x, nb, seq, lw, shift_rows)
    s0 = wkv0.transpose(3, 2, 0, 1).reshape(HEAD_DIM, HEAD_DIM, nb * A_HEADS)
    nbh = nb * A_HEADS
    if shift_rows is None:
        ts = lambda z: z.reshape(seq, nbh, HEAD_DIM).transpose(0, 2, 1)
    else:
        ts = lambda z: z.reshape(nb, seq, A_HEADS, HEAD_DIM).transpose(1, 3, 0, 2).reshape(seq, HEAD_DIM, nbh)
    o, s_fin = _scan(ts(r), ts(w), ts(k), ts(v), ts(na), ts(nb_), s0)
    wkv1 = s_fin.reshape(HEAD_DIM, HEAD_DIM, nb, A_HEADS).transpose(2, 3, 1, 0)
    if shift_rows is None:
        shift1 = shift_o.reshape(nb, RWKV_COLS)
        o_tok = o.transpose(0, 2, 1).reshape(seq, nb * A_DIM)
    else:
        shift1 = shift_o.reshape(nb, seq, RWKV_COLS)[:, -1]
        o_tok = o.reshape(seq, HEAD_DIM, nb, A_HEADS).transpose(2, 0, 3, 1).reshape(nb * seq, A_DIM)
    return o_tok, bonus, g, shift1, wkv1


def _layer_tail(x1, o_scan, bonus, g, o_b, o_c, p_emb, lw, nb, seq):
    x2 = _merge(x1, o_scan, bonus, g, o_b, o_c, lw, nb, seq)
    return _ffn(x2, lw["ffn2_norm"], lw["ffn2_wg"], lw["ffn2_wu"], lw["ffn2_wd"],
                ple=(p_emb, lw["ple_norm"], lw["ple_gate"], lw["ple_proj"]))


def _prompt_layer(x, p_emb, nb, seq, lw, tabs):
    x1 = _ffn(x, lw["ffn1_norm"], lw["ffn1_wg"], lw["ffn1_wu"], lw["ffn1_wd"])
    wkv0 = jnp.zeros((nb, A_HEADS, HEAD_DIM, HEAD_DIM), F32)
    o_scan, bonus, g, shift1, wkv1 = _rwkv_branch(x1, nb, seq, lw, wkv0, None)
    o_b, conv1 = _conv(x1, nb, seq, lw)
    q, k, qi, kw, kt, vt, kit = _attn_proj(x1, nb, seq, lw, tabs, False)
    o_c = _prompt_attention(q, k, vt, qi, kw, nb, seq)
    x4 = _layer_tail(x1, o_scan, bonus, g, o_b, o_c, p_emb, lw, nb, seq)
    heads = lambda z: z.reshape(nb, C_KV_HEADS, HEAD_DIM, seq).transpose(0, 3, 1, 2)
    st = (heads(kt), heads(vt), kit.transpose(0, 2, 1), wkv1, shift1, conv1)
    return x4, st


def _sample_layer(x, p_emb, nb, seq, lw, tabs, layer, cache_k, cache_v, cache_kidx, page_table,
                  wkv0, shift0, conv0):
    assert seq <= SROWS
    x1 = _ffn(x, lw["ffn1_norm"], lw["ffn1_wg"], lw["ffn1_wu"], lw["ffn1_wd"])
    rep_rows = lambda z: jnp.repeat(z, seq, axis=0)
    o_scan, bonus, g, shift1, wkv1 = _rwkv_branch(x1, nb, seq, lw, wkv0, rep_rows(shift0))
    o_b, z_all = _conv(x1, nb, seq, lw, init=(rep_rows(conv0[:, 0]), rep_rows(conv0[:, 1])))
    conv1 = z_all.reshape(nb, seq, B_DIM)[:, seq - (CONV_W - 1):]
    q, k, qi, kw, v, ki = _attn_proj(x1, nb, seq, lw, tabs, True)

    pad_t = lambda z: jnp.pad(z, ((0, 0), (0, SROWS - seq)) + ((0, 0),) * (z.ndim - 2))
    qh = pad_t(qi.reshape(nb, seq, IDX_HEADS, IDX_DIM)).transpose(0, 2, 1, 3)
    qh = qh.reshape(nb, IDX_HEADS * SROWS, IDX_DIM)
    wi = kw[:, IDX_DIM:IDX_DIM + IDX_HEADS].reshape(nb, seq, IDX_HEADS)
    wh = pad_t(wi).transpose(0, 2, 1).reshape(nb, IDX_HEADS * SROWS, 1)
    pad_keys = lambda z: jnp.pad(z.reshape(nb, seq, -1), ((0, 0), (0, LANES - seq), (0, 0)))
    sc_past, sc_new = _sample_index_scores(cache_kidx, page_table, layer, qh, wh, pad_keys(ki))
    past = sc_past.shape[-1]
    b_past, b_new = _sample_select(sc_past.reshape(nb * SROWS, past), sc_new.reshape(nb * SROWS, LANES), seq)

    rep = C_HEADS // C_KV_HEADS
    qg = pad_t(q.reshape(nb, seq, C_KV_HEADS, rep, HEAD_DIM)).transpose(0, 2, 3, 1, 4)
    qg = qg * (HEAD_DIM ** -0.5)
    eye = jnp.eye(C_KV_HEADS, dtype=F32)
    qbd = jnp.einsum("bgrtd,gh->bgrthd", qg, eye).reshape(nb, C_HEADS * SROWS, C_KV_DIM)
    o = _sample_attend(cache_k, cache_v, page_table, layer, qbd,
                       b_past.reshape(nb, SROWS, past), b_new.reshape(nb, SROWS, LANES),
                       pad_keys(k), pad_keys(v))
    o = o.reshape(nb, C_KV_HEADS, rep, SROWS, C_KV_HEADS, HEAD_DIM)
    o = jnp.einsum("bgrthd,gh->bgrtd", o, eye)[:, :, :, :seq]
    o_c = o.transpose(0, 3, 1, 2, 4).reshape(nb * seq, C_DIM)

    x4 = _layer_tail(x1, o_scan, bonus, g, o_b, o_c, p_emb, lw, 1, nb * seq)
    st = (k.reshape(nb, seq, C_KV_HEADS, HEAD_DIM), v.reshape(nb, seq, C_KV_HEADS, HEAD_DIM),
          ki.reshape(nb, seq, IDX_DIM), wkv1, shift1, conv1)
    return x4, st


def kernel(x_prompt, x_sample, cache_k, cache_v, cache_kidx, state_wkv, state_shift, state_conv, page_table, p_prompt, p_sample, ffn1_norm, ffn1_wg, ffn1_wu, ffn1_wd, mix_norm, w_in, rwkv_mu, rwkv_w0, rwkv_w2, rwkv_a0, rwkv_a2, rwkv_g2, rwkv_kk, rwkv_ka, rwkv_rk, rwkv_lnx_w, rwkv_lnx_b, conv_w, q_norm, k_norm, idx_k_norm, w_pa, w_pb, w_pc, w_out, ffn2_norm, ffn2_wg, ffn2_wu, ffn2_wd, ple_norm, ple_gate, ple_proj):
    params = dict(ffn1_norm=ffn1_norm, ffn1_wg=ffn1_wg, ffn1_wu=ffn1_wu, ffn1_wd=ffn1_wd,
                  mix_norm=mix_norm, w_in=w_in, rwkv_mu=rwkv_mu, rwkv_w0=rwkv_w0, rwkv_w2=rwkv_w2,
                  rwkv_a0=rwkv_a0, rwkv_a2=rwkv_a2, rwkv_g2=rwkv_g2, rwkv_kk=rwkv_kk, rwkv_ka=rwkv_ka,
                  rwkv_rk=rwkv_rk.reshape(rwkv_rk.shape[0], -1), rwkv_lnx_w=rwkv_lnx_w,
                  rwkv_lnx_b=rwkv_lnx_b, conv_w=conv_w, q_norm=q_norm, k_norm=k_norm,
                  idx_k_norm=idx_k_norm, w_pa=w_pa, w_pb=w_pb, w_pc=w_pc, w_out=w_out,
                  ffn2_norm=ffn2_norm, ffn2_wg=ffn2_wg, ffn2_wu=ffn2_wu, ffn2_wd=ffn2_wd,
                  ple_norm=ple_norm, ple_gate=ple_gate, ple_proj=ple_proj)
    nb, seq, d = x_prompt.shape
    db, dseq, _ = x_sample.shape
    depth = w_in.shape[0]
    past = page_table.shape[1] * cache_k.shape[2]
    tabs_p = _rope_tables(jnp.arange(seq, dtype=jnp.int32))
    tabs_s = tuple(jnp.tile(z, (db, 1)) for z in _rope_tables(past + jnp.arange(dseq, dtype=jnp.int32)))
    xp = x_prompt.reshape(nb * seq, d)
    xs = x_sample.reshape(db * dseq, d)
    outs_p, outs_s = [], []
    for i in range(depth):
        lw = _layer_weights(i, params)
        xp, st_p = _prompt_layer(xp, (p_prompt.reshape(depth * nb * seq, -1), i), nb, seq, lw, tabs_p)
        xs, st_s = _sample_layer(xs, (p_sample.reshape(depth * db * dseq, -1), i), db, dseq, lw, tabs_s, i,
                                 cache_k, cache_v, cache_kidx, page_table,
                                 state_wkv[i], state_shift[i], state_conv[i])
        outs_p.append(st_p)
        outs_s.append(st_s)
    k_p, v_p, kidx_p, wkv_p, shift_p, conv_p = [jnp.stack(z) for z in zip(*outs_p)]
    k_s, v_s, kidx_s, wkv_s, shift_s, conv_s = [jnp.stack(z) for z in zip(*outs_s)]
    return (xp.reshape(nb, seq, d), xs.reshape(db, dseq, d), k_p, v_p, kidx_p, wkv_p, shift_p, conv_p,
            k_s, v_s, kidx_s, wkv_s, shift_s, conv_s)
```

```python
import functools

import jax
import jax.numpy as jnp
import numpy as np
from jax import lax
from jax.experimental import pallas as pl
from jax.experimental.pallas import tpu as pltpu

F32 = jnp.float32
BF16 = jnp.bfloat16
I32 = jnp.int32

HEAD_DIM = 64
A_HEADS = 8
A_DIM = A_HEADS * HEAD_DIM
LORA_W = 64
LORA_A = 64
LORA_G = 128
B_DIM = 512
CONV_W = 3
C_HEADS = 8
C_KV_HEADS = 4
C_DIM = C_HEADS * HEAD_DIM
C_KV_DIM = C_KV_HEADS * HEAD_DIM
IDX_HEADS = 8
IDX_DIM = 64
TOPK_MAX = 256
ROT_DIM = HEAD_DIM // 4
ROT_HALF = ROT_DIM // 2
ROPE_THETA = 500000.0
N_BRANCH = 3
RMS_EPS = 1e-6
GN_EPS = 64e-5
RWKV_COLS = 3 * A_DIM + LORA_W + LORA_A + LORA_G
CONV_COLS = 3 * B_DIM
ATTN_COLS = C_DIM + 2 * C_KV_DIM + IDX_HEADS * IDX_DIM + IDX_DIM + IDX_HEADS
ATTN_COLS_PAD = 1664

LANES = 128
INT_MIN = -2 ** 31
NEG_BIG = -1e30
VMEM_LIMIT = 56 * 1024 * 1024
PAGES_PER_STEP = 32


def _pages_per_step(n_pages):
    npg = min(PAGES_PER_STEP, n_pages)
    while n_pages % npg:
        npg -= 1
    return npg


def _cparams(n_axes):
    return pltpu.CompilerParams(dimension_semantics=("arbitrary",) * n_axes,
                                vmem_limit_bytes=VMEM_LIMIT)


def _const_spec(shape):
    nd = len(shape)
    return pl.BlockSpec(shape, lambda *_: (0,) * nd, pipeline_mode=pl.Buffered(1))


def _tile(m, pref):
    t = min(m, pref)
    while m % t:
        t -= 8
    return t


def _rms(x, g):
    ms = jnp.mean(x * x, axis=-1, keepdims=True)
    return x * lax.rsqrt(ms + RMS_EPS) * g


def _dot(a, b):
    return jnp.dot(a, b, preferred_element_type=F32)


def _dot_nt(a, b):
    return lax.dot_general(a, b, (((1,), (1,)), ((), ())), preferred_element_type=F32)


def _segsum(x, bd):
    hi = x.astype(BF16)
    lo = (x - hi.astype(F32)).astype(BF16)
    return _dot(hi, bd) + _dot(lo, bd)


def _rope(x, cos, sa, sb):
    n = x.shape[-1]
    return x * cos + pltpu.roll(x, n - ROT_HALF, 1) * sa + pltpu.roll(x, ROT_HALF, 1) * sb


def _rows_before(x, k, fills, *, seg_len, first_chunk_rows=None):
    rows = x.shape[0]
    y = pltpu.roll(x, k, 0)
    ridx = lax.broadcasted_iota(I32, (rows, 1), 0)
    t = ridx if seg_len is None else ridx % seg_len
    for r in range(k):
        y = jnp.where(t == r, fills[k - 1 - r], y)
    return y


def _softplus(y):
    return jnp.maximum(y, 0.0) + jnp.log(1.0 + jnp.exp(-jnp.abs(y)))


def _ffn_kernel(*refs, chunks, ple):
    if ple:
        x_ref, g_ref, wg_ref, wu_ref, wd_ref, p_ref, pn_ref, pg_ref, pp_ref, o_ref = refs
    else:
        x_ref, g_ref, wg_ref, wu_ref, wd_ref, o_ref = refs
    x = x_ref[...]
    h = _rms(x, g_ref[...]).astype(BF16)
    acc = None
    for lo, hi in chunks:
        gt = _dot(h, wg_ref[:, lo:hi])
        ut = _dot(h, wu_ref[:, lo:hi])
        act = (gt * jax.nn.sigmoid(gt) * ut).astype(BF16)
        d = _dot(act, wd_ref[lo:hi, :])
        acc = d if acc is None else acc + d
    y = x + 0.5 * acc
    if ple:
        hg = _rms(y, pn_ref[...]).astype(BF16)
        gate = jax.nn.sigmoid(_dot(hg, pg_ref[...]))
        y = y + gate * _dot(p_ref[...].astype(BF16), pp_ref[...])
    o_ref[...] = y


def _ffn(x, g, wg, wu, wd, ple=None):
    m, d = x.shape
    f = wg.shape[1]
    tm = _tile(m, 512)
    step = 1024
    chunks = tuple((lo, min(lo + step, f)) for lo in range(0, f, step))
    row = lambda i: (i, 0)
    in_specs = [pl.BlockSpec((tm, d), row), _const_spec((1, d)), _const_spec((d, f)),
                _const_spec((d, f)), _const_spec((f, d))]
    args = [x, g.reshape(1, d), wg, wu, wd]
    if ple is not None:
        (p, layer), pn, pg, pp = ple
        off = layer * (m // tm)
        in_specs += [pl.BlockSpec((tm, p.shape[1]), lambda i: (i + off, 0)), _const_spec((1, d)),
                     _const_spec(pg.shape), _const_spec(pp.shape)]
        args += [p, pn.reshape(1, d), pg, pp]
    return pl.pallas_call(
        functools.partial(_ffn_kernel, chunks=chunks, ple=ple is not None),
        grid=(m // tm,), in_specs=in_specs, out_specs=pl.BlockSpec((tm, d), row),
        out_shape=jax.ShapeDtypeStruct((m, d), F32), compiler_params=_cparams(1),
        name="ffn_ple" if ple is not None else "ffn")(*args)


def _rwkv_pre_kernel(*refs, seg_len):
    (x_ref, g_ref, wa_ref, mu_ref, w0_ref, a0_ref, wwa_ref, g2_ref, kkw_ref, ka_ref,
     rk_ref, bd_ref) = refs[:12]
    rest = refs[12:]
    if seg_len is None:
        r_o, w_o, k_o, v_o, na_o, nb_o, g_o, bonus_o, shift_o, carry_ref = rest
    else:
        init_ref, r_o, w_o, k_o, v_o, na_o, nb_o, g_o, bonus_o, shift_o = rest
    x = x_ref[...]
    tm = x.shape[0]
    h = _rms(x, g_ref[...]).astype(BF16)
    u = _dot(h, wa_ref[...])
    if seg_len is None:
        @pl.when(pl.program_id(1) == 0)
        def _():
            carry_ref[...] = jnp.zeros_like(carry_ref)
        u_prev = _rows_before(u, 1, [carry_ref[0:1, :]], seg_len=None)
        carry_ref[0:1, :] = u[tm - 1:tm, :]
        shift_o[0] = u[tm - 1:tm, :]
    else:
        u_prev = _rows_before(u, 1, [init_ref[...]], seg_len=seg_len)
        shift_o[...] = u
    us = u + (u_prev - u) * mu_ref[...]
    r = us[:, 0:A_DIM]
    k = us[:, A_DIM:2 * A_DIM]
    v = us[:, 2 * A_DIM:3 * A_DIM]
    o3 = 3 * A_DIM
    xwa = us[:, o3:o3 + LORA_W + LORA_A]
    lane = lax.broadcasted_iota(I32, xwa.shape, 1)
    xwa = jnp.where(lane < LORA_W, jnp.tanh(xwa), xwa)
    lo = _dot(xwa.astype(BF16), wwa_ref[...])
    w_log = -_softplus(-(w0_ref[...] + lo[:, :A_DIM])) - 0.5
    decay = jnp.exp(-jnp.exp(w_log))
    a = jax.nn.sigmoid(a0_ref[...] + lo[:, A_DIM:])
    xg = us[:, o3 + LORA_W + LORA_A:]
    g = _dot(jax.nn.sigmoid(xg).astype(BF16), g2_ref[...])
    bd = bd_ref[...]
    kk = k * kkw_ref[...]
    kk = kk / jnp.maximum(jnp.sqrt(_segsum(kk * kk, bd)), 1e-12)
    k2 = k * (1.0 + (a - 1.0) * ka_ref[...])
    put = (lambda z: jnp.transpose(z)) if seg_len is None else (lambda z: z)
    r_o[...] = put(r)
    w_o[...] = put(decay)
    k_o[...] = put(k2)
    v_o[...] = put(v)
    na_o[...] = put(-kk)
    nb_o[...] = put(kk * a)
    g_o[...] = g
    bonus_o[...] = _segsum(r * k2 * rk_ref[...], bd) * v


def _rwkv_pre(x, nb, seq, lw, init_rows=None):
    m, d = x.shape
    consts = [lw["mix_norm"], lw["w_a"], lw["rwkv_mu"], lw["rwkv_w0"], lw["rwkv_a0"], lw["w_wa"],
              lw["rwkv_g2"], lw["rwkv_kk"], lw["rwkv_ka"], lw["rwkv_rk"], lw["bd"]]
    const_specs = [_const_spec(c.shape) for c in consts]
    outs = [jax.ShapeDtypeStruct((m, A_DIM), F32)] * 8
    if init_rows is None:
        tm = _tile(seq, 512)
        nc = seq // tm
        row = lambda b, c: (b * nc + c, 0)
        grid = (nb, nc)
        in_specs = [pl.BlockSpec((tm, d), row)] + const_specs
        out_specs = ([pl.BlockSpec((A_DIM, tm), lambda b, c: (b, c))] * 6 + [pl.BlockSpec((tm, A_DIM), row)] * 2
                     + [pl.BlockSpec((1, 1, RWKV_COLS), lambda b, c: (b, 0, 0))])
        outs = ([jax.ShapeDtypeStruct((nb * A_DIM, seq), F32)] * 6 + outs[6:]
                + [jax.ShapeDtypeStruct((nb, 1, RWKV_COLS), F32)])
        scratch = [pltpu.VMEM((8, RWKV_COLS), F32)]
        args = [x] + consts
        seg_len = None
    else:
        tm = m
        row = lambda i: (0, 0)
        grid = (1,)
        in_specs = [pl.BlockSpec((tm, d), row)] + const_specs + [pl.BlockSpec((tm, RWKV_COLS), row)]
        out_specs = [pl.BlockSpec((tm, A_DIM), row)] * 8 + [pl.BlockSpec((tm, RWKV_COLS), row)]
        outs = outs + [jax.ShapeDtypeStruct((m, RWKV_COLS), F32)]
        scratch = []
        args = [x] + consts + [init_rows]
        seg_len = seq
    return pl.pallas_call(
        functools.partial(_rwkv_pre_kernel, seg_len=seg_len), grid=grid, in_specs=in_specs,
        out_specs=out_specs, out_shape=outs, scratch_shapes=scratch,
        compiler_params=_cparams(len(grid)), name="rwkv_pre")(*args)


def _scan_kernel(r_ref, w_ref, k_ref, v_ref, a_ref, b_ref, s0_ref, o_ref, st_ref, s_ref, *, tc):
    c = pl.program_id(1)

    @pl.when(c == 0)
    def _():
        s_ref[...] = s0_ref[...]

    jb = 32
    ln = s_ref.shape[-1]

    def step(t, carry):
        def sa_body(q, sa):
            for jj in range(jb):
                j = q * jb + jj
                sa = sa + s_ref[j] * a_ref[t, pl.ds(j, 1), :]
            return sa

        sa = lax.fori_loop(0, HEAD_DIM // jb, sa_body, jnp.zeros((HEAD_DIM, ln), F32))
        vt = v_ref[t]

        def up_body(q, o):
            for jj in range(jb):
                j = q * jb + jj
                sn = (s_ref[j] * w_ref[t, pl.ds(j, 1), :] + sa * b_ref[t, pl.ds(j, 1), :]
                      + vt * k_ref[t, pl.ds(j, 1), :])
                s_ref[j] = sn
                o = o + sn * r_ref[t, pl.ds(j, 1), :]
            return o

        o_ref[t] = lax.fori_loop(0, HEAD_DIM // jb, up_body, jnp.zeros((HEAD_DIM, ln), F32))
        return carry

    lax.fori_loop(0, tc, step, 0)

    @pl.when(c == pl.num_programs(1) - 1)
    def _():
        st_ref[...] = s_ref[...]


def _scan(r, w, k, v, a, b, s0):
    t_len, hd, nbh = r.shape
    ln = min(LANES, nbh)
    tc = _tile(t_len, 32) if t_len % 8 == 0 else t_len
    grid = (nbh // ln, t_len // tc)
    seq_spec = pl.BlockSpec((tc, hd, ln), lambda l, c: (c, 0, l))
    st_spec = pl.BlockSpec((hd, hd, ln), lambda l, c: (0, 0, l))
    return pl.pallas_call(
        functools.partial(_scan_kernel, tc=tc), grid=grid,
        in_specs=[seq_spec] * 6 + [st_spec], out_specs=[seq_spec, st_spec],
        out_shape=[jax.ShapeDtypeStruct((t_len, hd, nbh), F32),
                   jax.ShapeDtypeStruct((hd, hd, nbh), F32)],
        scratch_shapes=[pltpu.VMEM((hd, hd, ln), F32)],
        compiler_params=_cparams(2), name="rwkv_scan")(r, w, k, v, a, b, s0)


def _conv_kernel(*refs, seg_len):
    x_ref, g_ref, wb_ref, cw_ref = refs[:4]
    rest = refs[4:]
    if seg_len is None:
        o_ref, st_ref, carry_ref = rest
    else:
        i0_ref, i1_ref, o_ref, st_ref = rest
    x = x_ref[...]
    tm = x.shape[0]
    h = _rms(x, g_ref[...]).astype(BF16)
    u = _dot(h, wb_ref[...])
    bg = u[:, :B_DIM]
    z = u[:, B_DIM:2 * B_DIM] * u[:, 2 * B_DIM:]
    if seg_len is None:
        @pl.when(pl.program_id(1) == 0)
        def _():
            carry_ref[...] = jnp.zeros_like(carry_ref)
        hist = [carry_ref[1:2, :], carry_ref[0:1, :]]
        z1 = _rows_before(z, 1, hist[:1], seg_len=None)
        z2 = _rows_before(z, 2, hist, seg_len=None)
        carry_ref[0:2, :] = z[tm - 2:tm, :]
        st_ref[0] = z[tm - 2:tm, :]
    else:
        hist = [i1_ref[...], i0_ref[...]]
        z1 = _rows_before(z, 1, hist[:1], seg_len=seg_len)
        z2 = _rows_before(z, 2, hist, seg_len=seg_len)
        st_ref[...] = z
    cw = cw_ref[...]
    y = z2 * cw[0:1, :] + z1 * cw[1:2, :] + z * cw[2:3, :]
    o_ref[...] = bg * y


def _conv(x, nb, seq, lw, init=None):
    m, d = x.shape
    assert seq >= CONV_W - 1
    consts = [lw["mix_norm"], lw["w_b"], lw["conv_w"]]
    const_specs = [_const_spec(c.shape) for c in consts]
    if init is None:
        tm = _tile(seq, 512)
        nc = seq // tm
        row = lambda b, c: (b * nc + c, 0)
        grid = (nb, nc)
        in_specs = [pl.BlockSpec((tm, d), row)] + const_specs
        out_specs = [pl.BlockSpec((tm, B_DIM), row),
                     pl.BlockSpec((1, CONV_W - 1, B_DIM), lambda b, c: (b, 0, 0))]
        outs = [jax.ShapeDtypeStruct((m, B_DIM), F32), jax.ShapeDtypeStruct((nb, CONV_W - 1, B_DIM), F32)]
        scratch = [pltpu.VMEM((8, B_DIM), F32)]
        args = [x] + consts
        seg_len = None
    else:
        tm = m
        row = lambda i: (0, 0)
        grid = (1,)
        in_specs = [pl.BlockSpec((tm, d), row)] + const_specs + [pl.BlockSpec((tm, B_DIM), row)] * 2
        out_specs = [pl.BlockSpec((tm, B_DIM), row)] * 2
        outs = [jax.ShapeDtypeStruct((m, B_DIM), F32)] * 2
        scratch = []
        args = [x] + consts + list(init)
        seg_len = seq
    return pl.pallas_call(
        functools.partial(_conv_kernel, seg_len=seg_len), grid=grid, in_specs=in_specs,
        out_specs=out_specs, out_shape=outs, scratch_shapes=scratch,
        compiler_params=_cparams(len(grid)), name="shortconv")(*args)


def _attn_proj_kernel(x_ref, g_ref, wc_ref, qn_ref, kn_ref, in_ref, bd_ref, cos_ref, sa_ref, sb_ref,
                      *outs, transposed):
    x = x_ref[...]
    h = _rms(x, g_ref[...]).astype(BF16)
    u = _dot(h, wc_ref[...])
    cos, sa, sb = cos_ref[...], sa_ref[...], sb_ref[...]
    bd = bd_ref[...]
    inv_hd = 1.0 / HEAD_DIM
    q = u[:, :C_DIM]
    q = q * lax.rsqrt(_segsum(q * q, bd) * inv_hd + RMS_EPS) * qn_ref[...]
    q = _rope(q, cos, sa, sb)
    k = u[:, C_DIM:C_DIM + C_KV_DIM]
    k = k * lax.rsqrt(_segsum(k * k, bd[:C_KV_DIM, :C_KV_DIM]) * inv_hd + RMS_EPS) * kn_ref[...]
    k = _rope(k, cos[:, :C_KV_DIM], sa[:, :C_KV_DIM], sb[:, :C_KV_DIM])
    o = C_DIM + C_KV_DIM
    v = u[:, o:o + C_KV_DIM]
    o += C_KV_DIM
    qi = _rope(u[:, o:o + IDX_HEADS * IDX_DIM], cos, sa, sb)
    o += IDX_HEADS * IDX_DIM
    kw = u[:, o:o + LANES]
    lane = lax.broadcasted_iota(I32, kw.shape, 1)
    is_ki = lane < IDX_DIM
    ms = jnp.sum(jnp.where(is_ki, kw * kw, 0.0), axis=-1, keepdims=True) * (1.0 / IDX_DIM)
    kin = _rope(kw * lax.rsqrt(ms + RMS_EPS) * in_ref[...], cos[:, :LANES], sa[:, :LANES], sb[:, :LANES])
    kw = jnp.where(is_ki, kin, kw)
    if transposed:
        q_o, k_o, qi_o, kw_o, kt_o, vt_o, kit_o = outs
        kt_o[0] = jnp.transpose(k)
        vt_o[0] = jnp.transpose(v)
        kit_o[0] = jnp.transpose(kw)[:IDX_DIM, :]
    else:
        q_o, k_o, qi_o, kw_o, v_o, ki_o = outs
        v_o[...] = v
        ki_o[...] = kw[:, :IDX_DIM]
    q_o[...] = q
    k_o[...] = k
    qi_o[...] = qi
    kw_o[...] = kw


def _attn_proj(x, nb, seq, lw, tabs, sample):
    m, d = x.shape
    consts = [lw["mix_norm"], lw["w_c"], lw["q_norm"], lw["k_norm"], lw["idx_k_norm"], lw["bd"]]
    const_specs = [_const_spec(c.shape) for c in consts]
    if sample:
        tm, nc = m, 1
        grid = (1, 1)
    else:
        tm = _tile(seq, 512)
        nc = seq // tm
        grid = (nb, nc)
    row = lambda b, c: (b * nc + c, 0)
    tab = lambda b, c: (c, 0)
    widths = [C_DIM, C_KV_DIM, IDX_HEADS * IDX_DIM, LANES]
    out_specs = [pl.BlockSpec((tm, w), row) for w in widths]
    out_shape = [jax.ShapeDtypeStruct((m, w), F32) for w in widths]
    if sample:
        for w in (C_KV_DIM, IDX_DIM):
            out_specs.append(pl.BlockSpec((tm, w), row))
            out_shape.append(jax.ShapeDtypeStruct((m, w), F32))
    else:
        for w in (C_KV_DIM, C_KV_DIM, IDX_DIM):
            out_specs.append(pl.BlockSpec((1, w, tm), lambda b, c: (b, 0, c)))
            out_shape.append(jax.ShapeDtypeStruct((nb, w, seq), F32))
    return pl.pallas_call(
        functools.partial(_attn_proj_kernel, transposed=not sample), grid=grid,
        in_specs=[pl.BlockSpec((tm, d), row)] + const_specs + [pl.BlockSpec((tm, C_DIM), tab)] * 3,
        out_specs=out_specs, out_shape=out_shape,
        compiler_params=_cparams(2), name="attn_proj")(x, *consts, *tabs)


def _score_keys(score):
    bits = pltpu.bitcast(score, I32)
    return jnp.where(bits < 0, bits ^ 0x7FFFFFFF, bits)


def _lane_total(acc):
    return jnp.broadcast_to(jnp.sum(acc, axis=-1, keepdims=True), acc.shape)


def _tree(parts, op):
    while len(parts) > 1:
        nxt = [op(parts[i], parts[i + 1]) for i in range(0, len(parts) - 1, 2)]
        parts = nxt + (parts[-1:] if len(parts) % 2 else [])
    return parts[0]


def _fold8(x, op):
    return _tree([x[i * 8:(i + 1) * 8] for i in range(x.shape[0] // 8)], op)


def _select_topk(count_fn, topk, pos_bits, shape, lp_ref):
    zeros = jnp.zeros(shape, I32)

    def vbit(it, acc):
        cand = acc | jnp.left_shift(jnp.int32(1), 31 - it)
        cmp = cand ^ INT_MIN
        cnt = count_fn(lambda kk, pos: jnp.where(kk >= cmp, 1, 0))
        return jnp.where(cnt >= topk, cand, acc)

    thr = lax.fori_loop(0, 32, vbit, zeros) ^ INT_MIN
    n_ge = count_fn(lambda kk, pos: jnp.where(kk >= thr, 1, 0))
    excess = jnp.where(thr == INT_MIN, 0, n_ge - topk)
    lp_ref[...] = jnp.full(shape, 2 ** 31 - 1, I32)

    @pl.when(jnp.max(excess) > 0)
    def _():
        need = topk - count_fn(lambda kk, pos: jnp.where(kk > thr, 1, 0))

        def pbit(it, acc):
            cand = acc | jnp.left_shift(jnp.int32(1), pos_bits - 1 - it)
            cnt = count_fn(lambda kk, pos: jnp.where(kk == thr, jnp.where(pos < cand, 1, 0), 0))
            return jnp.where(cnt < need, cand, acc)

        lp_ref[...] = lax.fori_loop(0, pos_bits, pbit, zeros)

    return thr, lp_ref[...]


def _select_bias(kk, pos, thr, last_pos):
    tie = jnp.where(pos <= last_pos, 0.0, NEG_BIG)
    bias = jnp.where(kk == thr, tie, jnp.where(kk > thr, 0.0, NEG_BIG))
    return jnp.where(kk == INT_MIN, NEG_BIG, bias)


def _pattn_kernel(qi_ref, kwq_ref, q_ref, kwk_ref, k_ref, vt_ref, o_ref,
                  keys_ref, s_ref, qip_ref, qg_ref, lp_ref, acc_ref, *, tq, topk, pos_bits):
    qb = pl.program_id(1)
    nch = qb + 1
    nslab = tq // 8
    shape8 = (8, tq)
    sub8 = lax.broadcasted_iota(I32, shape8, 0)
    lane_q = lax.broadcasted_iota(I32, (tq, LANES), 1)
    key_in = lax.broadcasted_iota(I32, (tq, tq), 0)
    qry_in = lax.broadcasted_iota(I32, (tq, tq), 1)

    wt = jnp.transpose(kwq_ref[...]) * ((IDX_DIM ** -0.5) * (IDX_HEADS ** -0.5))
    for hh in range(IDX_HEADS):
        slab = qi_ref[:, (hh // 2) * LANES:(hh // 2 + 1) * LANES]
        if hh % 2:
            slab = pltpu.roll(slab, IDX_DIM, 1)
        qip_ref[hh] = jnp.where(lane_q < IDX_DIM, slab, 0.0).astype(BF16)

    def idx_body(kc, carry):
        start = pl.multiple_of(kc * tq, tq)
        ks = kwk_ref[pl.ds(start, tq), :].astype(BF16)
        acc = None
        for hh in range(IDX_HEADS):
            s = _dot_nt(ks, qip_ref[hh])
            term = jnp.maximum(s, 0.0) * wt[IDX_DIM + hh:IDX_DIM + hh + 1, :]
            acc = term if acc is None else acc + term
        kk = jnp.where(acc == 0.0, 0, _score_keys(acc))
        keys_ref[kc] = jnp.where(kc * tq + key_in <= qb * tq + qry_in, kk, INT_MIN)
        return carry

    lax.fori_loop(0, nch, idx_body, 0)

    def count_fn(ind):
        def body(kc, acc):
            base = kc * tq
            parts = [ind(keys_ref[kc, i * 8:(i + 1) * 8, :], base + i * 8 + sub8) for i in range(nslab)]
            return acc + _tree(parts, jnp.add)
        acc = lax.fori_loop(0, nch, body, jnp.zeros(shape8, I32))
        return jnp.broadcast_to(jnp.sum(acc, axis=0, keepdims=True), shape8)

    thr, last_pos = _select_topk(count_fn, topk, pos_bits, shape8, lp_ref)

    scale = (HEAD_DIM ** -0.5) * 1.4426950408889634
    for g in range(C_KV_HEADS):
        qslab = q_ref[:, g * LANES:(g + 1) * LANES] * scale
        rolled = pltpu.roll(qslab, HEAD_DIM, 1)
        koff = g % 2
        in_half = (lane_q >= HEAD_DIM) if koff else (lane_q < HEAD_DIM)
        qg_ref[g] = jnp.concatenate(
            [jnp.where(in_half, qslab if r == koff else rolled, 0.0) for r in range(2)],
            axis=0).astype(BF16)

    def max_body(kc, carry):
        start = pl.multiple_of(kc * tq, tq)
        base = kc * tq
        bias = jnp.concatenate(
            [_select_bias(keys_ref[kc, i * 8:(i + 1) * 8, :], base + i * 8 + sub8, thr, last_pos)
             for i in range(nslab)], axis=0)
        bias2 = jnp.concatenate([bias, bias], axis=1)
        new = []
        for g in range(C_KV_HEADS):
            kcol = (g // 2) * LANES
            kch = k_ref[pl.ds(start, tq), kcol:kcol + LANES].astype(BF16)
            s = _dot_nt(kch, qg_ref[g]) + bias2
            s_ref[kc, :, g * 2 * tq:(g + 1) * 2 * tq] = s
            new.append(jnp.maximum(carry[g], _fold8(s, jnp.maximum)))
        return tuple(new)

    m8 = lax.fori_loop(0, nch, max_body,
                       tuple(jnp.full((8, 2 * tq), NEG_BIG, F32) for _ in range(C_KV_HEADS)))
    m_row = [jnp.max(m, axis=0, keepdims=True) for m in m8]

    acc_ref[...] = jnp.zeros_like(acc_ref)

    def sum_body(kc, carry):
        start = pl.multiple_of(kc * tq, tq)
        new = []
        for g in range(C_KV_HEADS):
            p = jnp.exp2(s_ref[kc, :, g * 2 * tq:(g + 1) * 2 * tq] - m_row[g])
            vt = vt_ref[0, g * HEAD_DIM:(g + 1) * HEAD_DIM, pl.ds(start, tq)].astype(BF16)
            acc_ref[g] += _dot(vt, p.astype(BF16))
            new.append(carry[g] + _fold8(p, jnp.add))
        return tuple(new)

    l8 = lax.fori_loop(0, nch, sum_body,
                       tuple(jnp.zeros((8, 2 * tq), F32) for _ in range(C_KV_HEADS)))
    for g in range(C_KV_HEADS):
        out_t = acc_ref[g] / jnp.sum(l8[g], axis=0, keepdims=True)
        o_ref[:, g * LANES:(g + 1) * LANES] = jnp.transpose(
            jnp.concatenate([out_t[:, :tq], out_t[:, tq:]], axis=0))


def _prompt_attention(q, k, vt, qi, kw, nb, seq):
    assert C_HEADS == 2 * C_KV_HEADS
    topk = min(TOPK_MAX, seq // 4)
    tq = _tile(seq, 256)
    assert tq % LANES == 0
    nq = seq // tq
    pos_bits = max(1, int(seq - 1).bit_length())
    qrow = lambda b, c: (b * nq + c, 0)
    krow = lambda b, c: (b, 0)
    return pl.pallas_call(
        functools.partial(_pattn_kernel, tq=tq, topk=topk, pos_bits=pos_bits),
        grid=(nb, nq),
        in_specs=[pl.BlockSpec((tq, IDX_HEADS * IDX_DIM), qrow), pl.BlockSpec((tq, LANES), qrow),
                  pl.BlockSpec((tq, C_DIM), qrow), pl.BlockSpec((seq, LANES), krow),
                  pl.BlockSpec((seq, C_KV_DIM), krow),
                  pl.BlockSpec((1, C_KV_DIM, seq), lambda b, c: (b, 0, 0))],
        out_specs=pl.BlockSpec((tq, C_DIM), qrow),
        out_shape=jax.ShapeDtypeStruct((nb * seq, C_DIM), F32),
        scratch_shapes=[pltpu.VMEM((nq, tq, tq), I32),
                        pltpu.VMEM((nq, tq, C_KV_HEADS * 2 * tq), F32),
                        pltpu.VMEM((IDX_HEADS, tq, LANES), BF16),
                        pltpu.VMEM((C_KV_HEADS, 2 * tq, LANES), BF16), pltpu.VMEM((8, tq), I32),
                        pltpu.VMEM((C_KV_HEADS, HEAD_DIM, 2 * tq), F32)],
        compiler_params=_cparams(2), name="prompt_attn")(qi, kw, q, kw, k, vt)


SROWS = 8


def _sidx_kernel(pt_ref, qh_ref, wh_ref, kn_ref, *rest, npg, page):
    pages = rest[:npg]
    past_o, new_o = rest[npg:]
    qh = qh_ref[0].astype(BF16)
    wh = wh_ref[0] * ((IDX_DIM ** -0.5) * (IDX_HEADS ** -0.5))

    def head_sum(s):
        s = jnp.maximum(s, 0.0) * wh
        return _tree([s[hh * SROWS:(hh + 1) * SROWS] for hh in range(IDX_HEADS)], jnp.add)

    keys_t = jnp.concatenate([pg[0] for pg in pages], axis=1).astype(BF16)
    past_o[0] = head_sum(_dot(qh, keys_t))

    @pl.when(pl.program_id(1) == 0)
    def _():
        new_o[0] = head_sum(_dot_nt(qh, kn_ref[0].astype(BF16)))


def _sample_index_scores(cache_kidx, page_table, layer, qh, wh, ki_new):
    depth, n_phys, page, _ = cache_kidx.shape
    db, n_pages = page_table.shape
    npg = _pages_per_step(n_pages)
    cache = cache_kidx.transpose(0, 1, 3, 2).reshape(depth * n_phys, IDX_DIM, page)
    base = layer * n_phys

    def page_spec(i):
        return pl.BlockSpec((1, IDX_DIM, page), lambda b, s, pt: (base + pt[b, s * npg + i], 0, 0))

    per_b = lambda b, s, pt: (b, 0, 0)
    grid_spec = pltpu.PrefetchScalarGridSpec(
        num_scalar_prefetch=1, grid=(db, n_pages // npg),
        in_specs=[pl.BlockSpec((1, IDX_HEADS * SROWS, IDX_DIM), per_b),
                  pl.BlockSpec((1, IDX_HEADS * SROWS, 1), per_b),
                  pl.BlockSpec((1, LANES, IDX_DIM), per_b)] + [page_spec(i) for i in range(npg)],
        out_specs=[pl.BlockSpec((1, SROWS, npg * page), lambda b, s, pt: (b, 0, s)),
                   pl.BlockSpec((1, SROWS, LANES), per_b)])
    return pl.pallas_call(
        functools.partial(_sidx_kernel, npg=npg, page=page), grid_spec=grid_spec,
        out_shape=[jax.ShapeDtypeStruct((db, SROWS, n_pages * page), F32),
                   jax.ShapeDtypeStruct((db, SROWS, LANES), F32)],
        compiler_params=_cparams(2), name="sample_index")(page_table, qh, wh, ki_new, *([cache] * npg))


def _ssel_kernel(past_ref, new_ref, bpast_o, bnew_o, keys_ref, lp_ref, *, topk, pos_bits, n_new):
    rows, past = past_ref.shape
    nch = past // LANES
    lane = lax.broadcasted_iota(I32, (rows, LANES), 1)
    t_row = lax.broadcasted_iota(I32, (rows, LANES), 0) % SROWS

    def fill(c, carry):
        start = pl.multiple_of(c * LANES, LANES)
        keys_ref[c] = _score_keys(past_ref[:, pl.ds(start, LANES)])
        return carry

    lax.fori_loop(0, nch, fill, 0)
    new_ok = lane <= jnp.minimum(t_row, n_new - 1)
    keys_ref[nch] = jnp.where(new_ok, _score_keys(new_ref[...]), INT_MIN)

    grp = 8
    assert nch % grp == 0

    def count_fn(pred):
        def body(q, acc):
            c0 = q * grp
            return acc + _tree([pred(keys_ref[c0 + i], (c0 + i) * LANES + lane) for i in range(grp)], jnp.add)
        acc = lax.fori_loop(0, nch // grp, body, jnp.zeros((rows, LANES), I32))
        return _lane_total(acc + pred(keys_ref[nch], nch * LANES + lane))

    thr, last_pos = _select_topk(count_fn, topk, pos_bits, (rows, LANES), lp_ref)

    def emit(c, carry):
        start = pl.multiple_of(c * LANES, LANES)
        bpast_o[:, pl.ds(start, LANES)] = _select_bias(keys_ref[c], c * LANES + lane, thr, last_pos)
        return carry

    lax.fori_loop(0, nch, emit, 0)
    bnew_o[...] = _select_bias(keys_ref[nch], nch * LANES + lane, thr, last_pos)


def _sample_select(sc_past, sc_new, n_new):
    rows, past = sc_past.shape
    topk = min(TOPK_MAX, (past + n_new) // 4)
    tr = _tile(rows, 64)
    pos_bits = int(past + LANES - 1).bit_length()
    row = lambda i: (i, 0)
    return pl.pallas_call(
        functools.partial(_ssel_kernel, topk=topk, pos_bits=pos_bits, n_new=n_new),
        grid=(rows // tr,),
        in_specs=[pl.BlockSpec((tr, past), row), pl.BlockSpec((tr, LANES), row)],
        out_specs=[pl.BlockSpec((tr, past), row), pl.BlockSpec((tr, LANES), row)],
        out_shape=[jax.ShapeDtypeStruct((rows, past), F32), jax.ShapeDtypeStruct((rows, LANES), F32)],
        scratch_shapes=[pltpu.VMEM((past // LANES + 1, tr, LANES), I32), pltpu.VMEM((tr, LANES), I32)],
        compiler_params=_cparams(1), name="sample_select")(sc_past, sc_new)


def _satt_kernel(pt_ref, qbd_ref, bpast_ref, bnew_ref, kn_ref, vn_ref, *rest, npg, page):
    kpages = rest[:npg]
    vpages = rest[npg:2 * npg]
    o_ref, m_ref, l_ref, acc_ref = rest[2 * npg:]
    s_id = pl.program_id(1)
    nrow = qbd_ref.shape[1]
    reps = nrow // SROWS

    @pl.when(s_id == 0)
    def _():
        m_ref[...] = jnp.full_like(m_ref, NEG_BIG)
        l_ref[...] = jnp.zeros_like(l_ref)
        acc_ref[...] = jnp.zeros_like(acc_ref)

    qbd = qbd_ref[0].astype(BF16)

    def update(s, bias, pv):
        s = s + jnp.concatenate([bias] * reps, axis=0)
        m_old = m_ref[...]
        m_new = jnp.maximum(m_old, jnp.max(s, axis=-1, keepdims=True))
        p = jnp.exp(s - m_new)
        alpha = jnp.exp(m_old - m_new)
        l_ref[...] = alpha * l_ref[...] + jnp.sum(p, axis=-1, keepdims=True)
        acc_ref[...] = alpha * acc_ref[...] + pv(p.astype(BF16))
        m_ref[...] = m_new

    keys_t = jnp.concatenate([kp[0] for kp in kpages], axis=1).astype(BF16)
    vals_t = jnp.concatenate([vp[0] for vp in vpages], axis=1).astype(BF16)
    update(_dot(qbd, keys_t), bpast_ref[0], lambda p: _dot_nt(p, vals_t))

    @pl.when(s_id == pl.num_programs(1) - 1)
    def _():
        vn = vn_ref[0].astype(BF16)
        update(_dot_nt(qbd, kn_ref[0].astype(BF16)), bnew_ref[0], lambda p: _dot(p, vn))
        o_ref[0] = acc_ref[...] / l_ref[...]


def _sample_attend(cache_k, cache_v, page_table, layer, qbd, bias_past, bias_new, k_new, v_new):
    depth, n_phys, page = cache_k.shape[:3]
    db, n_pages = page_table.shape
    npg = _pages_per_step(n_pages)
    ck = cache_k.transpose(0, 1, 3, 4, 2).reshape(depth * n_phys, C_KV_DIM, page)
    cv = cache_v.transpose(0, 1, 3, 4, 2).reshape(depth * n_phys, C_KV_DIM, page)
    base = layer * n_phys
    nrow = qbd.shape[1]

    def page_spec(i):
        return pl.BlockSpec((1, C_KV_DIM, page), lambda b, s, pt: (base + pt[b, s * npg + i], 0, 0))

    per_b = lambda b, s, pt: (b, 0, 0)
    grid_spec = pltpu.PrefetchScalarGridSpec(
        num_scalar_prefetch=1, grid=(db, n_pages // npg),
        in_specs=[pl.BlockSpec((1, nrow, C_KV_DIM), per_b),
                  pl.BlockSpec((1, SROWS, npg * page), lambda b, s, pt: (b, 0, s)),
                  pl.BlockSpec((1, SROWS, LANES), per_b),
                  pl.BlockSpec((1, LANES, C_KV_DIM), per_b),
                  pl.BlockSpec((1, LANES, C_KV_DIM), per_b)]
                 + [page_spec(i) for i in range(npg)] * 2,
        out_specs=pl.BlockSpec((1, nrow, C_KV_DIM), per_b),
        scratch_shapes=[pltpu.VMEM((nrow, 1), F32), pltpu.VMEM((nrow, 1), F32),
                        pltpu.VMEM((nrow, C_KV_DIM), F32)])
    return pl.pallas_call(
        functools.partial(_satt_kernel, npg=npg, page=page), grid_spec=grid_spec,
        out_shape=jax.ShapeDtypeStruct((db, nrow, C_KV_DIM), F32),
        compiler_params=_cparams(2), name="sample_attn")(
            page_table, qbd, bias_past, bias_new, k_new, v_new, *([ck] * npg), *([cv] * npg))


def _merge_kernel(x_ref, g_ref, wg_ref, os_ref, bon_ref, gg_ref, lw_ref, lb_ref, bd_ref,
                  ob_ref, oc_ref, wpa_ref, wpb_ref, wpc_ref, wo_ref, o_ref):
    x = x_ref[...]
    d = x.shape[1]
    h = _rms(x, g_ref[...]).astype(BF16)
    gates = jax.nn.sigmoid(_dot(h, wg_ref[...]))
    bd = bd_ref[...]
    o = os_ref[...]
    mean = _segsum(o, bd) * (1.0 / HEAD_DIM)
    cen = o - mean
    var = _segsum(cen * cen, bd) * (1.0 / HEAD_DIM)
    on = cen * lax.rsqrt(var + GN_EPS) * lw_ref[...] + lb_ref[...]
    oa = ((on + bon_ref[...]) * gg_ref[...]).astype(BF16)
    merged = (gates[:, :d] * _dot(oa, wpa_ref[...])
              + gates[:, d:2 * d] * _dot(ob_ref[...].astype(BF16), wpb_ref[...])
              + gates[:, 2 * d:] * _dot(oc_ref[...].astype(BF16), wpc_ref[...]))
    o_ref[...] = x + _dot(merged.astype(BF16), wo_ref[...])


def _merge(x, o_scan, bonus, g, o_b, o_c, lw, nb, seq):
    m, d = x.shape
    tm = _tile(seq, 512)
    nc = seq // tm
    row = lambda b, c: (b * nc + c, 0)
    tok = lambda w: pl.BlockSpec((tm, w), row)
    consts_a = [lw["mix_norm"], lw["w_g"]]
    consts_b = [lw["rwkv_lnx_w"], lw["rwkv_lnx_b"], lw["bd"]]
    consts_c = [lw["w_pa"], lw["w_pb"], lw["w_pc"], lw["w_out"]]
    in_specs = ([tok(d)] + [_const_spec(c.shape) for c in consts_a]
                + [pl.BlockSpec((tm, A_DIM), lambda b, c: (c, b))] + [tok(A_DIM)] * 2
                + [_const_spec(c.shape) for c in consts_b] + [tok(B_DIM), tok(C_DIM)]
                + [_const_spec(c.shape) for c in consts_c])
    return pl.pallas_call(
        _merge_kernel, grid=(nb, nc), in_specs=in_specs, out_specs=tok(d),
        out_shape=jax.ShapeDtypeStruct((m, d), F32), compiler_params=_cparams(2), name="merge")(
            x, *consts_a, o_scan, bonus, g, *consts_b, o_b, o_c, *consts_c)


def _rope_tables(pos):
    inv = ROPE_THETA ** (-jnp.arange(ROT_HALF, dtype=F32) / ROT_HALF)
    ang = pos.astype(F32)[:, None] * inv[None, :]
    c, s = jnp.cos(ang), jnp.sin(ang)
    t = pos.shape[0]
    pad = jnp.zeros((t, HEAD_DIM - ROT_DIM), F32)
    zer = jnp.zeros((t, ROT_HALF), F32)
    cos = jnp.concatenate([c, c, pad + 1.0], axis=1)
    sa = jnp.concatenate([-s, zer, pad], axis=1)
    sb = jnp.concatenate([zer, s, pad], axis=1)
    return tuple(jnp.tile(z, (1, C_HEADS)) for z in (cos, sa, sb))


def _layer_weights(i, p):
    d = p["w_in"].shape[1]
    w_in = p["w_in"][i]
    o1 = RWKV_COLS
    o2 = o1 + CONV_COLS
    o3 = o2 + ATTN_COLS
    row = lambda v: v.reshape(1, -1)
    tile_h = lambda v, n: jnp.tile(v, n).reshape(1, -1)
    z = jnp.zeros((LORA_W, A_DIM), F32)
    w_wa = jnp.concatenate([jnp.concatenate([p["rwkv_w2"][i], z], axis=1),
                            jnp.concatenate([z, p["rwkv_a2"][i]], axis=1)], axis=0)
    head = np.arange(A_DIM) // HEAD_DIM
    bd = jnp.asarray(head[:, None] == head[None, :], BF16)
    idx_norm = jnp.concatenate([p["idx_k_norm"][i], jnp.zeros((LANES - IDX_DIM,), F32)])
    bf = lambda w: w.astype(BF16)
    return dict(
        ffn1_norm=p["ffn1_norm"][i], ffn1_wg=bf(p["ffn1_wg"][i]), ffn1_wu=bf(p["ffn1_wu"][i]),
        ffn1_wd=bf(p["ffn1_wd"][i]),
        ffn2_norm=p["ffn2_norm"][i], ffn2_wg=bf(p["ffn2_wg"][i]), ffn2_wu=bf(p["ffn2_wu"][i]),
        ffn2_wd=bf(p["ffn2_wd"][i]),
        mix_norm=row(p["mix_norm"][i]),
        w_a=bf(w_in[:, :o1]), w_b=bf(w_in[:, o1:o2]),
        w_c=bf(jnp.pad(w_in[:, o2:o3], ((0, 0), (0, ATTN_COLS_PAD - ATTN_COLS)))),
        w_g=bf(w_in[:, o3:]),
        rwkv_mu=row(p["rwkv_mu"][i]), rwkv_w0=row(p["rwkv_w0"][i]), rwkv_a0=row(p["rwkv_a0"][i]),
        w_wa=bf(w_wa), rwkv_g2=bf(p["rwkv_g2"][i]), rwkv_kk=row(p["rwkv_kk"][i]),
        rwkv_ka=row(p["rwkv_ka"][i]), rwkv_rk=row(p["rwkv_rk"][i]),
        rwkv_lnx_w=row(p["rwkv_lnx_w"][i]), rwkv_lnx_b=row(p["rwkv_lnx_b"][i]),
        conv_w=p["conv_w"][i], bd=bd,
        q_norm=tile_h(p["q_norm"][i], C_HEADS), k_norm=tile_h(p["k_norm"][i], C_KV_HEADS),
        idx_k_norm=row(idx_norm),
        w_pa=bf(p["w_pa"][i]), w_pb=bf(p["w_pb"][i]), w_pc=bf(p["w_pc"][i]), w_out=bf(p["w_out"][i]),
        ple_norm=p["ple_norm"][i], ple_gate=bf(p["ple_gate"][i]), ple_proj=bf(p["ple_proj"][i]),
    )


def _rwkv_branch(x, nb, seq, lw, wkv0, shift_rows, between=None):
    pre = _rwkv_pre(x, nb, seq, lw, shift_rows)
    s0 = wkv0.transpose(3, 2, 0, 1).reshape(HEAD_DIM, HEAD_DIM, nb * A_HEADS)
    other = None
    if between is not None:
        x, pre = lax.optimization_barrier((x, pre))
        other = between(x)
    r, w, k, v, na, nb_, g, bonus, shift_o = pre
    nbh = nb * A_HEADS
    if shift_rows is None:
        ts = lambda z: z.reshape(nbh, HEAD_DIM, seq).transpose(2, 1, 0)
    else:
        ts = lambda z: z.reshape(nb, seq, A_HEADS, HEAD_DIM).transpose(1, 3, 0, 2).reshape(seq, HEAD_DIM, nbh)
    seqs = [ts(z) for z in (r, w, k, v, na, nb_)]
    if between is not None:
        s0, other = lax.optimization_barrier((s0, other))
    o, s_fin = _scan(*seqs, s0)
    wkv1 = s_fin.reshape(HEAD_DIM, HEAD_DIM, nb, A_HEADS).transpose(2, 3, 1, 0)
    if shift_rows is None:
        shift1 = shift_o.reshape(nb, RWKV_COLS)
        o_tok = o.transpose(0, 2, 1).reshape(seq, nb * A_DIM)
    else:
        shift1 = shift_o.reshape(nb, seq, RWKV_COLS)[:, -1]
        o_tok = o.reshape(seq, HEAD_DIM, nb, A_HEADS).transpose(2, 0, 3, 1).reshape(nb * seq, A_DIM)
    return o_tok, bonus, g, shift1, wkv1, other


def _layer_tail(x1, o_scan, bonus, g, o_b, o_c, p_emb, lw, nb, seq):
    x2 = _merge(x1, o_scan, bonus, g, o_b, o_c, lw, nb, seq)
    return _ffn(x2, lw["ffn2_norm"], lw["ffn2_wg"], lw["ffn2_wu"], lw["ffn2_wd"],
                ple=(p_emb, lw["ple_norm"], lw["ple_gate"], lw["ple_proj"]))


def _prompt_layer(x, p_emb, nb, seq, lw, tabs):
    x1 = _ffn(x, lw["ffn1_norm"], lw["ffn1_wg"], lw["ffn1_wu"], lw["ffn1_wd"])
    wkv0 = jnp.zeros((nb, A_HEADS, HEAD_DIM, HEAD_DIM), F32)

    def other_mixers(xin):
        o_b, conv1 = _conv(xin, nb, seq, lw)
        q, k, qi, kw, kt, vt, kit = _attn_proj(xin, nb, seq, lw, tabs, False)
        return o_b, conv1, kt, vt, kit, _prompt_attention(q, k, vt, qi, kw, nb, seq)

    o_scan, bonus, g, shift1, wkv1, other = _rwkv_branch(x1, nb, seq, lw, wkv0, None, other_mixers)
    o_b, conv1, kt, vt, kit, o_c = other
    x4 = _layer_tail(x1, o_scan, bonus, g, o_b, o_c, p_emb, lw, nb, seq)
    heads = lambda z: z.reshape(nb, C_KV_HEADS, HEAD_DIM, seq).transpose(0, 3, 1, 2)
    st = (heads(kt), heads(vt), kit.transpose(0, 2, 1), wkv1, shift1, conv1)
    return x4, st


def _sample_layer(x, p_emb, nb, seq, lw, tabs, layer, cache_k, cache_v, cache_kidx, page_table,
                  wkv0, shift0, conv0):
    assert seq <= SROWS
    x1 = _ffn(x, lw["ffn1_norm"], lw["ffn1_wg"], lw["ffn1_wu"], lw["ffn1_wd"])
    rep_rows = lambda z: jnp.repeat(z, seq, axis=0)
    o_scan, bonus, g, shift1, wkv1, _ = _rwkv_branch(x1, nb, seq, lw, wkv0, rep_rows(shift0))
    o_b, z_all = _conv(x1, nb, seq, lw, init=(rep_rows(conv0[:, 0]), rep_rows(conv0[:, 1])))
    conv1 = z_all.reshape(nb, seq, B_DIM)[:, seq - (CONV_W - 1):]
    q, k, qi, kw, v, ki = _attn_proj(x1, nb, seq, lw, tabs, True)

    pad_t = lambda z: jnp.pad(z, ((0, 0), (0, SROWS - seq)) + ((0, 0),) * (z.ndim - 2))
    qh = pad_t(qi.reshape(nb, seq, IDX_HEADS, IDX_DIM)).transpose(0, 2, 1, 3)
    qh = qh.reshape(nb, IDX_HEADS * SROWS, IDX_DIM)
    wi = kw[:, IDX_DIM:IDX_DIM + IDX_HEADS].reshape(nb, seq, IDX_HEADS)
    wh = pad_t(wi).transpose(0, 2, 1).reshape(nb, IDX_HEADS * SROWS, 1)
    pad_keys = lambda z: jnp.pad(z.reshape(nb, seq, -1), ((0, 0), (0, LANES - seq), (0, 0)))
    sc_past, sc_new = _sample_index_scores(cache_kidx, page_table, layer, qh, wh, pad_keys(ki))
    past = sc_past.shape[-1]
    b_past, b_new = _sample_select(sc_past.reshape(nb * SROWS, past), sc_new.reshape(nb * SROWS, LANES), seq)

    rep = C_HEADS // C_KV_HEADS
    qg = pad_t(q.reshape(nb, seq, C_KV_HEADS, rep, HEAD_DIM)).transpose(0, 2, 3, 1, 4)
    qg = qg * (HEAD_DIM ** -0.5)
    eye = jnp.eye(C_KV_HEADS, dtype=F32)
    qbd = jnp.einsum("bgrtd,gh->bgrthd", qg, eye).reshape(nb, C_HEADS * SROWS, C_KV_DIM)
    o = _sample_attend(cache_k, cache_v, page_table, layer, qbd,
                       b_past.reshape(nb, SROWS, past), b_new.reshape(nb, SROWS, LANES),
                       pad_keys(k), pad_keys(v))
    o = o.reshape(nb, C_KV_HEADS, rep, SROWS, C_KV_HEADS, HEAD_DIM)
    o = jnp.einsum("bgrthd,gh->bgrtd", o, eye)[:, :, :, :seq]
    o_c = o.transpose(0, 3, 1, 2, 4).reshape(nb * seq, C_DIM)

    x4 = _layer_tail(x1, o_scan, bonus, g, o_b, o_c, p_emb, lw, 1, nb * seq)
    st = (k.reshape(nb, seq, C_KV_HEADS, HEAD_DIM), v.reshape(nb, seq, C_KV_HEADS, HEAD_DIM),
          ki.reshape(nb, seq, IDX_DIM), wkv1, shift1, conv1)
    return x4, st


def kernel(x_prompt, x_sample, cache_k, cache_v, cache_kidx, state_wkv, state_shift, state_conv, page_table, p_prompt, p_sample, ffn1_norm, ffn1_wg, ffn1_wu, ffn1_wd, mix_norm, w_in, rwkv_mu, rwkv_w0, rwkv_w2, rwkv_a0, rwkv_a2, rwkv_g2, rwkv_kk, rwkv_ka, rwkv_rk, rwkv_lnx_w, rwkv_lnx_b, conv_w, q_norm, k_norm, idx_k_norm, w_pa, w_pb, w_pc, w_out, ffn2_norm, ffn2_wg, ffn2_wu, ffn2_wd, ple_norm, ple_gate, ple_proj):
    params = dict(ffn1_norm=ffn1_norm, ffn1_wg=ffn1_wg, ffn1_wu=ffn1_wu, ffn1_wd=ffn1_wd,
                  mix_norm=mix_norm, w_in=w_in, rwkv_mu=rwkv_mu, rwkv_w0=rwkv_w0, rwkv_w2=rwkv_w2,
                  rwkv_a0=rwkv_a0, rwkv_a2=rwkv_a2, rwkv_g2=rwkv_g2, rwkv_kk=rwkv_kk, rwkv_ka=rwkv_ka,
                  rwkv_rk=rwkv_rk.reshape(rwkv_rk.shape[0], -1), rwkv_lnx_w=rwkv_lnx_w,
                  rwkv_lnx_b=rwkv_lnx_b, conv_w=conv_w, q_norm=q_norm, k_norm=k_norm,
                  idx_k_norm=idx_k_norm, w_pa=w_pa, w_pb=w_pb, w_pc=w_pc, w_out=w_out,
                  ffn2_norm=ffn2_norm, ffn2_wg=ffn2_wg, ffn2_wu=ffn2_wu, ffn2_wd=ffn2_wd,
                  ple_norm=ple_norm, ple_gate=ple_gate, ple_proj=ple_proj)
    nb, seq, d = x_prompt.shape
    db, dseq, _ = x_sample.shape
    depth = w_in.shape[0]
    past = page_table.shape[1] * cache_k.shape[2]
    tabs_p = _rope_tables(jnp.arange(seq, dtype=jnp.int32))
    tabs_s = tuple(jnp.tile(z, (db, 1)) for z in _rope_tables(past + jnp.arange(dseq, dtype=jnp.int32)))
    xp = x_prompt.reshape(nb * seq, d)
    xs = x_sample.reshape(db * dseq, d)
    outs_p, outs_s = [], []
    for i in range(depth):
        lw = _layer_weights(i, params)
        xp, st_p = _prompt_layer(xp, (p_prompt.reshape(depth * nb * seq, -1), i), nb, seq, lw, tabs_p)
        xs, st_s = _sample_layer(xs, (p_sample.reshape(depth * db * dseq, -1), i), db, dseq, lw, tabs_s, i,
                                 cache_k, cache_v, cache_kidx, page_table,
                                 state_wkv[i], state_shift[i], state_conv[i])
        outs_p.append(st_p)
        outs_s.append(st_s)
    k_p, v_p, kidx_p, wkv_p, shift_p, conv_p = [jnp.stack(z) for z in zip(*outs_p)]
    k_s, v_s, kidx_s, wkv_s, shift_s, conv_s = [jnp.stack(z) for z in zip(*outs_s)]
    return (xp.reshape(nb, seq, d), xs.reshape(db, dseq, d), k_p, v_p, kidx_p, wkv_p, shift_p, conv_p,
            k_s, v_s, kidx_s, wkv_s, shift_s, conv_s)
```

```python
import functools

import jax
import jax.numpy as jnp
import numpy as np
from jax import lax
from jax.experimental import pallas as pl
from jax.experimental.pallas import tpu as pltpu

F32 = jnp.float32
BF16 = jnp.bfloat16
I32 = jnp.int32

HEAD_DIM = 64
A_HEADS = 8
A_DIM = A_HEADS * HEAD_DIM
LORA_W = 64
LORA_A = 64
LORA_G = 128
B_DIM = 512
CONV_W = 3
C_HEADS = 8
C_KV_HEADS = 4
C_DIM = C_HEADS * HEAD_DIM
C_KV_DIM = C_KV_HEADS * HEAD_DIM
IDX_HEADS = 8
IDX_DIM = 64
TOPK_MAX = 256
ROT_DIM = HEAD_DIM // 4
ROT_HALF = ROT_DIM // 2
ROPE_THETA = 500000.0
N_BRANCH = 3
RMS_EPS = 1e-6
GN_EPS = 64e-5
RWKV_COLS = 3 * A_DIM + LORA_W + LORA_A + LORA_G
CONV_COLS = 3 * B_DIM
ATTN_COLS = C_DIM + 2 * C_KV_DIM + IDX_HEADS * IDX_DIM + IDX_DIM + IDX_HEADS
ATTN_COLS_PAD = 1664

LANES = 128
INT_MIN = -2 ** 31
NEG_BIG = -1e30
VMEM_LIMIT = 56 * 1024 * 1024
PAGES_PER_STEP = 32


def _pages_per_step(n_pages):
    npg = min(PAGES_PER_STEP, n_pages)
    while n_pages % npg:
        npg -= 1
    return npg


def _cparams(n_axes):
    return pltpu.CompilerParams(dimension_semantics=("arbitrary",) * n_axes,
                                vmem_limit_bytes=VMEM_LIMIT)


def _const_spec(shape):
    nd = len(shape)
    return pl.BlockSpec(shape, lambda *_: (0,) * nd, pipeline_mode=pl.Buffered(1))


def _tile(m, pref):
    t = min(m, pref)
    while m % t:
        t -= 8
    return t


def _rms(x, g):
    ms = jnp.mean(x * x, axis=-1, keepdims=True)
    return x * lax.rsqrt(ms + RMS_EPS) * g


def _dot(a, b):
    return jnp.dot(a, b, preferred_element_type=F32)


def _dot_nt(a, b):
    return lax.dot_general(a, b, (((1,), (1,)), ((), ())), preferred_element_type=F32)


def _segsum(x, bd, split=False):
    hi = x.astype(BF16)
    if not split:
        return _dot(hi, bd)
    lo = (x - hi.astype(F32)).astype(BF16)
    return _dot(hi, bd) + _dot(lo, bd)


def _rope(x, cos, sa, sb):
    n = x.shape[-1]
    return x * cos + pltpu.roll(x, n - ROT_HALF, 1) * sa + pltpu.roll(x, ROT_HALF, 1) * sb


def _rows_before(x, k, fills, *, seg_len, first_chunk_rows=None):
    rows = x.shape[0]
    y = pltpu.roll(x, k, 0)
    ridx = lax.broadcasted_iota(I32, (rows, 1), 0)
    t = ridx if seg_len is None else ridx % seg_len
    for r in range(k):
        y = jnp.where(t == r, fills[k - 1 - r], y)
    return y


def _softplus(y):
    return jnp.maximum(y, 0.0) + jnp.log(1.0 + jnp.exp(-jnp.abs(y)))


def _ffn_kernel(*refs, chunks, ple):
    if ple:
        x_ref, g_ref, wg_ref, wu_ref, wd_ref, p_ref, pn_ref, pg_ref, pp_ref, o_ref = refs
    else:
        x_ref, g_ref, wg_ref, wu_ref, wd_ref, o_ref = refs
    x = x_ref[...]
    h = _rms(x, g_ref[...]).astype(BF16)
    acc = None
    for lo, hi in chunks:
        gt = _dot(h, wg_ref[:, lo:hi])
        ut = _dot(h, wu_ref[:, lo:hi])
        act = (gt * jax.nn.sigmoid(gt) * ut).astype(BF16)
        d = _dot(act, wd_ref[lo:hi, :])
        acc = d if acc is None else acc + d
    y = x + 0.5 * acc
    if ple:
        hg = _rms(y, pn_ref[...]).astype(BF16)
        gate = jax.nn.sigmoid(_dot(hg, pg_ref[...]))
        y = y + gate * _dot(p_ref[...].astype(BF16), pp_ref[...])
    o_ref[...] = y


def _ffn(x, g, wg, wu, wd, ple=None):
    m, d = x.shape
    f = wg.shape[1]
    tm = _tile(m, 512)
    step = 1024
    chunks = tuple((lo, min(lo + step, f)) for lo in range(0, f, step))
    row = lambda i: (i, 0)
    in_specs = [pl.BlockSpec((tm, d), row), _const_spec((1, d)), _const_spec((d, f)),
                _const_spec((d, f)), _const_spec((f, d))]
    args = [x, g.reshape(1, d), wg, wu, wd]
    if ple is not None:
        (p, layer), pn, pg, pp = ple
        off = layer * (m // tm)
        in_specs += [pl.BlockSpec((tm, p.shape[1]), lambda i: (i + off, 0)), _const_spec((1, d)),
                     _const_spec(pg.shape), _const_spec(pp.shape)]
        args += [p, pn.reshape(1, d), pg, pp]
    return pl.pallas_call(
        functools.partial(_ffn_kernel, chunks=chunks, ple=ple is not None),
        grid=(m // tm,), in_specs=in_specs, out_specs=pl.BlockSpec((tm, d), row),
        out_shape=jax.ShapeDtypeStruct((m, d), F32), compiler_params=_cparams(1),
        name="ffn_ple" if ple is not None else "ffn")(*args)


def _rwkv_pre_kernel(*refs, seg_len):
    (x_ref, g_ref, wa_ref, mu_ref, w0_ref, a0_ref, wwa_ref, g2_ref, kkw_ref, ka_ref,
     rk_ref, bd_ref) = refs[:12]
    rest = refs[12:]
    if seg_len is None:
        r_o, w_o, k_o, v_o, na_o, nb_o, g_o, bonus_o, shift_o, carry_ref = rest
    else:
        init_ref, r_o, w_o, k_o, v_o, na_o, nb_o, g_o, bonus_o, shift_o = rest
    x = x_ref[...]
    tm = x.shape[0]
    h = _rms(x, g_ref[...]).astype(BF16)
    u = _dot(h, wa_ref[...])
    if seg_len is None:
        @pl.when(pl.program_id(1) == 0)
        def _():
            carry_ref[...] = jnp.zeros_like(carry_ref)
        u_prev = _rows_before(u, 1, [carry_ref[0:1, :]], seg_len=None)
        carry_ref[0:1, :] = u[tm - 1:tm, :]
        shift_o[0] = u[tm - 1:tm, :]
    else:
        u_prev = _rows_before(u, 1, [init_ref[...]], seg_len=seg_len)
        shift_o[...] = u
    us = u + (u_prev - u) * mu_ref[...]
    r = us[:, 0:A_DIM]
    k = us[:, A_DIM:2 * A_DIM]
    v = us[:, 2 * A_DIM:3 * A_DIM]
    o3 = 3 * A_DIM
    xwa = us[:, o3:o3 + LORA_W + LORA_A]
    lane = lax.broadcasted_iota(I32, xwa.shape, 1)
    xwa = jnp.where(lane < LORA_W, jnp.tanh(xwa), xwa)
    lo = _dot(xwa.astype(BF16), wwa_ref[...])
    w_log = -_softplus(-(w0_ref[...] + lo[:, :A_DIM])) - 0.5
    decay = jnp.exp(-jnp.exp(w_log))
    a = jax.nn.sigmoid(a0_ref[...] + lo[:, A_DIM:])
    xg = us[:, o3 + LORA_W + LORA_A:]
    g = _dot(jax.nn.sigmoid(xg).astype(BF16), g2_ref[...])
    bd = bd_ref[...]
    kk = k * kkw_ref[...]
    kk = kk / jnp.maximum(jnp.sqrt(_segsum(kk * kk, bd)), 1e-12)
    k2 = k * (1.0 + (a - 1.0) * ka_ref[...])
    put = (lambda z: jnp.transpose(z)) if seg_len is None else (lambda z: z)
    r_o[...] = put(r)
    w_o[...] = put(decay)
    k_o[...] = put(k2)
    v_o[...] = put(v)
    na_o[...] = put(-kk)
    nb_o[...] = put(kk * a)
    g_o[...] = g
    bonus_o[...] = _segsum(r * k2 * rk_ref[...], bd) * v


def _rwkv_pre(x, nb, seq, lw, init_rows=None):
    m, d = x.shape
    consts = [lw["mix_norm"], lw["w_a"], lw["rwkv_mu"], lw["rwkv_w0"], lw["rwkv_a0"], lw["w_wa"],
              lw["rwkv_g2"], lw["rwkv_kk"], lw["rwkv_ka"], lw["rwkv_rk"], lw["bd"]]
    const_specs = [_const_spec(c.shape) for c in consts]
    outs = [jax.ShapeDtypeStruct((m, A_DIM), F32)] * 8
    if init_rows is None:
        tm = _tile(seq, 512)
        nc = seq // tm
        row = lambda b, c: (b * nc + c, 0)
        grid = (nb, nc)
        in_specs = [pl.BlockSpec((tm, d), row)] + const_specs
        out_specs = ([pl.BlockSpec((A_DIM, tm), lambda b, c: (b, c))] * 6 + [pl.BlockSpec((tm, A_DIM), row)] * 2
                     + [pl.BlockSpec((1, 1, RWKV_COLS), lambda b, c: (b, 0, 0))])
        outs = ([jax.ShapeDtypeStruct((nb * A_DIM, seq), F32)] * 6 + outs[6:]
                + [jax.ShapeDtypeStruct((nb, 1, RWKV_COLS), F32)])
        scratch = [pltpu.VMEM((8, RWKV_COLS), F32)]
        args = [x] + consts
        seg_len = None
    else:
        tm = m
        row = lambda i: (0, 0)
        grid = (1,)
        in_specs = [pl.BlockSpec((tm, d), row)] + const_specs + [pl.BlockSpec((tm, RWKV_COLS), row)]
        out_specs = [pl.BlockSpec((tm, A_DIM), row)] * 8 + [pl.BlockSpec((tm, RWKV_COLS), row)]
        outs = outs + [jax.ShapeDtypeStruct((m, RWKV_COLS), F32)]
        scratch = []
        args = [x] + consts + [init_rows]
        seg_len = seq
    return pl.pallas_call(
        functools.partial(_rwkv_pre_kernel, seg_len=seg_len), grid=grid, in_specs=in_specs,
        out_specs=out_specs, out_shape=outs, scratch_shapes=scratch,
        compiler_params=_cparams(len(grid)), name="rwkv_pre")(*args)


def _scan_kernel(r_ref, w_ref, k_ref, v_ref, a_ref, b_ref, s0_ref, o_ref, st_ref, s_ref, *, tc):
    c = pl.program_id(1)

    @pl.when(c == 0)
    def _():
        s_ref[...] = s0_ref[...]

    jb = 32
    ln = s_ref.shape[-1]

    def step(t, carry):
        def sa_body(q, sa):
            for jj in range(jb):
                j = q * jb + jj
                sa = sa + s_ref[j] * a_ref[t, pl.ds(j, 1), :]
            return sa

        sa = lax.fori_loop(0, HEAD_DIM // jb, sa_body, jnp.zeros((HEAD_DIM, ln), F32))
        vt = v_ref[t]

        def up_body(q, o):
            for jj in range(jb):
                j = q * jb + jj
                sn = (s_ref[j] * w_ref[t, pl.ds(j, 1), :] + sa * b_ref[t, pl.ds(j, 1), :]
                      + vt * k_ref[t, pl.ds(j, 1), :])
                s_ref[j] = sn
                o = o + sn * r_ref[t, pl.ds(j, 1), :]
            return o

        o_ref[t] = lax.fori_loop(0, HEAD_DIM // jb, up_body, jnp.zeros((HEAD_DIM, ln), F32))
        return carry

    lax.fori_loop(0, tc, step, 0)

    @pl.when(c == pl.num_programs(1) - 1)
    def _():
        st_ref[...] = s_ref[...]


def _scan(r, w, k, v, a, b, s0):
    t_len, hd, nbh = r.shape
    ln = min(LANES, nbh)
    tc = _tile(t_len, 32) if t_len % 8 == 0 else t_len
    grid = (nbh // ln, t_len // tc)
    seq_spec = pl.BlockSpec((tc, hd, ln), lambda l, c: (c, 0, l))
    st_spec = pl.BlockSpec((hd, hd, ln), lambda l, c: (0, 0, l))
    return pl.pallas_call(
        functools.partial(_scan_kernel, tc=tc), grid=grid,
        in_specs=[seq_spec] * 6 + [st_spec], out_specs=[seq_spec, st_spec],
        out_shape=[jax.ShapeDtypeStruct((t_len, hd, nbh), F32),
                   jax.ShapeDtypeStruct((hd, hd, nbh), F32)],
        scratch_shapes=[pltpu.VMEM((hd, hd, ln), F32)],
        compiler_params=_cparams(2), name="rwkv_scan")(r, w, k, v, a, b, s0)


def _conv_kernel(*refs, seg_len):
    x_ref, g_ref, wb_ref, cw_ref = refs[:4]
    rest = refs[4:]
    if seg_len is None:
        o_ref, st_ref, carry_ref = rest
    else:
        i0_ref, i1_ref, o_ref, st_ref = rest
    x = x_ref[...]
    tm = x.shape[0]
    h = _rms(x, g_ref[...]).astype(BF16)
    u = _dot(h, wb_ref[...])
    bg = u[:, :B_DIM]
    z = u[:, B_DIM:2 * B_DIM] * u[:, 2 * B_DIM:]
    if seg_len is None:
        @pl.when(pl.program_id(1) == 0)
        def _():
            carry_ref[...] = jnp.zeros_like(carry_ref)
        hist = [carry_ref[1:2, :], carry_ref[0:1, :]]
        z1 = _rows_before(z, 1, hist[:1], seg_len=None)
        z2 = _rows_before(z, 2, hist, seg_len=None)
        carry_ref[0:2, :] = z[tm - 2:tm, :]
        st_ref[0] = z[tm - 2:tm, :]
    else:
        hist = [i1_ref[...], i0_ref[...]]
        z1 = _rows_before(z, 1, hist[:1], seg_len=seg_len)
        z2 = _rows_before(z, 2, hist, seg_len=seg_len)
        st_ref[...] = z
    cw = cw_ref[...]
    y = z2 * cw[0:1, :] + z1 * cw[1:2, :] + z * cw[2:3, :]
    o_ref[...] = bg * y


def _conv(x, nb, seq, lw, init=None):
    m, d = x.shape
    assert seq >= CONV_W - 1
    consts = [lw["mix_norm"], lw["w_b"], lw["conv_w"]]
    const_specs = [_const_spec(c.shape) for c in consts]
    if init is None:
        tm = _tile(seq, 512)
        nc = seq // tm
        row = lambda b, c: (b * nc + c, 0)
        grid = (nb, nc)
        in_specs = [pl.BlockSpec((tm, d), row)] + const_specs
        out_specs = [pl.BlockSpec((tm, B_DIM), row),
                     pl.BlockSpec((1, CONV_W - 1, B_DIM), lambda b, c: (b, 0, 0))]
        outs = [jax.ShapeDtypeStruct((m, B_DIM), F32), jax.ShapeDtypeStruct((nb, CONV_W - 1, B_DIM), F32)]
        scratch = [pltpu.VMEM((8, B_DIM), F32)]
        args = [x] + consts
        seg_len = None
    else:
        tm = m
        row = lambda i: (0, 0)
        grid = (1,)
        in_specs = [pl.BlockSpec((tm, d), row)] + const_specs + [pl.BlockSpec((tm, B_DIM), row)] * 2
        out_specs = [pl.BlockSpec((tm, B_DIM), row)] * 2
        outs = [jax.ShapeDtypeStruct((m, B_DIM), F32)] * 2
        scratch = []
        args = [x] + consts + list(init)
        seg_len = seq
    return pl.pallas_call(
        functools.partial(_conv_kernel, seg_len=seg_len), grid=grid, in_specs=in_specs,
        out_specs=out_specs, out_shape=outs, scratch_shapes=scratch,
        compiler_params=_cparams(len(grid)), name="shortconv")(*args)


def _attn_proj_kernel(x_ref, g_ref, wc_ref, qn_ref, kn_ref, in_ref, bd_ref, cos_ref, sa_ref, sb_ref,
                      *outs, transposed):
    x = x_ref[...]
    h = _rms(x, g_ref[...]).astype(BF16)
    u = _dot(h, wc_ref[...])
    cos, sa, sb = cos_ref[...], sa_ref[...], sb_ref[...]
    bd = bd_ref[...]
    inv_hd = 1.0 / HEAD_DIM
    q = u[:, :C_DIM]
    q = q * lax.rsqrt(_segsum(q * q, bd) * inv_hd + RMS_EPS) * qn_ref[...]
    q = _rope(q, cos, sa, sb)
    k = u[:, C_DIM:C_DIM + C_KV_DIM]
    k = k * lax.rsqrt(_segsum(k * k, bd[:C_KV_DIM, :C_KV_DIM]) * inv_hd + RMS_EPS) * kn_ref[...]
    k = _rope(k, cos[:, :C_KV_DIM], sa[:, :C_KV_DIM], sb[:, :C_KV_DIM])
    o = C_DIM + C_KV_DIM
    v = u[:, o:o + C_KV_DIM]
    o += C_KV_DIM
    qi = _rope(u[:, o:o + IDX_HEADS * IDX_DIM], cos, sa, sb)
    o += IDX_HEADS * IDX_DIM
    kw = u[:, o:o + LANES]
    lane = lax.broadcasted_iota(I32, kw.shape, 1)
    is_ki = lane < IDX_DIM
    ms = jnp.sum(jnp.where(is_ki, kw * kw, 0.0), axis=-1, keepdims=True) * (1.0 / IDX_DIM)
    kin = _rope(kw * lax.rsqrt(ms + RMS_EPS) * in_ref[...], cos[:, :LANES], sa[:, :LANES], sb[:, :LANES])
    kw = jnp.where(is_ki, kin, kw)
    if transposed:
        q_o, k_o, qi_o, kw_o, kt_o, vt_o, kit_o = outs
        kt_o[0] = jnp.transpose(k)
        vt_o[0] = jnp.transpose(v)
        kit_o[0] = jnp.transpose(kw)[:IDX_DIM, :]
    else:
        q_o, k_o, qi_o, kw_o, v_o, ki_o = outs
        v_o[...] = v
        ki_o[...] = kw[:, :IDX_DIM]
    q_o[...] = q
    k_o[...] = k
    qi_o[...] = qi
    kw_o[...] = kw


def _attn_proj(x, nb, seq, lw, tabs, sample):
    m, d = x.shape
    consts = [lw["mix_norm"], lw["w_c"], lw["q_norm"], lw["k_norm"], lw["idx_k_norm"], lw["bd"]]
    const_specs = [_const_spec(c.shape) for c in consts]
    if sample:
        tm, nc = m, 1
        grid = (1, 1)
    else:
        tm = _tile(seq, 512)
        nc = seq // tm
        grid = (nb, nc)
    row = lambda b, c: (b * nc + c, 0)
    tab = lambda b, c: (c, 0)
    widths = [C_DIM, C_KV_DIM, IDX_HEADS * IDX_DIM, LANES]
    out_specs = [pl.BlockSpec((tm, w), row) for w in widths]
    out_shape = [jax.ShapeDtypeStruct((m, w), F32) for w in widths]
    if sample:
        for w in (C_KV_DIM, IDX_DIM):
            out_specs.append(pl.BlockSpec((tm, w), row))
            out_shape.append(jax.ShapeDtypeStruct((m, w), F32))
    else:
        for w in (C_KV_DIM, C_KV_DIM, IDX_DIM):
            out_specs.append(pl.BlockSpec((1, w, tm), lambda b, c: (b, 0, c)))
            out_shape.append(jax.ShapeDtypeStruct((nb, w, seq), F32))
    return pl.pallas_call(
        functools.partial(_attn_proj_kernel, transposed=not sample), grid=grid,
        in_specs=[pl.BlockSpec((tm, d), row)] + const_specs + [pl.BlockSpec((tm, C_DIM), tab)] * 3,
        out_specs=out_specs, out_shape=out_shape,
        compiler_params=_cparams(2), name="attn_proj")(x, *consts, *tabs)


def _score_keys(score):
    bits = pltpu.bitcast(score, I32)
    return jnp.where(bits < 0, bits ^ 0x7FFFFFFF, bits)


def _lane_total(acc):
    return jnp.broadcast_to(jnp.sum(acc, axis=-1, keepdims=True), acc.shape)


def _tree(parts, op):
    while len(parts) > 1:
        nxt = [op(parts[i], parts[i + 1]) for i in range(0, len(parts) - 1, 2)]
        parts = nxt + (parts[-1:] if len(parts) % 2 else [])
    return parts[0]


def _fold8(x, op):
    return _tree([x[i * 8:(i + 1) * 8] for i in range(x.shape[0] // 8)], op)


def _kth_key(count_fn, topk, shape):
    def vbit(it, acc):
        cand = acc | jnp.left_shift(jnp.int32(1), 31 - it)
        cmp = cand ^ INT_MIN
        cnt = count_fn(lambda kk, pos: jnp.where(kk >= cmp, 1, 0))
        return jnp.where(cnt >= topk, cand, acc)

    return lax.fori_loop(0, 32, vbit, jnp.zeros(shape, I32)) ^ INT_MIN


def _select_topk(count_fn, topk, pos_bits, shape, lp_ref):
    zeros = jnp.zeros(shape, I32)
    thr = _kth_key(count_fn, topk, shape)
    n_ge = count_fn(lambda kk, pos: jnp.where(kk >= thr, 1, 0))
    excess = jnp.where(thr == INT_MIN, 0, n_ge - topk)
    lp_ref[...] = jnp.full(shape, 2 ** 31 - 1, I32)

    @pl.when(jnp.max(excess) > 0)
    def _():
        need = topk - count_fn(lambda kk, pos: jnp.where(kk > thr, 1, 0))

        def pbit(it, acc):
            cand = acc | jnp.left_shift(jnp.int32(1), pos_bits - 1 - it)
            cnt = count_fn(lambda kk, pos: jnp.where(kk == thr, jnp.where(pos < cand, 1, 0), 0))
            return jnp.where(cnt < need, cand, acc)

        lp_ref[...] = lax.fori_loop(0, pos_bits, pbit, zeros)

    return thr, lp_ref[...]


def _select_bias(kk, pos, thr, last_pos):
    tie = jnp.where(pos <= last_pos, 0.0, NEG_BIG)
    bias = jnp.where(kk == thr, tie, jnp.where(kk > thr, 0.0, NEG_BIG))
    return jnp.where(kk == INT_MIN, NEG_BIG, bias)


def _pattn_kernel(qi_ref, kwq_ref, q_ref, kwk_ref, k_ref, vt_ref, o_ref,
                  keys_ref, s_ref, qip_ref, qg_ref, acc_ref, *, tq, topk):
    qb = pl.program_id(1)
    nch = qb + 1
    nslab = tq // 8
    shape8 = (8, tq)
    sub8 = lax.broadcasted_iota(I32, shape8, 0)
    lane_q = lax.broadcasted_iota(I32, (tq, LANES), 1)
    key_in = lax.broadcasted_iota(I32, (tq, tq), 0)
    qry_in = lax.broadcasted_iota(I32, (tq, tq), 1)

    wt = jnp.transpose(kwq_ref[...]) * ((IDX_DIM ** -0.5) * (IDX_HEADS ** -0.5))
    for hh in range(IDX_HEADS):
        slab = qi_ref[:, (hh // 2) * LANES:(hh // 2 + 1) * LANES]
        if hh % 2:
            slab = pltpu.roll(slab, IDX_DIM, 1)
        qip_ref[hh] = jnp.where(lane_q < IDX_DIM, slab, 0.0).astype(BF16)

    def idx_body(kc, carry):
        start = pl.multiple_of(kc * tq, tq)
        ks = kwk_ref[pl.ds(start, tq), :].astype(BF16)
        acc = None
        for hh in range(IDX_HEADS):
            s = _dot_nt(ks, qip_ref[hh])
            term = jnp.maximum(s, 0.0) * wt[IDX_DIM + hh:IDX_DIM + hh + 1, :]
            acc = term if acc is None else acc + term
        kk = jnp.where(acc == 0.0, 0, _score_keys(acc))
        keys_ref[kc] = jnp.where(kc * tq + key_in <= qb * tq + qry_in, kk, INT_MIN)
        return carry

    lax.fori_loop(0, nch, idx_body, 0)

    def count_fn(ind):
        def body(kc, acc):
            base = kc * tq
            parts = [ind(keys_ref[kc, i * 8:(i + 1) * 8, :], base + i * 8 + sub8) for i in range(nslab)]
            return acc + _tree(parts, jnp.add)
        acc = lax.fori_loop(0, nch, body, jnp.zeros(shape8, I32))
        return jnp.broadcast_to(jnp.sum(acc, axis=0, keepdims=True), shape8)

    thr = _kth_key(count_fn, topk, shape8)
    need = topk - count_fn(lambda kk, pos: jnp.where(kk > thr, 1, 0))

    def slab_bias(kk, ties_before):
        tie = jnp.where(kk == thr, 1, 0)
        incl = tie
        for sh in (1, 2, 4):
            incl = incl + jnp.where(sub8 >= sh, pltpu.roll(incl, sh, 0), 0)
        rank = ties_before + incl - tie
        bias = jnp.where(kk > thr, 0.0, NEG_BIG)
        bias = jnp.where(kk == thr, jnp.where(rank < need, 0.0, NEG_BIG), bias)
        bias = jnp.where(kk == INT_MIN, NEG_BIG, bias)
        total = jnp.broadcast_to(jnp.sum(tie, axis=0, keepdims=True), shape8)
        return bias, ties_before + total

    scale = (HEAD_DIM ** -0.5) * 1.4426950408889634
    for g in range(C_KV_HEADS):
        qslab = q_ref[:, g * LANES:(g + 1) * LANES] * scale
        rolled = pltpu.roll(qslab, HEAD_DIM, 1)
        koff = g % 2
        in_half = (lane_q >= HEAD_DIM) if koff else (lane_q < HEAD_DIM)
        qg_ref[g] = jnp.concatenate(
            [jnp.where(in_half, qslab if r == koff else rolled, 0.0) for r in range(2)],
            axis=0).astype(BF16)

    def max_body(kc, carry):
        start = pl.multiple_of(kc * tq, tq)
        ties = carry[C_KV_HEADS]
        slabs = []
        for i in range(nslab):
            b, ties = slab_bias(keys_ref[kc, i * 8:(i + 1) * 8, :], ties)
            slabs.append(b)
        bias = jnp.concatenate(slabs, axis=0)
        bias2 = jnp.concatenate([bias, bias], axis=1)
        new = []
        for g in range(C_KV_HEADS):
            kcol = (g // 2) * LANES
            kch = k_ref[pl.ds(start, tq), kcol:kcol + LANES].astype(BF16)
            s = _dot_nt(kch, qg_ref[g]) + bias2
            s_ref[kc, :, g * 2 * tq:(g + 1) * 2 * tq] = s
            new.append(jnp.maximum(carry[g], _fold8(s, jnp.maximum)))
        return tuple(new) + (ties,)

    m8 = lax.fori_loop(0, nch, max_body,
                       tuple(jnp.full((8, 2 * tq), NEG_BIG, F32) for _ in range(C_KV_HEADS))
                       + (jnp.zeros(shape8, I32),))[:C_KV_HEADS]
    m_row = [jnp.max(m, axis=0, keepdims=True) for m in m8]

    acc_ref[...] = jnp.zeros_like(acc_ref)

    def sum_body(kc, carry):
        start = pl.multiple_of(kc * tq, tq)
        new = []
        for g in range(C_KV_HEADS):
            p = jnp.exp2(s_ref[kc, :, g * 2 * tq:(g + 1) * 2 * tq] - m_row[g])
            vt = vt_ref[0, g * HEAD_DIM:(g + 1) * HEAD_DIM, pl.ds(start, tq)].astype(BF16)
            acc_ref[g] += _dot(vt, p.astype(BF16))
            new.append(carry[g] + _fold8(p, jnp.add))
        return tuple(new)

    l8 = lax.fori_loop(0, nch, sum_body,
                       tuple(jnp.zeros((8, 2 * tq), F32) for _ in range(C_KV_HEADS)))
    for g in range(C_KV_HEADS):
        out_t = acc_ref[g] / jnp.sum(l8[g], axis=0, keepdims=True)
        o_ref[:, g * LANES:(g + 1) * LANES] = jnp.transpose(
            jnp.concatenate([out_t[:, :tq], out_t[:, tq:]], axis=0))


def _prompt_attention(q, k, vt, qi, kw, nb, seq):
    assert C_HEADS == 2 * C_KV_HEADS
    topk = min(TOPK_MAX, seq // 4)
    tq = _tile(seq, 256)
    assert tq % LANES == 0
    nq = seq // tq
    qrow = lambda b, c: (b * nq + c, 0)
    krow = lambda b, c: (b, 0)
    return pl.pallas_call(
        functools.partial(_pattn_kernel, tq=tq, topk=topk),
        grid=(nb, nq),
        in_specs=[pl.BlockSpec((tq, IDX_HEADS * IDX_DIM), qrow), pl.BlockSpec((tq, LANES), qrow),
                  pl.BlockSpec((tq, C_DIM), qrow), pl.BlockSpec((seq, LANES), krow),
                  pl.BlockSpec((seq, C_KV_DIM), krow),
                  pl.BlockSpec((1, C_KV_DIM, seq), lambda b, c: (b, 0, 0))],
        out_specs=pl.BlockSpec((tq, C_DIM), qrow),
        out_shape=jax.ShapeDtypeStruct((nb * seq, C_DIM), F32),
        scratch_shapes=[pltpu.VMEM((nq, tq, tq), I32),
                        pltpu.VMEM((nq, tq, C_KV_HEADS * 2 * tq), F32),
                        pltpu.VMEM((IDX_HEADS, tq, LANES), BF16),
                        pltpu.VMEM((C_KV_HEADS, 2 * tq, LANES), BF16),
                        pltpu.VMEM((C_KV_HEADS, HEAD_DIM, 2 * tq), F32)],
        compiler_params=_cparams(2), name="prompt_attn")(qi, kw, q, kw, k, vt)


SROWS = 8


def _sidx_kernel(pt_ref, qh_ref, wh_ref, kn_ref, *rest, npg, page):
    pages = rest[:npg]
    past_o, new_o = rest[npg:]
    qh = qh_ref[0].astype(BF16)
    wh = wh_ref[0] * ((IDX_DIM ** -0.5) * (IDX_HEADS ** -0.5))

    def head_sum(s):
        s = jnp.maximum(s, 0.0) * wh
        return _tree([s[hh * SROWS:(hh + 1) * SROWS] for hh in range(IDX_HEADS)], jnp.add)

    keys_t = jnp.concatenate([pg[0] for pg in pages], axis=1).astype(BF16)
    past_o[0] = head_sum(_dot(qh, keys_t))

    @pl.when(pl.program_id(1) == 0)
    def _():
        new_o[0] = head_sum(_dot_nt(qh, kn_ref[0].astype(BF16)))


def _sample_index_scores(cache_kidx, page_table, layer, qh, wh, ki_new):
    depth, n_phys, page, _ = cache_kidx.shape
    db, n_pages = page_table.shape
    npg = _pages_per_step(n_pages)
    cache = cache_kidx.transpose(0, 1, 3, 2).reshape(depth * n_phys, IDX_DIM, page)
    base = layer * n_phys

    def page_spec(i):
        return pl.BlockSpec((1, IDX_DIM, page), lambda b, s, pt: (base + pt[b, s * npg + i], 0, 0))

    per_b = lambda b, s, pt: (b, 0, 0)
    grid_spec = pltpu.PrefetchScalarGridSpec(
        num_scalar_prefetch=1, grid=(db, n_pages // npg),
        in_specs=[pl.BlockSpec((1, IDX_HEADS * SROWS, IDX_DIM), per_b),
                  pl.BlockSpec((1, IDX_HEADS * SROWS, 1), per_b),
                  pl.BlockSpec((1, LANES, IDX_DIM), per_b)] + [page_spec(i) for i in range(npg)],
        out_specs=[pl.BlockSpec((1, SROWS, npg * page), lambda b, s, pt: (b, 0, s)),
                   pl.BlockSpec((1, SROWS, LANES), per_b)])
    return pl.pallas_call(
        functools.partial(_sidx_kernel, npg=npg, page=page), grid_spec=grid_spec,
        out_shape=[jax.ShapeDtypeStruct((db, SROWS, n_pages * page), F32),
                   jax.ShapeDtypeStruct((db, SROWS, LANES), F32)],
        compiler_params=_cparams(2), name="sample_index")(page_table, qh, wh, ki_new, *([cache] * npg))


def _ssel_kernel(past_ref, new_ref, bpast_o, bnew_o, keys_ref, lp_ref, *, topk, pos_bits, n_new):
    rows, past = past_ref.shape
    nch = past // LANES
    lane = lax.broadcasted_iota(I32, (rows, LANES), 1)
    t_row = lax.broadcasted_iota(I32, (rows, LANES), 0) % SROWS

    def fill(c, carry):
        start = pl.multiple_of(c * LANES, LANES)
        keys_ref[c] = _score_keys(past_ref[:, pl.ds(start, LANES)])
        return carry

    lax.fori_loop(0, nch, fill, 0)
    new_ok = lane <= jnp.minimum(t_row, n_new - 1)
    keys_ref[nch] = jnp.where(new_ok, _score_keys(new_ref[...]), INT_MIN)

    grp = 8
    assert nch % grp == 0

    def count_fn(pred):
        def body(q, acc):
            c0 = q * grp
            return acc + _tree([pred(keys_ref[c0 + i], (c0 + i) * LANES + lane) for i in range(grp)], jnp.add)
        acc = lax.fori_loop(0, nch // grp, body, jnp.zeros((rows, LANES), I32))
        return _lane_total(acc + pred(keys_ref[nch], nch * LANES + lane))

    thr, last_pos = _select_topk(count_fn, topk, pos_bits, (rows, LANES), lp_ref)

    def emit(c, carry):
        start = pl.multiple_of(c * LANES, LANES)
        bpast_o[:, pl.ds(start, LANES)] = _select_bias(keys_ref[c], c * LANES + lane, thr, last_pos)
        return carry

    lax.fori_loop(0, nch, emit, 0)
    bnew_o[...] = _select_bias(keys_ref[nch], nch * LANES + lane, thr, last_pos)


def _sample_select(sc_past, sc_new, n_new):
    rows, past = sc_past.shape
    topk = min(TOPK_MAX, (past + n_new) // 4)
    tr = _tile(rows, 64)
    pos_bits = int(past + LANES - 1).bit_length()
    row = lambda i: (i, 0)
    return pl.pallas_call(
        functools.partial(_ssel_kernel, topk=topk, pos_bits=pos_bits, n_new=n_new),
        grid=(rows // tr,),
        in_specs=[pl.BlockSpec((tr, past), row), pl.BlockSpec((tr, LANES), row)],
        out_specs=[pl.BlockSpec((tr, past), row), pl.BlockSpec((tr, LANES), row)],
        out_shape=[jax.ShapeDtypeStruct((rows, past), F32), jax.ShapeDtypeStruct((rows, LANES), F32)],
        scratch_shapes=[pltpu.VMEM((past // LANES + 1, tr, LANES), I32), pltpu.VMEM((tr, LANES), I32)],
        compiler_params=_cparams(1), name="sample_select")(sc_past, sc_new)


def _satt_kernel(pt_ref, qbd_ref, bpast_ref, bnew_ref, kn_ref, vn_ref, *rest, npg, page):
    kpages = rest[:npg]
    vpages = rest[npg:2 * npg]
    o_ref, m_ref, l_ref, acc_ref = rest[2 * npg:]
    s_id = pl.program_id(1)
    nrow = qbd_ref.shape[1]
    reps = nrow // SROWS

    @pl.when(s_id == 0)
    def _():
        m_ref[...] = jnp.full_like(m_ref, NEG_BIG)
        l_ref[...] = jnp.zeros_like(l_ref)
        acc_ref[...] = jnp.zeros_like(acc_ref)

    qbd = qbd_ref[0].astype(BF16)

    def update(s, bias, pv):
        s = s + jnp.concatenate([bias] * reps, axis=0)
        m_old = m_ref[...]
        m_new = jnp.maximum(m_old, jnp.max(s, axis=-1, keepdims=True))
        p = jnp.exp(s - m_new)
        alpha = jnp.exp(m_old - m_new)
        l_ref[...] = alpha * l_ref[...] + jnp.sum(p, axis=-1, keepdims=True)
        acc_ref[...] = alpha * acc_ref[...] + pv(p.astype(BF16))
        m_ref[...] = m_new

    keys_t = jnp.concatenate([kp[0] for kp in kpages], axis=1).astype(BF16)
    vals_t = jnp.concatenate([vp[0] for vp in vpages], axis=1).astype(BF16)
    update(_dot(qbd, keys_t), bpast_ref[0], lambda p: _dot_nt(p, vals_t))

    @pl.when(s_id == pl.num_programs(1) - 1)
    def _():
        vn = vn_ref[0].astype(BF16)
        update(_dot_nt(qbd, kn_ref[0].astype(BF16)), bnew_ref[0], lambda p: _dot(p, vn))
        o_ref[0] = acc_ref[...] / l_ref[...]


def _sample_attend(cache_k, cache_v, page_table, layer, qbd, bias_past, bias_new, k_new, v_new):
    depth, n_phys, page = cache_k.shape[:3]
    db, n_pages = page_table.shape
    npg = _pages_per_step(n_pages)
    ck = cache_k.transpose(0, 1, 3, 4, 2).reshape(depth * n_phys, C_KV_DIM, page)
    cv = cache_v.transpose(0, 1, 3, 4, 2).reshape(depth * n_phys, C_KV_DIM, page)
    base = layer * n_phys
    nrow = qbd.shape[1]

    def page_spec(i):
        return pl.BlockSpec((1, C_KV_DIM, page), lambda b, s, pt: (base + pt[b, s * npg + i], 0, 0))

    per_b = lambda b, s, pt: (b, 0, 0)
    grid_spec = pltpu.PrefetchScalarGridSpec(
        num_scalar_prefetch=1, grid=(db, n_pages // npg),
        in_specs=[pl.BlockSpec((1, nrow, C_KV_DIM), per_b),
                  pl.BlockSpec((1, SROWS, npg * page), lambda b, s, pt: (b, 0, s)),
                  pl.BlockSpec((1, SROWS, LANES), per_b),
                  pl.BlockSpec((1, LANES, C_KV_DIM), per_b),
                  pl.BlockSpec((1, LANES, C_KV_DIM), per_b)]
                 + [page_spec(i) for i in range(npg)] * 2,
        out_specs=pl.BlockSpec((1, nrow, C_KV_DIM), per_b),
        scratch_shapes=[pltpu.VMEM((nrow, 1), F32), pltpu.VMEM((nrow, 1), F32),
                        pltpu.VMEM((nrow, C_KV_DIM), F32)])
    return pl.pallas_call(
        functools.partial(_satt_kernel, npg=npg, page=page), grid_spec=grid_spec,
        out_shape=jax.ShapeDtypeStruct((db, nrow, C_KV_DIM), F32),
        compiler_params=_cparams(2), name="sample_attn")(
            page_table, qbd, bias_past, bias_new, k_new, v_new, *([ck] * npg), *([cv] * npg))


def _merge_kernel(x_ref, g_ref, wg_ref, os_ref, bon_ref, gg_ref, lw_ref, lb_ref, bd_ref,
                  ob_ref, oc_ref, wpa_ref, wpb_ref, wpc_ref, wo_ref, o_ref):
    x = x_ref[...]
    d = x.shape[1]
    h = _rms(x, g_ref[...]).astype(BF16)
    gates = jax.nn.sigmoid(_dot(h, wg_ref[...]))
    bd = bd_ref[...]
    o = os_ref[...]
    mean = _segsum(o, bd, split=True) * (1.0 / HEAD_DIM)
    cen = o - mean
    var = _segsum(cen * cen, bd) * (1.0 / HEAD_DIM)
    on = cen * lax.rsqrt(var + GN_EPS) * lw_ref[...] + lb_ref[...]
    oa = ((on + bon_ref[...]) * gg_ref[...]).astype(BF16)
    merged = (gates[:, :d] * _dot(oa, wpa_ref[...])
              + gates[:, d:2 * d] * _dot(ob_ref[...].astype(BF16), wpb_ref[...])
              + gates[:, 2 * d:] * _dot(oc_ref[...].astype(BF16), wpc_ref[...]))
    o_ref[...] = x + _dot(merged.astype(BF16), wo_ref[...])


def _merge(x, o_scan, bonus, g, o_b, o_c, lw, nb, seq):
    m, d = x.shape
    tm = _tile(seq, 512)
    nc = seq // tm
    row = lambda b, c: (b * nc + c, 0)
    tok = lambda w: pl.BlockSpec((tm, w), row)
    consts_a = [lw["mix_norm"], lw["w_g"]]
    consts_b = [lw["rwkv_lnx_w"], lw["rwkv_lnx_b"], lw["bd"]]
    consts_c = [lw["w_pa"], lw["w_pb"], lw["w_pc"], lw["w_out"]]
    in_specs = ([tok(d)] + [_const_spec(c.shape) for c in consts_a]
                + [pl.BlockSpec((tm, A_DIM), lambda b, c: (c, b))] + [tok(A_DIM)] * 2
                + [_const_spec(c.shape) for c in consts_b] + [tok(B_DIM), tok(C_DIM)]
                + [_const_spec(c.shape) for c in consts_c])
    return pl.pallas_call(
        _merge_kernel, grid=(nb, nc), in_specs=in_specs, out_specs=tok(d),
        out_shape=jax.ShapeDtypeStruct((m, d), F32), compiler_params=_cparams(2), name="merge")(
            x, *consts_a, o_scan, bonus, g, *consts_b, o_b, o_c, *consts_c)


def _rope_tables(pos):
    inv = ROPE_THETA ** (-jnp.arange(ROT_HALF, dtype=F32) / ROT_HALF)
    ang = pos.astype(F32)[:, None] * inv[None, :]
    c, s = jnp.cos(ang), jnp.sin(ang)
    t = pos.shape[0]
    pad = jnp.zeros((t, HEAD_DIM - ROT_DIM), F32)
    zer = jnp.zeros((t, ROT_HALF), F32)
    cos = jnp.concatenate([c, c, pad + 1.0], axis=1)
    sa = jnp.concatenate([-s, zer, pad], axis=1)
    sb = jnp.concatenate([zer, s, pad], axis=1)
    return tuple(jnp.tile(z, (1, C_HEADS)) for z in (cos, sa, sb))


def _layer_weights(i, p):
    d = p["w_in"].shape[1]
    w_in = p["w_in"][i]
    o1 = RWKV_COLS
    o2 = o1 + CONV_COLS
    o3 = o2 + ATTN_COLS
    row = lambda v: v.reshape(1, -1)
    tile_h = lambda v, n: jnp.tile(v, n).reshape(1, -1)
    z = jnp.zeros((LORA_W, A_DIM), F32)
    w_wa = jnp.concatenate([jnp.concatenate([p["rwkv_w2"][i], z], axis=1),
                            jnp.concatenate([z, p["rwkv_a2"][i]], axis=1)], axis=0)
    head = np.arange(A_DIM) // HEAD_DIM
    bd = jnp.asarray(head[:, None] == head[None, :], BF16)
    idx_norm = jnp.concatenate([p["idx_k_norm"][i], jnp.zeros((LANES - IDX_DIM,), F32)])
    bf = lambda w: w.astype(BF16)
    return dict(
        ffn1_norm=p["ffn1_norm"][i], ffn1_wg=bf(p["ffn1_wg"][i]), ffn1_wu=bf(p["ffn1_wu"][i]),
        ffn1_wd=bf(p["ffn1_wd"][i]),
        ffn2_norm=p["ffn2_norm"][i], ffn2_wg=bf(p["ffn2_wg"][i]), ffn2_wu=bf(p["ffn2_wu"][i]),
        ffn2_wd=bf(p["ffn2_wd"][i]),
        mix_norm=row(p["mix_norm"][i]),
        w_a=bf(w_in[:, :o1]), w_b=bf(w_in[:, o1:o2]),
        w_c=bf(jnp.pad(w_in[:, o2:o3], ((0, 0), (0, ATTN_COLS_PAD - ATTN_COLS)))),
        w_g=bf(w_in[:, o3:]),
        rwkv_mu=row(p["rwkv_mu"][i]), rwkv_w0=row(p["rwkv_w0"][i]), rwkv_a0=row(p["rwkv_a0"][i]),
        w_wa=bf(w_wa), rwkv_g2=bf(p["rwkv_g2"][i]), rwkv_kk=row(p["rwkv_kk"][i]),
        rwkv_ka=row(p["rwkv_ka"][i]), rwkv_rk=row(p["rwkv_rk"][i]),
        rwkv_lnx_w=row(p["rwkv_lnx_w"][i]), rwkv_lnx_b=row(p["rwkv_lnx_b"][i]),
        conv_w=p["conv_w"][i], bd=bd,
        q_norm=tile_h(p["q_norm"][i], C_HEADS), k_norm=tile_h(p["k_norm"][i], C_KV_HEADS),
        idx_k_norm=row(idx_norm),
        w_pa=bf(p["w_pa"][i]), w_pb=bf(p["w_pb"][i]), w_pc=bf(p["w_pc"][i]), w_out=bf(p["w_out"][i]),
        ple_norm=p["ple_norm"][i], ple_gate=bf(p["ple_gate"][i]), ple_proj=bf(p["ple_proj"][i]),
    )


def _rwkv_branch(x, nb, seq, lw, wkv0, shift_rows, between=None):
    pre = _rwkv_pre(x, nb, seq, lw, shift_rows)
    s0 = wkv0.transpose(3, 2, 0, 1).reshape(HEAD_DIM, HEAD_DIM, nb * A_HEADS)
    other = None
    if between is not None:
        x, pre = lax.optimization_barrier((x, pre))
        other = between(x)
    r, w, k, v, na, nb_, g, bonus, shift_o = pre
    nbh = nb * A_HEADS
    if shift_rows is None:
        ts = lambda z: z.reshape(nbh, HEAD_DIM, seq).transpose(2, 1, 0)
    else:
        ts = lambda z: z.reshape(nb, seq, A_HEADS, HEAD_DIM).transpose(1, 3, 0, 2).reshape(seq, HEAD_DIM, nbh)
    seqs = [ts(z) for z in (r, w, k, v, na, nb_)]
    if between is not None:
        s0, other = lax.optimization_barrier((s0, other))
    o, s_fin = _scan(*seqs, s0)
    wkv1 = s_fin.reshape(HEAD_DIM, HEAD_DIM, nb, A_HEADS).transpose(2, 3, 1, 0)
    if shift_rows is None:
        shift1 = shift_o.reshape(nb, RWKV_COLS)
        o_tok = o.transpose(0, 2, 1).reshape(seq, nb * A_DIM)
    else:
        shift1 = shift_o.reshape(nb, seq, RWKV_COLS)[:, -1]
        o_tok = o.reshape(seq, HEAD_DIM, nb, A_HEADS).transpose(2, 0, 3, 1).reshape(nb * seq, A_DIM)
    return o_tok, bonus, g, shift1, wkv1, other


def _layer_tail(x1, o_scan, bonus, g, o_b, o_c, p_emb, lw, nb, seq):
    x2 = _merge(x1, o_scan, bonus, g, o_b, o_c, lw, nb, seq)
    return _ffn(x2, lw["ffn2_norm"], lw["ffn2_wg"], lw["ffn2_wu"], lw["ffn2_wd"],
                ple=(p_emb, lw["ple_norm"], lw["ple_gate"], lw["ple_proj"]))


def _prompt_layer(x, p_emb, nb, seq, lw, tabs):
    x1 = _ffn(x, lw["ffn1_norm"], lw["ffn1_wg"], lw["ffn1_wu"], lw["ffn1_wd"])
    wkv0 = jnp.zeros((nb, A_HEADS, HEAD_DIM, HEAD_DIM), F32)

    def other_mixers(xin):
        o_b, conv1 = _conv(xin, nb, seq, lw)
        q, k, qi, kw, kt, vt, kit = _attn_proj(xin, nb, seq, lw, tabs, False)
        return o_b, conv1, kt, vt, kit, _prompt_attention(q, k, vt, qi, kw, nb, seq)

    o_scan, bonus, g, shift1, wkv1, other = _rwkv_branch(x1, nb, seq, lw, wkv0, None, other_mixers)
    o_b, conv1, kt, vt, kit, o_c = other
    x4 = _layer_tail(x1, o_scan, bonus, g, o_b, o_c, p_emb, lw, nb, seq)
    heads = lambda z: z.reshape(nb, C_KV_HEADS, HEAD_DIM, seq).transpose(0, 3, 1, 2)
    st = (heads(kt), heads(vt), kit.transpose(0, 2, 1), wkv1, shift1, conv1)
    return x4, st


def _sample_layer(x, p_emb, nb, seq, lw, tabs, layer, cache_k, cache_v, cache_kidx, page_table,
                  wkv0, shift0, conv0):
    assert seq <= SROWS
    x1 = _ffn(x, lw["ffn1_norm"], lw["ffn1_wg"], lw["ffn1_wu"], lw["ffn1_wd"])
    rep_rows = lambda z: jnp.repeat(z, seq, axis=0)
    o_scan, bonus, g, shift1, wkv1, _ = _rwkv_branch(x1, nb, seq, lw, wkv0, rep_rows(shift0))
    o_b, z_all = _conv(x1, nb, seq, lw, init=(rep_rows(conv0[:, 0]), rep_rows(conv0[:, 1])))
    conv1 = z_all.reshape(nb, seq, B_DIM)[:, seq - (CONV_W - 1):]
    q, k, qi, kw, v, ki = _attn_proj(x1, nb, seq, lw, tabs, True)

    pad_t = lambda z: jnp.pad(z, ((0, 0), (0, SROWS - seq)) + ((0, 0),) * (z.ndim - 2))
    qh = pad_t(qi.reshape(nb, seq, IDX_HEADS, IDX_DIM)).transpose(0, 2, 1, 3)
    qh = qh.reshape(nb, IDX_HEADS * SROWS, IDX_DIM)
    wi = kw[:, IDX_DIM:IDX_DIM + IDX_HEADS].reshape(nb, seq, IDX_HEADS)
    wh = pad_t(wi).transpose(0, 2, 1).reshape(nb, IDX_HEADS * SROWS, 1)
    pad_keys = lambda z: jnp.pad(z.reshape(nb, seq, -1), ((0, 0), (0, LANES - seq), (0, 0)))
    sc_past, sc_new = _sample_index_scores(cache_kidx, page_table, layer, qh, wh, pad_keys(ki))
    past = sc_past.shape[-1]
    b_past, b_new = _sample_select(sc_past.reshape(nb * SROWS, past), sc_new.reshape(nb * SROWS, LANES), seq)

    rep = C_HEADS // C_KV_HEADS
    qg = pad_t(q.reshape(nb, seq, C_KV_HEADS, rep, HEAD_DIM)).transpose(0, 2, 3, 1, 4)
    qg = qg * (HEAD_DIM ** -0.5)
    eye = jnp.eye(C_KV_HEADS, dtype=F32)
    qbd = jnp.einsum("bgrtd,gh->bgrthd", qg, eye).reshape(nb, C_HEADS * SROWS, C_KV_DIM)
    o = _sample_attend(cache_k, cache_v, page_table, layer, qbd,
                       b_past.reshape(nb, SROWS, past), b_new.reshape(nb, SROWS, LANES),
                       pad_keys(k), pad_keys(v))
    o = o.reshape(nb, C_KV_HEADS, rep, SROWS, C_KV_HEADS, HEAD_DIM)
    o = jnp.einsum("bgrthd,gh->bgrtd", o, eye)[:, :, :, :seq]
    o_c = o.transpose(0, 3, 1, 2, 4).reshape(nb * seq, C_DIM)

    x4 = _layer_tail(x1, o_scan, bonus, g, o_b, o_c, p_emb, lw, 1, nb * seq)
    st = (k.reshape(nb, seq, C_KV_HEADS, HEAD_DIM), v.reshape(nb, seq, C_KV_HEADS, HEAD_DIM),
          ki.reshape(nb, seq, IDX_DIM), wkv1, shift1, conv1)
    return x4, st


def kernel(x_prompt, x_sample, cache_k, cache_v, cache_kidx, state_wkv, state_shift, state_conv, page_table, p_prompt, p_sample, ffn1_norm, ffn1_wg, ffn1_wu, ffn1_wd, mix_norm, w_in, rwkv_mu, rwkv_w0, rwkv_w2, rwkv_a0, rwkv_a2, rwkv_g2, rwkv_kk, rwkv_ka, rwkv_rk, rwkv_lnx_w, rwkv_lnx_b, conv_w, q_norm, k_norm, idx_k_norm, w_pa, w_pb, w_pc, w_out, ffn2_norm, ffn2_wg, ffn2_wu, ffn2_wd, ple_norm, ple_gate, ple_proj):
    params = dict(ffn1_norm=ffn1_norm, ffn1_wg=ffn1_wg, ffn1_wu=ffn1_wu, ffn1_wd=ffn1_wd,
                  mix_norm=mix_norm, w_in=w_in, rwkv_mu=rwkv_mu, rwkv_w0=rwkv_w0, rwkv_w2=rwkv_w2,
                  rwkv_a0=rwkv_a0, rwkv_a2=rwkv_a2, rwkv_g2=rwkv_g2, rwkv_kk=rwkv_kk, rwkv_ka=rwkv_ka,
                  rwkv_rk=rwkv_rk.reshape(rwkv_rk.shape[0], -1), rwkv_lnx_w=rwkv_lnx_w,
                  rwkv_lnx_b=rwkv_lnx_b, conv_w=conv_w, q_norm=q_norm, k_norm=k_norm,
                  idx_k_norm=idx_k_norm, w_pa=w_pa, w_pb=w_pb, w_pc=w_pc, w_out=w_out,
                  ffn2_norm=ffn2_norm, ffn2_wg=ffn2_wg, ffn2_wu=ffn2_wu, ffn2_wd=ffn2_wd,
                  ple_norm=ple_norm, ple_gate=ple_gate, ple_proj=ple_proj)
    nb, seq, d = x_prompt.shape
    db, dseq, _ = x_sample.shape
    depth = w_in.shape[0]
    past = page_table.shape[1] * cache_k.shape[2]
    tabs_p = _rope_tables(jnp.arange(seq, dtype=jnp.int32))
    tabs_s = tuple(jnp.tile(z, (db, 1)) for z in _rope_tables(past + jnp.arange(dseq, dtype=jnp.int32)))
    xp = x_prompt.reshape(nb * seq, d)
    xs = x_sample.reshape(db * dseq, d)
    outs_p, outs_s = [], []
    for i in range(depth):
        lw = _layer_weights(i, params)
        xp, st_p = _prompt_layer(xp, (p_prompt.reshape(depth * nb * seq, -1), i), nb, seq, lw, tabs_p)
        xs, st_s = _sample_layer(xs, (p_sample.reshape(depth * db * dseq, -1), i), db, dseq, lw, tabs_s, i,
                                 cache_k, cache_v, cache_kidx, page_table,
                                 state_wkv[i], state_shift[i], state_conv[i])
        outs_p.append(st_p)
        outs_s.append(st_s)
    k_p, v_p, kidx_p, wkv_p, shift_p, conv_p = [jnp.stack(z) for z in zip(*outs_p)]
    k_s, v_s, kidx_s, wkv_s, shift_s, conv_s = [jnp.stack(z) for z in zip(*outs_s)]
    return (xp.reshape(nb, seq, d), xs.reshape(db, dseq, d), k_p, v_p, kidx_p, wkv_p, shift_p, conv_p,
            k_s, v_s, kidx_s, wkv_s, shift_s, conv_s)
```

```python
import functools

import jax
import jax.numpy as jnp
import numpy as np
from jax import lax
from jax.experimental import pallas as pl
from jax.experimental.pallas import tpu as pltpu

F32 = jnp.float32
BF16 = jnp.bfloat16
I32 = jnp.int32

HEAD_DIM = 64
A_HEADS = 8
A_DIM = A_HEADS * HEAD_DIM
LORA_W = 64
LORA_A = 64
LORA_G = 128
B_DIM = 512
CONV_W = 3
C_HEADS = 8
C_KV_HEADS = 4
C_DIM = C_HEADS * HEAD_DIM
C_KV_DIM = C_KV_HEADS * HEAD_DIM
IDX_HEADS = 8
IDX_DIM = 64
TOPK_MAX = 256
ROT_DIM = HEAD_DIM // 4
ROT_HALF = ROT_DIM // 2
ROPE_THETA = 500000.0
N_BRANCH = 3
RMS_EPS = 1e-6
GN_EPS = 64e-5
RWKV_COLS = 3 * A_DIM + LORA_W + LORA_A + LORA_G
CONV_COLS = 3 * B_DIM
ATTN_COLS = C_DIM + 2 * C_KV_DIM + IDX_HEADS * IDX_DIM + IDX_DIM + IDX_HEADS
ATTN_COLS_PAD = 1664

LANES = 128
INT_MIN = -2 ** 31
NEG_BIG = -1e30
VMEM_LIMIT = 56 * 1024 * 1024
PAGES_PER_STEP = 32


def _pages_per_step(n_pages):
    npg = min(PAGES_PER_STEP, n_pages)
    while n_pages % npg:
        npg -= 1
    return npg


def _cparams(n_axes):
    return pltpu.CompilerParams(dimension_semantics=("arbitrary",) * n_axes,
                                vmem_limit_bytes=VMEM_LIMIT)


def _const_spec(shape):
    nd = len(shape)
    return pl.BlockSpec(shape, lambda *_: (0,) * nd, pipeline_mode=pl.Buffered(1))


def _tile(m, pref):
    t = min(m, pref)
    while m % t:
        t -= 8
    return t


def _rms(x, g):
    ms = jnp.mean(x * x, axis=-1, keepdims=True)
    return x * lax.rsqrt(ms + RMS_EPS) * g


def _dot(a, b):
    return jnp.dot(a, b, preferred_element_type=F32)


def _dot_nt(a, b):
    return lax.dot_general(a, b, (((1,), (1,)), ((), ())), preferred_element_type=F32)


def _segsum(x, bd, split=False):
    hi = x.astype(BF16)
    if not split:
        return _dot(hi, bd)
    lo = (x - hi.astype(F32)).astype(BF16)
    return _dot(hi, bd) + _dot(lo, bd)


def _rope(x, cos, sa, sb):
    n = x.shape[-1]
    return x * cos + pltpu.roll(x, n - ROT_HALF, 1) * sa + pltpu.roll(x, ROT_HALF, 1) * sb


def _rows_before(x, k, fills, *, seg_len, first_chunk_rows=None):
    rows = x.shape[0]
    y = pltpu.roll(x, k, 0)
    ridx = lax.broadcasted_iota(I32, (rows, 1), 0)
    t = ridx if seg_len is None else ridx % seg_len
    for r in range(k):
        y = jnp.where(t == r, fills[k - 1 - r], y)
    return y


ROW_BLOCK = 256


def _row_blocks(tm, seg_len=None):
    if seg_len is not None or tm % ROW_BLOCK:
        return [slice(0, tm)]
    return [slice(i, i + ROW_BLOCK) for i in range(0, tm, ROW_BLOCK)]


def _softplus(y):
    return jnp.maximum(y, 0.0) + jnp.log(1.0 + jnp.exp(-jnp.abs(y)))


def _ffn_kernel(*refs, chunks, ple):
    if ple:
        x_ref, g_ref, wg_ref, wu_ref, wd_ref, p_ref, pn_ref, pg_ref, pp_ref, o_ref = refs
    else:
        x_ref, g_ref, wg_ref, wu_ref, wd_ref, o_ref = refs
    x = x_ref[...]
    h = _rms(x, g_ref[...]).astype(BF16)
    acc = None
    for lo, hi in chunks:
        gt = _dot(h, wg_ref[:, lo:hi])
        ut = _dot(h, wu_ref[:, lo:hi])
        act = (gt * jax.nn.sigmoid(gt) * ut).astype(BF16)
        d = _dot(act, wd_ref[lo:hi, :])
        acc = d if acc is None else acc + d
    y = x + 0.5 * acc
    if ple:
        hg = _rms(y, pn_ref[...]).astype(BF16)
        gate = jax.nn.sigmoid(_dot(hg, pg_ref[...]))
        y = y + gate * _dot(p_ref[...].astype(BF16), pp_ref[...])
    o_ref[...] = y


def _ffn(x, g, wg, wu, wd, ple=None):
    m, d = x.shape
    f = wg.shape[1]
    tm = _tile(m, 512)
    step = 1024
    chunks = tuple((lo, min(lo + step, f)) for lo in range(0, f, step))
    row = lambda i: (i, 0)
    in_specs = [pl.BlockSpec((tm, d), row), _const_spec((1, d)), _const_spec((d, f)),
                _const_spec((d, f)), _const_spec((f, d))]
    args = [x, g.reshape(1, d), wg, wu, wd]
    if ple is not None:
        (p, layer), pn, pg, pp = ple
        off = layer * (m // tm)
        in_specs += [pl.BlockSpec((tm, p.shape[1]), lambda i: (i + off, 0)), _const_spec((1, d)),
                     _const_spec(pg.shape), _const_spec(pp.shape)]
        args += [p, pn.reshape(1, d), pg, pp]
    return pl.pallas_call(
        functools.partial(_ffn_kernel, chunks=chunks, ple=ple is not None),
        grid=(m // tm,), in_specs=in_specs, out_specs=pl.BlockSpec((tm, d), row),
        out_shape=jax.ShapeDtypeStruct((m, d), F32), compiler_params=_cparams(1),
        name="ffn_ple" if ple is not None else "ffn")(*args)


def _rwkv_pre_kernel(*refs, seg_len):
    (x_ref, g_ref, wa_ref, mu_ref, w0_ref, a0_ref, wwa_ref, g2_ref, kkw_ref, ka_ref,
     rk_ref, bd_ref) = refs[:12]
    rest = refs[12:]
    if seg_len is None:
        r_o, w_o, k_o, v_o, na_o, nb_o, g_o, bonus_o, shift_o, carry_ref = rest
    else:
        init_ref, r_o, w_o, k_o, v_o, na_o, nb_o, g_o, bonus_o, shift_o = rest
    tm = x_ref.shape[0]
    bd = bd_ref[...]
    if seg_len is None:
        @pl.when(pl.program_id(1) == 0)
        def _():
            carry_ref[...] = jnp.zeros_like(carry_ref)
        prev_row = carry_ref[0:1, :]
    for rows in _row_blocks(tm, seg_len):
        rb = rows.stop - rows.start
        h = _rms(x_ref[rows, :], g_ref[...]).astype(BF16)
        u = _dot(h, wa_ref[...])
        if seg_len is None:
            u_prev = _rows_before(u, 1, [prev_row], seg_len=None)
            prev_row = u[rb - 1:rb, :]
        else:
            u_prev = _rows_before(u, 1, [init_ref[rows, :]], seg_len=seg_len)
            shift_o[rows, :] = u
        us = u + (u_prev - u) * mu_ref[...]
        r = us[:, 0:A_DIM]
        k = us[:, A_DIM:2 * A_DIM]
        v = us[:, 2 * A_DIM:3 * A_DIM]
        o3 = 3 * A_DIM
        xwa = us[:, o3:o3 + LORA_W + LORA_A]
        lane = lax.broadcasted_iota(I32, xwa.shape, 1)
        xwa = jnp.where(lane < LORA_W, jnp.tanh(xwa), xwa)
        lo = _dot(xwa.astype(BF16), wwa_ref[...])
        w_log = -_softplus(-(w0_ref[...] + lo[:, :A_DIM])) - 0.5
        decay = jnp.exp(-jnp.exp(w_log))
        a = jax.nn.sigmoid(a0_ref[...] + lo[:, A_DIM:])
        xg = us[:, o3 + LORA_W + LORA_A:]
        g = _dot(jax.nn.sigmoid(xg).astype(BF16), g2_ref[...])
        kk = k * kkw_ref[...]
        kk = kk / jnp.maximum(jnp.sqrt(_segsum(kk * kk, bd)), 1e-12)
        k2 = k * (1.0 + (a - 1.0) * ka_ref[...])
        seq_outs = zip((r_o, w_o, k_o, v_o, na_o, nb_o), (r, decay, k2, v, -kk, kk * a))
        for o_ref, val in seq_outs:
            if seg_len is None:
                o_ref[:, rows] = jnp.transpose(val)
            else:
                o_ref[rows, :] = val
        g_o[rows, :] = g
        bonus_o[rows, :] = _segsum(r * k2 * rk_ref[...], bd) * v
    if seg_len is None:
        carry_ref[0:1, :] = prev_row
        shift_o[0] = prev_row


def _rwkv_pre(x, nb, seq, lw, init_rows=None):
    m, d = x.shape
    consts = [lw["mix_norm"], lw["w_a"], lw["rwkv_mu"], lw["rwkv_w0"], lw["rwkv_a0"], lw["w_wa"],
              lw["rwkv_g2"], lw["rwkv_kk"], lw["rwkv_ka"], lw["rwkv_rk"], lw["bd"]]
    const_specs = [_const_spec(c.shape) for c in consts]
    outs = [jax.ShapeDtypeStruct((m, A_DIM), F32)] * 8
    if init_rows is None:
        tm = _tile(seq, 512)
        nc = seq // tm
        row = lambda b, c: (b * nc + c, 0)
        grid = (nb, nc)
        in_specs = [pl.BlockSpec((tm, d), row)] + const_specs
        out_specs = ([pl.BlockSpec((A_DIM, tm), lambda b, c: (b, c))] * 6 + [pl.BlockSpec((tm, A_DIM), row)] * 2
                     + [pl.BlockSpec((1, 1, RWKV_COLS), lambda b, c: (b, 0, 0))])
        outs = ([jax.ShapeDtypeStruct((nb * A_DIM, seq), F32)] * 6 + outs[6:]
                + [jax.ShapeDtypeStruct((nb, 1, RWKV_COLS), F32)])
        scratch = [pltpu.VMEM((8, RWKV_COLS), F32)]
        args = [x] + consts
        seg_len = None
    else:
        tm = m
        row = lambda i: (0, 0)
        grid = (1,)
        in_specs = [pl.BlockSpec((tm, d), row)] + const_specs + [pl.BlockSpec((tm, RWKV_COLS), row)]
        out_specs = [pl.BlockSpec((tm, A_DIM), row)] * 8 + [pl.BlockSpec((tm, RWKV_COLS), row)]
        outs = outs + [jax.ShapeDtypeStruct((m, RWKV_COLS), F32)]
        scratch = []
        args = [x] + consts + [init_rows]
        seg_len = seq
    return pl.pallas_call(
        functools.partial(_rwkv_pre_kernel, seg_len=seg_len), grid=grid, in_specs=in_specs,
        out_specs=out_specs, out_shape=outs, scratch_shapes=scratch,
        compiler_params=_cparams(len(grid)), name="rwkv_pre")(*args)


def _scan_kernel(r_ref, w_ref, k_ref, v_ref, a_ref, b_ref, s0_ref, o_ref, st_ref, s_ref, *, tc):
    c = pl.program_id(1)

    @pl.when(c == 0)
    def _():
        s_ref[...] = s0_ref[...]

    jb = 32
    ln = s_ref.shape[-1]

    def step(t, carry):
        def sa_body(q, sa):
            for jj in range(jb):
                j = q * jb + jj
                sa = sa + s_ref[j] * a_ref[t, pl.ds(j, 1), :]
            return sa

        sa = lax.fori_loop(0, HEAD_DIM // jb, sa_body, jnp.zeros((HEAD_DIM, ln), F32))
        vt = v_ref[t]

        def up_body(q, o):
            for jj in range(jb):
                j = q * jb + jj
                sn = (s_ref[j] * w_ref[t, pl.ds(j, 1), :] + sa * b_ref[t, pl.ds(j, 1), :]
                      + vt * k_ref[t, pl.ds(j, 1), :])
                s_ref[j] = sn
                o = o + sn * r_ref[t, pl.ds(j, 1), :]
            return o

        o_ref[t] = lax.fori_loop(0, HEAD_DIM // jb, up_body, jnp.zeros((HEAD_DIM, ln), F32))
        return carry

    lax.fori_loop(0, tc, step, 0)

    @pl.when(c == pl.num_programs(1) - 1)
    def _():
        st_ref[...] = s_ref[...]


def _scan(r, w, k, v, a, b, s0):
    t_len, hd, nbh = r.shape
    ln = min(LANES, nbh)
    tc = _tile(t_len, 32) if t_len % 8 == 0 else t_len
    grid = (nbh // ln, t_len // tc)
    seq_spec = pl.BlockSpec((tc, hd, ln), lambda l, c: (c, 0, l))
    st_spec = pl.BlockSpec((hd, hd, ln), lambda l, c: (0, 0, l))
    return pl.pallas_call(
        functools.partial(_scan_kernel, tc=tc), grid=grid,
        in_specs=[seq_spec] * 6 + [st_spec], out_specs=[seq_spec, st_spec],
        out_shape=[jax.ShapeDtypeStruct((t_len, hd, nbh), F32),
                   jax.ShapeDtypeStruct((hd, hd, nbh), F32)],
        scratch_shapes=[pltpu.VMEM((hd, hd, ln), F32)],
        compiler_params=_cparams(2), name="rwkv_scan")(r, w, k, v, a, b, s0)


def _conv_kernel(*refs, seg_len):
    x_ref, g_ref, wb_ref, cw_ref = refs[:4]
    rest = refs[4:]
    if seg_len is None:
        o_ref, st_ref, carry_ref = rest
    else:
        i0_ref, i1_ref, o_ref, st_ref = rest
    tm = x_ref.shape[0]
    cw = cw_ref[...]
    if seg_len is None:
        @pl.when(pl.program_id(1) == 0)
        def _():
            carry_ref[...] = jnp.zeros_like(carry_ref)
        hist = [carry_ref[1:2, :], carry_ref[0:1, :]]
    for rows in _row_blocks(tm, seg_len):
        rb = rows.stop - rows.start
        h = _rms(x_ref[rows, :], g_ref[...]).astype(BF16)
        u = _dot(h, wb_ref[...])
        bg = u[:, :B_DIM]
        z = u[:, B_DIM:2 * B_DIM] * u[:, 2 * B_DIM:]
        if seg_len is None:
            z1 = _rows_before(z, 1, hist[:1], seg_len=None)
            z2 = _rows_before(z, 2, hist, seg_len=None)
            hist = [z[rb - 1:rb, :], z[rb - 2:rb - 1, :]]
        else:
            init = [i1_ref[rows, :], i0_ref[rows, :]]
            z1 = _rows_before(z, 1, init[:1], seg_len=seg_len)
            z2 = _rows_before(z, 2, init, seg_len=seg_len)
            st_ref[rows, :] = z
        o_ref[rows, :] = bg * (z2 * cw[0:1, :] + z1 * cw[1:2, :] + z * cw[2:3, :])
    if seg_len is None:
        last2 = jnp.concatenate([hist[1], hist[0]], axis=0)
        carry_ref[0:2, :] = last2
        st_ref[0] = last2


def _conv(x, nb, seq, lw, init=None):
    m, d = x.shape
    assert seq >= CONV_W - 1
    consts = [lw["mix_norm"], lw["w_b"], lw["conv_w"]]
    const_specs = [_const_spec(c.shape) for c in consts]
    if init is None:
        tm = _tile(seq, 512)
        nc = seq // tm
        row = lambda b, c: (b * nc + c, 0)
        grid = (nb, nc)
        in_specs = [pl.BlockSpec((tm, d), row)] + const_specs
        out_specs = [pl.BlockSpec((tm, B_DIM), row),
                     pl.BlockSpec((1, CONV_W - 1, B_DIM), lambda b, c: (b, 0, 0))]
        outs = [jax.ShapeDtypeStruct((m, B_DIM), F32), jax.ShapeDtypeStruct((nb, CONV_W - 1, B_DIM), F32)]
        scratch = [pltpu.VMEM((8, B_DIM), F32)]
        args = [x] + consts
        seg_len = None
    else:
        tm = m
        row = lambda i: (0, 0)
        grid = (1,)
        in_specs = [pl.BlockSpec((tm, d), row)] + const_specs + [pl.BlockSpec((tm, B_DIM), row)] * 2
        out_specs = [pl.BlockSpec((tm, B_DIM), row)] * 2
        outs = [jax.ShapeDtypeStruct((m, B_DIM), F32)] * 2
        scratch = []
        args = [x] + consts + list(init)
        seg_len = seq
    return pl.pallas_call(
        functools.partial(_conv_kernel, seg_len=seg_len), grid=grid, in_specs=in_specs,
        out_specs=out_specs, out_shape=outs, scratch_shapes=scratch,
        compiler_params=_cparams(len(grid)), name="shortconv")(*args)


def _attn_proj_kernel(x_ref, g_ref, wc_ref, qn_ref, kn_ref, in_ref, bd_ref, cos_ref, sa_ref, sb_ref,
                      *outs, transposed):
    bd = bd_ref[...]
    inv_hd = 1.0 / HEAD_DIM
    if transposed:
        q_o, k_o, qi_o, kw_o, kt_o, vt_o, kit_o = outs
    else:
        q_o, k_o, qi_o, kw_o, v_o, ki_o = outs
    for rows in _row_blocks(x_ref.shape[0], None if transposed else 1):
        h = _rms(x_ref[rows, :], g_ref[...]).astype(BF16)
        u = _dot(h, wc_ref[...])
        cos, sa, sb = cos_ref[rows, :], sa_ref[rows, :], sb_ref[rows, :]
        q = u[:, :C_DIM]
        q = q * lax.rsqrt(_segsum(q * q, bd) * inv_hd + RMS_EPS) * qn_ref[...]
        q = _rope(q, cos, sa, sb)
        k = u[:, C_DIM:C_DIM + C_KV_DIM]
        k = k * lax.rsqrt(_segsum(k * k, bd[:C_KV_DIM, :C_KV_DIM]) * inv_hd + RMS_EPS) * kn_ref[...]
        k = _rope(k, cos[:, :C_KV_DIM], sa[:, :C_KV_DIM], sb[:, :C_KV_DIM])
        o = C_DIM + C_KV_DIM
        v = u[:, o:o + C_KV_DIM]
        o += C_KV_DIM
        qi = _rope(u[:, o:o + IDX_HEADS * IDX_DIM], cos, sa, sb)
        o += IDX_HEADS * IDX_DIM
        kw = u[:, o:o + LANES]
        lane = lax.broadcasted_iota(I32, kw.shape, 1)
        is_ki = lane < IDX_DIM
        ms = jnp.sum(jnp.where(is_ki, kw * kw, 0.0), axis=-1, keepdims=True) * (1.0 / IDX_DIM)
        kin = _rope(kw * lax.rsqrt(ms + RMS_EPS) * in_ref[...], cos[:, :LANES], sa[:, :LANES], sb[:, :LANES])
        kw = jnp.where(is_ki, kin, kw)
        if transposed:
            kt_o[0, :, rows] = jnp.transpose(k)
            vt_o[0, :, rows] = jnp.transpose(v)
            kit_o[0, :, rows] = jnp.transpose(kw)[:IDX_DIM, :]
        else:
            v_o[rows, :] = v
            ki_o[rows, :] = kw[:, :IDX_DIM]
        q_o[rows, :] = q
        k_o[rows, :] = k
        qi_o[rows, :] = qi
        kw_o[rows, :] = kw


def _attn_proj(x, nb, seq, lw, tabs, sample):
    m, d = x.shape
    consts = [lw["mix_norm"], lw["w_c"], lw["q_norm"], lw["k_norm"], lw["idx_k_norm"], lw["bd"]]
    const_specs = [_const_spec(c.shape) for c in consts]
    if sample:
        tm, nc = m, 1
        grid = (1, 1)
    else:
        tm = _tile(seq, 512)
        nc = seq // tm
        grid = (nb, nc)
    row = lambda b, c: (b * nc + c, 0)
    tab = lambda b, c: (c, 0)
    widths = [C_DIM, C_KV_DIM, IDX_HEADS * IDX_DIM, LANES]
    out_specs = [pl.BlockSpec((tm, w), row) for w in widths]
    out_shape = [jax.ShapeDtypeStruct((m, w), F32) for w in widths]
    if sample:
        for w in (C_KV_DIM, IDX_DIM):
            out_specs.append(pl.BlockSpec((tm, w), row))
            out_shape.append(jax.ShapeDtypeStruct((m, w), F32))
    else:
        for w in (C_KV_DIM, C_KV_DIM, IDX_DIM):
            out_specs.append(pl.BlockSpec((1, w, tm), lambda b, c: (b, 0, c)))
            out_shape.append(jax.ShapeDtypeStruct((nb, w, seq), F32))
    return pl.pallas_call(
        functools.partial(_attn_proj_kernel, transposed=not sample), grid=grid,
        in_specs=[pl.BlockSpec((tm, d), row)] + const_specs + [pl.BlockSpec((tm, C_DIM), tab)] * 3,
        out_specs=out_specs, out_shape=out_shape,
        compiler_params=_cparams(2), name="attn_proj")(x, *consts, *tabs)


def _score_keys(score):
    bits = pltpu.bitcast(score, I32)
    return jnp.where(bits < 0, bits ^ 0x7FFFFFFF, bits)


def _lane_total(acc):
    return jnp.broadcast_to(jnp.sum(acc, axis=-1, keepdims=True), acc.shape)


def _tree(parts, op):
    while len(parts) > 1:
        nxt = [op(parts[i], parts[i + 1]) for i in range(0, len(parts) - 1, 2)]
        parts = nxt + (parts[-1:] if len(parts) % 2 else [])
    return parts[0]


def _fold8(x, op):
    return _tree([x[i * 8:(i + 1) * 8] for i in range(x.shape[0] // 8)], op)


def _kth_key(count_fn, topk, shape):
    def vbit(it, acc):
        cand = acc | jnp.left_shift(jnp.int32(1), 31 - it)
        cmp = cand ^ INT_MIN
        cnt = count_fn(lambda kk, pos: jnp.where(kk >= cmp, 1, 0))
        return jnp.where(cnt >= topk, cand, acc)

    return lax.fori_loop(0, 32, vbit, jnp.zeros(shape, I32)) ^ INT_MIN


def _select_topk(count_fn, topk, pos_bits, shape, lp_ref):
    zeros = jnp.zeros(shape, I32)
    thr = _kth_key(count_fn, topk, shape)
    n_ge = count_fn(lambda kk, pos: jnp.where(kk >= thr, 1, 0))
    excess = jnp.where(thr == INT_MIN, 0, n_ge - topk)
    lp_ref[...] = jnp.full(shape, 2 ** 31 - 1, I32)

    @pl.when(jnp.max(excess) > 0)
    def _():
        need = topk - count_fn(lambda kk, pos: jnp.where(kk > thr, 1, 0))

        def pbit(it, acc):
            cand = acc | jnp.left_shift(jnp.int32(1), pos_bits - 1 - it)
            cnt = count_fn(lambda kk, pos: jnp.where(kk == thr, jnp.where(pos < cand, 1, 0), 0))
            return jnp.where(cnt < need, cand, acc)

        lp_ref[...] = lax.fori_loop(0, pos_bits, pbit, zeros)

    return thr, lp_ref[...]


def _select_bias(kk, pos, thr, last_pos):
    tie = jnp.where(pos <= last_pos, 0.0, NEG_BIG)
    bias = jnp.where(kk == thr, tie, jnp.where(kk > thr, 0.0, NEG_BIG))
    return jnp.where(kk == INT_MIN, NEG_BIG, bias)


def _pattn_kernel(qi_ref, kwq_ref, q_ref, kwk_ref, k_ref, vt_ref, o_ref,
                  keys_ref, s_ref, qip_ref, qg_ref, acc_ref, m_ref, l_ref, ties_ref, *, tq, topk):
    qb = pl.program_id(1)
    nch = qb + 1
    nslab = tq // 8
    shape8 = (8, tq)
    sub8 = lax.broadcasted_iota(I32, shape8, 0)
    lane_q = lax.broadcasted_iota(I32, (tq, LANES), 1)
    key_in = lax.broadcasted_iota(I32, (tq, tq), 0)
    qry_in = lax.broadcasted_iota(I32, (tq, tq), 1)

    wt = jnp.transpose(kwq_ref[...]) * ((IDX_DIM ** -0.5) * (IDX_HEADS ** -0.5))
    for hh in range(IDX_HEADS):
        slab = qi_ref[:, (hh // 2) * LANES:(hh // 2 + 1) * LANES]
        if hh % 2:
            slab = pltpu.roll(slab, IDX_DIM, 1)
        qip_ref[hh] = jnp.where(lane_q < IDX_DIM, slab, 0.0).astype(BF16)

    def for_chunks(body):
        odd = nch % 2

        @pl.when(odd == 1)
        def _():
            body(0)

        def pair(p, carry):
            kc = odd + 2 * p
            body(kc)
            body(kc + 1)
            return carry

        lax.fori_loop(0, nch // 2, pair, 0)

    def idx_body(kc):
        start = pl.multiple_of(kc * tq, tq)
        ks = kwk_ref[pl.ds(start, tq), :].astype(BF16)
        acc = None
        for hh in range(IDX_HEADS):
            s = _dot_nt(ks, qip_ref[hh])
            term = jnp.maximum(s, 0.0) * wt[IDX_DIM + hh:IDX_DIM + hh + 1, :]
            acc = term if acc is None else acc + term
        kk = jnp.where(acc == 0.0, 0, _score_keys(acc))
        keys_ref[kc] = jnp.where(kc * tq + key_in <= qb * tq + qry_in, kk, INT_MIN)

    for_chunks(idx_body)

    def count_fn(ind):
        def body(kc, acc):
            base = kc * tq
            parts = [ind(keys_ref[kc, i * 8:(i + 1) * 8, :], base + i * 8 + sub8) for i in range(nslab)]
            return acc + _tree(parts, jnp.add)
        acc = lax.fori_loop(0, nch, body, jnp.zeros(shape8, I32))
        return jnp.broadcast_to(jnp.sum(acc, axis=0, keepdims=True), shape8)

    thr = _kth_key(count_fn, topk, shape8)
    need = topk - count_fn(lambda kk, pos: jnp.where(kk > thr, 1, 0))

    def chunk_bias(kc, ties_before):
        slabs = []
        for i in range(nslab):
            kk = keys_ref[kc, i * 8:(i + 1) * 8, :]
            tie = jnp.where(kk == thr, 1, 0)
            incl = tie
            for sh in (1, 2, 4):
                incl = incl + jnp.where(sub8 >= sh, pltpu.roll(incl, sh, 0), 0)
            slabs.append((kk, incl - tie, jnp.broadcast_to(incl[7:8, :], shape8)))
        out = []
        for kk, excl, total in slabs:
            rank = ties_before + excl
            bias = jnp.where(kk > thr, 0.0, NEG_BIG)
            bias = jnp.where(kk == thr, jnp.where(rank < need, 0.0, NEG_BIG), bias)
            out.append(jnp.where(kk == INT_MIN, NEG_BIG, bias))
            ties_before = ties_before + total
        return jnp.concatenate(out, axis=0), ties_before

    scale = (HEAD_DIM ** -0.5) * 1.4426950408889634
    for g in range(C_KV_HEADS):
        qslab = q_ref[:, g * LANES:(g + 1) * LANES] * scale
        rolled = pltpu.roll(qslab, HEAD_DIM, 1)
        koff = g % 2
        in_half = (lane_q >= HEAD_DIM) if koff else (lane_q < HEAD_DIM)
        qg_ref[g] = jnp.concatenate(
            [jnp.where(in_half, qslab if r == koff else rolled, 0.0) for r in range(2)],
            axis=0).astype(BF16)

    m_ref[...] = jnp.full(m_ref.shape, NEG_BIG, F32)
    ties_ref[...] = jnp.zeros(shape8, I32)

    def max_body(kc):
        start = pl.multiple_of(kc * tq, tq)
        bias, ties = chunk_bias(kc, ties_ref[...])
        ties_ref[...] = ties
        bias2 = jnp.concatenate([bias, bias], axis=1)
        for g in range(C_KV_HEADS):
            kcol = (g // 2) * LANES
            kch = k_ref[pl.ds(start, tq), kcol:kcol + LANES].astype(BF16)
            s = _dot_nt(kch, qg_ref[g]) + bias2
            s_ref[kc, :, g * 2 * tq:(g + 1) * 2 * tq] = s
            m_ref[g] = jnp.maximum(m_ref[g], _fold8(s, jnp.maximum))

    for_chunks(max_body)
    m_row = [jnp.max(m_ref[g], axis=0, keepdims=True) for g in range(C_KV_HEADS)]

    acc_ref[...] = jnp.zeros_like(acc_ref)
    l_ref[...] = jnp.zeros_like(l_ref)

    def sum_body(kc):
        start = pl.multiple_of(kc * tq, tq)
        for g in range(C_KV_HEADS):
            p = jnp.exp2(s_ref[kc, :, g * 2 * tq:(g + 1) * 2 * tq] - m_row[g])
            vt = vt_ref[0, g * HEAD_DIM:(g + 1) * HEAD_DIM, pl.ds(start, tq)].astype(BF16)
            acc_ref[g] += _dot(vt, p.astype(BF16))
            l_ref[g] += _fold8(p, jnp.add)

    for_chunks(sum_body)
    for g in range(C_KV_HEADS):
        out_t = acc_ref[g] / jnp.sum(l_ref[g], axis=0, keepdims=True)
        o_ref[:, g * LANES:(g + 1) * LANES] = jnp.transpose(
            jnp.concatenate([out_t[:, :tq], out_t[:, tq:]], axis=0))


def _prompt_attention(q, k, vt, qi, kw, nb, seq):
    assert C_HEADS == 2 * C_KV_HEADS
    topk = min(TOPK_MAX, seq // 4)
    tq = _tile(seq, 256)
    assert tq % LANES == 0
    nq = seq // tq
    qrow = lambda b, c: (b * nq + c, 0)
    krow = lambda b, c: (b, 0)
    return pl.pallas_call(
        functools.partial(_pattn_kernel, tq=tq, topk=topk),
        grid=(nb, nq),
        in_specs=[pl.BlockSpec((tq, IDX_HEADS * IDX_DIM), qrow), pl.BlockSpec((tq, LANES), qrow),
                  pl.BlockSpec((tq, C_DIM), qrow), pl.BlockSpec((seq, LANES), krow),
                  pl.BlockSpec((seq, C_KV_DIM), krow),
                  pl.BlockSpec((1, C_KV_DIM, seq), lambda b, c: (b, 0, 0))],
        out_specs=pl.BlockSpec((tq, C_DIM), qrow),
        out_shape=jax.ShapeDtypeStruct((nb * seq, C_DIM), F32),
        scratch_shapes=[pltpu.VMEM((nq, tq, tq), I32),
                        pltpu.VMEM((nq, tq, C_KV_HEADS * 2 * tq), F32),
                        pltpu.VMEM((IDX_HEADS, tq, LANES), BF16),
                        pltpu.VMEM((C_KV_HEADS, 2 * tq, LANES), BF16),
                        pltpu.VMEM((C_KV_HEADS, HEAD_DIM, 2 * tq), F32),
                        pltpu.VMEM((C_KV_HEADS, 8, 2 * tq), F32),
                        pltpu.VMEM((C_KV_HEADS, 8, 2 * tq), F32), pltpu.VMEM((8, tq), I32)],
        compiler_params=_cparams(2), name="prompt_attn")(qi, kw, q, kw, k, vt)


SROWS = 8


def _sidx_kernel(pt_ref, qh_ref, wh_ref, kn_ref, *rest, npg, page):
    pages = rest[:npg]
    past_o, new_o = rest[npg:]
    qh = qh_ref[0].astype(BF16)
    wh = wh_ref[0] * ((IDX_DIM ** -0.5) * (IDX_HEADS ** -0.5))

    def head_sum(s):
        s = jnp.maximum(s, 0.0) * wh
        return _tree([s[hh * SROWS:(hh + 1) * SROWS] for hh in range(IDX_HEADS)], jnp.add)

    keys_t = jnp.concatenate([pg[0] for pg in pages], axis=1).astype(BF16)
    past_o[0] = head_sum(_dot(qh, keys_t))

    @pl.when(pl.program_id(1) == 0)
    def _():
        new_o[0] = head_sum(_dot_nt(qh, kn_ref[0].astype(BF16)))


def _sample_index_scores(cache_kidx, page_table, layer, qh, wh, ki_new):
    depth, n_phys, page, _ = cache_kidx.shape
    db, n_pages = page_table.shape
    npg = _pages_per_step(n_pages)
    cache = cache_kidx.transpose(0, 1, 3, 2).reshape(depth * n_phys, IDX_DIM, page)
    base = layer * n_phys

    def page_spec(i):
        return pl.BlockSpec((1, IDX_DIM, page), lambda b, s, pt: (base + pt[b, s * npg + i], 0, 0))

    per_b = lambda b, s, pt: (b, 0, 0)
    grid_spec = pltpu.PrefetchScalarGridSpec(
        num_scalar_prefetch=1, grid=(db, n_pages // npg),
        in_specs=[pl.BlockSpec((1, IDX_HEADS * SROWS, IDX_DIM), per_b),
                  pl.BlockSpec((1, IDX_HEADS * SROWS, 1), per_b),
                  pl.BlockSpec((1, LANES, IDX_DIM), per_b)] + [page_spec(i) for i in range(npg)],
        out_specs=[pl.BlockSpec((1, SROWS, npg * page), lambda b, s, pt: (b, 0, s)),
                   pl.BlockSpec((1, SROWS, LANES), per_b)])
    return pl.pallas_call(
        functools.partial(_sidx_kernel, npg=npg, page=page), grid_spec=grid_spec,
        out_shape=[jax.ShapeDtypeStruct((db, SROWS, n_pages * page), F32),
                   jax.ShapeDtypeStruct((db, SROWS, LANES), F32)],
        compiler_params=_cparams(2), name="sample_index")(page_table, qh, wh, ki_new, *([cache] * npg))


def _ssel_kernel(past_ref, new_ref, bpast_o, bnew_o, keys_ref, lp_ref, *, topk, pos_bits, n_new):
    rows, past = past_ref.shape
    nch = past // LANES
    lane = lax.broadcasted_iota(I32, (rows, LANES), 1)
    t_row = lax.broadcasted_iota(I32, (rows, LANES), 0) % SROWS

    def fill(c, carry):
        start = pl.multiple_of(c * LANES, LANES)
        keys_ref[c] = _score_keys(past_ref[:, pl.ds(start, LANES)])
        return carry

    lax.fori_loop(0, nch, fill, 0)
    new_ok = lane <= jnp.minimum(t_row, n_new - 1)
    keys_ref[nch] = jnp.where(new_ok, _score_keys(new_ref[...]), INT_MIN)

    grp = 8
    assert nch % grp == 0

    def count_fn(pred):
        def body(q, acc):
            c0 = q * grp
            return acc + _tree([pred(keys_ref[c0 + i], (c0 + i) * LANES + lane) for i in range(grp)], jnp.add)
        acc = lax.fori_loop(0, nch // grp, body, jnp.zeros((rows, LANES), I32))
        return _lane_total(acc + pred(keys_ref[nch], nch * LANES + lane))

    thr, last_pos = _select_topk(count_fn, topk, pos_bits, (rows, LANES), lp_ref)

    def emit(c, carry):
        start = pl.multiple_of(c * LANES, LANES)
        bpast_o[:, pl.ds(start, LANES)] = _select_bias(keys_ref[c], c * LANES + lane, thr, last_pos)
        return carry

    lax.fori_loop(0, nch, emit, 0)
    bnew_o[...] = _select_bias(keys_ref[nch], nch * LANES + lane, thr, last_pos)


def _sample_select(sc_past, sc_new, n_new):
    rows, past = sc_past.shape
    topk = min(TOPK_MAX, (past + n_new) // 4)
    tr = _tile(rows, 64)
    pos_bits = int(past + LANES - 1).bit_length()
    row = lambda i: (i, 0)
    return pl.pallas_call(
        functools.partial(_ssel_kernel, topk=topk, pos_bits=pos_bits, n_new=n_new),
        grid=(rows // tr,),
        in_specs=[pl.BlockSpec((tr, past), row), pl.BlockSpec((tr, LANES), row)],
        out_specs=[pl.BlockSpec((tr, past), row), pl.BlockSpec((tr, LANES), row)],
        out_shape=[jax.ShapeDtypeStruct((rows, past), F32), jax.ShapeDtypeStruct((rows, LANES), F32)],
        scratch_shapes=[pltpu.VMEM((past // LANES + 1, tr, LANES), I32), pltpu.VMEM((tr, LANES), I32)],
        compiler_params=_cparams(1), name="sample_select")(sc_past, sc_new)


def _satt_kernel(pt_ref, qbd_ref, bpast_ref, bnew_ref, kn_ref, vn_ref, *rest, npg, page):
    kpages = rest[:npg]
    vpages = rest[npg:2 * npg]
    o_ref, m_ref, l_ref, acc_ref = rest[2 * npg:]
    s_id = pl.program_id(1)
    nrow = qbd_ref.shape[1]
    reps = nrow // SROWS

    @pl.when(s_id == 0)
    def _():
        m_ref[...] = jnp.full_like(m_ref, NEG_BIG)
        l_ref[...] = jnp.zeros_like(l_ref)
        acc_ref[...] = jnp.zeros_like(acc_ref)

    qbd = qbd_ref[0].astype(BF16)

    def update(s, bias, pv):
        s = s + jnp.concatenate([bias] * reps, axis=0)
        m_old = m_ref[...]
        m_new = jnp.maximum(m_old, jnp.max(s, axis=-1, keepdims=True))
        p = jnp.exp(s - m_new)
        alpha = jnp.exp(m_old - m_new)
        l_ref[...] = alpha * l_ref[...] + jnp.sum(p, axis=-1, keepdims=True)
        acc_ref[...] = alpha * acc_ref[...] + pv(p.astype(BF16))
        m_ref[...] = m_new

    keys_t = jnp.concatenate([kp[0] for kp in kpages], axis=1).astype(BF16)
    vals_t = jnp.concatenate([vp[0] for vp in vpages], axis=1).astype(BF16)
    update(_dot(qbd, keys_t), bpast_ref[0], lambda p: _dot_nt(p, vals_t))

    @pl.when(s_id == pl.num_programs(1) - 1)
    def _():
        vn = vn_ref[0].astype(BF16)
        update(_dot_nt(qbd, kn_ref[0].astype(BF16)), bnew_ref[0], lambda p: _dot(p, vn))
        o_ref[0] = acc_ref[...] / l_ref[...]


def _sample_attend(cache_k, cache_v, page_table, layer, qbd, bias_past, bias_new, k_new, v_new):
    depth, n_phys, page = cache_k.shape[:3]
    db, n_pages = page_table.shape
    npg = _pages_per_step(n_pages)
    ck = cache_k.transpose(0, 1, 3, 4, 2).reshape(depth * n_phys, C_KV_DIM, page)
    cv = cache_v.transpose(0, 1, 3, 4, 2).reshape(depth * n_phys, C_KV_DIM, page)
    base = layer * n_phys
    nrow = qbd.shape[1]

    def page_spec(i):
        return pl.BlockSpec((1, C_KV_DIM, page), lambda b, s, pt: (base + pt[b, s * npg + i], 0, 0))

    per_b = lambda b, s, pt: (b, 0, 0)
    grid_spec = pltpu.PrefetchScalarGridSpec(
        num_scalar_prefetch=1, grid=(db, n_pages // npg),
        in_specs=[pl.BlockSpec((1, nrow, C_KV_DIM), per_b),
                  pl.BlockSpec((1, SROWS, npg * page), lambda b, s, pt: (b, 0, s)),
                  pl.BlockSpec((1, SROWS, LANES), per_b),
                  pl.BlockSpec((1, LANES, C_KV_DIM), per_b),
                  pl.BlockSpec((1, LANES, C_KV_DIM), per_b)]
                 + [page_spec(i) for i in range(npg)] * 2,
        out_specs=pl.BlockSpec((1, nrow, C_KV_DIM), per_b),
        scratch_shapes=[pltpu.VMEM((nrow, 1), F32), pltpu.VMEM((nrow, 1), F32),
                        pltpu.VMEM((nrow, C_KV_DIM), F32)])
    return pl.pallas_call(
        functools.partial(_satt_kernel, npg=npg, page=page), grid_spec=grid_spec,
        out_shape=jax.ShapeDtypeStruct((db, nrow, C_KV_DIM), F32),
        compiler_params=_cparams(2), name="sample_attn")(
            page_table, qbd, bias_past, bias_new, k_new, v_new, *([ck] * npg), *([cv] * npg))


def _merge_kernel(x_ref, g_ref, wg_ref, os_ref, bon_ref, gg_ref, lw_ref, lb_ref, bd_ref,
                  ob_ref, oc_ref, wpa_ref, wpb_ref, wpc_ref, wo_ref, o_ref):
    x = x_ref[...]
    d = x.shape[1]
    h = _rms(x, g_ref[...]).astype(BF16)
    gates = jax.nn.sigmoid(_dot(h, wg_ref[...]))
    bd = bd_ref[...]
    o = os_ref[...]
    mean = _segsum(o, bd, split=True) * (1.0 / HEAD_DIM)
    cen = o - mean
    var = _segsum(cen * cen, bd) * (1.0 / HEAD_DIM)
    on = cen * lax.rsqrt(var + GN_EPS) * lw_ref[...] + lb_ref[...]
    oa = ((on + bon_ref[...]) * gg_ref[...]).astype(BF16)
    merged = (gates[:, :d] * _dot(oa, wpa_ref[...])
              + gates[:, d:2 * d] * _dot(ob_ref[...].astype(BF16), wpb_ref[...])
              + gates[:, 2 * d:] * _dot(oc_ref[...].astype(BF16), wpc_ref[...]))
    o_ref[...] = x + _dot(merged.astype(BF16), wo_ref[...])


def _merge(x, o_scan, bonus, g, o_b, o_c, lw, nb, seq):
    m, d = x.shape
    tm = _tile(seq, 512)
    nc = seq // tm
    row = lambda b, c: (b * nc + c, 0)
    tok = lambda w: pl.BlockSpec((tm, w), row)
    consts_a = [lw["mix_norm"], lw["w_g"]]
    consts_b = [lw["rwkv_lnx_w"], lw["rwkv_lnx_b"], lw["bd"]]
    consts_c = [lw["w_pa"], lw["w_pb"], lw["w_pc"], lw["w_out"]]
    in_specs = ([tok(d)] + [_const_spec(c.shape) for c in consts_a]
                + [pl.BlockSpec((tm, A_DIM), lambda b, c: (c, b))] + [tok(A_DIM)] * 2
                + [_const_spec(c.shape) for c in consts_b] + [tok(B_DIM), tok(C_DIM)]
                + [_const_spec(c.shape) for c in consts_c])
    return pl.pallas_call(
        _merge_kernel, grid=(nb, nc), in_specs=in_specs, out_specs=tok(d),
        out_shape=jax.ShapeDtypeStruct((m, d), F32), compiler_params=_cparams(2), name="merge")(
            x, *consts_a, o_scan, bonus, g, *consts_b, o_b, o_c, *consts_c)


def _rope_tables(pos):
    inv = ROPE_THETA ** (-jnp.arange(ROT_HALF, dtype=F32) / ROT_HALF)
    ang = pos.astype(F32)[:, None] * inv[None, :]
    c, s = jnp.cos(ang), jnp.sin(ang)
    t = pos.shape[0]
    pad = jnp.zeros((t, HEAD_DIM - ROT_DIM), F32)
    zer = jnp.zeros((t, ROT_HALF), F32)
    cos = jnp.concatenate([c, c, pad + 1.0], axis=1)
    sa = jnp.concatenate([-s, zer, pad], axis=1)
    sb = jnp.concatenate([zer, s, pad], axis=1)
    return tuple(jnp.tile(z, (1, C_HEADS)) for z in (cos, sa, sb))


def _layer_weights(i, p):
    d = p["w_in"].shape[1]
    w_in = p["w_in"][i]
    o1 = RWKV_COLS
    o2 = o1 + CONV_COLS
    o3 = o2 + ATTN_COLS
    row = lambda v: v.reshape(1, -1)
    tile_h = lambda v, n: jnp.tile(v, n).reshape(1, -1)
    z = jnp.zeros((LORA_W, A_DIM), F32)
    w_wa = jnp.concatenate([jnp.concatenate([p["rwkv_w2"][i], z], axis=1),
                            jnp.concatenate([z, p["rwkv_a2"][i]], axis=1)], axis=0)
    head = np.arange(A_DIM) // HEAD_DIM
    bd = jnp.asarray(head[:, None] == head[None, :], BF16)
    idx_norm = jnp.concatenate([p["idx_k_norm"][i], jnp.zeros((LANES - IDX_DIM,), F32)])
    bf = lambda w: w.astype(BF16)
    return dict(
        ffn1_norm=p["ffn1_norm"][i], ffn1_wg=bf(p["ffn1_wg"][i]), ffn1_wu=bf(p["ffn1_wu"][i]),
        ffn1_wd=bf(p["ffn1_wd"][i]),
        ffn2_norm=p["ffn2_norm"][i], ffn2_wg=bf(p["ffn2_wg"][i]), ffn2_wu=bf(p["ffn2_wu"][i]),
        ffn2_wd=bf(p["ffn2_wd"][i]),
        mix_norm=row(p["mix_norm"][i]),
        w_a=bf(w_in[:, :o1]), w_b=bf(w_in[:, o1:o2]),
        w_c=bf(jnp.pad(w_in[:, o2:o3], ((0, 0), (0, ATTN_COLS_PAD - ATTN_COLS)))),
        w_g=bf(w_in[:, o3:]),
        rwkv_mu=row(p["rwkv_mu"][i]), rwkv_w0=row(p["rwkv_w0"][i]), rwkv_a0=row(p["rwkv_a0"][i]),
        w_wa=bf(w_wa), rwkv_g2=bf(p["rwkv_g2"][i]), rwkv_kk=row(p["rwkv_kk"][i]),
        rwkv_ka=row(p["rwkv_ka"][i]), rwkv_rk=row(p["rwkv_rk"][i]),
        rwkv_lnx_w=row(p["rwkv_lnx_w"][i]), rwkv_lnx_b=row(p["rwkv_lnx_b"][i]),
        conv_w=p["conv_w"][i], bd=bd,
        q_norm=tile_h(p["q_norm"][i], C_HEADS), k_norm=tile_h(p["k_norm"][i], C_KV_HEADS),
        idx_k_norm=row(idx_norm),
        w_pa=bf(p["w_pa"][i]), w_pb=bf(p["w_pb"][i]), w_pc=bf(p["w_pc"][i]), w_out=bf(p["w_out"][i]),
        ple_norm=p["ple_norm"][i], ple_gate=bf(p["ple_gate"][i]), ple_proj=bf(p["ple_proj"][i]),
    )


def _rwkv_branch(x, nb, seq, lw, wkv0, shift_rows, between=None):
    pre = _rwkv_pre(x, nb, seq, lw, shift_rows)
    s0 = wkv0.transpose(3, 2, 0, 1).reshape(HEAD_DIM, HEAD_DIM, nb * A_HEADS)
    other = None
    if between is not None:
        x, pre = lax.optimization_barrier((x, pre))
        other = between(x)
    r, w, k, v, na, nb_, g, bonus, shift_o = pre
    nbh = nb * A_HEADS
    if shift_rows is None:
        ts = lambda z: z.reshape(nbh, HEAD_DIM, seq).transpose(2, 1, 0)
    else:
        ts = lambda z: z.reshape(nb, seq, A_HEADS, HEAD_DIM).transpose(1, 3, 0, 2).reshape(seq, HEAD_DIM, nbh)
    seqs = [ts(z) for z in (r, w, k, v, na, nb_)]
    if between is not None:
        s0, other = lax.optimization_barrier((s0, other))
    o, s_fin = _scan(*seqs, s0)
    wkv1 = s_fin.reshape(HEAD_DIM, HEAD_DIM, nb, A_HEADS).transpose(2, 3, 1, 0)
    if shift_rows is None:
        shift1 = shift_o.reshape(nb, RWKV_COLS)
        o_tok = o.transpose(0, 2, 1).reshape(seq, nb * A_DIM)
    else:
        shift1 = shift_o.reshape(nb, seq, RWKV_COLS)[:, -1]
        o_tok = o.reshape(seq, HEAD_DIM, nb, A_HEADS).transpose(2, 0, 3, 1).reshape(nb * seq, A_DIM)
    return o_tok, bonus, g, shift1, wkv1, other


def _layer_tail(x1, o_scan, bonus, g, o_b, o_c, p_emb, lw, nb, seq):
    x2 = _merge(x1, o_scan, bonus, g, o_b, o_c, lw, nb, seq)
    return _ffn(x2, lw["ffn2_norm"], lw["ffn2_wg"], lw["ffn2_wu"], lw["ffn2_wd"],
                ple=(p_emb, lw["ple_norm"], lw["ple_gate"], lw["ple_proj"]))


def _prompt_layer(x, p_emb, nb, seq, lw, tabs):
    x1 = _ffn(x, lw["ffn1_norm"], lw["ffn1_wg"], lw["ffn1_wu"], lw["ffn1_wd"])
    wkv0 = jnp.zeros((nb, A_HEADS, HEAD_DIM, HEAD_DIM), F32)

    def other_mixers(xin):
        o_b, conv1 = _conv(xin, nb, seq, lw)
        q, k, qi, kw, kt, vt, kit = _attn_proj(xin, nb, seq, lw, tabs, False)
        return o_b, conv1, kt, vt, kit, _prompt_attention(q, k, vt, qi, kw, nb, seq)

    o_scan, bonus, g, shift1, wkv1, other = _rwkv_branch(x1, nb, seq, lw, wkv0, None, other_mixers)
    o_b, conv1, kt, vt, kit, o_c = other
    x4 = _layer_tail(x1, o_scan, bonus, g, o_b, o_c, p_emb, lw, nb, seq)
    heads = lambda z: z.reshape(nb, C_KV_HEADS, HEAD_DIM, seq).transpose(0, 3, 1, 2)
    st = (heads(kt), heads(vt), kit.transpose(0, 2, 1), wkv1, shift1, conv1)
    return x4, st


def _sample_layer(x, p_emb, nb, seq, lw, tabs, layer, cache_k, cache_v, cache_kidx, page_table,
                  wkv0, shift0, conv0):
    assert seq <= SROWS
    x1 = _ffn(x, lw["ffn1_norm"], lw["ffn1_wg"], lw["ffn1_wu"], lw["ffn1_wd"])
    rep_rows = lambda z: jnp.repeat(z, seq, axis=0)
    o_scan, bonus, g, shift1, wkv1, _ = _rwkv_branch(x1, nb, seq, lw, wkv0, rep_rows(shift0))
    o_b, z_all = _conv(x1, nb, seq, lw, init=(rep_rows(conv0[:, 0]), rep_rows(conv0[:, 1])))
    conv1 = z_all.reshape(nb, seq, B_DIM)[:, seq - (CONV_W - 1):]
    q, k, qi, kw, v, ki = _attn_proj(x1, nb, seq, lw, tabs, True)

    pad_t = lambda z: jnp.pad(z, ((0, 0), (0, SROWS - seq)) + ((0, 0),) * (z.ndim - 2))
    qh = pad_t(qi.reshape(nb, seq, IDX_HEADS, IDX_DIM)).transpose(0, 2, 1, 3)
    qh = qh.reshape(nb, IDX_HEADS * SROWS, IDX_DIM)
    wi = kw[:, IDX_DIM:IDX_DIM + IDX_HEADS].reshape(nb, seq, IDX_HEADS)
    wh = pad_t(wi).transpose(0, 2, 1).reshape(nb, IDX_HEADS * SROWS, 1)
    pad_keys = lambda z: jnp.pad(z.reshape(nb, seq, -1), ((0, 0), (0, LANES - seq), (0, 0)))
    sc_past, sc_new = _sample_index_scores(cache_kidx, page_table, layer, qh, wh, pad_keys(ki))
    past = sc_past.shape[-1]
    b_past, b_new = _sample_select(sc_past.reshape(nb * SROWS, past), sc_new.reshape(nb * SROWS, LANES), seq)

    rep = C_HEADS // C_KV_HEADS
    qg = pad_t(q.reshape(nb, seq, C_KV_HEADS, rep, HEAD_DIM)).transpose(0, 2, 3, 1, 4)
    qg = qg * (HEAD_DIM ** -0.5)
    eye = jnp.eye(C_KV_HEADS, dtype=F32)
    qbd = jnp.einsum("bgrtd,gh->bgrthd", qg, eye).reshape(nb, C_HEADS * SROWS, C_KV_DIM)
    o = _sample_attend(cache_k, cache_v, page_table, layer, qbd,
                       b_past.reshape(nb, SROWS, past), b_new.reshape(nb, SROWS, LANES),
                       pad_keys(k), pad_keys(v))
    o = o.reshape(nb, C_KV_HEADS, rep, SROWS, C_KV_HEADS, HEAD_DIM)
    o = jnp.einsum("bgrthd,gh->bgrtd", o, eye)[:, :, :, :seq]
    o_c = o.transpose(0, 3, 1, 2, 4).reshape(nb * seq, C_DIM)

    x4 = _layer_tail(x1, o_scan, bonus, g, o_b, o_c, p_emb, lw, 1, nb * seq)
    st = (k.reshape(nb, seq, C_KV_HEADS, HEAD_DIM), v.reshape(nb, seq, C_KV_HEADS, HEAD_DIM),
          ki.reshape(nb, seq, IDX_DIM), wkv1, shift1, conv1)
    return x4, st


def kernel(x_prompt, x_sample, cache_k, cache_v, cache_kidx, state_wkv, state_shift, state_conv, page_table, p_prompt, p_sample, ffn1_norm, ffn1_wg, ffn1_wu, ffn1_wd, mix_norm, w_in, rwkv_mu, rwkv_w0, rwkv_w2, rwkv_a0, rwkv_a2, rwkv_g2, rwkv_kk, rwkv_ka, rwkv_rk, rwkv_lnx_w, rwkv_lnx_b, conv_w, q_norm, k_norm, idx_k_norm, w_pa, w_pb, w_pc, w_out, ffn2_norm, ffn2_wg, ffn2_wu, ffn2_wd, ple_norm, ple_gate, ple_proj):
    params = dict(ffn1_norm=ffn1_norm, ffn1_wg=ffn1_wg, ffn1_wu=ffn1_wu, ffn1_wd=ffn1_wd,
                  mix_norm=mix_norm, w_in=w_in, rwkv_mu=rwkv_mu, rwkv_w0=rwkv_w0, rwkv_w2=rwkv_w2,
                  rwkv_a0=rwkv_a0, rwkv_a2=rwkv_a2, rwkv_g2=rwkv_g2, rwkv_kk=rwkv_kk, rwkv_ka=rwkv_ka,
                  rwkv_rk=rwkv_rk.reshape(rwkv_rk.shape[0], -1), rwkv_lnx_w=rwkv_lnx_w,
                  rwkv_lnx_b=rwkv_lnx_b, conv_w=conv_w, q_norm=q_norm, k_norm=k_norm,
                  idx_k_norm=idx_k_norm, w_pa=w_pa, w_pb=w_pb, w_pc=w_pc, w_out=w_out,
                  ffn2_norm=ffn2_norm, ffn2_wg=ffn2_wg, ffn2_wu=ffn2_wu, ffn2_wd=ffn2_wd,
                  ple_norm=ple_norm, ple_gate=ple_gate, ple_proj=ple_proj)
    nb, seq, d = x_prompt.shape
    db, dseq, _ = x_sample.shape
    depth = w_in.shape[0]
    past = page_table.shape[1] * cache_k.shape[2]
    tabs_p = _rope_tables(jnp.arange(seq, dtype=jnp.int32))
    tabs_s = tuple(jnp.tile(z, (db, 1)) for z in _rope_tables(past + jnp.arange(dseq, dtype=jnp.int32)))
    xp = x_prompt.reshape(nb * seq, d)
    xs = x_sample.reshape(db * dseq, d)
    outs_p, outs_s = [], []
    for i in range(depth):
        lw = _layer_weights(i, params)
        xp, st_p = _prompt_layer(xp, (p_prompt.reshape(depth * nb * seq, -1), i), nb, seq, lw, tabs_p)
        xs, st_s = _sample_layer(xs, (p_sample.reshape(depth * db * dseq, -1), i), db, dseq, lw, tabs_s, i,
                                 cache_k, cache_v, cache_kidx, page_table,
                                 state_wkv[i], state_shift[i], state_conv[i])
        outs_p.append(st_p)
        outs_s.append(st_s)
    k_p, v_p, kidx_p, wkv_p, shift_p, conv_p = [jnp.stack(z) for z in zip(*outs_p)]
    k_s, v_s, kidx_s, wkv_s, shift_s, conv_s = [jnp.stack(z) for z in zip(*outs_s)]
    return (xp.reshape(nb, seq, d), xs.reshape(db, dseq, d), k_p, v_p, kidx_p, wkv_p, shift_p, conv_p,
            k_s, v_s, kidx_s, wkv_s, shift_s, conv_s)
```

```python
import functools

import jax
import jax.numpy as jnp
import numpy as np
from jax import lax
from jax.experimental import pallas as pl
from jax.experimental.pallas import tpu as pltpu

F32 = jnp.float32
BF16 = jnp.bfloat16
I32 = jnp.int32

HEAD_DIM = 64
A_HEADS = 8
A_DIM = A_HEADS * HEAD_DIM
LORA_W = 64
LORA_A = 64
LORA_G = 128
B_DIM = 512
CONV_W = 3
C_HEADS = 8
C_KV_HEADS = 4
C_DIM = C_HEADS * HEAD_DIM
C_KV_DIM = C_KV_HEADS * HEAD_DIM
IDX_HEADS = 8
IDX_DIM = 64
TOPK_MAX = 256
ROT_DIM = HEAD_DIM // 4
ROT_HALF = ROT_DIM // 2
ROPE_THETA = 500000.0
N_BRANCH = 3
RMS_EPS = 1e-6
GN_EPS = 64e-5
RWKV_COLS = 3 * A_DIM + LORA_W + LORA_A + LORA_G
CONV_COLS = 3 * B_DIM
ATTN_COLS = C_DIM + 2 * C_KV_DIM + IDX_HEADS * IDX_DIM + IDX_DIM + IDX_HEADS
ATTN_COLS_PAD = 1664

LANES = 128
INT_MIN = -2 ** 31
NEG_BIG = -1e30
VMEM_LIMIT = 56 * 1024 * 1024
PAGES_PER_STEP = 32


def _pages_per_step(n_pages):
    npg = min(PAGES_PER_STEP, n_pages)
    while n_pages % npg:
        npg -= 1
    return npg


def _cparams(n_axes):
    return pltpu.CompilerParams(dimension_semantics=("arbitrary",) * n_axes,
                                vmem_limit_bytes=VMEM_LIMIT)


def _const_spec(shape):
    nd = len(shape)
    return pl.BlockSpec(shape, lambda *_: (0,) * nd, pipeline_mode=pl.Buffered(1))


def _tile(m, pref):
    t = min(m, pref)
    while m % t:
        t -= 8
    return t


def _rms(x, g):
    ms = jnp.mean(x * x, axis=-1, keepdims=True)
    return x * lax.rsqrt(ms + RMS_EPS) * g


def _dot(a, b):
    return jnp.dot(a, b, preferred_element_type=F32)


def _dot_nt(a, b):
    return lax.dot_general(a, b, (((1,), (1,)), ((), ())), preferred_element_type=F32)


def _segsum(x, bd, split=False):
    hi = x.astype(BF16)
    if not split:
        return _dot(hi, bd)
    lo = (x - hi.astype(F32)).astype(BF16)
    return _dot(hi, bd) + _dot(lo, bd)


def _rope(x, cos, sa, sb):
    n = x.shape[-1]
    return x * cos + pltpu.roll(x, n - ROT_HALF, 1) * sa + pltpu.roll(x, ROT_HALF, 1) * sb


def _rows_before(x, k, fills, *, seg_len, first_chunk_rows=None):
    rows = x.shape[0]
    y = pltpu.roll(x, k, 0)
    ridx = lax.broadcasted_iota(I32, (rows, 1), 0)
    t = ridx if seg_len is None else ridx % seg_len
    for r in range(k):
        y = jnp.where(t == r, fills[k - 1 - r], y)
    return y


ROW_BLOCK = 256


def _row_blocks(tm, seg_len=None, blk=ROW_BLOCK):
    if seg_len is not None or tm % blk:
        return [slice(0, tm)]
    return [slice(i, i + blk) for i in range(0, tm, blk)]


def _softplus(y):
    return jnp.maximum(y, 0.0) + jnp.log(1.0 + jnp.exp(-jnp.abs(y)))


def _ffn_kernel(*refs, chunks, ple):
    if ple:
        x_ref, g_ref, wg_ref, wu_ref, wd_ref, p_ref, pn_ref, pg_ref, pp_ref, o_ref = refs
    else:
        x_ref, g_ref, wg_ref, wu_ref, wd_ref, o_ref = refs
    x = x_ref[...]
    h = _rms(x, g_ref[...]).astype(BF16)
    acc = None
    for lo, hi in chunks:
        gt = _dot(h, wg_ref[:, lo:hi])
        ut = _dot(h, wu_ref[:, lo:hi])
        act = (gt * jax.nn.sigmoid(gt) * ut).astype(BF16)
        d = _dot(act, wd_ref[lo:hi, :])
        acc = d if acc is None else acc + d
    y = x + 0.5 * acc
    if ple:
        hg = _rms(y, pn_ref[...]).astype(BF16)
        gate = jax.nn.sigmoid(_dot(hg, pg_ref[...]))
        y = y + gate * _dot(p_ref[...].astype(BF16), pp_ref[...])
    o_ref[...] = y


def _ffn(x, g, wg, wu, wd, ple=None):
    m, d = x.shape
    f = wg.shape[1]
    tm = _tile(m, 512)
    step = 1024
    chunks = tuple((lo, min(lo + step, f)) for lo in range(0, f, step))
    row = lambda i: (i, 0)
    in_specs = [pl.BlockSpec((tm, d), row), _const_spec((1, d)), _const_spec((d, f)),
                _const_spec((d, f)), _const_spec((f, d))]
    args = [x, g.reshape(1, d), wg, wu, wd]
    if ple is not None:
        (p, layer), pn, pg, pp = ple
        off = layer * (m // tm)
        in_specs += [pl.BlockSpec((tm, p.shape[1]), lambda i: (i + off, 0)), _const_spec((1, d)),
                     _const_spec(pg.shape), _const_spec(pp.shape)]
        args += [p, pn.reshape(1, d), pg, pp]
    return pl.pallas_call(
        functools.partial(_ffn_kernel, chunks=chunks, ple=ple is not None),
        grid=(m // tm,), in_specs=in_specs, out_specs=pl.BlockSpec((tm, d), row),
        out_shape=jax.ShapeDtypeStruct((m, d), F32), compiler_params=_cparams(1),
        name="ffn_ple" if ple is not None else "ffn")(*args)


def _rwkv_pre_kernel(*refs, seg_len):
    (x_ref, g_ref, wa_ref, mu_ref, w0_ref, a0_ref, wwa_ref, g2_ref, kkw_ref, ka_ref,
     rk_ref, bd_ref) = refs[:12]
    rest = refs[12:]
    if seg_len is None:
        r_o, w_o, k_o, v_o, na_o, nb_o, g_o, bonus_o, shift_o, carry_ref = rest
    else:
        init_ref, r_o, w_o, k_o, v_o, na_o, nb_o, g_o, bonus_o, shift_o = rest
    tm = x_ref.shape[0]
    bd = bd_ref[...]
    if seg_len is None:
        @pl.when(pl.program_id(1) == 0)
        def _():
            carry_ref[...] = jnp.zeros_like(carry_ref)
        prev_row = carry_ref[0:1, :]
    for rows in _row_blocks(tm, seg_len):
        rb = rows.stop - rows.start
        h = _rms(x_ref[rows, :], g_ref[...]).astype(BF16)
        u = _dot(h, wa_ref[...])
        if seg_len is None:
            u_prev = _rows_before(u, 1, [prev_row], seg_len=None)
            prev_row = u[rb - 1:rb, :]
        else:
            u_prev = _rows_before(u, 1, [init_ref[rows, :]], seg_len=seg_len)
            shift_o[rows, :] = u
        us = u + (u_prev - u) * mu_ref[...]
        r = us[:, 0:A_DIM]
        k = us[:, A_DIM:2 * A_DIM]
        v = us[:, 2 * A_DIM:3 * A_DIM]
        o3 = 3 * A_DIM
        xwa = us[:, o3:o3 + LORA_W + LORA_A]
        lane = lax.broadcasted_iota(I32, xwa.shape, 1)
        xwa = jnp.where(lane < LORA_W, jnp.tanh(xwa), xwa)
        lo = _dot(xwa.astype(BF16), wwa_ref[...])
        w_log = -_softplus(-(w0_ref[...] + lo[:, :A_DIM])) - 0.5
        log_decay = -jnp.exp(w_log)
        a = jax.nn.sigmoid(a0_ref[...] + lo[:, A_DIM:])
        xg = us[:, o3 + LORA_W + LORA_A:]
        g = _dot(jax.nn.sigmoid(xg).astype(BF16), g2_ref[...])
        kk = k * kkw_ref[...]
        kk = kk / jnp.maximum(jnp.sqrt(_segsum(kk * kk, bd)), 1e-12)
        k2 = k * (1.0 + (a - 1.0) * ka_ref[...])
        seq_outs = zip((r_o, w_o, k_o, v_o, na_o, nb_o), (r, log_decay, k2, v, -kk, kk * a))
        for o_ref, val in seq_outs:
            if seg_len is None:
                o_ref[:, rows] = jnp.transpose(val)
            else:
                o_ref[rows, :] = val
        g_o[rows, :] = g
        bonus_o[rows, :] = _segsum(r * k2 * rk_ref[...], bd) * v
    if seg_len is None:
        carry_ref[0:1, :] = prev_row
        shift_o[0] = prev_row


def _rwkv_pre(x, nb, seq, lw, init_rows=None):
    m, d = x.shape
    consts = [lw["mix_norm"], lw["w_a"], lw["rwkv_mu"], lw["rwkv_w0"], lw["rwkv_a0"], lw["w_wa"],
              lw["rwkv_g2"], lw["rwkv_kk"], lw["rwkv_ka"], lw["rwkv_rk"], lw["bd"]]
    const_specs = [_const_spec(c.shape) for c in consts]
    outs = [jax.ShapeDtypeStruct((m, A_DIM), F32)] * 8
    if init_rows is None:
        tm = _tile(seq, 512)
        nc = seq // tm
        row = lambda b, c: (b * nc + c, 0)
        grid = (nb, nc)
        in_specs = [pl.BlockSpec((tm, d), row)] + const_specs
        out_specs = ([pl.BlockSpec((A_DIM, tm), lambda b, c: (b, c))] * 6 + [pl.BlockSpec((tm, A_DIM), row)] * 2
                     + [pl.BlockSpec((1, 1, RWKV_COLS), lambda b, c: (b, 0, 0))])
        outs = ([jax.ShapeDtypeStruct((nb * A_DIM, seq), F32)] * 6 + outs[6:]
                + [jax.ShapeDtypeStruct((nb, 1, RWKV_COLS), F32)])
        scratch = [pltpu.VMEM((8, RWKV_COLS), F32)]
        args = [x] + consts
        seg_len = None
    else:
        tm = m
        row = lambda i: (0, 0)
        grid = (1,)
        in_specs = [pl.BlockSpec((tm, d), row)] + const_specs + [pl.BlockSpec((tm, RWKV_COLS), row)]
        out_specs = [pl.BlockSpec((tm, A_DIM), row)] * 8 + [pl.BlockSpec((tm, RWKV_COLS), row)]
        outs = outs + [jax.ShapeDtypeStruct((m, RWKV_COLS), F32)]
        scratch = []
        args = [x] + consts + [init_rows]
        seg_len = seq
    return pl.pallas_call(
        functools.partial(_rwkv_pre_kernel, seg_len=seg_len), grid=grid, in_specs=in_specs,
        out_specs=out_specs, out_shape=outs, scratch_shapes=scratch,
        compiler_params=_cparams(len(grid)), name="rwkv_pre")(*args)


def _scan_kernel(r_ref, lw_ref, k_ref, v_ref, a_ref, b_ref, s0_ref, o_ref, st_ref, s_ref, vec_ref, *, tc):
    c = pl.program_id(1)

    @pl.when(c == 0)
    def _():
        s_ref[...] = s0_ref[...]

    jb = 32
    ln = s_ref.shape[-1]

    def step(t, carry):
        cum, gam_prev = carry
        cum = cum + lw_ref[t]
        gam = jnp.exp(cum)
        ginv = jnp.exp(-cum)
        vec_ref[0] = a_ref[t] * gam_prev
        vec_ref[1] = b_ref[t] * ginv
        vec_ref[2] = k_ref[t] * ginv
        vec_ref[3] = r_ref[t] * gam

        def sa_body(q, sa):
            for jj in range(jb):
                j = q * jb + jj
                sa = sa + s_ref[j] * vec_ref[0, pl.ds(j, 1), :]
            return sa

        sa = lax.fori_loop(0, HEAD_DIM // jb, sa_body, jnp.zeros((HEAD_DIM, ln), F32))
        vt = v_ref[t]

        def up_body(q, o):
            for jj in range(jb):
                j = q * jb + jj
                sn = s_ref[j] + sa * vec_ref[1, pl.ds(j, 1), :] + vt * vec_ref[2, pl.ds(j, 1), :]
                s_ref[j] = sn
                o = o + sn * vec_ref[3, pl.ds(j, 1), :]
            return o

        o_ref[t] = lax.fori_loop(0, HEAD_DIM // jb, up_body, jnp.zeros((HEAD_DIM, ln), F32))
        return cum, gam

    zeros = jnp.zeros((HEAD_DIM, ln), F32)
    _, gam_end = lax.fori_loop(0, tc, step, (zeros, zeros + 1.0))
    vec_ref[0] = gam_end
    for j in range(HEAD_DIM):
        s_ref[j] = s_ref[j] * vec_ref[0, j:j + 1, :]

    @pl.when(c == pl.num_programs(1) - 1)
    def _():
        st_ref[...] = s_ref[...]


def _scan(r, w, k, v, a, b, s0):
    t_len, hd, nbh = r.shape
    ln = min(LANES, nbh)
    tc = _tile(t_len, 32) if t_len % 8 == 0 else t_len
    grid = (nbh // ln, t_len // tc)
    seq_spec = pl.BlockSpec((tc, hd, ln), lambda l, c: (c, 0, l))
    st_spec = pl.BlockSpec((hd, hd, ln), lambda l, c: (0, 0, l))
    return pl.pallas_call(
        functools.partial(_scan_kernel, tc=tc), grid=grid,
        in_specs=[seq_spec] * 6 + [st_spec], out_specs=[seq_spec, st_spec],
        out_shape=[jax.ShapeDtypeStruct((t_len, hd, nbh), F32),
                   jax.ShapeDtypeStruct((hd, hd, nbh), F32)],
        scratch_shapes=[pltpu.VMEM((hd, hd, ln), F32), pltpu.VMEM((4, hd, ln), F32)],
        compiler_params=_cparams(2), name="rwkv_scan")(r, w, k, v, a, b, s0)


def _conv_kernel(*refs, seg_len):
    x_ref, g_ref, wb_ref, cw_ref = refs[:4]
    rest = refs[4:]
    if seg_len is None:
        o_ref, st_ref, carry_ref = rest
    else:
        i0_ref, i1_ref, o_ref, st_ref = rest
    tm = x_ref.shape[0]
    cw = cw_ref[...]
    if seg_len is None:
        @pl.when(pl.program_id(1) == 0)
        def _():
            carry_ref[...] = jnp.zeros_like(carry_ref)
        hist = [carry_ref[1:2, :], carry_ref[0:1, :]]
    for rows in _row_blocks(tm, seg_len, ROW_BLOCK // 2):
        rb = rows.stop - rows.start
        h = _rms(x_ref[rows, :], g_ref[...]).astype(BF16)
        u = _dot(h, wb_ref[...])
        bg = u[:, :B_DIM]
        z = u[:, B_DIM:2 * B_DIM] * u[:, 2 * B_DIM:]
        if seg_len is None:
            z1 = _rows_before(z, 1, hist[:1], seg_len=None)
            z2 = _rows_before(z, 2, hist, seg_len=None)
            hist = [z[rb - 1:rb, :], z[rb - 2:rb - 1, :]]
        else:
            init = [i1_ref[rows, :], i0_ref[rows, :]]
            z1 = _rows_before(z, 1, init[:1], seg_len=seg_len)
            z2 = _rows_before(z, 2, init, seg_len=seg_len)
            st_ref[rows, :] = z
        o_ref[rows, :] = bg * (z2 * cw[0:1, :] + z1 * cw[1:2, :] + z * cw[2:3, :])
    if seg_len is None:
        last2 = jnp.concatenate([hist[1], hist[0]], axis=0)
        carry_ref[0:2, :] = last2
        st_ref[0] = last2


def _conv(x, nb, seq, lw, init=None):
    m, d = x.shape
    assert seq >= CONV_W - 1
    consts = [lw["mix_norm"], lw["w_b"], lw["conv_w"]]
    const_specs = [_const_spec(c.shape) for c in consts]
    if init is None:
        tm = _tile(seq, 512)
        nc = seq // tm
        row = lambda b, c: (b * nc + c, 0)
        grid = (nb, nc)
        in_specs = [pl.BlockSpec((tm, d), row)] + const_specs
        out_specs = [pl.BlockSpec((tm, B_DIM), row),
                     pl.BlockSpec((1, CONV_W - 1, B_DIM), lambda b, c: (b, 0, 0))]
        outs = [jax.ShapeDtypeStruct((m, B_DIM), F32), jax.ShapeDtypeStruct((nb, CONV_W - 1, B_DIM), F32)]
        scratch = [pltpu.VMEM((8, B_DIM), F32)]
        args = [x] + consts
        seg_len = None
    else:
        tm = m
        row = lambda i: (0, 0)
        grid = (1,)
        in_specs = [pl.BlockSpec((tm, d), row)] + const_specs + [pl.BlockSpec((tm, B_DIM), row)] * 2
        out_specs = [pl.BlockSpec((tm, B_DIM), row)] * 2
        outs = [jax.ShapeDtypeStruct((m, B_DIM), F32)] * 2
        scratch = []
        args = [x] + consts + list(init)
        seg_len = seq
    return pl.pallas_call(
        functools.partial(_conv_kernel, seg_len=seg_len), grid=grid, in_specs=in_specs,
        out_specs=out_specs, out_shape=outs, scratch_shapes=scratch,
        compiler_params=_cparams(len(grid)), name="shortconv")(*args)


def _attn_proj_kernel(x_ref, g_ref, wc_ref, qn_ref, kn_ref, in_ref, bd_ref, cos_ref, sa_ref, sb_ref,
                      *outs, transposed):
    bd = bd_ref[...]
    inv_hd = 1.0 / HEAD_DIM
    if transposed:
        q_o, k_o, qi_o, kw_o, kt_o, vt_o, kit_o = outs
    else:
        q_o, k_o, qi_o, kw_o, v_o, ki_o = outs
    for rows in _row_blocks(x_ref.shape[0], None if transposed else 1):
        h = _rms(x_ref[rows, :], g_ref[...]).astype(BF16)
        u = _dot(h, wc_ref[...])
        cos, sa, sb = cos_ref[rows, :], sa_ref[rows, :], sb_ref[rows, :]
        q = u[:, :C_DIM]
        q = q * lax.rsqrt(_segsum(q * q, bd) * inv_hd + RMS_EPS) * qn_ref[...]
        q = _rope(q, cos, sa, sb)
        k = u[:, C_DIM:C_DIM + C_KV_DIM]
        k = k * lax.rsqrt(_segsum(k * k, bd[:C_KV_DIM, :C_KV_DIM]) * inv_hd + RMS_EPS) * kn_ref[...]
        k = _rope(k, cos[:, :C_KV_DIM], sa[:, :C_KV_DIM], sb[:, :C_KV_DIM])
        o = C_DIM + C_KV_DIM
        v = u[:, o:o + C_KV_DIM]
        o += C_KV_DIM
        qi = _rope(u[:, o:o + IDX_HEADS * IDX_DIM], cos, sa, sb)
        o += IDX_HEADS * IDX_DIM
        kw = u[:, o:o + LANES]
        lane = lax.broadcasted_iota(I32, kw.shape, 1)
        is_ki = lane < IDX_DIM
        ms = jnp.sum(jnp.where(is_ki, kw * kw, 0.0), axis=-1, keepdims=True) * (1.0 / IDX_DIM)
        kin = _rope(kw * lax.rsqrt(ms + RMS_EPS) * in_ref[...], cos[:, :LANES], sa[:, :LANES], sb[:, :LANES])
        kw = jnp.where(is_ki, kin, kw)
        if transposed:
            kt_o[0, :, rows] = jnp.transpose(k)
            vt_o[0, :, rows] = jnp.transpose(v)
            kit_o[0, :, rows] = jnp.transpose(kw)[:IDX_DIM, :]
        else:
            v_o[rows, :] = v
            ki_o[rows, :] = kw[:, :IDX_DIM]
        q_o[rows, :] = q
        k_o[rows, :] = k
        qi_o[rows, :] = qi
        kw_o[rows, :] = kw


def _attn_proj(x, nb, seq, lw, tabs, sample):
    m, d = x.shape
    consts = [lw["mix_norm"], lw["w_c"], lw["q_norm"], lw["k_norm"], lw["idx_k_norm"], lw["bd"]]
    const_specs = [_const_spec(c.shape) for c in consts]
    if sample:
        tm, nc = m, 1
        grid = (1, 1)
    else:
        tm = _tile(seq, 512)
        nc = seq // tm
        grid = (nb, nc)
    row = lambda b, c: (b * nc + c, 0)
    tab = lambda b, c: (c, 0)
    widths = [C_DIM, C_KV_DIM, IDX_HEADS * IDX_DIM, LANES]
    out_specs = [pl.BlockSpec((tm, w), row) for w in widths]
    out_shape = [jax.ShapeDtypeStruct((m, w), F32) for w in widths]
    if sample:
        for w in (C_KV_DIM, IDX_DIM):
            out_specs.append(pl.BlockSpec((tm, w), row))
            out_shape.append(jax.ShapeDtypeStruct((m, w), F32))
    else:
        for w in (C_KV_DIM, C_KV_DIM, IDX_DIM):
            out_specs.append(pl.BlockSpec((1, w, tm), lambda b, c: (b, 0, c)))
            out_shape.append(jax.ShapeDtypeStruct((nb, w, seq), F32))
    return pl.pallas_call(
        functools.partial(_attn_proj_kernel, transposed=not sample), grid=grid,
        in_specs=[pl.BlockSpec((tm, d), row)] + const_specs + [pl.BlockSpec((tm, C_DIM), tab)] * 3,
        out_specs=out_specs, out_shape=out_shape,
        compiler_params=_cparams(2), name="attn_proj")(x, *consts, *tabs)


def _score_keys(score):
    bits = pltpu.bitcast(score, I32)
    return jnp.where(bits < 0, bits ^ 0x7FFFFFFF, bits)


def _lane_total(acc):
    return jnp.broadcast_to(jnp.sum(acc, axis=-1, keepdims=True), acc.shape)


def _tree(parts, op):
    while len(parts) > 1:
        nxt = [op(parts[i], parts[i + 1]) for i in range(0, len(parts) - 1, 2)]
        parts = nxt + (parts[-1:] if len(parts) % 2 else [])
    return parts[0]


def _fold8(x, op):
    return _tree([x[i * 8:(i + 1) * 8] for i in range(x.shape[0] // 8)], op)


def _kth_key(count_fn, topk, shape):
    def vbit(it, acc):
        cand = acc | jnp.left_shift(jnp.int32(1), 31 - it)
        cmp = cand ^ INT_MIN
        cnt = count_fn(lambda kk, pos: jnp.where(kk >= cmp, 1, 0))
        return jnp.where(cnt >= topk, cand, acc)

    return lax.fori_loop(0, 32, vbit, jnp.zeros(shape, I32)) ^ INT_MIN


def _select_topk(count_fn, topk, pos_bits, shape, lp_ref):
    zeros = jnp.zeros(shape, I32)
    thr = _kth_key(count_fn, topk, shape)
    n_ge = count_fn(lambda kk, pos: jnp.where(kk >= thr, 1, 0))
    excess = jnp.where(thr == INT_MIN, 0, n_ge - topk)
    lp_ref[...] = jnp.full(shape, 2 ** 31 - 1, I32)

    @pl.when(jnp.max(excess) > 0)
    def _():
        need = topk - count_fn(lambda kk, pos: jnp.where(kk > thr, 1, 0))

        def pbit(it, acc):
            cand = acc | jnp.left_shift(jnp.int32(1), pos_bits - 1 - it)
            cnt = count_fn(lambda kk, pos: jnp.where(kk == thr, jnp.where(pos < cand, 1, 0), 0))
            return jnp.where(cnt < need, cand, acc)

        lp_ref[...] = lax.fori_loop(0, pos_bits, pbit, zeros)

    return thr, lp_ref[...]


def _select_bias(kk, pos, thr, last_pos):
    tie = jnp.where(pos <= last_pos, 0.0, NEG_BIG)
    bias = jnp.where(kk == thr, tie, jnp.where(kk > thr, 0.0, NEG_BIG))
    return jnp.where(kk == INT_MIN, NEG_BIG, bias)


def _pattn_kernel(qi_ref, kwq_ref, q_ref, kwk_ref, k_ref, vt_ref, o_ref,
                  keys_ref, s_ref, qip_ref, qg_ref, acc_ref, m_ref, l_ref, ties_ref, *, tq, topk):
    qb = pl.program_id(1)
    nch = qb + 1
    nslab = tq // 8
    shape8 = (8, tq)
    sub8 = lax.broadcasted_iota(I32, shape8, 0)
    lane_q = lax.broadcasted_iota(I32, (tq, LANES), 1)
    key_in = lax.broadcasted_iota(I32, (tq, tq), 0)
    qry_in = lax.broadcasted_iota(I32, (tq, tq), 1)

    wt = jnp.transpose(kwq_ref[...]) * ((IDX_DIM ** -0.5) * (IDX_HEADS ** -0.5))
    for hh in range(IDX_HEADS):
        slab = qi_ref[:, (hh // 2) * LANES:(hh // 2 + 1) * LANES]
        if hh % 2:
            slab = pltpu.roll(slab, IDX_DIM, 1)
        qip_ref[hh] = jnp.where(lane_q < IDX_DIM, slab, 0.0).astype(BF16)

    def for_chunks(body):
        odd = nch % 2

        @pl.when(odd == 1)
        def _():
            body(0)

        def pair(p, carry):
            kc = odd + 2 * p
            body(kc)
            body(kc + 1)
            return carry

        lax.fori_loop(0, nch // 2, pair, 0)

    def idx_body(kc):
        start = pl.multiple_of(kc * tq, tq)
        ks = kwk_ref[pl.ds(start, tq), :].astype(BF16)
        acc = None
        for hh in range(IDX_HEADS):
            s = _dot_nt(ks, qip_ref[hh])
            term = jnp.maximum(s, 0.0) * wt[IDX_DIM + hh:IDX_DIM + hh + 1, :]
            acc = term if acc is None else acc + term
        kk = jnp.where(acc == 0.0, 0, _score_keys(acc))
        keys_ref[kc] = jnp.where(kc * tq + key_in <= qb * tq + qry_in, kk, INT_MIN)

    for_chunks(idx_body)

    def count_fn(ind):
        def body(kc, acc):
            base = kc * tq
            parts = [ind(keys_ref[kc, i * 8:(i + 1) * 8, :], base + i * 8 + sub8) for i in range(nslab)]
            return acc + _tree(parts, jnp.add)
        acc = lax.fori_loop(0, nch, body, jnp.zeros(shape8, I32))
        return jnp.broadcast_to(jnp.sum(acc, axis=0, keepdims=True), shape8)

    thr = _kth_key(count_fn, topk, shape8)
    need = topk - count_fn(lambda kk, pos: jnp.where(kk > thr, 1, 0))

    def chunk_bias(kc, ties_before):
        slabs = []
        for i in range(nslab):
            kk = keys_ref[kc, i * 8:(i + 1) * 8, :]
            tie = jnp.where(kk == thr, 1, 0)
            incl = tie
            for sh in (1, 2, 4):
                incl = incl + jnp.where(sub8 >= sh, pltpu.roll(incl, sh, 0), 0)
            slabs.append((kk, incl - tie, jnp.broadcast_to(incl[7:8, :], shape8)))
        out = []
        for kk, excl, total in slabs:
            rank = ties_before + excl
            bias = jnp.where(kk > thr, 0.0, NEG_BIG)
            bias = jnp.where(kk == thr, jnp.where(rank < need, 0.0, NEG_BIG), bias)
            out.append(jnp.where(kk == INT_MIN, NEG_BIG, bias))
            ties_before = ties_before + total
        return jnp.concatenate(out, axis=0), ties_before

    scale = (HEAD_DIM ** -0.5) * 1.4426950408889634
    for g in range(C_KV_HEADS):
        qslab = q_ref[:, g * LANES:(g + 1) * LANES] * scale
        rolled = pltpu.roll(qslab, HEAD_DIM, 1)
        koff = g % 2
        in_half = (lane_q >= HEAD_DIM) if koff else (lane_q < HEAD_DIM)
        qg_ref[g] = jnp.concatenate(
            [jnp.where(in_half, qslab if r == koff else rolled, 0.0) for r in range(2)],
            axis=0).astype(BF16)

    m_ref[...] = jnp.full(m_ref.shape, NEG_BIG, F32)
    ties_ref[...] = jnp.zeros(shape8, I32)

    def max_body(kc):
        start = pl.multiple_of(kc * tq, tq)
        bias, ties = chunk_bias(kc, ties_ref[...])
        ties_ref[...] = ties
        bias2 = jnp.concatenate([bias, bias], axis=1)
        for g in range(C_KV_HEADS):
            kcol = (g // 2) * LANES
            kch = k_ref[pl.ds(start, tq), kcol:kcol + LANES].astype(BF16)
            s = _dot_nt(kch, qg_ref[g]) + bias2
            s_ref[kc, :, g * 2 * tq:(g + 1) * 2 * tq] = s
            m_ref[g] = jnp.maximum(m_ref[g], _fold8(s, jnp.maximum))

    for_chunks(max_body)
    m_row = [jnp.max(m_ref[g], axis=0, keepdims=True) for g in range(C_KV_HEADS)]

    acc_ref[...] = jnp.zeros_like(acc_ref)
    l_ref[...] = jnp.zeros_like(l_ref)

    def sum_body(kc):
        start = pl.multiple_of(kc * tq, tq)
        for g in range(C_KV_HEADS):
            p = jnp.exp2(s_ref[kc, :, g * 2 * tq:(g + 1) * 2 * tq] - m_row[g])
            vt = vt_ref[0, g * HEAD_DIM:(g + 1) * HEAD_DIM, pl.ds(start, tq)].astype(BF16)
            acc_ref[g] += _dot(vt, p.astype(BF16))
            l_ref[g] += _fold8(p, jnp.add)

    for_chunks(sum_body)
    for g in range(C_KV_HEADS):
        out_t = acc_ref[g] / jnp.sum(l_ref[g], axis=0, keepdims=True)
        o_ref[:, g * LANES:(g + 1) * LANES] = jnp.transpose(
            jnp.concatenate([out_t[:, :tq], out_t[:, tq:]], axis=0))


def _prompt_attention(q, k, vt, qi, kw, nb, seq):
    assert C_HEADS == 2 * C_KV_HEADS
    topk = min(TOPK_MAX, seq // 4)
    tq = _tile(seq, 256)
    assert tq % LANES == 0
    nq = seq // tq
    qrow = lambda b, c: (b * nq + c, 0)
    krow = lambda b, c: (b, 0)
    return pl.pallas_call(
        functools.partial(_pattn_kernel, tq=tq, topk=topk),
        grid=(nb, nq),
        in_specs=[pl.BlockSpec((tq, IDX_HEADS * IDX_DIM), qrow), pl.BlockSpec((tq, LANES), qrow),
                  pl.BlockSpec((tq, C_DIM), qrow), pl.BlockSpec((seq, LANES), krow),
                  pl.BlockSpec((seq, C_KV_DIM), krow),
                  pl.BlockSpec((1, C_KV_DIM, seq), lambda b, c: (b, 0, 0))],
        out_specs=pl.BlockSpec((tq, C_DIM), qrow),
        out_shape=jax.ShapeDtypeStruct((nb * seq, C_DIM), F32),
        scratch_shapes=[pltpu.VMEM((nq, tq, tq), I32),
                        pltpu.VMEM((nq, tq, C_KV_HEADS * 2 * tq), F32),
                        pltpu.VMEM((IDX_HEADS, tq, LANES), BF16),
                        pltpu.VMEM((C_KV_HEADS, 2 * tq, LANES), BF16),
                        pltpu.VMEM((C_KV_HEADS, HEAD_DIM, 2 * tq), F32),
                        pltpu.VMEM((C_KV_HEADS, 8, 2 * tq), F32),
                        pltpu.VMEM((C_KV_HEADS, 8, 2 * tq), F32), pltpu.VMEM((8, tq), I32)],
        compiler_params=_cparams(2), name="prompt_attn")(qi, kw, q, kw, k, vt)


SROWS = 8


def _sidx_kernel(pt_ref, qh_ref, wh_ref, kn_ref, *rest, npg, page):
    pages = rest[:npg]
    past_o, new_o = rest[npg:]
    qh = qh_ref[0].astype(BF16)
    wh = wh_ref[0] * ((IDX_DIM ** -0.5) * (IDX_HEADS ** -0.5))

    def head_sum(s):
        s = jnp.maximum(s, 0.0) * wh
        return _tree([s[hh * SROWS:(hh + 1) * SROWS] for hh in range(IDX_HEADS)], jnp.add)

    keys_t = jnp.concatenate([pg[0] for pg in pages], axis=1).astype(BF16)
    past_o[0] = head_sum(_dot(qh, keys_t))

    @pl.when(pl.program_id(1) == 0)
    def _():
        new_o[0] = head_sum(_dot_nt(qh, kn_ref[0].astype(BF16)))


def _sample_index_scores(cache_kidx, page_table, layer, qh, wh, ki_new):
    depth, n_phys, page, _ = cache_kidx.shape
    db, n_pages = page_table.shape
    npg = _pages_per_step(n_pages)
    cache = cache_kidx.transpose(0, 1, 3, 2).reshape(depth * n_phys, IDX_DIM, page)
    base = layer * n_phys

    def page_spec(i):
        return pl.BlockSpec((1, IDX_DIM, page), lambda b, s, pt: (base + pt[b, s * npg + i], 0, 0))

    per_b = lambda b, s, pt: (b, 0, 0)
    grid_spec = pltpu.PrefetchScalarGridSpec(
        num_scalar_prefetch=1, grid=(db, n_pages // npg),
        in_specs=[pl.BlockSpec((1, IDX_HEADS * SROWS, IDX_DIM), per_b),
                  pl.BlockSpec((1, IDX_HEADS * SROWS, 1), per_b),
                  pl.BlockSpec((1, LANES, IDX_DIM), per_b)] + [page_spec(i) for i in range(npg)],
        out_specs=[pl.BlockSpec((1, SROWS, npg * page), lambda b, s, pt: (b, 0, s)),
                   pl.BlockSpec((1, SROWS, LANES), per_b)])
    return pl.pallas_call(
        functools.partial(_sidx_kernel, npg=npg, page=page), grid_spec=grid_spec,
        out_shape=[jax.ShapeDtypeStruct((db, SROWS, n_pages * page), F32),
                   jax.ShapeDtypeStruct((db, SROWS, LANES), F32)],
        compiler_params=_cparams(2), name="sample_index")(page_table, qh, wh, ki_new, *([cache] * npg))


def _ssel_kernel(past_ref, new_ref, bpast_o, bnew_o, keys_ref, lp_ref, *, topk, pos_bits, n_new):
    rows, past = past_ref.shape
    nch = past // LANES
    lane = lax.broadcasted_iota(I32, (rows, LANES), 1)
    t_row = lax.broadcasted_iota(I32, (rows, LANES), 0) % SROWS

    def fill(c, carry):
        start = pl.multiple_of(c * LANES, LANES)
        keys_ref[c] = _score_keys(past_ref[:, pl.ds(start, LANES)])
        return carry

    lax.fori_loop(0, nch, fill, 0)
    new_ok = lane <= jnp.minimum(t_row, n_new - 1)
    keys_ref[nch] = jnp.where(new_ok, _score_keys(new_ref[...]), INT_MIN)

    grp = 8
    assert nch % grp == 0

    def count_fn(pred):
        def body(q, acc):
            c0 = q * grp
            return acc + _tree([pred(keys_ref[c0 + i], (c0 + i) * LANES + lane) for i in range(grp)], jnp.add)
        acc = lax.fori_loop(0, nch // grp, body, jnp.zeros((rows, LANES), I32))
        return _lane_total(acc + pred(keys_ref[nch], nch * LANES + lane))

    thr, last_pos = _select_topk(count_fn, topk, pos_bits, (rows, LANES), lp_ref)

    def emit(c, carry):
        start = pl.multiple_of(c * LANES, LANES)
        bpast_o[:, pl.ds(start, LANES)] = _select_bias(keys_ref[c], c * LANES + lane, thr, last_pos)
        return carry

    lax.fori_loop(0, nch, emit, 0)
    bnew_o[...] = _select_bias(keys_ref[nch], nch * LANES + lane, thr, last_pos)


def _sample_select(sc_past, sc_new, n_new):
    rows, past = sc_past.shape
    topk = min(TOPK_MAX, (past + n_new) // 4)
    tr = _tile(rows, 64)
    pos_bits = int(past + LANES - 1).bit_length()
    row = lambda i: (i, 0)
    return pl.pallas_call(
        functools.partial(_ssel_kernel, topk=topk, pos_bits=pos_bits, n_new=n_new),
        grid=(rows // tr,),
        in_specs=[pl.BlockSpec((tr, past), row), pl.BlockSpec((tr, LANES), row)],
        out_specs=[pl.BlockSpec((tr, past), row), pl.BlockSpec((tr, LANES), row)],
        out_shape=[jax.ShapeDtypeStruct((rows, past), F32), jax.ShapeDtypeStruct((rows, LANES), F32)],
        scratch_shapes=[pltpu.VMEM((past // LANES + 1, tr, LANES), I32), pltpu.VMEM((tr, LANES), I32)],
        compiler_params=_cparams(1), name="sample_select")(sc_past, sc_new)


def _satt_kernel(pt_ref, qbd_ref, bpast_ref, bnew_ref, kn_ref, vn_ref, *rest, npg, page):
    kpages = rest[:npg]
    vpages = rest[npg:2 * npg]
    o_ref, m_ref, l_ref, acc_ref = rest[2 * npg:]
    s_id = pl.program_id(1)
    nrow = qbd_ref.shape[1]
    reps = nrow // SROWS

    @pl.when(s_id == 0)
    def _():
        m_ref[...] = jnp.full_like(m_ref, NEG_BIG)
        l_ref[...] = jnp.zeros_like(l_ref)
        acc_ref[...] = jnp.zeros_like(acc_ref)

    qbd = qbd_ref[0].astype(BF16)

    def update(s, bias, pv):
        s = s + jnp.concatenate([bias] * reps, axis=0)
        m_old = m_ref[...]
        m_new = jnp.maximum(m_old, jnp.max(s, axis=-1, keepdims=True))
        p = jnp.exp(s - m_new)
        alpha = jnp.exp(m_old - m_new)
        l_ref[...] = alpha * l_ref[...] + jnp.sum(p, axis=-1, keepdims=True)
        acc_ref[...] = alpha * acc_ref[...] + pv(p.astype(BF16))
        m_ref[...] = m_new

    keys_t = jnp.concatenate([kp[0] for kp in kpages], axis=1).astype(BF16)
    vals_t = jnp.concatenate([vp[0] for vp in vpages], axis=1).astype(BF16)
    update(_dot(qbd, keys_t), bpast_ref[0], lambda p: _dot_nt(p, vals_t))

    @pl.when(s_id == pl.num_programs(1) - 1)
    def _():
        vn = vn_ref[0].astype(BF16)
        update(_dot_nt(qbd, kn_ref[0].astype(BF16)), bnew_ref[0], lambda p: _dot(p, vn))
        o_ref[0] = acc_ref[...] / l_ref[...]


def _sample_attend(cache_k, cache_v, page_table, layer, qbd, bias_past, bias_new, k_new, v_new):
    depth, n_phys, page = cache_k.shape[:3]
    db, n_pages = page_table.shape
    npg = _pages_per_step(n_pages)
    ck = cache_k.transpose(0, 1, 3, 4, 2).reshape(depth * n_phys, C_KV_DIM, page)
    cv = cache_v.transpose(0, 1, 3, 4, 2).reshape(depth * n_phys, C_KV_DIM, page)
    base = layer * n_phys
    nrow = qbd.shape[1]

    def page_spec(i):
        return pl.BlockSpec((1, C_KV_DIM, page), lambda b, s, pt: (base + pt[b, s * npg + i], 0, 0))

    per_b = lambda b, s, pt: (b, 0, 0)
    grid_spec = pltpu.PrefetchScalarGridSpec(
        num_scalar_prefetch=1, grid=(db, n_pages // npg),
        in_specs=[pl.BlockSpec((1, nrow, C_KV_DIM), per_b),
                  pl.BlockSpec((1, SROWS, npg * page), lambda b, s, pt: (b, 0, s)),
                  pl.BlockSpec((1, SROWS, LANES), per_b),
                  pl.BlockSpec((1, LANES, C_KV_DIM), per_b),
                  pl.BlockSpec((1, LANES, C_KV_DIM), per_b)]
                 + [page_spec(i) for i in range(npg)] * 2,
        out_specs=pl.BlockSpec((1, nrow, C_KV_DIM), per_b),
        scratch_shapes=[pltpu.VMEM((nrow, 1), F32), pltpu.VMEM((nrow, 1), F32),
                        pltpu.VMEM((nrow, C_KV_DIM), F32)])
    return pl.pallas_call(
        functools.partial(_satt_kernel, npg=npg, page=page), grid_spec=grid_spec,
        out_shape=jax.ShapeDtypeStruct((db, nrow, C_KV_DIM), F32),
        compiler_params=_cparams(2), name="sample_attn")(
            page_table, qbd, bias_past, bias_new, k_new, v_new, *([ck] * npg), *([cv] * npg))


def _merge_kernel(x_ref, g_ref, wg_ref, os_ref, bon_ref, gg_ref, lw_ref, lb_ref, bd_ref,
                  ob_ref, oc_ref, wpa_ref, wpb_ref, wpc_ref, wo_ref, o_ref):
    x = x_ref[...]
    d = x.shape[1]
    h = _rms(x, g_ref[...]).astype(BF16)
    gates = jax.nn.sigmoid(_dot(h, wg_ref[...]))
    bd = bd_ref[...]
    o = os_ref[...]
    mean = _segsum(o, bd, split=True) * (1.0 / HEAD_DIM)
    cen = o - mean
    var = _segsum(cen * cen, bd) * (1.0 / HEAD_DIM)
    on = cen * lax.rsqrt(var + GN_EPS) * lw_ref[...] + lb_ref[...]
    oa = ((on + bon_ref[...]) * gg_ref[...]).astype(BF16)
    merged = (gates[:, :d] * _dot(oa, wpa_ref[...])
              + gates[:, d:2 * d] * _dot(ob_ref[...].astype(BF16), wpb_ref[...])
              + gates[:, 2 * d:] * _dot(oc_ref[...].astype(BF16), wpc_ref[...]))
    o_ref[...] = x + _dot(merged.astype(BF16), wo_ref[...])


def _merge(x, o_scan, bonus, g, o_b, o_c, lw, nb, seq):
    m, d = x.shape
    tm = _tile(seq, 512)
    nc = seq // tm
    row = lambda b, c: (b * nc + c, 0)
    tok = lambda w: pl.BlockSpec((tm, w), row)
    consts_a = [lw["mix_norm"], lw["w_g"]]
    consts_b = [lw["rwkv_lnx_w"], lw["rwkv_lnx_b"], lw["bd"]]
    consts_c = [lw["w_pa"], lw["w_pb"], lw["w_pc"], lw["w_out"]]
    in_specs = ([tok(d)] + [_const_spec(c.shape) for c in consts_a]
                + [pl.BlockSpec((tm, A_DIM), lambda b, c: (c, b))] + [tok(A_DIM)] * 2
                + [_const_spec(c.shape) for c in consts_b] + [tok(B_DIM), tok(C_DIM)]
                + [_const_spec(c.shape) for c in consts_c])
    return pl.pallas_call(
        _merge_kernel, grid=(nb, nc), in_specs=in_specs, out_specs=tok(d),
        out_shape=jax.ShapeDtypeStruct((m, d), F32), compiler_params=_cparams(2), name="merge")(
            x, *consts_a, o_scan, bonus, g, *consts_b, o_b, o_c, *consts_c)


def _rope_tables(pos):
    inv = ROPE_THETA ** (-jnp.arange(ROT_HALF, dtype=F32) / ROT_HALF)
    ang = pos.astype(F32)[:, None] * inv[None, :]
    c, s = jnp.cos(ang), jnp.sin(ang)
    t = pos.shape[0]
    pad = jnp.zeros((t, HEAD_DIM - ROT_DIM), F32)
    zer = jnp.zeros((t, ROT_HALF), F32)
    cos = jnp.concatenate([c, c, pad + 1.0], axis=1)
    sa = jnp.concatenate([-s, zer, pad], axis=1)
    sb = jnp.concatenate([zer, s, pad], axis=1)
    return tuple(jnp.tile(z, (1, C_HEADS)) for z in (cos, sa, sb))


def _layer_weights(i, p):
    d = p["w_in"].shape[1]
    w_in = p["w_in"][i]
    o1 = RWKV_COLS
    o2 = o1 + CONV_COLS
    o3 = o2 + ATTN_COLS
    row = lambda v: v.reshape(1, -1)
    tile_h = lambda v, n: jnp.tile(v, n).reshape(1, -1)
    z = jnp.zeros((LORA_W, A_DIM), F32)
    w_wa = jnp.concatenate([jnp.concatenate([p["rwkv_w2"][i], z], axis=1),
                            jnp.concatenate([z, p["rwkv_a2"][i]], axis=1)], axis=0)
    head = np.arange(A_DIM) // HEAD_DIM
    bd = jnp.asarray(head[:, None] == head[None, :], BF16)
    idx_norm = jnp.concatenate([p["idx_k_norm"][i], jnp.zeros((LANES - IDX_DIM,), F32)])
    bf = lambda w: w.astype(BF16)
    return dict(
        ffn1_norm=p["ffn1_norm"][i], ffn1_wg=bf(p["ffn1_wg"][i]), ffn1_wu=bf(p["ffn1_wu"][i]),
        ffn1_wd=bf(p["ffn1_wd"][i]),
        ffn2_norm=p["ffn2_norm"][i], ffn2_wg=bf(p["ffn2_wg"][i]), ffn2_wu=bf(p["ffn2_wu"][i]),
        ffn2_wd=bf(p["ffn2_wd"][i]),
        mix_norm=row(p["mix_norm"][i]),
        w_a=bf(w_in[:, :o1]), w_b=bf(w_in[:, o1:o2]),
        w_c=bf(jnp.pad(w_in[:, o2:o3], ((0, 0), (0, ATTN_COLS_PAD - ATTN_COLS)))),
        w_g=bf(w_in[:, o3:]),
        rwkv_mu=row(p["rwkv_mu"][i]), rwkv_w0=row(p["rwkv_w0"][i]), rwkv_a0=row(p["rwkv_a0"][i]),
        w_wa=bf(w_wa), rwkv_g2=bf(p["rwkv_g2"][i]), rwkv_kk=row(p["rwkv_kk"][i]),
        rwkv_ka=row(p["rwkv_ka"][i]), rwkv_rk=row(p["rwkv_rk"][i]),
        rwkv_lnx_w=row(p["rwkv_lnx_w"][i]), rwkv_lnx_b=row(p["rwkv_lnx_b"][i]),
        conv_w=p["conv_w"][i], bd=bd,
        q_norm=tile_h(p["q_norm"][i], C_HEADS), k_norm=tile_h(p["k_norm"][i], C_KV_HEADS),
        idx_k_norm=row(idx_norm),
        w_pa=bf(p["w_pa"][i]), w_pb=bf(p["w_pb"][i]), w_pc=bf(p["w_pc"][i]), w_out=bf(p["w_out"][i]),
        ple_norm=p["ple_norm"][i], ple_gate=bf(p["ple_gate"][i]), ple_proj=bf(p["ple_proj"][i]),
    )


def _rwkv_branch(x, nb, seq, lw, wkv0, shift_rows, between=None):
    pre = _rwkv_pre(x, nb, seq, lw, shift_rows)
    s0 = wkv0.transpose(3, 2, 0, 1).reshape(HEAD_DIM, HEAD_DIM, nb * A_HEADS)
    other = None
    if between is not None:
        x, pre = lax.optimization_barrier((x, pre))
        other = between(x)
    r, w, k, v, na, nb_, g, bonus, shift_o = pre
    nbh = nb * A_HEADS
    if shift_rows is None:
        ts = lambda z: z.reshape(nbh, HEAD_DIM, seq).transpose(2, 1, 0)
    else:
        ts = lambda z: z.reshape(nb, seq, A_HEADS, HEAD_DIM).transpose(1, 3, 0, 2).reshape(seq, HEAD_DIM, nbh)
    seqs = [ts(z) for z in (r, w, k, v, na, nb_)]
    if between is not None:
        s0, other = lax.optimization_barrier((s0, other))
    o, s_fin = _scan(*seqs, s0)
    wkv1 = s_fin.reshape(HEAD_DIM, HEAD_DIM, nb, A_HEADS).transpose(2, 3, 1, 0)
    if shift_rows is None:
        shift1 = shift_o.reshape(nb, RWKV_COLS)
        o_tok = o.transpose(0, 2, 1).reshape(seq, nb * A_DIM)
    else:
        shift1 = shift_o.reshape(nb, seq, RWKV_COLS)[:, -1]
        o_tok = o.reshape(seq, HEAD_DIM, nb, A_HEADS).transpose(2, 0, 3, 1).reshape(nb * seq, A_DIM)
    return o_tok, bonus, g, shift1, wkv1, other


def _layer_tail(x1, o_scan, bonus, g, o_b, o_c, p_emb, lw, nb, seq):
    x2 = _merge(x1, o_scan, bonus, g, o_b, o_c, lw, nb, seq)
    return _ffn(x2, lw["ffn2_norm"], lw["ffn2_wg"], lw["ffn2_wu"], lw["ffn2_wd"],
                ple=(p_emb, lw["ple_norm"], lw["ple_gate"], lw["ple_proj"]))


def _prompt_layer(x, p_emb, nb, seq, lw, tabs):
    x1 = _ffn(x, lw["ffn1_norm"], lw["ffn1_wg"], lw["ffn1_wu"], lw["ffn1_wd"])
    wkv0 = jnp.zeros((nb, A_HEADS, HEAD_DIM, HEAD_DIM), F32)

    def other_mixers(xin):
        o_b, conv1 = _conv(xin, nb, seq, lw)
        q, k, qi, kw, kt, vt, kit = _attn_proj(xin, nb, seq, lw, tabs, False)
        return o_b, conv1, kt, vt, kit, _prompt_attention(q, k, vt, qi, kw, nb, seq)

    o_scan, bonus, g, shift1, wkv1, other = _rwkv_branch(x1, nb, seq, lw, wkv0, None, other_mixers)
    o_b, conv1, kt, vt, kit, o_c = other
    x4 = _layer_tail(x1, o_scan, bonus, g, o_b, o_c, p_emb, lw, nb, seq)
    heads = lambda z: z.reshape(nb, C_KV_HEADS, HEAD_DIM, seq).transpose(0, 3, 1, 2)
    st = (heads(kt), heads(vt), kit.transpose(0, 2, 1), wkv1, shift1, conv1)
    return x4, st


def _sample_layer(x, p_emb, nb, seq, lw, tabs, layer, cache_k, cache_v, cache_kidx, page_table,
                  wkv0, shift0, conv0):
    assert seq <= SROWS
    x1 = _ffn(x, lw["ffn1_norm"], lw["ffn1_wg"], lw["ffn1_wu"], lw["ffn1_wd"])
    rep_rows = lambda z: jnp.repeat(z, seq, axis=0)
    o_scan, bonus, g, shift1, wkv1, _ = _rwkv_branch(x1, nb, seq, lw, wkv0, rep_rows(shift0))
    o_b, z_all = _conv(x1, nb, seq, lw, init=(rep_rows(conv0[:, 0]), rep_rows(conv0[:, 1])))
    conv1 = z_all.reshape(nb, seq, B_DIM)[:, seq - (CONV_W - 1):]
    q, k, qi, kw, v, ki = _attn_proj(x1, nb, seq, lw, tabs, True)

    pad_t = lambda z: jnp.pad(z, ((0, 0), (0, SROWS - seq)) + ((0, 0),) * (z.ndim - 2))
    qh = pad_t(qi.reshape(nb, seq, IDX_HEADS, IDX_DIM)).transpose(0, 2, 1, 3)
    qh = qh.reshape(nb, IDX_HEADS * SROWS, IDX_DIM)
    wi = kw[:, IDX_DIM:IDX_DIM + IDX_HEADS].reshape(nb, seq, IDX_HEADS)
    wh = pad_t(wi).transpose(0, 2, 1).reshape(nb, IDX_HEADS * SROWS, 1)
    pad_keys = lambda z: jnp.pad(z.reshape(nb, seq, -1), ((0, 0), (0, LANES - seq), (0, 0)))
    sc_past, sc_new = _sample_index_scores(cache_kidx, page_table, layer, qh, wh, pad_keys(ki))
    past = sc_past.shape[-1]
    b_past, b_new = _sample_select(sc_past.reshape(nb * SROWS, past), sc_new.reshape(nb * SROWS, LANES), seq)

    rep = C_HEADS // C_KV_HEADS
    qg = pad_t(q.reshape(nb, seq, C_KV_HEADS, rep, HEAD_DIM)).transpose(0, 2, 3, 1, 4)
    qg = qg * (HEAD_DIM ** -0.5)
    eye = jnp.eye(C_KV_HEADS, dtype=F32)
    qbd = jnp.einsum("bgrtd,gh->bgrthd", qg, eye).reshape(nb, C_HEADS * SROWS, C_KV_DIM)
    o = _sample_attend(cache_k, cache_v, page_table, layer, qbd,
                       b_past.reshape(nb, SROWS, past), b_new.reshape(nb, SROWS, LANES),
                       pad_keys(k), pad_keys(v))
    o = o.reshape(nb, C_KV_HEADS, rep, SROWS, C_KV_HEADS, HEAD_DIM)
    o = jnp.einsum("bgrthd,gh->bgrtd", o, eye)[:, :, :, :seq]
    o_c = o.transpose(0, 3, 1, 2, 4).reshape(nb * seq, C_DIM)

    x4 = _layer_tail(x1, o_scan, bonus, g, o_b, o_c, p_emb, lw, 1, nb * seq)
    st = (k.reshape(nb, seq, C_KV_HEADS, HEAD_DIM), v.reshape(nb, seq, C_KV_HEADS, HEAD_DIM),
          ki.reshape(nb, seq, IDX_DIM), wkv1, shift1, conv1)
    return x4, st


def kernel(x_prompt, x_sample, cache_k, cache_v, cache_kidx, state_wkv, state_shift, state_conv, page_table, p_prompt, p_sample, ffn1_norm, ffn1_wg, ffn1_wu, ffn1_wd, mix_norm, w_in, rwkv_mu, rwkv_w0, rwkv_w2, rwkv_a0, rwkv_a2, rwkv_g2, rwkv_kk, rwkv_ka, rwkv_rk, rwkv_lnx_w, rwkv_lnx_b, conv_w, q_norm, k_norm, idx_k_norm, w_pa, w_pb, w_pc, w_out, ffn2_norm, ffn2_wg, ffn2_wu, ffn2_wd, ple_norm, ple_gate, ple_proj):
    params = dict(ffn1_norm=ffn1_norm, ffn1_wg=ffn1_wg, ffn1_wu=ffn1_wu, ffn1_wd=ffn1_wd,
                  mix_norm=mix_norm, w_in=w_in, rwkv_mu=rwkv_mu, rwkv_w0=rwkv_w0, rwkv_w2=rwkv_w2,
                  rwkv_a0=rwkv_a0, rwkv_a2=rwkv_a2, rwkv_g2=rwkv_g2, rwkv_kk=rwkv_kk, rwkv_ka=rwkv_ka,
                  rwkv_rk=rwkv_rk.reshape(rwkv_rk.shape[0], -1), rwkv_lnx_w=rwkv_lnx_w,
                  rwkv_lnx_b=rwkv_lnx_b, conv_w=conv_w, q_norm=q_norm, k_norm=k_norm,
                  idx_k_norm=idx_k_norm, w_pa=w_pa, w_pb=w_pb, w_pc=w_pc, w_out=w_out,
                  ffn2_norm=ffn2_norm, ffn2_wg=ffn2_wg, ffn2_wu=ffn2_wu, ffn2_wd=ffn2_wd,
                  ple_norm=ple_norm, ple_gate=ple_gate, ple_proj=ple_proj)
    nb, seq, d = x_prompt.shape
    db, dseq, _ = x_sample.shape
    depth = w_in.shape[0]
    past = page_table.shape[1] * cache_k.shape[2]
    tabs_p = _rope_tables(jnp.arange(seq, dtype=jnp.int32))
    tabs_s = tuple(jnp.tile(z, (db, 1)) for z in _rope_tables(past + jnp.arange(dseq, dtype=jnp.int32)))
    xp = x_prompt.reshape(nb * seq, d)
    xs = x_sample.reshape(db * dseq, d)
    outs_p, outs_s = [], []
    for i in range(depth):
        lw = _layer_weights(i, params)
        xp, st_p = _prompt_layer(xp, (p_prompt.reshape(depth * nb * seq, -1), i), nb, seq, lw, tabs_p)
        xs, st_s = _sample_layer(xs, (p_sample.reshape(depth * db * dseq, -1), i), db, dseq, lw, tabs_s, i,
                                 cache_k, cache_v, cache_kidx, page_table,
                                 state_wkv[i], state_shift[i], state_conv[i])
        outs_p.append(st_p)
        outs_s.append(st_s)
    k_p, v_p, kidx_p, wkv_p, shift_p, conv_p = [jnp.stack(z) for z in zip(*outs_p)]
    k_s, v_s, kidx_s, wkv_s, shift_s, conv_s = [jnp.stack(z) for z in zip(*outs_s)]
    return (xp.reshape(nb, seq, d), xs.reshape(db, dseq, d), k_p, v_p, kidx_p, wkv_p, shift_p, conv_p,
            k_s, v_s, kidx_s, wkv_s, shift_s, conv_s)
```

```python
import functools

import jax
import jax.numpy as jnp
import numpy as np
from jax import lax
from jax.experimental import pallas as pl
from jax.experimental.pallas import tpu as pltpu

F32 = jnp.float32
BF16 = jnp.bfloat16
I32 = jnp.int32

HEAD_DIM = 64
A_HEADS = 8
A_DIM = A_HEADS * HEAD_DIM
LORA_W = 64
LORA_A = 64
LORA_G = 128
B_DIM = 512
CONV_W = 3
C_HEADS = 8
C_KV_HEADS = 4
C_DIM = C_HEADS * HEAD_DIM
C_KV_DIM = C_KV_HEADS * HEAD_DIM
IDX_HEADS = 8
IDX_DIM = 64
TOPK_MAX = 256
ROT_DIM = HEAD_DIM // 4
ROT_HALF = ROT_DIM // 2
ROPE_THETA = 500000.0
N_BRANCH = 3
RMS_EPS = 1e-6
GN_EPS = 64e-5
RWKV_COLS = 3 * A_DIM + LORA_W + LORA_A + LORA_G
CONV_COLS = 3 * B_DIM
ATTN_COLS = C_DIM + 2 * C_KV_DIM + IDX_HEADS * IDX_DIM + IDX_DIM + IDX_HEADS
ATTN_COLS_PAD = 1664

LANES = 128
INT_MIN = -2 ** 31
NEG_BIG = -1e30
VMEM_LIMIT = 56 * 1024 * 1024
PAGES_PER_STEP = 32


def _pages_per_step(n_pages, cap=PAGES_PER_STEP):
    npg = min(cap, n_pages)
    while n_pages % npg:
        npg -= 1
    return npg


def _cparams(n_axes):
    return pltpu.CompilerParams(dimension_semantics=("arbitrary",) * n_axes,
                                vmem_limit_bytes=VMEM_LIMIT)


def _const_spec(shape):
    nd = len(shape)
    return pl.BlockSpec(shape, lambda *_: (0,) * nd, pipeline_mode=pl.Buffered(1))


def _tile(m, pref):
    t = min(m, pref)
    while m % t:
        t -= 8
    return t


def _rms(x, g):
    ms = jnp.mean(x * x, axis=-1, keepdims=True)
    return x * lax.rsqrt(ms + RMS_EPS) * g


def _dot(a, b):
    return jnp.dot(a, b, preferred_element_type=F32)


def _dot_nt(a, b):
    return lax.dot_general(a, b, (((1,), (1,)), ((), ())), preferred_element_type=F32)


def _segsum(x, bd, split=False):
    hi = x.astype(BF16)
    if not split:
        return _dot(hi, bd)
    lo = (x - hi.astype(F32)).astype(BF16)
    return _dot(hi, bd) + _dot(lo, bd)


def _rope(x, cos, sa, sb):
    n = x.shape[-1]
    return x * cos + pltpu.roll(x, n - ROT_HALF, 1) * sa + pltpu.roll(x, ROT_HALF, 1) * sb


def _rows_before(x, k, fills, *, seg_len, first_chunk_rows=None):
    rows = x.shape[0]
    y = pltpu.roll(x, k, 0)
    ridx = lax.broadcasted_iota(I32, (rows, 1), 0)
    t = ridx if seg_len is None else ridx % seg_len
    for r in range(k):
        y = jnp.where(t == r, fills[k - 1 - r], y)
    return y


ROW_BLOCK = 256


def _row_blocks(tm, seg_len=None, blk=ROW_BLOCK):
    if seg_len is not None or tm % blk:
        return [slice(0, tm)]
    return [slice(i, i + blk) for i in range(0, tm, blk)]


def _softplus(y):
    return jnp.maximum(y, 0.0) + jnp.log(1.0 + jnp.exp(-jnp.abs(y)))


def _ffn_kernel(*refs, chunks, ple):
    if ple:
        x_ref, g_ref, wg_ref, wu_ref, wd_ref, p_ref, pn_ref, pg_ref, pp_ref, o_ref = refs
    else:
        x_ref, g_ref, wg_ref, wu_ref, wd_ref, o_ref = refs
    x = x_ref[...]
    h = _rms(x, g_ref[...]).astype(BF16)
    acc = None
    for lo, hi in chunks:
        gt = _dot(h, wg_ref[:, lo:hi])
        ut = _dot(h, wu_ref[:, lo:hi])
        act = (gt * jax.nn.sigmoid(gt) * ut).astype(BF16)
        d = _dot(act, wd_ref[lo:hi, :])
        acc = d if acc is None else acc + d
    y = x + 0.5 * acc
    if ple:
        hg = _rms(y, pn_ref[...]).astype(BF16)
        gate = jax.nn.sigmoid(_dot(hg, pg_ref[...]))
        y = y + gate * _dot(p_ref[...].astype(BF16), pp_ref[...])
    o_ref[...] = y


def _ffn(x, g, wg, wu, wd, ple=None):
    m, d = x.shape
    f = wg.shape[1]
    tm = _tile(m, 512)
    step = 1024
    chunks = tuple((lo, min(lo + step, f)) for lo in range(0, f, step))
    row = lambda i: (i, 0)
    in_specs = [pl.BlockSpec((tm, d), row), _const_spec((1, d)), _const_spec((d, f)),
                _const_spec((d, f)), _const_spec((f, d))]
    args = [x, g.reshape(1, d), wg, wu, wd]
    if ple is not None:
        (p, layer), pn, pg, pp = ple
        off = layer * (m // tm)
        in_specs += [pl.BlockSpec((tm, p.shape[1]), lambda i: (i + off, 0)), _const_spec((1, d)),
                     _const_spec(pg.shape), _const_spec(pp.shape)]
        args += [p, pn.reshape(1, d), pg, pp]
    return pl.pallas_call(
        functools.partial(_ffn_kernel, chunks=chunks, ple=ple is not None),
        grid=(m // tm,), in_specs=in_specs, out_specs=pl.BlockSpec((tm, d), row),
        out_shape=jax.ShapeDtypeStruct((m, d), F32), compiler_params=_cparams(1),
        name="ffn_ple" if ple is not None else "ffn")(*args)


def _rwkv_pre_kernel(*refs, seg_len):
    (x_ref, g_ref, wa_ref, mu_ref, w0_ref, a0_ref, wwa_ref, g2_ref, kkw_ref, ka_ref,
     rk_ref, bd_ref) = refs[:12]
    rest = refs[12:]
    if seg_len is None:
        r_o, w_o, k_o, v_o, na_o, nb_o, g_o, bonus_o, shift_o, carry_ref = rest
    else:
        init_ref, r_o, w_o, k_o, v_o, na_o, nb_o, g_o, bonus_o, shift_o = rest
    tm = x_ref.shape[0]
    bd = bd_ref[...]
    if seg_len is None:
        @pl.when(pl.program_id(1) == 0)
        def _():
            carry_ref[...] = jnp.zeros_like(carry_ref)
        prev_row = carry_ref[0:1, :]
    for rows in _row_blocks(tm, seg_len):
        rb = rows.stop - rows.start
        h = _rms(x_ref[rows, :], g_ref[...]).astype(BF16)
        u = _dot(h, wa_ref[...])
        if seg_len is None:
            u_prev = _rows_before(u, 1, [prev_row], seg_len=None)
            prev_row = u[rb - 1:rb, :]
        else:
            u_prev = _rows_before(u, 1, [init_ref[rows, :]], seg_len=seg_len)
            shift_o[rows, :] = u
        us = u + (u_prev - u) * mu_ref[...]
        r = us[:, 0:A_DIM]
        k = us[:, A_DIM:2 * A_DIM]
        v = us[:, 2 * A_DIM:3 * A_DIM]
        o3 = 3 * A_DIM
        xwa = us[:, o3:o3 + LORA_W + LORA_A]
        lane = lax.broadcasted_iota(I32, xwa.shape, 1)
        xwa = jnp.where(lane < LORA_W, jnp.tanh(xwa), xwa)
        lo = _dot(xwa.astype(BF16), wwa_ref[...])
        w_log = -_softplus(-(w0_ref[...] + lo[:, :A_DIM])) - 0.5
        log_decay = -jnp.exp(w_log)
        a = jax.nn.sigmoid(a0_ref[...] + lo[:, A_DIM:])
        xg = us[:, o3 + LORA_W + LORA_A:]
        g = _dot(jax.nn.sigmoid(xg).astype(BF16), g2_ref[...])
        kk = k * kkw_ref[...]
        kk = kk / jnp.maximum(jnp.sqrt(_segsum(kk * kk, bd)), 1e-12)
        k2 = k * (1.0 + (a - 1.0) * ka_ref[...])
        seq_outs = zip((r_o, w_o, k_o, v_o, na_o, nb_o), (r, log_decay, k2, v, -kk, kk * a))
        for o_ref, val in seq_outs:
            if seg_len is None:
                o_ref[:, rows] = jnp.transpose(val)
            else:
                o_ref[rows, :] = val
        g_o[rows, :] = g
        bonus_o[rows, :] = _segsum(r * k2 * rk_ref[...], bd) * v
    if seg_len is None:
        carry_ref[0:1, :] = prev_row
        shift_o[0] = prev_row


def _rwkv_pre(x, nb, seq, lw, init_rows=None):
    m, d = x.shape
    consts = [lw["mix_norm"], lw["w_a"], lw["rwkv_mu"], lw["rwkv_w0"], lw["rwkv_a0"], lw["w_wa"],
              lw["rwkv_g2"], lw["rwkv_kk"], lw["rwkv_ka"], lw["rwkv_rk"], lw["bd"]]
    const_specs = [_const_spec(c.shape) for c in consts]
    outs = [jax.ShapeDtypeStruct((m, A_DIM), F32)] * 8
    if init_rows is None:
        tm = _tile(seq, 512)
        nc = seq // tm
        row = lambda b, c: (b * nc + c, 0)
        grid = (nb, nc)
        in_specs = [pl.BlockSpec((tm, d), row)] + const_specs
        out_specs = ([pl.BlockSpec((A_DIM, tm), lambda b, c: (b, c))] * 6 + [pl.BlockSpec((tm, A_DIM), row)] * 2
                     + [pl.BlockSpec((1, 1, RWKV_COLS), lambda b, c: (b, 0, 0))])
        outs = ([jax.ShapeDtypeStruct((nb * A_DIM, seq), F32)] * 6 + outs[6:]
                + [jax.ShapeDtypeStruct((nb, 1, RWKV_COLS), F32)])
        scratch = [pltpu.VMEM((8, RWKV_COLS), F32)]
        args = [x] + consts
        seg_len = None
    else:
        tm = m
        row = lambda i: (0, 0)
        grid = (1,)
        in_specs = [pl.BlockSpec((tm, d), row)] + const_specs + [pl.BlockSpec((tm, RWKV_COLS), row)]
        out_specs = [pl.BlockSpec((tm, A_DIM), row)] * 8 + [pl.BlockSpec((tm, RWKV_COLS), row)]
        outs = outs + [jax.ShapeDtypeStruct((m, RWKV_COLS), F32)]
        scratch = []
        args = [x] + consts + [init_rows]
        seg_len = seq
    return pl.pallas_call(
        functools.partial(_rwkv_pre_kernel, seg_len=seg_len), grid=grid, in_specs=in_specs,
        out_specs=out_specs, out_shape=outs, scratch_shapes=scratch,
        compiler_params=_cparams(len(grid)), name="rwkv_pre")(*args)


def _scan_kernel(r_ref, lw_ref, k_ref, v_ref, a_ref, b_ref, s0_ref, o_ref, st_ref, s_ref, vec_ref, *, tc):
    c = pl.program_id(1)

    @pl.when(c == 0)
    def _():
        s_ref[...] = s0_ref[...]

    jb = 32
    ln = s_ref.shape[-1]

    def step(t, carry):
        cum, gam_prev = carry
        cum = cum + lw_ref[t]
        gam = jnp.exp(cum)
        ginv = jnp.exp(-cum)
        vec_ref[0] = a_ref[t] * gam_prev
        vec_ref[1] = b_ref[t] * ginv
        vec_ref[2] = k_ref[t] * ginv
        vec_ref[3] = r_ref[t] * gam

        def sa_body(q, sa):
            for jj in range(jb):
                j = q * jb + jj
                sa = sa + s_ref[j] * vec_ref[0, pl.ds(j, 1), :]
            return sa

        sa = lax.fori_loop(0, HEAD_DIM // jb, sa_body, jnp.zeros((HEAD_DIM, ln), F32))
        vt = v_ref[t]

        def up_body(q, o):
            for jj in range(jb):
                j = q * jb + jj
                sn = s_ref[j] + sa * vec_ref[1, pl.ds(j, 1), :] + vt * vec_ref[2, pl.ds(j, 1), :]
                s_ref[j] = sn
                o = o + sn * vec_ref[3, pl.ds(j, 1), :]
            return o

        o_ref[t] = lax.fori_loop(0, HEAD_DIM // jb, up_body, jnp.zeros((HEAD_DIM, ln), F32))
        return cum, gam

    zeros = jnp.zeros((HEAD_DIM, ln), F32)
    _, gam_end = lax.fori_loop(0, tc, step, (zeros, zeros + 1.0))
    vec_ref[0] = gam_end
    for j in range(HEAD_DIM):
        s_ref[j] = s_ref[j] * vec_ref[0, j:j + 1, :]

    @pl.when(c == pl.num_programs(1) - 1)
    def _():
        st_ref[...] = s_ref[...]


def _scan(r, w, k, v, a, b, s0):
    t_len, hd, nbh = r.shape
    ln = min(LANES, nbh)
    tc = _tile(t_len, 64) if t_len % 8 == 0 else t_len
    grid = (nbh // ln, t_len // tc)
    seq_spec = pl.BlockSpec((tc, hd, ln), lambda l, c: (c, 0, l))
    st_spec = pl.BlockSpec((hd, hd, ln), lambda l, c: (0, 0, l))
    return pl.pallas_call(
        functools.partial(_scan_kernel, tc=tc), grid=grid,
        in_specs=[seq_spec] * 6 + [st_spec], out_specs=[seq_spec, st_spec],
        out_shape=[jax.ShapeDtypeStruct((t_len, hd, nbh), F32),
                   jax.ShapeDtypeStruct((hd, hd, nbh), F32)],
        scratch_shapes=[pltpu.VMEM((hd, hd, ln), F32), pltpu.VMEM((4, hd, ln), F32)],
        compiler_params=_cparams(2), name="rwkv_scan")(r, w, k, v, a, b, s0)


def _conv_kernel(*refs, seg_len):
    x_ref, g_ref, wb_ref, cw_ref = refs[:4]
    rest = refs[4:]
    if seg_len is None:
        o_ref, st_ref, carry_ref = rest
    else:
        i0_ref, i1_ref, o_ref, st_ref = rest
    tm = x_ref.shape[0]
    cw = cw_ref[...]
    if seg_len is None:
        @pl.when(pl.program_id(1) == 0)
        def _():
            carry_ref[...] = jnp.zeros_like(carry_ref)
        hist = [carry_ref[1:2, :], carry_ref[0:1, :]]
    for rows in _row_blocks(tm, seg_len, ROW_BLOCK // 2):
        rb = rows.stop - rows.start
        h = _rms(x_ref[rows, :], g_ref[...]).astype(BF16)
        u = _dot(h, wb_ref[...])
        bg = u[:, :B_DIM]
        z = u[:, B_DIM:2 * B_DIM] * u[:, 2 * B_DIM:]
        if seg_len is None:
            z1 = _rows_before(z, 1, hist[:1], seg_len=None)
            z2 = _rows_before(z, 2, hist, seg_len=None)
            hist = [z[rb - 1:rb, :], z[rb - 2:rb - 1, :]]
        else:
            init = [i1_ref[rows, :], i0_ref[rows, :]]
            z1 = _rows_before(z, 1, init[:1], seg_len=seg_len)
            z2 = _rows_before(z, 2, init, seg_len=seg_len)
            st_ref[rows, :] = z
        o_ref[rows, :] = bg * (z2 * cw[0:1, :] + z1 * cw[1:2, :] + z * cw[2:3, :])
    if seg_len is None:
        last2 = jnp.concatenate([hist[1], hist[0]], axis=0)
        carry_ref[0:2, :] = last2
        st_ref[0] = last2


def _conv(x, nb, seq, lw, init=None):
    m, d = x.shape
    assert seq >= CONV_W - 1
    consts = [lw["mix_norm"], lw["w_b"], lw["conv_w"]]
    const_specs = [_const_spec(c.shape) for c in consts]
    if init is None:
        tm = _tile(seq, 512)
        nc = seq // tm
        row = lambda b, c: (b * nc + c, 0)
        grid = (nb, nc)
        in_specs = [pl.BlockSpec((tm, d), row)] + const_specs
        out_specs = [pl.BlockSpec((tm, B_DIM), row),
                     pl.BlockSpec((1, CONV_W - 1, B_DIM), lambda b, c: (b, 0, 0))]
        outs = [jax.ShapeDtypeStruct((m, B_DIM), F32), jax.ShapeDtypeStruct((nb, CONV_W - 1, B_DIM), F32)]
        scratch = [pltpu.VMEM((8, B_DIM), F32)]
        args = [x] + consts
        seg_len = None
    else:
        tm = m
        row = lambda i: (0, 0)
        grid = (1,)
        in_specs = [pl.BlockSpec((tm, d), row)] + const_specs + [pl.BlockSpec((tm, B_DIM), row)] * 2
        out_specs = [pl.BlockSpec((tm, B_DIM), row)] * 2
        outs = [jax.ShapeDtypeStruct((m, B_DIM), F32)] * 2
        scratch = []
        args = [x] + consts + list(init)
        seg_len = seq
    return pl.pallas_call(
        functools.partial(_conv_kernel, seg_len=seg_len), grid=grid, in_specs=in_specs,
        out_specs=out_specs, out_shape=outs, scratch_shapes=scratch,
        compiler_params=_cparams(len(grid)), name="shortconv")(*args)


def _attn_proj_kernel(x_ref, g_ref, wc_ref, qn_ref, kn_ref, in_ref, bd_ref, cos_ref, sa_ref, sb_ref,
                      *outs, transposed):
    bd = bd_ref[...]
    inv_hd = 1.0 / HEAD_DIM
    if transposed:
        q_o, k_o, qi_o, kw_o, kt_o, vt_o, kit_o = outs
    else:
        q_o, k_o, qi_o, kw_o, v_o, ki_o = outs
    for rows in _row_blocks(x_ref.shape[0], None if transposed else 1):
        h = _rms(x_ref[rows, :], g_ref[...]).astype(BF16)
        u = _dot(h, wc_ref[...])
        cos, sa, sb = cos_ref[rows, :], sa_ref[rows, :], sb_ref[rows, :]
        q = u[:, :C_DIM]
        q = q * lax.rsqrt(_segsum(q * q, bd) * inv_hd + RMS_EPS) * qn_ref[...]
        q = _rope(q, cos, sa, sb)
        k = u[:, C_DIM:C_DIM + C_KV_DIM]
        k = k * lax.rsqrt(_segsum(k * k, bd[:C_KV_DIM, :C_KV_DIM]) * inv_hd + RMS_EPS) * kn_ref[...]
        k = _rope(k, cos[:, :C_KV_DIM], sa[:, :C_KV_DIM], sb[:, :C_KV_DIM])
        o = C_DIM + C_KV_DIM
        v = u[:, o:o + C_KV_DIM]
        o += C_KV_DIM
        qi = _rope(u[:, o:o + IDX_HEADS * IDX_DIM], cos, sa, sb)
        o += IDX_HEADS * IDX_DIM
        kw = u[:, o:o + LANES]
        lane = lax.broadcasted_iota(I32, kw.shape, 1)
        is_ki = lane < IDX_DIM
        ms = jnp.sum(jnp.where(is_ki, kw * kw, 0.0), axis=-1, keepdims=True) * (1.0 / IDX_DIM)
        kin = _rope(kw * lax.rsqrt(ms + RMS_EPS) * in_ref[...], cos[:, :LANES], sa[:, :LANES], sb[:, :LANES])
        kw = jnp.where(is_ki, kin, kw)
        if transposed:
            kt_o[0, :, rows] = jnp.transpose(k)
            vt_o[0, :, rows] = jnp.transpose(v)
            kit_o[0, :, rows] = jnp.transpose(kw)[:IDX_DIM, :]
        else:
            v_o[rows, :] = v
            ki_o[rows, :] = kw[:, :IDX_DIM]
        q_o[rows, :] = q
        k_o[rows, :] = k
        qi_o[rows, :] = qi
        kw_o[rows, :] = kw


def _attn_proj(x, nb, seq, lw, tabs, sample):
    m, d = x.shape
    consts = [lw["mix_norm"], lw["w_c"], lw["q_norm"], lw["k_norm"], lw["idx_k_norm"], lw["bd"]]
    const_specs = [_const_spec(c.shape) for c in consts]
    if sample:
        tm, nc = m, 1
        grid = (1, 1)
    else:
        tm = _tile(seq, 512)
        nc = seq // tm
        grid = (nb, nc)
    row = lambda b, c: (b * nc + c, 0)
    tab = lambda b, c: (c, 0)
    widths = [C_DIM, C_KV_DIM, IDX_HEADS * IDX_DIM, LANES]
    out_specs = [pl.BlockSpec((tm, w), row) for w in widths]
    out_shape = [jax.ShapeDtypeStruct((m, w), F32) for w in widths]
    if sample:
        for w in (C_KV_DIM, IDX_DIM):
            out_specs.append(pl.BlockSpec((tm, w), row))
            out_shape.append(jax.ShapeDtypeStruct((m, w), F32))
    else:
        for w in (C_KV_DIM, C_KV_DIM, IDX_DIM):
            out_specs.append(pl.BlockSpec((1, w, tm), lambda b, c: (b, 0, c)))
            out_shape.append(jax.ShapeDtypeStruct((nb, w, seq), F32))
    return pl.pallas_call(
        functools.partial(_attn_proj_kernel, transposed=not sample), grid=grid,
        in_specs=[pl.BlockSpec((tm, d), row)] + const_specs + [pl.BlockSpec((tm, C_DIM), tab)] * 3,
        out_specs=out_specs, out_shape=out_shape,
        compiler_params=_cparams(2), name="attn_proj")(x, *consts, *tabs)


def _score_keys(score):
    bits = pltpu.bitcast(score, I32)
    return jnp.where(bits < 0, bits ^ 0x7FFFFFFF, bits)


def _lane_total(acc):
    return jnp.broadcast_to(jnp.sum(acc, axis=-1, keepdims=True), acc.shape)


def _tree(parts, op):
    while len(parts) > 1:
        nxt = [op(parts[i], parts[i + 1]) for i in range(0, len(parts) - 1, 2)]
        parts = nxt + (parts[-1:] if len(parts) % 2 else [])
    return parts[0]


def _fold8(x, op):
    return _tree([x[i * 8:(i + 1) * 8] for i in range(x.shape[0] // 8)], op)


def _kth_key(count_fn, topk, shape):
    def vbit(it, acc):
        cand = acc | jnp.left_shift(jnp.int32(1), 31 - it)
        cmp = cand ^ INT_MIN
        cnt = count_fn(lambda kk, pos: jnp.where(kk >= cmp, 1, 0))
        return jnp.where(cnt >= topk, cand, acc)

    return lax.fori_loop(0, 32, vbit, jnp.zeros(shape, I32)) ^ INT_MIN


def _select_topk(count_fn, topk, pos_bits, shape, lp_ref):
    zeros = jnp.zeros(shape, I32)
    thr = _kth_key(count_fn, topk, shape)
    n_ge = count_fn(lambda kk, pos: jnp.where(kk >= thr, 1, 0))
    excess = jnp.where(thr == INT_MIN, 0, n_ge - topk)
    lp_ref[...] = jnp.full(shape, 2 ** 31 - 1, I32)

    @pl.when(jnp.max(excess) > 0)
    def _():
        need = topk - count_fn(lambda kk, pos: jnp.where(kk > thr, 1, 0))

        def pbit(it, acc):
            cand = acc | jnp.left_shift(jnp.int32(1), pos_bits - 1 - it)
            cnt = count_fn(lambda kk, pos: jnp.where(kk == thr, jnp.where(pos < cand, 1, 0), 0))
            return jnp.where(cnt < need, cand, acc)

        lp_ref[...] = lax.fori_loop(0, pos_bits, pbit, zeros)

    return thr, lp_ref[...]


def _select_bias(kk, pos, thr, last_pos):
    tie = jnp.where(pos <= last_pos, 0.0, NEG_BIG)
    bias = jnp.where(kk == thr, tie, jnp.where(kk > thr, 0.0, NEG_BIG))
    return jnp.where(kk == INT_MIN, NEG_BIG, bias)


def _pattn_kernel(qi_ref, kwq_ref, q_ref, kwk_ref, k_ref, vt_ref, o_ref,
                  keys_ref, s_ref, qip_ref, qg_ref, acc_ref, m_ref, l_ref, ties_ref, *, tq, topk):
    qb = pl.program_id(1)
    nch = qb + 1
    nslab = tq // 8
    shape8 = (8, tq)
    sub8 = lax.broadcasted_iota(I32, shape8, 0)
    lane_q = lax.broadcasted_iota(I32, (tq, LANES), 1)
    key_in = lax.broadcasted_iota(I32, (tq, tq), 0)
    qry_in = lax.broadcasted_iota(I32, (tq, tq), 1)

    wt = jnp.transpose(kwq_ref[...]) * ((IDX_DIM ** -0.5) * (IDX_HEADS ** -0.5))
    for hh in range(IDX_HEADS):
        slab = qi_ref[:, (hh // 2) * LANES:(hh // 2 + 1) * LANES]
        if hh % 2:
            slab = pltpu.roll(slab, IDX_DIM, 1)
        qip_ref[hh] = jnp.where(lane_q < IDX_DIM, slab, 0.0).astype(BF16)

    def for_chunks(body):
        odd = nch % 2

        @pl.when(odd == 1)
        def _():
            body(0)

        def pair(p, carry):
            kc = odd + 2 * p
            body(kc)
            body(kc + 1)
            return carry

        lax.fori_loop(0, nch // 2, pair, 0)

    def idx_body(kc):
        start = pl.multiple_of(kc * tq, tq)
        ks = kwk_ref[pl.ds(start, tq), :].astype(BF16)
        acc = None
        for hh in range(IDX_HEADS):
            s = _dot_nt(ks, qip_ref[hh])
            term = jnp.maximum(s, 0.0) * wt[IDX_DIM + hh:IDX_DIM + hh + 1, :]
            acc = term if acc is None else acc + term
        kk = jnp.where(acc == 0.0, 0, _score_keys(acc))
        keys_ref[kc] = jnp.where(kc * tq + key_in <= qb * tq + qry_in, kk, INT_MIN)

    for_chunks(idx_body)

    def count_fn(ind):
        def body(kc, acc):
            base = kc * tq
            parts = [ind(keys_ref[kc, i * 8:(i + 1) * 8, :], base + i * 8 + sub8) for i in range(nslab)]
            return acc + _tree(parts, jnp.add)
        acc = lax.fori_loop(0, nch, body, jnp.zeros(shape8, I32))
        return jnp.broadcast_to(jnp.sum(acc, axis=0, keepdims=True), shape8)

    thr = _kth_key(count_fn, topk, shape8)
    need = topk - count_fn(lambda kk, pos: jnp.where(kk > thr, 1, 0))

    def chunk_bias(kc, ties_before):
        slabs = []
        for i in range(nslab):
            kk = keys_ref[kc, i * 8:(i + 1) * 8, :]
            tie = jnp.where(kk == thr, 1, 0)
            incl = tie
            for sh in (1, 2, 4):
                incl = incl + jnp.where(sub8 >= sh, pltpu.roll(incl, sh, 0), 0)
            slabs.append((kk, incl - tie, jnp.broadcast_to(incl[7:8, :], shape8)))
        out = []
        for kk, excl, total in slabs:
            rank = ties_before + excl
            bias = jnp.where(kk > thr, 0.0, NEG_BIG)
            bias = jnp.where(kk == thr, jnp.where(rank < need, 0.0, NEG_BIG), bias)
            out.append(jnp.where(kk == INT_MIN, NEG_BIG, bias))
            ties_before = ties_before + total
        return jnp.concatenate(out, axis=0), ties_before

    scale = (HEAD_DIM ** -0.5) * 1.4426950408889634
    for g in range(C_KV_HEADS):
        qslab = q_ref[:, g * LANES:(g + 1) * LANES] * scale
        rolled = pltpu.roll(qslab, HEAD_DIM, 1)
        koff = g % 2
        in_half = (lane_q >= HEAD_DIM) if koff else (lane_q < HEAD_DIM)
        qg_ref[g] = jnp.concatenate(
            [jnp.where(in_half, qslab if r == koff else rolled, 0.0) for r in range(2)],
            axis=0).astype(BF16)

    m_ref[...] = jnp.full(m_ref.shape, NEG_BIG, F32)
    ties_ref[...] = jnp.zeros(shape8, I32)

    def max_body(kc):
        start = pl.multiple_of(kc * tq, tq)
        bias, ties = chunk_bias(kc, ties_ref[...])
        ties_ref[...] = ties
        bias2 = jnp.concatenate([bias, bias], axis=1)
        for g in range(C_KV_HEADS):
            kcol = (g // 2) * LANES
            kch = k_ref[pl.ds(start, tq), kcol:kcol + LANES].astype(BF16)
            s = _dot_nt(kch, qg_ref[g]) + bias2
            s_ref[kc, :, g * 2 * tq:(g + 1) * 2 * tq] = s
            m_ref[g] = jnp.maximum(m_ref[g], _fold8(s, jnp.maximum))

    for_chunks(max_body)
    m_row = [jnp.max(m_ref[g], axis=0, keepdims=True) for g in range(C_KV_HEADS)]

    acc_ref[...] = jnp.zeros_like(acc_ref)
    l_ref[...] = jnp.zeros_like(l_ref)

    def sum_body(kc):
        start = pl.multiple_of(kc * tq, tq)
        for g in range(C_KV_HEADS):
            p = jnp.exp2(s_ref[kc, :, g * 2 * tq:(g + 1) * 2 * tq] - m_row[g])
            vt = vt_ref[0, g * HEAD_DIM:(g + 1) * HEAD_DIM, pl.ds(start, tq)].astype(BF16)
            acc_ref[g] += _dot(vt, p.astype(BF16))
            l_ref[g] += _fold8(p, jnp.add)

    for_chunks(sum_body)
    for g in range(C_KV_HEADS):
        out_t = acc_ref[g] / jnp.sum(l_ref[g], axis=0, keepdims=True)
        o_ref[:, g * LANES:(g + 1) * LANES] = jnp.transpose(
            jnp.concatenate([out_t[:, :tq], out_t[:, tq:]], axis=0))


def _prompt_attention(q, k, vt, qi, kw, nb, seq):
    assert C_HEADS == 2 * C_KV_HEADS
    topk = min(TOPK_MAX, seq // 4)
    tq = _tile(seq, 256)
    assert tq % LANES == 0
    nq = seq // tq
    qrow = lambda b, c: (b * nq + c, 0)
    krow = lambda b, c: (b, 0)
    return pl.pallas_call(
        functools.partial(_pattn_kernel, tq=tq, topk=topk),
        grid=(nb, nq),
        in_specs=[pl.BlockSpec((tq, IDX_HEADS * IDX_DIM), qrow), pl.BlockSpec((tq, LANES), qrow),
                  pl.BlockSpec((tq, C_DIM), qrow), pl.BlockSpec((seq, LANES), krow),
                  pl.BlockSpec((seq, C_KV_DIM), krow),
                  pl.BlockSpec((1, C_KV_DIM, seq), lambda b, c: (b, 0, 0))],
        out_specs=pl.BlockSpec((tq, C_DIM), qrow),
        out_shape=jax.ShapeDtypeStruct((nb * seq, C_DIM), F32),
        scratch_shapes=[pltpu.VMEM((nq, tq, tq), I32),
                        pltpu.VMEM((nq, tq, C_KV_HEADS * 2 * tq), F32),
                        pltpu.VMEM((IDX_HEADS, tq, LANES), BF16),
                        pltpu.VMEM((C_KV_HEADS, 2 * tq, LANES), BF16),
                        pltpu.VMEM((C_KV_HEADS, HEAD_DIM, 2 * tq), F32),
                        pltpu.VMEM((C_KV_HEADS, 8, 2 * tq), F32),
                        pltpu.VMEM((C_KV_HEADS, 8, 2 * tq), F32), pltpu.VMEM((8, tq), I32)],
        compiler_params=_cparams(2), name="prompt_attn")(qi, kw, q, kw, k, vt)


SROWS = 8


def _sidx_kernel(pt_ref, qh_ref, wh_ref, kn_ref, *rest, npg, page):
    pages = rest[:npg]
    past_o, new_o = rest[npg:]
    qh = qh_ref[0].astype(BF16)
    wh = wh_ref[0] * ((IDX_DIM ** -0.5) * (IDX_HEADS ** -0.5))

    def head_sum(s):
        s = jnp.maximum(s, 0.0) * wh
        return _tree([s[hh * SROWS:(hh + 1) * SROWS] for hh in range(IDX_HEADS)], jnp.add)

    keys_t = jnp.concatenate([pg[0] for pg in pages], axis=1).astype(BF16)
    past_o[0] = head_sum(_dot(qh, keys_t))

    @pl.when(pl.program_id(1) == 0)
    def _():
        new_o[0] = head_sum(_dot_nt(qh, kn_ref[0].astype(BF16)))


def _sample_index_scores(cache_kidx, page_table, layer, qh, wh, ki_new):
    depth, n_phys, page, _ = cache_kidx.shape
    db, n_pages = page_table.shape
    npg = _pages_per_step(n_pages, 2 * PAGES_PER_STEP)
    cache = cache_kidx.transpose(0, 1, 3, 2).reshape(depth * n_phys, IDX_DIM, page)
    base = layer * n_phys

    def page_spec(i):
        return pl.BlockSpec((1, IDX_DIM, page), lambda b, s, pt: (base + pt[b, s * npg + i], 0, 0))

    per_b = lambda b, s, pt: (b, 0, 0)
    grid_spec = pltpu.PrefetchScalarGridSpec(
        num_scalar_prefetch=1, grid=(db, n_pages // npg),
        in_specs=[pl.BlockSpec((1, IDX_HEADS * SROWS, IDX_DIM), per_b),
                  pl.BlockSpec((1, IDX_HEADS * SROWS, 1), per_b),
                  pl.BlockSpec((1, LANES, IDX_DIM), per_b)] + [page_spec(i) for i in range(npg)],
        out_specs=[pl.BlockSpec((1, SROWS, npg * page), lambda b, s, pt: (b, 0, s)),
                   pl.BlockSpec((1, SROWS, LANES), per_b)])
    return pl.pallas_call(
        functools.partial(_sidx_kernel, npg=npg, page=page), grid_spec=grid_spec,
        out_shape=[jax.ShapeDtypeStruct((db, SROWS, n_pages * page), F32),
                   jax.ShapeDtypeStruct((db, SROWS, LANES), F32)],
        compiler_params=_cparams(2), name="sample_index")(page_table, qh, wh, ki_new, *([cache] * npg))


def _ssel_kernel(past_ref, new_ref, bpast_o, bnew_o, keys_ref, lp_ref, *, topk, pos_bits, n_new):
    rows, past = past_ref.shape
    nch = past // LANES
    lane = lax.broadcasted_iota(I32, (rows, LANES), 1)
    t_row = lax.broadcasted_iota(I32, (rows, LANES), 0) % SROWS

    def fill(c, carry):
        start = pl.multiple_of(c * LANES, LANES)
        keys_ref[c] = _score_keys(past_ref[:, pl.ds(start, LANES)])
        return carry

    lax.fori_loop(0, nch, fill, 0)
    new_ok = lane <= jnp.minimum(t_row, n_new - 1)
    keys_ref[nch] = jnp.where(new_ok, _score_keys(new_ref[...]), INT_MIN)

    grp = 8
    assert nch % grp == 0

    def count_fn(pred):
        def body(q, acc):
            c0 = q * grp
            return acc + _tree([pred(keys_ref[c0 + i], (c0 + i) * LANES + lane) for i in range(grp)], jnp.add)
        acc = lax.fori_loop(0, nch // grp, body, jnp.zeros((rows, LANES), I32))
        return _lane_total(acc + pred(keys_ref[nch], nch * LANES + lane))

    thr, last_pos = _select_topk(count_fn, topk, pos_bits, (rows, LANES), lp_ref)

    def emit(c, carry):
        start = pl.multiple_of(c * LANES, LANES)
        bpast_o[:, pl.ds(start, LANES)] = _select_bias(keys_ref[c], c * LANES + lane, thr, last_pos)
        return carry

    lax.fori_loop(0, nch, emit, 0)
    bnew_o[...] = _select_bias(keys_ref[nch], nch * LANES + lane, thr, last_pos)


def _sample_select(sc_past, sc_new, n_new):
    rows, past = sc_past.shape
    topk = min(TOPK_MAX, (past + n_new) // 4)
    tr = _tile(rows, 64)
    pos_bits = int(past + LANES - 1).bit_length()
    row = lambda i: (i, 0)
    return pl.pallas_call(
        functools.partial(_ssel_kernel, topk=topk, pos_bits=pos_bits, n_new=n_new),
        grid=(rows // tr,),
        in_specs=[pl.BlockSpec((tr, past), row), pl.BlockSpec((tr, LANES), row)],
        out_specs=[pl.BlockSpec((tr, past), row), pl.BlockSpec((tr, LANES), row)],
        out_shape=[jax.ShapeDtypeStruct((rows, past), F32), jax.ShapeDtypeStruct((rows, LANES), F32)],
        scratch_shapes=[pltpu.VMEM((past // LANES + 1, tr, LANES), I32), pltpu.VMEM((tr, LANES), I32)],
        compiler_params=_cparams(1), name="sample_select")(sc_past, sc_new)


def _satt_kernel(pt_ref, qbd_ref, bpast_ref, bnew_ref, kn_ref, vn_ref, *rest, npg, page):
    kpages = rest[:npg]
    vpages = rest[npg:2 * npg]
    o_ref, m_ref, l_ref, acc_ref = rest[2 * npg:]
    s_id = pl.program_id(1)
    nrow = qbd_ref.shape[1]
    reps = nrow // SROWS

    @pl.when(s_id == 0)
    def _():
        m_ref[...] = jnp.full_like(m_ref, NEG_BIG)
        l_ref[...] = jnp.zeros_like(l_ref)
        acc_ref[...] = jnp.zeros_like(acc_ref)

    qbd = qbd_ref[0].astype(BF16)

    def update(s, bias, pv):
        s = s + jnp.concatenate([bias] * reps, axis=0)
        m_old = m_ref[...]
        m_new = jnp.maximum(m_old, jnp.max(s, axis=-1, keepdims=True))
        p = jnp.exp(s - m_new)
        alpha = jnp.exp(m_old - m_new)
        l_ref[...] = alpha * l_ref[...] + jnp.sum(p, axis=-1, keepdims=True)
        acc_ref[...] = alpha * acc_ref[...] + pv(p.astype(BF16))
        m_ref[...] = m_new

    keys_t = jnp.concatenate([kp[0] for kp in kpages], axis=1).astype(BF16)
    vals_t = jnp.concatenate([vp[0] for vp in vpages], axis=1).astype(BF16)
    update(_dot(qbd, keys_t), bpast_ref[0], lambda p: _dot_nt(p, vals_t))

    @pl.when(s_id == pl.num_programs(1) - 1)
    def _():
        vn = vn_ref[0].astype(BF16)
        update(_dot_nt(qbd, kn_ref[0].astype(BF16)), bnew_ref[0], lambda p: _dot(p, vn))
        o_ref[0] = acc_ref[...] / l_ref[...]


def _sample_attend(cache_k, cache_v, page_table, layer, qbd, bias_past, bias_new, k_new, v_new):
    depth, n_phys, page = cache_k.shape[:3]
    db, n_pages = page_table.shape
    npg = _pages_per_step(n_pages)
    ck = cache_k.transpose(0, 1, 3, 4, 2).reshape(depth * n_phys, C_KV_DIM, page)
    cv = cache_v.transpose(0, 1, 3, 4, 2).reshape(depth * n_phys, C_KV_DIM, page)
    base = layer * n_phys
    nrow = qbd.shape[1]

    def page_spec(i):
        return pl.BlockSpec((1, C_KV_DIM, page), lambda b, s, pt: (base + pt[b, s * npg + i], 0, 0))

    per_b = lambda b, s, pt: (b, 0, 0)
    grid_spec = pltpu.PrefetchScalarGridSpec(
        num_scalar_prefetch=1, grid=(db, n_pages // npg),
        in_specs=[pl.BlockSpec((1, nrow, C_KV_DIM), per_b),
                  pl.BlockSpec((1, SROWS, npg * page), lambda b, s, pt: (b, 0, s)),
                  pl.BlockSpec((1, SROWS, LANES), per_b),
                  pl.BlockSpec((1, LANES, C_KV_DIM), per_b),
                  pl.BlockSpec((1, LANES, C_KV_DIM), per_b)]
                 + [page_spec(i) for i in range(npg)] * 2,
        out_specs=pl.BlockSpec((1, nrow, C_KV_DIM), per_b),
        scratch_shapes=[pltpu.VMEM((nrow, 1), F32), pltpu.VMEM((nrow, 1), F32),
                        pltpu.VMEM((nrow, C_KV_DIM), F32)])
    return pl.pallas_call(
        functools.partial(_satt_kernel, npg=npg, page=page), grid_spec=grid_spec,
        out_shape=jax.ShapeDtypeStruct((db, nrow, C_KV_DIM), F32),
        compiler_params=_cparams(2), name="sample_attn")(
            page_table, qbd, bias_past, bias_new, k_new, v_new, *([ck] * npg), *([cv] * npg))


def _merge_kernel(x_ref, g_ref, wg_ref, os_ref, bon_ref, gg_ref, lw_ref, lb_ref, bd_ref,
                  ob_ref, oc_ref, wpa_ref, wpb_ref, wpc_ref, wo_ref, o_ref):
    x = x_ref[...]
    d = x.shape[1]
    h = _rms(x, g_ref[...]).astype(BF16)
    gates = jax.nn.sigmoid(_dot(h, wg_ref[...]))
    bd = bd_ref[...]
    o = os_ref[...]
    mean = _segsum(o, bd, split=True) * (1.0 / HEAD_DIM)
    cen = o - mean
    var = _segsum(cen * cen, bd) * (1.0 / HEAD_DIM)
    on = cen * lax.rsqrt(var + GN_EPS) * lw_ref[...] + lb_ref[...]
    oa = ((on + bon_ref[...]) * gg_ref[...]).astype(BF16)
    merged = (gates[:, :d] * _dot(oa, wpa_ref[...])
              + gates[:, d:2 * d] * _dot(ob_ref[...].astype(BF16), wpb_ref[...])
              + gates[:, 2 * d:] * _dot(oc_ref[...].astype(BF16), wpc_ref[...]))
    o_ref[...] = x + _dot(merged.astype(BF16), wo_ref[...])


def _merge(x, o_scan, bonus, g, o_b, o_c, lw, nb, seq):
    m, d = x.shape
    tm = _tile(seq, 512)
    nc = seq // tm
    row = lambda b, c: (b * nc + c, 0)
    tok = lambda w: pl.BlockSpec((tm, w), row)
    consts_a = [lw["mix_norm"], lw["w_g"]]
    consts_b = [lw["rwkv_lnx_w"], lw["rwkv_lnx_b"], lw["bd"]]
    consts_c = [lw["w_pa"], lw["w_pb"], lw["w_pc"], lw["w_out"]]
    in_specs = ([tok(d)] + [_const_spec(c.shape) for c in consts_a]
                + [pl.BlockSpec((tm, A_DIM), lambda b, c: (c, b))] + [tok(A_DIM)] * 2
                + [_const_spec(c.shape) for c in consts_b] + [tok(B_DIM), tok(C_DIM)]
                + [_const_spec(c.shape) for c in consts_c])
    return pl.pallas_call(
        _merge_kernel, grid=(nb, nc), in_specs=in_specs, out_specs=tok(d),
        out_shape=jax.ShapeDtypeStruct((m, d), F32), compiler_params=_cparams(2), name="merge")(
            x, *consts_a, o_scan, bonus, g, *consts_b, o_b, o_c, *consts_c)


def _rope_tables(pos):
    inv = ROPE_THETA ** (-jnp.arange(ROT_HALF, dtype=F32) / ROT_HALF)
    ang = pos.astype(F32)[:, None] * inv[None, :]
    c, s = jnp.cos(ang), jnp.sin(ang)
    t = pos.shape[0]
    pad = jnp.zeros((t, HEAD_DIM - ROT_DIM), F32)
    zer = jnp.zeros((t, ROT_HALF), F32)
    cos = jnp.concatenate([c, c, pad + 1.0], axis=1)
    sa = jnp.concatenate([-s, zer, pad], axis=1)
    sb = jnp.concatenate([zer, s, pad], axis=1)
    return tuple(jnp.tile(z, (1, C_HEADS)) for z in (cos, sa, sb))


def _layer_weights(i, p):
    d = p["w_in"].shape[1]
    w_in = p["w_in"][i]
    o1 = RWKV_COLS
    o2 = o1 + CONV_COLS
    o3 = o2 + ATTN_COLS
    row = lambda v: v.reshape(1, -1)
    tile_h = lambda v, n: jnp.tile(v, n).reshape(1, -1)
    z = jnp.zeros((LORA_W, A_DIM), F32)
    w_wa = jnp.concatenate([jnp.concatenate([p["rwkv_w2"][i], z], axis=1),
                            jnp.concatenate([z, p["rwkv_a2"][i]], axis=1)], axis=0)
    head = np.arange(A_DIM) // HEAD_DIM
    bd = jnp.asarray(head[:, None] == head[None, :], BF16)
    idx_norm = jnp.concatenate([p["idx_k_norm"][i], jnp.zeros((LANES - IDX_DIM,), F32)])
    bf = lambda w: w.astype(BF16)
    return dict(
        ffn1_norm=p["ffn1_norm"][i], ffn1_wg=bf(p["ffn1_wg"][i]), ffn1_wu=bf(p["ffn1_wu"][i]),
        ffn1_wd=bf(p["ffn1_wd"][i]),
        ffn2_norm=p["ffn2_norm"][i], ffn2_wg=bf(p["ffn2_wg"][i]), ffn2_wu=bf(p["ffn2_wu"][i]),
        ffn2_wd=bf(p["ffn2_wd"][i]),
        mix_norm=row(p["mix_norm"][i]),
        w_a=bf(w_in[:, :o1]), w_b=bf(w_in[:, o1:o2]),
        w_c=bf(jnp.pad(w_in[:, o2:o3], ((0, 0), (0, ATTN_COLS_PAD - ATTN_COLS)))),
        w_g=bf(w_in[:, o3:]),
        rwkv_mu=row(p["rwkv_mu"][i]), rwkv_w0=row(p["rwkv_w0"][i]), rwkv_a0=row(p["rwkv_a0"][i]),
        w_wa=bf(w_wa), rwkv_g2=bf(p["rwkv_g2"][i]), rwkv_kk=row(p["rwkv_kk"][i]),
        rwkv_ka=row(p["rwkv_ka"][i]), rwkv_rk=row(p["rwkv_rk"][i]),
        rwkv_lnx_w=row(p["rwkv_lnx_w"][i]), rwkv_lnx_b=row(p["rwkv_lnx_b"][i]),
        conv_w=p["conv_w"][i], bd=bd,
        q_norm=tile_h(p["q_norm"][i], C_HEADS), k_norm=tile_h(p["k_norm"][i], C_KV_HEADS),
        idx_k_norm=row(idx_norm),
        w_pa=bf(p["w_pa"][i]), w_pb=bf(p["w_pb"][i]), w_pc=bf(p["w_pc"][i]), w_out=bf(p["w_out"][i]),
        ple_norm=p["ple_norm"][i], ple_gate=bf(p["ple_gate"][i]), ple_proj=bf(p["ple_proj"][i]),
    )


def _rwkv_branch(x, nb, seq, lw, wkv0, shift_rows, between=None):
    pre = _rwkv_pre(x, nb, seq, lw, shift_rows)
    s0 = wkv0.transpose(3, 2, 0, 1).reshape(HEAD_DIM, HEAD_DIM, nb * A_HEADS)
    other = None
    if between is not None:
        x, pre = lax.optimization_barrier((x, pre))
        other = between(x)
    r, w, k, v, na, nb_, g, bonus, shift_o = pre
    nbh = nb * A_HEADS
    if shift_rows is None:
        ts = lambda z: z.reshape(nbh, HEAD_DIM, seq).transpose(2, 1, 0)
    else:
        ts = lambda z: z.reshape(nb, seq, A_HEADS, HEAD_DIM).transpose(1, 3, 0, 2).reshape(seq, HEAD_DIM, nbh)
    seqs = [ts(z) for z in (r, w, k, v, na, nb_)]
    if between is not None:
        s0, other = lax.optimization_barrier((s0, other))
    o, s_fin = _scan(*seqs, s0)
    wkv1 = s_fin.reshape(HEAD_DIM, HEAD_DIM, nb, A_HEADS).transpose(2, 3, 1, 0)
    if shift_rows is None:
        shift1 = shift_o.reshape(nb, RWKV_COLS)
        o_tok = o.transpose(0, 2, 1).reshape(seq, nb * A_DIM)
    else:
        shift1 = shift_o.reshape(nb, seq, RWKV_COLS)[:, -1]
        o_tok = o.reshape(seq, HEAD_DIM, nb, A_HEADS).transpose(2, 0, 3, 1).reshape(nb * seq, A_DIM)
    return o_tok, bonus, g, shift1, wkv1, other


def _layer_tail(x1, o_scan, bonus, g, o_b, o_c, p_emb, lw, nb, seq):
    x2 = _merge(x1, o_scan, bonus, g, o_b, o_c, lw, nb, seq)
    return _ffn(x2, lw["ffn2_norm"], lw["ffn2_wg"], lw["ffn2_wu"], lw["ffn2_wd"],
                ple=(p_emb, lw["ple_norm"], lw["ple_gate"], lw["ple_proj"]))


def _prompt_layer(x, p_emb, nb, seq, lw, tabs):
    x1 = _ffn(x, lw["ffn1_norm"], lw["ffn1_wg"], lw["ffn1_wu"], lw["ffn1_wd"])
    wkv0 = jnp.zeros((nb, A_HEADS, HEAD_DIM, HEAD_DIM), F32)

    def other_mixers(xin):
        o_b, conv1 = _conv(xin, nb, seq, lw)
        q, k, qi, kw, kt, vt, kit = _attn_proj(xin, nb, seq, lw, tabs, False)
        return o_b, conv1, kt, vt, kit, _prompt_attention(q, k, vt, qi, kw, nb, seq)

    o_scan, bonus, g, shift1, wkv1, other = _rwkv_branch(x1, nb, seq, lw, wkv0, None, other_mixers)
    o_b, conv1, kt, vt, kit, o_c = other
    x4 = _layer_tail(x1, o_scan, bonus, g, o_b, o_c, p_emb, lw, nb, seq)
    heads = lambda z: z.reshape(nb, C_KV_HEADS, HEAD_DIM, seq).transpose(0, 3, 1, 2)
    st = (heads(kt), heads(vt), kit.transpose(0, 2, 1), wkv1, shift1, conv1)
    return x4, st


def _sample_layer(x, p_emb, nb, seq, lw, tabs, layer, cache_k, cache_v, cache_kidx, page_table,
                  wkv0, shift0, conv0):
    assert seq <= SROWS
    x1 = _ffn(x, lw["ffn1_norm"], lw["ffn1_wg"], lw["ffn1_wu"], lw["ffn1_wd"])
    rep_rows = lambda z: jnp.repeat(z, seq, axis=0)
    o_scan, bonus, g, shift1, wkv1, _ = _rwkv_branch(x1, nb, seq, lw, wkv0, rep_rows(shift0))
    o_b, z_all = _conv(x1, nb, seq, lw, init=(rep_rows(conv0[:, 0]), rep_rows(conv0[:, 1])))
    conv1 = z_all.reshape(nb, seq, B_DIM)[:, seq - (CONV_W - 1):]
    q, k, qi, kw, v, ki = _attn_proj(x1, nb, seq, lw, tabs, True)

    pad_t = lambda z: jnp.pad(z, ((0, 0), (0, SROWS - seq)) + ((0, 0),) * (z.ndim - 2))
    qh = pad_t(qi.reshape(nb, seq, IDX_HEADS, IDX_DIM)).transpose(0, 2, 1, 3)
    qh = qh.reshape(nb, IDX_HEADS * SROWS, IDX_DIM)
    wi = kw[:, IDX_DIM:IDX_DIM + IDX_HEADS].reshape(nb, seq, IDX_HEADS)
    wh = pad_t(wi).transpose(0, 2, 1).reshape(nb, IDX_HEADS * SROWS, 1)
    pad_keys = lambda z: jnp.pad(z.reshape(nb, seq, -1), ((0, 0), (0, LANES - seq), (0, 0)))
    sc_past, sc_new = _sample_index_scores(cache_kidx, page_table, layer, qh, wh, pad_keys(ki))
    past = sc_past.shape[-1]
    b_past, b_new = _sample_select(sc_past.reshape(nb * SROWS, past), sc_new.reshape(nb * SROWS, LANES), seq)

    rep = C_HEADS // C_KV_HEADS
    qg = pad_t(q.reshape(nb, seq, C_KV_HEADS, rep, HEAD_DIM)).transpose(0, 2, 3, 1, 4)
    qg = qg * (HEAD_DIM ** -0.5)
    eye = jnp.eye(C_KV_HEADS, dtype=F32)
    qbd = jnp.einsum("bgrtd,gh->bgrthd", qg, eye).reshape(nb, C_HEADS * SROWS, C_KV_DIM)
    o = _sample_attend(cache_k, cache_v, page_table, layer, qbd,
                       b_past.reshape(nb, SROWS, past), b_new.reshape(nb, SROWS, LANES),
                       pad_keys(k), pad_keys(v))
    o = o.reshape(nb, C_KV_HEADS, rep, SROWS, C_KV_HEADS, HEAD_DIM)
    o = jnp.einsum("bgrthd,gh->bgrtd", o, eye)[:, :, :, :seq]
    o_c = o.transpose(0, 3, 1, 2, 4).reshape(nb * seq, C_DIM)

    x4 = _layer_tail(x1, o_scan, bonus, g, o_b, o_c, p_emb, lw, 1, nb * seq)
    st = (k.reshape(nb, seq, C_KV_HEADS, HEAD_DIM), v.reshape(nb, seq, C_KV_HEADS, HEAD_DIM),
          ki.reshape(nb, seq, IDX_DIM), wkv1, shift1, conv1)
    return x4, st


def kernel(x_prompt, x_sample, cache_k, cache_v, cache_kidx, state_wkv, state_shift, state_conv, page_table, p_prompt, p_sample, ffn1_norm, ffn1_wg, ffn1_wu, ffn1_wd, mix_norm, w_in, rwkv_mu, rwkv_w0, rwkv_w2, rwkv_a0, rwkv_a2, rwkv_g2, rwkv_kk, rwkv_ka, rwkv_rk, rwkv_lnx_w, rwkv_lnx_b, conv_w, q_norm, k_norm, idx_k_norm, w_pa, w_pb, w_pc, w_out, ffn2_norm, ffn2_wg, ffn2_wu, ffn2_wd, ple_norm, ple_gate, ple_proj):
    params = dict(ffn1_norm=ffn1_norm, ffn1_wg=ffn1_wg, ffn1_wu=ffn1_wu, ffn1_wd=ffn1_wd,
                  mix_norm=mix_norm, w_in=w_in, rwkv_mu=rwkv_mu, rwkv_w0=rwkv_w0, rwkv_w2=rwkv_w2,
                  rwkv_a0=rwkv_a0, rwkv_a2=rwkv_a2, rwkv_g2=rwkv_g2, rwkv_kk=rwkv_kk, rwkv_ka=rwkv_ka,
                  rwkv_rk=rwkv_rk.reshape(rwkv_rk.shape[0], -1), rwkv_lnx_w=rwkv_lnx_w,
                  rwkv_lnx_b=rwkv_lnx_b, conv_w=conv_w, q_norm=q_norm, k_norm=k_norm,
                  idx_k_norm=idx_k_norm, w_pa=w_pa, w_pb=w_pb, w_pc=w_pc, w_out=w_out,
                  ffn2_norm=ffn2_norm, ffn2_wg=ffn2_wg, ffn2_wu=ffn2_wu, ffn2_wd=ffn2_wd,
                  ple_norm=ple_norm, ple_gate=ple_gate, ple_proj=ple_proj)
    nb, seq, d = x_prompt.shape
    db, dseq, _ = x_sample.shape
    depth = w_in.shape[0]
    past = page_table.shape[1] * cache_k.shape[2]
    tabs_p = _rope_tables(jnp.arange(seq, dtype=jnp.int32))
    tabs_s = tuple(jnp.tile(z, (db, 1)) for z in _rope_tables(past + jnp.arange(dseq, dtype=jnp.int32)))
    xp = x_prompt.reshape(nb * seq, d)
    xs = x_sample.reshape(db * dseq, d)
    outs_p, outs_s = [], []
    for i in range(depth):
        lw = _layer_weights(i, params)
        xp, st_p = _prompt_layer(xp, (p_prompt.reshape(depth * nb * seq, -1), i), nb, seq, lw, tabs_p)
        xs, st_s = _sample_layer(xs, (p_sample.reshape(depth * db * dseq, -1), i), db, dseq, lw, tabs_s, i,
                                 cache_k, cache_v, cache_kidx, page_table,
                                 state_wkv[i], state_shift[i], state_conv[i])
        outs_p.append(st_p)
        outs_s.append(st_s)
    k_p, v_p, kidx_p, wkv_p, shift_p, conv_p = [jnp.stack(z) for z in zip(*outs_p)]
    k_s, v_s, kidx_s, wkv_s, shift_s, conv_s = [jnp.stack(z) for z in zip(*outs_s)]
    return (xp.reshape(nb, seq, d), xs.reshape(db, dseq, d), k_p, v_p, kidx_p, wkv_p, shift_p, conv_p,
            k_s, v_s, kidx_s, wkv_s, shift_s, conv_s)
```

```python
import functools

import jax
import jax.numpy as jnp
import numpy as np
from jax import lax
from jax.experimental import pallas as pl
from jax.experimental.pallas import tpu as pltpu

F32 = jnp.float32
BF16 = jnp.bfloat16
I32 = jnp.int32

HEAD_DIM = 64
A_HEADS = 8
A_DIM = A_HEADS * HEAD_DIM
LORA_W = 64
LORA_A = 64
LORA_G = 128
B_DIM = 512
CONV_W = 3
C_HEADS = 8
C_KV_HEADS = 4
C_DIM = C_HEADS * HEAD_DIM
C_KV_DIM = C_KV_HEADS * HEAD_DIM
IDX_HEADS = 8
IDX_DIM = 64
TOPK_MAX = 256
ROT_DIM = HEAD_DIM // 4
ROT_HALF = ROT_DIM // 2
ROPE_THETA = 500000.0
N_BRANCH = 3
RMS_EPS = 1e-6
GN_EPS = 64e-5
RWKV_COLS = 3 * A_DIM + LORA_W + LORA_A + LORA_G
CONV_COLS = 3 * B_DIM
ATTN_COLS = C_DIM + 2 * C_KV_DIM + IDX_HEADS * IDX_DIM + IDX_DIM + IDX_HEADS
ATTN_COLS_PAD = 1664

LANES = 128
INT_MIN = -2 ** 31
NEG_BIG = -1e30
VMEM_LIMIT = 56 * 1024 * 1024
PAGES_PER_STEP = 32


def _pages_per_step(n_pages, cap=PAGES_PER_STEP):
    npg = min(cap, n_pages)
    while n_pages % npg:
        npg -= 1
    return npg


def _cparams(n_axes):
    return pltpu.CompilerParams(dimension_semantics=("arbitrary",) * n_axes,
                                vmem_limit_bytes=VMEM_LIMIT)


def _const_spec(shape):
    nd = len(shape)
    return pl.BlockSpec(shape, lambda *_: (0,) * nd, pipeline_mode=pl.Buffered(1))


def _tile(m, pref):
    t = min(m, pref)
    while m % t:
        t -= 8
    return t


def _rms(x, g):
    ms = jnp.mean(x * x, axis=-1, keepdims=True)
    return x * lax.rsqrt(ms + RMS_EPS) * g


def _dot(a, b):
    return jnp.dot(a, b, preferred_element_type=F32)


def _dot_nt(a, b):
    return lax.dot_general(a, b, (((1,), (1,)), ((), ())), preferred_element_type=F32)


def _segsum(x, bd, split=False):
    hi = x.astype(BF16)
    if not split:
        return _dot(hi, bd)
    lo = (x - hi.astype(F32)).astype(BF16)
    return _dot(hi, bd) + _dot(lo, bd)


def _rope(x, cos, sa, sb):
    n = x.shape[-1]
    return x * cos + pltpu.roll(x, n - ROT_HALF, 1) * sa + pltpu.roll(x, ROT_HALF, 1) * sb


def _rows_before(x, k, fills, *, seg_len, first_chunk_rows=None):
    rows = x.shape[0]
    y = pltpu.roll(x, k, 0)
    ridx = lax.broadcasted_iota(I32, (rows, 1), 0)
    t = ridx if seg_len is None else ridx % seg_len
    for r in range(k):
        y = jnp.where(t == r, fills[k - 1 - r], y)
    return y


ROW_BLOCK = 256


def _row_blocks(tm, seg_len=None, blk=ROW_BLOCK):
    if seg_len is not None or tm % blk:
        return [slice(0, tm)]
    return [slice(i, i + blk) for i in range(0, tm, blk)]


def _softplus(y):
    return jnp.maximum(y, 0.0) + jnp.log(1.0 + jnp.exp(-jnp.abs(y)))


def _ffn_kernel(*refs, chunks, ple):
    if ple:
        x_ref, g_ref, wg_ref, wu_ref, wd_ref, p_ref, pn_ref, pg_ref, pp_ref, o_ref = refs
    else:
        x_ref, g_ref, wg_ref, wu_ref, wd_ref, o_ref = refs
    x = x_ref[...]
    h = _rms(x, g_ref[...]).astype(BF16)
    acc = None
    for lo, hi in chunks:
        gt = _dot(h, wg_ref[:, lo:hi])
        ut = _dot(h, wu_ref[:, lo:hi])
        act = (gt * jax.nn.sigmoid(gt) * ut).astype(BF16)
        d = _dot(act, wd_ref[lo:hi, :])
        acc = d if acc is None else acc + d
    y = x + 0.5 * acc
    if ple:
        hg = _rms(y, pn_ref[...]).astype(BF16)
        gate = jax.nn.sigmoid(_dot(hg, pg_ref[...]))
        y = y + gate * _dot(p_ref[...].astype(BF16), pp_ref[...])
    o_ref[...] = y


def _ffn(x, g, wg, wu, wd, ple=None):
    m, d = x.shape
    f = wg.shape[1]
    tm = _tile(m, 512)
    step = 1024
    chunks = tuple((lo, min(lo + step, f)) for lo in range(0, f, step))
    row = lambda i: (i, 0)
    in_specs = [pl.BlockSpec((tm, d), row), _const_spec((1, d)), _const_spec((d, f)),
                _const_spec((d, f)), _const_spec((f, d))]
    args = [x, g.reshape(1, d), wg, wu, wd]
    if ple is not None:
        (p, layer), pn, pg, pp = ple
        off = layer * (m // tm)
        in_specs += [pl.BlockSpec((tm, p.shape[1]), lambda i: (i + off, 0)), _const_spec((1, d)),
                     _const_spec(pg.shape), _const_spec(pp.shape)]
        args += [p, pn.reshape(1, d), pg, pp]
    return pl.pallas_call(
        functools.partial(_ffn_kernel, chunks=chunks, ple=ple is not None),
        grid=(m // tm,), in_specs=in_specs, out_specs=pl.BlockSpec((tm, d), row),
        out_shape=jax.ShapeDtypeStruct((m, d), F32), compiler_params=_cparams(1),
        name="ffn_ple" if ple is not None else "ffn")(*args)


def _rwkv_pre_kernel(*refs, seg_len):
    (x_ref, g_ref, wa_ref, mu_ref, w0_ref, a0_ref, wwa_ref, g2_ref, kkw_ref, ka_ref,
     rk_ref, bd_ref) = refs[:12]
    rest = refs[12:]
    if seg_len is None:
        r_o, w_o, k_o, v_o, na_o, nb_o, g_o, bonus_o, shift_o, carry_ref = rest
    else:
        init_ref, r_o, w_o, k_o, v_o, na_o, nb_o, g_o, bonus_o, shift_o = rest
    tm = x_ref.shape[0]
    bd = bd_ref[...]
    if seg_len is None:
        @pl.when(pl.program_id(1) == 0)
        def _():
            carry_ref[...] = jnp.zeros_like(carry_ref)
        prev_row = carry_ref[0:1, :]
    for rows in _row_blocks(tm, seg_len):
        rb = rows.stop - rows.start
        h = _rms(x_ref[rows, :], g_ref[...]).astype(BF16)
        u = _dot(h, wa_ref[...])
        if seg_len is None:
            u_prev = _rows_before(u, 1, [prev_row], seg_len=None)
            prev_row = u[rb - 1:rb, :]
        else:
            u_prev = _rows_before(u, 1, [init_ref[rows, :]], seg_len=seg_len)
            shift_o[rows, :] = u
        us = u + (u_prev - u) * mu_ref[...]
        r = us[:, 0:A_DIM]
        k = us[:, A_DIM:2 * A_DIM]
        v = us[:, 2 * A_DIM:3 * A_DIM]
        o3 = 3 * A_DIM
        xwa = us[:, o3:o3 + LORA_W + LORA_A]
        lane = lax.broadcasted_iota(I32, xwa.shape, 1)
        xwa = jnp.where(lane < LORA_W, jnp.tanh(xwa), xwa)
        lo = _dot(xwa.astype(BF16), wwa_ref[...])
        w_log = -_softplus(-(w0_ref[...] + lo[:, :A_DIM])) - 0.5
        log_decay = -jnp.exp(w_log)
        a = jax.nn.sigmoid(a0_ref[...] + lo[:, A_DIM:])
        xg = us[:, o3 + LORA_W + LORA_A:]
        g = _dot(jax.nn.sigmoid(xg).astype(BF16), g2_ref[...])
        kk = k * kkw_ref[...]
        kk = kk / jnp.maximum(jnp.sqrt(_segsum(kk * kk, bd)), 1e-12)
        k2 = k * (1.0 + (a - 1.0) * ka_ref[...])
        seq_outs = zip((r_o, w_o, k_o, v_o, na_o, nb_o), (r, log_decay, k2, v, -kk, kk * a))
        for o_ref, val in seq_outs:
            if seg_len is None:
                o_ref[:, rows] = jnp.transpose(val)
            else:
                o_ref[rows, :] = val
        g_o[rows, :] = g
        bonus_o[rows, :] = _segsum(r * k2 * rk_ref[...], bd) * v
    if seg_len is None:
        carry_ref[0:1, :] = prev_row
        shift_o[0] = prev_row


def _rwkv_pre(x, nb, seq, lw, init_rows=None):
    m, d = x.shape
    consts = [lw["mix_norm"], lw["w_a"], lw["rwkv_mu"], lw["rwkv_w0"], lw["rwkv_a0"], lw["w_wa"],
              lw["rwkv_g2"], lw["rwkv_kk"], lw["rwkv_ka"], lw["rwkv_rk"], lw["bd"]]
    const_specs = [_const_spec(c.shape) for c in consts]
    outs = [jax.ShapeDtypeStruct((m, A_DIM), F32)] * 8
    if init_rows is None:
        tm = _tile(seq, 512)
        nc = seq // tm
        row = lambda b, c: (b * nc + c, 0)
        grid = (nb, nc)
        in_specs = [pl.BlockSpec((tm, d), row)] + const_specs
        out_specs = ([pl.BlockSpec((A_DIM, tm), lambda b, c: (b, c))] * 6 + [pl.BlockSpec((tm, A_DIM), row)] * 2
                     + [pl.BlockSpec((1, 1, RWKV_COLS), lambda b, c: (b, 0, 0))])
        outs = ([jax.ShapeDtypeStruct((nb * A_DIM, seq), F32)] * 6 + outs[6:]
                + [jax.ShapeDtypeStruct((nb, 1, RWKV_COLS), F32)])
        scratch = [pltpu.VMEM((8, RWKV_COLS), F32)]
        args = [x] + consts
        seg_len = None
    else:
        tm = m
        row = lambda i: (0, 0)
        grid = (1,)
        in_specs = [pl.BlockSpec((tm, d), row)] + const_specs + [pl.BlockSpec((tm, RWKV_COLS), row)]
        out_specs = [pl.BlockSpec((tm, A_DIM), row)] * 8 + [pl.BlockSpec((tm, RWKV_COLS), row)]
        outs = outs + [jax.ShapeDtypeStruct((m, RWKV_COLS), F32)]
        scratch = []
        args = [x] + consts + [init_rows]
        seg_len = seq
    return pl.pallas_call(
        functools.partial(_rwkv_pre_kernel, seg_len=seg_len), grid=grid, in_specs=in_specs,
        out_specs=out_specs, out_shape=outs, scratch_shapes=scratch,
        compiler_params=_cparams(len(grid)), name="rwkv_pre")(*args)


def _scan_kernel(r_ref, lw_ref, k_ref, v_ref, a_ref, b_ref, s0_ref, o_ref, st_ref, s_ref, vec_ref, *, tc):
    c = pl.program_id(1)

    @pl.when(c == 0)
    def _():
        s_ref[...] = s0_ref[...]

    jb = 32
    ln = s_ref.shape[-1]

    def step(t, carry):
        cum, gam_prev = carry
        cum = cum + lw_ref[t]
        gam = jnp.exp(cum)
        ginv = jnp.exp(-cum)
        vec_ref[0] = a_ref[t] * gam_prev
        vec_ref[1] = b_ref[t] * ginv
        vec_ref[2] = k_ref[t] * ginv
        vec_ref[3] = r_ref[t] * gam

        def sa_body(q, sa):
            for jj in range(jb):
                j = q * jb + jj
                sa = sa + s_ref[j] * vec_ref[0, pl.ds(j, 1), :]
            return sa

        sa = lax.fori_loop(0, HEAD_DIM // jb, sa_body, jnp.zeros((HEAD_DIM, ln), F32))
        vt = v_ref[t]

        def up_body(q, o):
            for jj in range(jb):
                j = q * jb + jj
                sn = s_ref[j] + sa * vec_ref[1, pl.ds(j, 1), :] + vt * vec_ref[2, pl.ds(j, 1), :]
                s_ref[j] = sn
                o = o + sn * vec_ref[3, pl.ds(j, 1), :]
            return o

        o_ref[t] = lax.fori_loop(0, HEAD_DIM // jb, up_body, jnp.zeros((HEAD_DIM, ln), F32))
        return cum, gam

    zeros = jnp.zeros((HEAD_DIM, ln), F32)
    _, gam_end = lax.fori_loop(0, tc, step, (zeros, zeros + 1.0))
    vec_ref[0] = gam_end
    for j in range(HEAD_DIM):
        s_ref[j] = s_ref[j] * vec_ref[0, j:j + 1, :]

    @pl.when(c == pl.num_programs(1) - 1)
    def _():
        st_ref[...] = s_ref[...]


def _scan(r, w, k, v, a, b, s0):
    t_len, hd, nbh = r.shape
    ln = min(LANES, nbh)
    tc = _tile(t_len, 64) if t_len % 8 == 0 else t_len
    grid = (nbh // ln, t_len // tc)
    seq_spec = pl.BlockSpec((tc, hd, ln), lambda l, c: (c, 0, l))
    st_spec = pl.BlockSpec((hd, hd, ln), lambda l, c: (0, 0, l))
    return pl.pallas_call(
        functools.partial(_scan_kernel, tc=tc), grid=grid,
        in_specs=[seq_spec] * 6 + [st_spec], out_specs=[seq_spec, st_spec],
        out_shape=[jax.ShapeDtypeStruct((t_len, hd, nbh), F32),
                   jax.ShapeDtypeStruct((hd, hd, nbh), F32)],
        scratch_shapes=[pltpu.VMEM((hd, hd, ln), F32), pltpu.VMEM((4, hd, ln), F32)],
        compiler_params=_cparams(2), name="rwkv_scan")(r, w, k, v, a, b, s0)


def _conv_kernel(*refs, seg_len):
    x_ref, g_ref, wb_ref, cw_ref = refs[:4]
    rest = refs[4:]
    if seg_len is None:
        o_ref, st_ref, carry_ref = rest
    else:
        i0_ref, i1_ref, o_ref, st_ref = rest
    tm = x_ref.shape[0]
    cw = cw_ref[...]
    if seg_len is None:
        @pl.when(pl.program_id(1) == 0)
        def _():
            carry_ref[...] = jnp.zeros_like(carry_ref)
        hist = [carry_ref[1:2, :], carry_ref[0:1, :]]
    for rows in _row_blocks(tm, seg_len, ROW_BLOCK // 2):
        rb = rows.stop - rows.start
        h = _rms(x_ref[rows, :], g_ref[...]).astype(BF16)
        u = _dot(h, wb_ref[...])
        bg = u[:, :B_DIM]
        z = u[:, B_DIM:2 * B_DIM] * u[:, 2 * B_DIM:]
        if seg_len is None:
            z1 = _rows_before(z, 1, hist[:1], seg_len=None)
            z2 = _rows_before(z, 2, hist, seg_len=None)
            hist = [z[rb - 1:rb, :], z[rb - 2:rb - 1, :]]
        else:
            init = [i1_ref[rows, :], i0_ref[rows, :]]
            z1 = _rows_before(z, 1, init[:1], seg_len=seg_len)
            z2 = _rows_before(z, 2, init, seg_len=seg_len)
            st_ref[rows, :] = z
        o_ref[rows, :] = bg * (z2 * cw[0:1, :] + z1 * cw[1:2, :] + z * cw[2:3, :])
    if seg_len is None:
        last2 = jnp.concatenate([hist[1], hist[0]], axis=0)
        carry_ref[0:2, :] = last2
        st_ref[0] = last2


def _conv(x, nb, seq, lw, init=None):
    m, d = x.shape
    assert seq >= CONV_W - 1
    consts = [lw["mix_norm"], lw["w_b"], lw["conv_w"]]
    const_specs = [_const_spec(c.shape) for c in consts]
    if init is None:
        tm = _tile(seq, 512)
        nc = seq // tm
        row = lambda b, c: (b * nc + c, 0)
        grid = (nb, nc)
        in_specs = [pl.BlockSpec((tm, d), row)] + const_specs
        out_specs = [pl.BlockSpec((tm, B_DIM), row),
                     pl.BlockSpec((1, CONV_W - 1, B_DIM), lambda b, c: (b, 0, 0))]
        outs = [jax.ShapeDtypeStruct((m, B_DIM), F32), jax.ShapeDtypeStruct((nb, CONV_W - 1, B_DIM), F32)]
        scratch = [pltpu.VMEM((8, B_DIM), F32)]
        args = [x] + consts
        seg_len = None
    else:
        tm = m
        row = lambda i: (0, 0)
        grid = (1,)
        in_specs = [pl.BlockSpec((tm, d), row)] + const_specs + [pl.BlockSpec((tm, B_DIM), row)] * 2
        out_specs = [pl.BlockSpec((tm, B_DIM), row)] * 2
        outs = [jax.ShapeDtypeStruct((m, B_DIM), F32)] * 2
        scratch = []
        args = [x] + consts + list(init)
        seg_len = seq
    return pl.pallas_call(
        functools.partial(_conv_kernel, seg_len=seg_len), grid=grid, in_specs=in_specs,
        out_specs=out_specs, out_shape=outs, scratch_shapes=scratch,
        compiler_params=_cparams(len(grid)), name="shortconv")(*args)


def _attn_proj_kernel(x_ref, g_ref, wc_ref, qn_ref, kn_ref, in_ref, bd_ref, cos_ref, sa_ref, sb_ref,
                      *outs, transposed):
    bd = bd_ref[...]
    inv_hd = 1.0 / HEAD_DIM
    if transposed:
        q_o, k_o, qi_o, kw_o, kt_o, vt_o, kit_o = outs
    else:
        q_o, k_o, qi_o, kw_o, v_o, ki_o = outs
    for rows in _row_blocks(x_ref.shape[0], None if transposed else 1):
        h = _rms(x_ref[rows, :], g_ref[...]).astype(BF16)
        u = _dot(h, wc_ref[...])
        cos, sa, sb = cos_ref[rows, :], sa_ref[rows, :], sb_ref[rows, :]
        q = u[:, :C_DIM]
        q = q * lax.rsqrt(_segsum(q * q, bd) * inv_hd + RMS_EPS) * qn_ref[...]
        q = _rope(q, cos, sa, sb)
        k = u[:, C_DIM:C_DIM + C_KV_DIM]
        k = k * lax.rsqrt(_segsum(k * k, bd[:C_KV_DIM, :C_KV_DIM]) * inv_hd + RMS_EPS) * kn_ref[...]
        k = _rope(k, cos[:, :C_KV_DIM], sa[:, :C_KV_DIM], sb[:, :C_KV_DIM])
        o = C_DIM + C_KV_DIM
        v = u[:, o:o + C_KV_DIM]
        o += C_KV_DIM
        qi = _rope(u[:, o:o + IDX_HEADS * IDX_DIM], cos, sa, sb)
        o += IDX_HEADS * IDX_DIM
        kw = u[:, o:o + LANES]
        lane = lax.broadcasted_iota(I32, kw.shape, 1)
        is_ki = lane < IDX_DIM
        ms = jnp.sum(jnp.where(is_ki, kw * kw, 0.0), axis=-1, keepdims=True) * (1.0 / IDX_DIM)
        kin = _rope(kw * lax.rsqrt(ms + RMS_EPS) * in_ref[...], cos[:, :LANES], sa[:, :LANES], sb[:, :LANES])
        kw = jnp.where(is_ki, kin, kw)
        if transposed:
            kt_o[0, :, rows] = jnp.transpose(k)
            vt_o[0, :, rows] = jnp.transpose(v)
            kit_o[0, :, rows] = jnp.transpose(kw)[:IDX_DIM, :]
        else:
            v_o[rows, :] = v
            ki_o[rows, :] = kw[:, :IDX_DIM]
        q_o[rows, :] = q
        k_o[rows, :] = k
        qi_o[rows, :] = qi
        kw_o[rows, :] = kw


def _attn_proj(x, nb, seq, lw, tabs, sample):
    m, d = x.shape
    consts = [lw["mix_norm"], lw["w_c"], lw["q_norm"], lw["k_norm"], lw["idx_k_norm"], lw["bd"]]
    const_specs = [_const_spec(c.shape) for c in consts]
    if sample:
        tm, nc = m, 1
        grid = (1, 1)
    else:
        tm = _tile(seq, 512)
        nc = seq // tm
        grid = (nb, nc)
    row = lambda b, c: (b * nc + c, 0)
    tab = lambda b, c: (c, 0)
    widths = [C_DIM, C_KV_DIM, IDX_HEADS * IDX_DIM, LANES]
    out_specs = [pl.BlockSpec((tm, w), row) for w in widths]
    out_shape = [jax.ShapeDtypeStruct((m, w), F32) for w in widths]
    if sample:
        for w in (C_KV_DIM, IDX_DIM):
            out_specs.append(pl.BlockSpec((tm, w), row))
            out_shape.append(jax.ShapeDtypeStruct((m, w), F32))
    else:
        for w in (C_KV_DIM, C_KV_DIM, IDX_DIM):
            out_specs.append(pl.BlockSpec((1, w, tm), lambda b, c: (b, 0, c)))
            out_shape.append(jax.ShapeDtypeStruct((nb, w, seq), F32))
    return pl.pallas_call(
        functools.partial(_attn_proj_kernel, transposed=not sample), grid=grid,
        in_specs=[pl.BlockSpec((tm, d), row)] + const_specs + [pl.BlockSpec((tm, C_DIM), tab)] * 3,
        out_specs=out_specs, out_shape=out_shape,
        compiler_params=_cparams(2), name="attn_proj")(x, *consts, *tabs)


def _score_keys(score):
    bits = pltpu.bitcast(score, I32)
    return jnp.where(bits < 0, bits ^ 0x7FFFFFFF, bits)


def _lane_total(acc):
    return jnp.broadcast_to(jnp.sum(acc, axis=-1, keepdims=True), acc.shape)


def _tree(parts, op):
    while len(parts) > 1:
        nxt = [op(parts[i], parts[i + 1]) for i in range(0, len(parts) - 1, 2)]
        parts = nxt + (parts[-1:] if len(parts) % 2 else [])
    return parts[0]


def _fold8(x, op):
    return _tree([x[i * 8:(i + 1) * 8] for i in range(x.shape[0] // 8)], op)


def _kth_key(count_fn, topk, shape):
    def vbit(it, acc):
        cand = acc | jnp.left_shift(jnp.int32(1), 31 - it)
        cmp = cand ^ INT_MIN
        cnt = count_fn(lambda kk, pos: jnp.where(kk >= cmp, 1, 0))
        return jnp.where(cnt >= topk, cand, acc)

    return lax.fori_loop(0, 32, vbit, jnp.zeros(shape, I32)) ^ INT_MIN


def _select_topk(count_fn, topk, pos_bits, shape, lp_ref):
    zeros = jnp.zeros(shape, I32)
    thr = _kth_key(count_fn, topk, shape)
    n_ge = count_fn(lambda kk, pos: jnp.where(kk >= thr, 1, 0))
    excess = jnp.where(thr == INT_MIN, 0, n_ge - topk)
    lp_ref[...] = jnp.full(shape, 2 ** 31 - 1, I32)

    @pl.when(jnp.max(excess) > 0)
    def _():
        need = topk - count_fn(lambda kk, pos: jnp.where(kk > thr, 1, 0))

        def pbit(it, acc):
            cand = acc | jnp.left_shift(jnp.int32(1), pos_bits - 1 - it)
            cnt = count_fn(lambda kk, pos: jnp.where(kk == thr, jnp.where(pos < cand, 1, 0), 0))
            return jnp.where(cnt < need, cand, acc)

        lp_ref[...] = lax.fori_loop(0, pos_bits, pbit, zeros)

    return thr, lp_ref[...]


def _select_bias(kk, pos, thr, last_pos):
    tie = jnp.where(pos <= last_pos, 0.0, NEG_BIG)
    bias = jnp.where(kk == thr, tie, jnp.where(kk > thr, 0.0, NEG_BIG))
    return jnp.where(kk == INT_MIN, NEG_BIG, bias)


def _pattn_kernel(qi_ref, kwq_ref, q_ref, kwk_ref, k_ref, vt_ref, o_ref,
                  keys_ref, s_ref, qip_ref, qg_ref, acc_ref, m_ref, l_ref, ties_ref, *, tq, topk):
    qb = pl.program_id(1)
    nch = qb + 1
    nslab = tq // 8
    shape8 = (8, tq)
    sub8 = lax.broadcasted_iota(I32, shape8, 0)
    lane_q = lax.broadcasted_iota(I32, (tq, LANES), 1)
    key_in = lax.broadcasted_iota(I32, (tq, tq), 0)
    qry_in = lax.broadcasted_iota(I32, (tq, tq), 1)

    wt = jnp.transpose(kwq_ref[...]) * ((IDX_DIM ** -0.5) * (IDX_HEADS ** -0.5))
    for hh in range(IDX_HEADS):
        slab = qi_ref[:, (hh // 2) * LANES:(hh // 2 + 1) * LANES]
        if hh % 2:
            slab = pltpu.roll(slab, IDX_DIM, 1)
        qip_ref[hh] = jnp.where(lane_q < IDX_DIM, slab, 0.0).astype(BF16)

    def for_chunks(body):
        odd = nch % 2

        @pl.when(odd == 1)
        def _():
            body(0)

        def pair(p, carry):
            kc = odd + 2 * p
            body(kc)
            body(kc + 1)
            return carry

        lax.fori_loop(0, nch // 2, pair, 0)

    def idx_body(kc):
        start = pl.multiple_of(kc * tq, tq)
        ks = kwk_ref[pl.ds(start, tq), :].astype(BF16)
        acc = None
        for hh in range(IDX_HEADS):
            s = _dot_nt(ks, qip_ref[hh])
            term = jnp.maximum(s, 0.0) * wt[IDX_DIM + hh:IDX_DIM + hh + 1, :]
            acc = term if acc is None else acc + term
        kk = jnp.where(acc == 0.0, 0, _score_keys(acc))
        keys_ref[kc] = jnp.where(kc * tq + key_in <= qb * tq + qry_in, kk, INT_MIN)

    for_chunks(idx_body)

    def count_fn(ind):
        def body(kc, acc):
            base = kc * tq
            parts = [ind(keys_ref[kc, i * 8:(i + 1) * 8, :], base + i * 8 + sub8) for i in range(nslab)]
            return acc + _tree(parts, jnp.add)
        acc = lax.fori_loop(0, nch, body, jnp.zeros(shape8, I32))
        return jnp.broadcast_to(jnp.sum(acc, axis=0, keepdims=True), shape8)

    thr = _kth_key(count_fn, topk, shape8)
    need = topk - count_fn(lambda kk, pos: jnp.where(kk > thr, 1, 0))

    def chunk_bias(kc, ties_before):
        slabs = []
        for i in range(nslab):
            kk = keys_ref[kc, i * 8:(i + 1) * 8, :]
            tie = jnp.where(kk == thr, 1, 0)
            incl = tie
            for sh in (1, 2, 4):
                incl = incl + jnp.where(sub8 >= sh, pltpu.roll(incl, sh, 0), 0)
            slabs.append((kk, incl - tie, jnp.broadcast_to(incl[7:8, :], shape8)))
        out = []
        for kk, excl, total in slabs:
            rank = ties_before + excl
            bias = jnp.where(kk > thr, 0.0, NEG_BIG)
            bias = jnp.where(kk == thr, jnp.where(rank < need, 0.0, NEG_BIG), bias)
            out.append(jnp.where(kk == INT_MIN, NEG_BIG, bias))
            ties_before = ties_before + total
        return jnp.concatenate(out, axis=0), ties_before

    scale = (HEAD_DIM ** -0.5) * 1.4426950408889634
    for g in range(C_KV_HEADS):
        qslab = q_ref[:, g * LANES:(g + 1) * LANES] * scale
        rolled = pltpu.roll(qslab, HEAD_DIM, 1)
        koff = g % 2
        in_half = (lane_q >= HEAD_DIM) if koff else (lane_q < HEAD_DIM)
        qg_ref[g] = jnp.concatenate(
            [jnp.where(in_half, qslab if r == koff else rolled, 0.0) for r in range(2)],
            axis=0).astype(BF16)

    m_ref[...] = jnp.full(m_ref.shape, NEG_BIG, F32)
    ties_ref[...] = jnp.zeros(shape8, I32)

    def max_body(kc):
        start = pl.multiple_of(kc * tq, tq)
        bias, ties = chunk_bias(kc, ties_ref[...])
        ties_ref[...] = ties
        bias2 = jnp.concatenate([bias, bias], axis=1)
        for g in range(C_KV_HEADS):
            kcol = (g // 2) * LANES
            kch = k_ref[pl.ds(start, tq), kcol:kcol + LANES].astype(BF16)
            s = _dot_nt(kch, qg_ref[g]) + bias2
            s_ref[kc, :, g * 2 * tq:(g + 1) * 2 * tq] = s
            m_ref[g] = jnp.maximum(m_ref[g], _fold8(s, jnp.maximum))

    for_chunks(max_body)
    m_row = [jnp.max(m_ref[g], axis=0, keepdims=True) for g in range(C_KV_HEADS)]

    acc_ref[...] = jnp.zeros_like(acc_ref)
    l_ref[...] = jnp.zeros_like(l_ref)

    def sum_body(kc):
        start = pl.multiple_of(kc * tq, tq)
        for g in range(C_KV_HEADS):
            p = jnp.exp2(s_ref[kc, :, g * 2 * tq:(g + 1) * 2 * tq] - m_row[g])
            vt = vt_ref[0, g * HEAD_DIM:(g + 1) * HEAD_DIM, pl.ds(start, tq)].astype(BF16)
            acc_ref[g] += _dot(vt, p.astype(BF16))
            l_ref[g] += _fold8(p, jnp.add)

    for_chunks(sum_body)
    for g in range(C_KV_HEADS):
        out_t = acc_ref[g] / jnp.sum(l_ref[g], axis=0, keepdims=True)
        o_ref[:, g * LANES:(g + 1) * LANES] = jnp.transpose(
            jnp.concatenate([out_t[:, :tq], out_t[:, tq:]], axis=0))


def _prompt_attention(q, k, vt, qi, kw, nb, seq):
    assert C_HEADS == 2 * C_KV_HEADS
    topk = min(TOPK_MAX, seq // 4)
    tq = _tile(seq, 256)
    assert tq % LANES == 0
    nq = seq // tq
    qrow = lambda b, c: (b * nq + c, 0)
    krow = lambda b, c: (b, 0)
    return pl.pallas_call(
        functools.partial(_pattn_kernel, tq=tq, topk=topk),
        grid=(nb, nq),
        in_specs=[pl.BlockSpec((tq, IDX_HEADS * IDX_DIM), qrow), pl.BlockSpec((tq, LANES), qrow),
                  pl.BlockSpec((tq, C_DIM), qrow), pl.BlockSpec((seq, LANES), krow),
                  pl.BlockSpec((seq, C_KV_DIM), krow),
                  pl.BlockSpec((1, C_KV_DIM, seq), lambda b, c: (b, 0, 0))],
        out_specs=pl.BlockSpec((tq, C_DIM), qrow),
        out_shape=jax.ShapeDtypeStruct((nb * seq, C_DIM), F32),
        scratch_shapes=[pltpu.VMEM((nq, tq, tq), I32),
                        pltpu.VMEM((nq, tq, C_KV_HEADS * 2 * tq), F32),
                        pltpu.VMEM((IDX_HEADS, tq, LANES), BF16),
                        pltpu.VMEM((C_KV_HEADS, 2 * tq, LANES), BF16),
                        pltpu.VMEM((C_KV_HEADS, HEAD_DIM, 2 * tq), F32),
                        pltpu.VMEM((C_KV_HEADS, 8, 2 * tq), F32),
                        pltpu.VMEM((C_KV_HEADS, 8, 2 * tq), F32), pltpu.VMEM((8, tq), I32)],
        compiler_params=_cparams(2), name="prompt_attn")(qi, kw, q, kw, k, vt)


SROWS = 8


def _sidx_kernel(pt_ref, qh_ref, wh_ref, kn_ref, *rest, npg, page):
    pages = rest[:npg]
    past_o, new_o = rest[npg:]
    qh = qh_ref[0].astype(BF16)
    wh = wh_ref[0] * ((IDX_DIM ** -0.5) * (IDX_HEADS ** -0.5))

    n_out = past_o.shape[1]

    def head_sum(s):
        s = jnp.maximum(s, 0.0) * wh
        return _tree([s[hh * SROWS:(hh + 1) * SROWS] for hh in range(IDX_HEADS)], jnp.add)[:n_out]

    keys_t = jnp.concatenate([pg[0] for pg in pages], axis=1).astype(BF16)
    past_o[0] = head_sum(_dot(qh, keys_t))

    @pl.when(pl.program_id(1) == 0)
    def _():
        new_o[0] = head_sum(_dot_nt(qh, kn_ref[0].astype(BF16)))


def _sample_index_scores(cache_kidx, page_table, layer, qh, wh, ki_new, seq):
    depth, n_phys, page, _ = cache_kidx.shape
    db, n_pages = page_table.shape
    npg = _pages_per_step(n_pages, 2 * PAGES_PER_STEP)
    cache = cache_kidx.transpose(0, 1, 3, 2).reshape(depth * n_phys, IDX_DIM, page)
    base = layer * n_phys

    def page_spec(i):
        return pl.BlockSpec((1, IDX_DIM, page), lambda b, s, pt: (base + pt[b, s * npg + i], 0, 0))

    per_b = lambda b, s, pt: (b, 0, 0)
    grid_spec = pltpu.PrefetchScalarGridSpec(
        num_scalar_prefetch=1, grid=(db, n_pages // npg),
        in_specs=[pl.BlockSpec((1, IDX_HEADS * SROWS, IDX_DIM), per_b),
                  pl.BlockSpec((1, IDX_HEADS * SROWS, 1), per_b),
                  pl.BlockSpec((1, LANES, IDX_DIM), per_b)] + [page_spec(i) for i in range(npg)],
        out_specs=[pl.BlockSpec((1, seq, npg * page), lambda b, s, pt: (b, 0, s)),
                   pl.BlockSpec((1, seq, LANES), per_b)])
    return pl.pallas_call(
        functools.partial(_sidx_kernel, npg=npg, page=page), grid_spec=grid_spec,
        out_shape=[jax.ShapeDtypeStruct((db, seq, n_pages * page), F32),
                   jax.ShapeDtypeStruct((db, seq, LANES), F32)],
        compiler_params=_cparams(2), name="sample_index")(page_table, qh, wh, ki_new, *([cache] * npg))


def _ssel_kernel(past_ref, new_ref, bpast_o, bnew_o, keys_ref, lp_ref, *, topk, pos_bits, n_new):
    rows, past = past_ref.shape
    nch = past // LANES
    lane = lax.broadcasted_iota(I32, (rows, LANES), 1)
    t_row = lax.broadcasted_iota(I32, (rows, LANES), 0) % n_new

    def fill(c, carry):
        start = pl.multiple_of(c * LANES, LANES)
        keys_ref[c] = _score_keys(past_ref[:, pl.ds(start, LANES)])
        return carry

    lax.fori_loop(0, nch, fill, 0)
    new_ok = lane <= jnp.minimum(t_row, n_new - 1)
    keys_ref[nch] = jnp.where(new_ok, _score_keys(new_ref[...]), INT_MIN)

    grp = 8
    assert nch % grp == 0

    def count_fn(pred):
        def body(q, acc):
            c0 = q * grp
            return acc + _tree([pred(keys_ref[c0 + i], (c0 + i) * LANES + lane) for i in range(grp)], jnp.add)
        acc = lax.fori_loop(0, nch // grp, body, jnp.zeros((rows, LANES), I32))
        return _lane_total(acc + pred(keys_ref[nch], nch * LANES + lane))

    thr, last_pos = _select_topk(count_fn, topk, pos_bits, (rows, LANES), lp_ref)

    def emit(c, carry):
        start = pl.multiple_of(c * LANES, LANES)
        bpast_o[:, pl.ds(start, LANES)] = _select_bias(keys_ref[c], c * LANES + lane, thr, last_pos)
        return carry

    lax.fori_loop(0, nch, emit, 0)
    bnew_o[...] = _select_bias(keys_ref[nch], nch * LANES + lane, thr, last_pos)


def _sample_select(sc_past, sc_new, n_new):
    rows, past = sc_past.shape
    topk = min(TOPK_MAX, (past + n_new) // 4)
    tr = _tile(rows, 64)
    assert tr % n_new == 0
    pos_bits = int(past + LANES - 1).bit_length()
    row = lambda i: (i, 0)
    return pl.pallas_call(
        functools.partial(_ssel_kernel, topk=topk, pos_bits=pos_bits, n_new=n_new),
        grid=(rows // tr,),
        in_specs=[pl.BlockSpec((tr, past), row), pl.BlockSpec((tr, LANES), row)],
        out_specs=[pl.BlockSpec((tr, past), row), pl.BlockSpec((tr, LANES), row)],
        out_shape=[jax.ShapeDtypeStruct((rows, past), F32), jax.ShapeDtypeStruct((rows, LANES), F32)],
        scratch_shapes=[pltpu.VMEM((past // LANES + 1, tr, LANES), I32), pltpu.VMEM((tr, LANES), I32)],
        compiler_params=_cparams(1), name="sample_select")(sc_past, sc_new)


def _satt_kernel(pt_ref, qbd_ref, bpast_ref, bnew_ref, kn_ref, vn_ref, *rest, npg, page):
    kpages = rest[:npg]
    vpages = rest[npg:2 * npg]
    o_ref, m_ref, l_ref, acc_ref = rest[2 * npg:]
    s_id = pl.program_id(1)
    nrow = qbd_ref.shape[1]
    reps = nrow // SROWS

    @pl.when(s_id == 0)
    def _():
        m_ref[...] = jnp.full_like(m_ref, NEG_BIG)
        l_ref[...] = jnp.zeros_like(l_ref)
        acc_ref[...] = jnp.zeros_like(acc_ref)

    qbd = qbd_ref[0].astype(BF16)

    def update(s, bias, pv):
        s = s + jnp.concatenate([bias] * reps, axis=0)
        m_old = m_ref[...]
        m_new = jnp.maximum(m_old, jnp.max(s, axis=-1, keepdims=True))
        p = jnp.exp(s - m_new)
        alpha = jnp.exp(m_old - m_new)
        l_ref[...] = alpha * l_ref[...] + jnp.sum(p, axis=-1, keepdims=True)
        acc_ref[...] = alpha * acc_ref[...] + pv(p.astype(BF16))
        m_ref[...] = m_new

    keys_t = jnp.concatenate([kp[0] for kp in kpages], axis=1).astype(BF16)
    vals_t = jnp.concatenate([vp[0] for vp in vpages], axis=1).astype(BF16)
    update(_dot(qbd, keys_t), bpast_ref[0], lambda p: _dot_nt(p, vals_t))

    @pl.when(s_id == pl.num_programs(1) - 1)
    def _():
        vn = vn_ref[0].astype(BF16)
        update(_dot_nt(qbd, kn_ref[0].astype(BF16)), bnew_ref[0], lambda p: _dot(p, vn))
        o_ref[0] = acc_ref[...] / l_ref[...]


def _sample_attend(cache_k, cache_v, page_table, layer, qbd, bias_past, bias_new, k_new, v_new):
    depth, n_phys, page = cache_k.shape[:3]
    db, n_pages = page_table.shape
    npg = _pages_per_step(n_pages)
    ck = cache_k.transpose(0, 1, 3, 4, 2).reshape(depth * n_phys, C_KV_DIM, page)
    cv = cache_v.transpose(0, 1, 3, 4, 2).reshape(depth * n_phys, C_KV_DIM, page)
    base = layer * n_phys
    nrow = qbd.shape[1]

    def page_spec(i):
        return pl.BlockSpec((1, C_KV_DIM, page), lambda b, s, pt: (base + pt[b, s * npg + i], 0, 0))

    per_b = lambda b, s, pt: (b, 0, 0)
    grid_spec = pltpu.PrefetchScalarGridSpec(
        num_scalar_prefetch=1, grid=(db, n_pages // npg),
        in_specs=[pl.BlockSpec((1, nrow, C_KV_DIM), per_b),
                  pl.BlockSpec((1, SROWS, npg * page), lambda b, s, pt: (b, 0, s)),
                  pl.BlockSpec((1, SROWS, LANES), per_b),
                  pl.BlockSpec((1, LANES, C_KV_DIM), per_b),
                  pl.BlockSpec((1, LANES, C_KV_DIM), per_b)]
                 + [page_spec(i) for i in range(npg)] * 2,
        out_specs=pl.BlockSpec((1, nrow, C_KV_DIM), per_b),
        scratch_shapes=[pltpu.VMEM((nrow, 1), F32), pltpu.VMEM((nrow, 1), F32),
                        pltpu.VMEM((nrow, C_KV_DIM), F32)])
    return pl.pallas_call(
        functools.partial(_satt_kernel, npg=npg, page=page), grid_spec=grid_spec,
        out_shape=jax.ShapeDtypeStruct((db, nrow, C_KV_DIM), F32),
        compiler_params=_cparams(2), name="sample_attn")(
            page_table, qbd, bias_past, bias_new, k_new, v_new, *([ck] * npg), *([cv] * npg))


def _merge_kernel(x_ref, g_ref, wg_ref, os_ref, bon_ref, gg_ref, lw_ref, lb_ref, bd_ref,
                  ob_ref, oc_ref, wpa_ref, wpb_ref, wpc_ref, wo_ref, o_ref):
    x = x_ref[...]
    d = x.shape[1]
    h = _rms(x, g_ref[...]).astype(BF16)
    gates = jax.nn.sigmoid(_dot(h, wg_ref[...]))
    bd = bd_ref[...]
    o = os_ref[...]
    mean = _segsum(o, bd, split=True) * (1.0 / HEAD_DIM)
    cen = o - mean
    var = _segsum(cen * cen, bd) * (1.0 / HEAD_DIM)
    on = cen * lax.rsqrt(var + GN_EPS) * lw_ref[...] + lb_ref[...]
    oa = ((on + bon_ref[...]) * gg_ref[...]).astype(BF16)
    merged = (gates[:, :d] * _dot(oa, wpa_ref[...])
              + gates[:, d:2 * d] * _dot(ob_ref[...].astype(BF16), wpb_ref[...])
              + gates[:, 2 * d:] * _dot(oc_ref[...].astype(BF16), wpc_ref[...]))
    o_ref[...] = x + _dot(merged.astype(BF16), wo_ref[...])


def _merge(x, o_scan, bonus, g, o_b, o_c, lw, nb, seq):
    m, d = x.shape
    tm = _tile(seq, 512)
    nc = seq // tm
    row = lambda b, c: (b * nc + c, 0)
    tok = lambda w: pl.BlockSpec((tm, w), row)
    consts_a = [lw["mix_norm"], lw["w_g"]]
    consts_b = [lw["rwkv_lnx_w"], lw["rwkv_lnx_b"], lw["bd"]]
    consts_c = [lw["w_pa"], lw["w_pb"], lw["w_pc"], lw["w_out"]]
    in_specs = ([tok(d)] + [_const_spec(c.shape) for c in consts_a]
                + [pl.BlockSpec((tm, A_DIM), lambda b, c: (c, b))] + [tok(A_DIM)] * 2
                + [_const_spec(c.shape) for c in consts_b] + [tok(B_DIM), tok(C_DIM)]
                + [_const_spec(c.shape) for c in consts_c])
    return pl.pallas_call(
        _merge_kernel, grid=(nb, nc), in_specs=in_specs, out_specs=tok(d),
        out_shape=jax.ShapeDtypeStruct((m, d), F32), compiler_params=_cparams(2), name="merge")(
            x, *consts_a, o_scan, bonus, g, *consts_b, o_b, o_c, *consts_c)


def _rope_tables(pos):
    inv = ROPE_THETA ** (-jnp.arange(ROT_HALF, dtype=F32) / ROT_HALF)
    ang = pos.astype(F32)[:, None] * inv[None, :]
    c, s = jnp.cos(ang), jnp.sin(ang)
    t = pos.shape[0]
    pad = jnp.zeros((t, HEAD_DIM - ROT_DIM), F32)
    zer = jnp.zeros((t, ROT_HALF), F32)
    cos = jnp.concatenate([c, c, pad + 1.0], axis=1)
    sa = jnp.concatenate([-s, zer, pad], axis=1)
    sb = jnp.concatenate([zer, s, pad], axis=1)
    return tuple(jnp.tile(z, (1, C_HEADS)) for z in (cos, sa, sb))


def _layer_weights(i, p):
    d = p["w_in"].shape[1]
    w_in = p["w_in"][i]
    o1 = RWKV_COLS
    o2 = o1 + CONV_COLS
    o3 = o2 + ATTN_COLS
    row = lambda v: v.reshape(1, -1)
    tile_h = lambda v, n: jnp.tile(v, n).reshape(1, -1)
    z = jnp.zeros((LORA_W, A_DIM), F32)
    w_wa = jnp.concatenate([jnp.concatenate([p["rwkv_w2"][i], z], axis=1),
                            jnp.concatenate([z, p["rwkv_a2"][i]], axis=1)], axis=0)
    head = np.arange(A_DIM) // HEAD_DIM
    bd = jnp.asarray(head[:, None] == head[None, :], BF16)
    idx_norm = jnp.concatenate([p["idx_k_norm"][i], jnp.zeros((LANES - IDX_DIM,), F32)])
    bf = lambda w: w.astype(BF16)
    return dict(
        ffn1_norm=p["ffn1_norm"][i], ffn1_wg=bf(p["ffn1_wg"][i]), ffn1_wu=bf(p["ffn1_wu"][i]),
        ffn1_wd=bf(p["ffn1_wd"][i]),
        ffn2_norm=p["ffn2_norm"][i], ffn2_wg=bf(p["ffn2_wg"][i]), ffn2_wu=bf(p["ffn2_wu"][i]),
        ffn2_wd=bf(p["ffn2_wd"][i]),
        mix_norm=row(p["mix_norm"][i]),
        w_a=bf(w_in[:, :o1]), w_b=bf(w_in[:, o1:o2]),
        w_c=bf(jnp.pad(w_in[:, o2:o3], ((0, 0), (0, ATTN_COLS_PAD - ATTN_COLS)))),
        w_g=bf(w_in[:, o3:]),
        rwkv_mu=row(p["rwkv_mu"][i]), rwkv_w0=row(p["rwkv_w0"][i]), rwkv_a0=row(p["rwkv_a0"][i]),
        w_wa=bf(w_wa), rwkv_g2=bf(p["rwkv_g2"][i]), rwkv_kk=row(p["rwkv_kk"][i]),
        rwkv_ka=row(p["rwkv_ka"][i]), rwkv_rk=row(p["rwkv_rk"][i]),
        rwkv_lnx_w=row(p["rwkv_lnx_w"][i]), rwkv_lnx_b=row(p["rwkv_lnx_b"][i]),
        conv_w=p["conv_w"][i], bd=bd,
        q_norm=tile_h(p["q_norm"][i], C_HEADS), k_norm=tile_h(p["k_norm"][i], C_KV_HEADS),
        idx_k_norm=row(idx_norm),
        w_pa=bf(p["w_pa"][i]), w_pb=bf(p["w_pb"][i]), w_pc=bf(p["w_pc"][i]), w_out=bf(p["w_out"][i]),
        ple_norm=p["ple_norm"][i], ple_gate=bf(p["ple_gate"][i]), ple_proj=bf(p["ple_proj"][i]),
    )


def _rwkv_branch(x, nb, seq, lw, wkv0, shift_rows, between=None):
    pre = _rwkv_pre(x, nb, seq, lw, shift_rows)
    s0 = wkv0.transpose(3, 2, 0, 1).reshape(HEAD_DIM, HEAD_DIM, nb * A_HEADS)
    other = None
    if between is not None:
        x, pre = lax.optimization_barrier((x, pre))
        other = between(x)
    r, w, k, v, na, nb_, g, bonus, shift_o = pre
    nbh = nb * A_HEADS
    if shift_rows is None:
        ts = lambda z: z.reshape(nbh, HEAD_DIM, seq).transpose(2, 1, 0)
    else:
        ts = lambda z: z.reshape(nb, seq, A_HEADS, HEAD_DIM).transpose(1, 3, 0, 2).reshape(seq, HEAD_DIM, nbh)
    seqs = [ts(z) for z in (r, w, k, v, na, nb_)]
    if between is not None:
        s0, other = lax.optimization_barrier((s0, other))
    o, s_fin = _scan(*seqs, s0)
    wkv1 = s_fin.reshape(HEAD_DIM, HEAD_DIM, nb, A_HEADS).transpose(2, 3, 1, 0)
    if shift_rows is None:
        shift1 = shift_o.reshape(nb, RWKV_COLS)
        o_tok = o.transpose(0, 2, 1).reshape(seq, nb * A_DIM)
    else:
        shift1 = shift_o.reshape(nb, seq, RWKV_COLS)[:, -1]
        o_tok = o.reshape(seq, HEAD_DIM, nb, A_HEADS).transpose(2, 0, 3, 1).reshape(nb * seq, A_DIM)
    return o_tok, bonus, g, shift1, wkv1, other


def _layer_tail(x1, o_scan, bonus, g, o_b, o_c, p_emb, lw, nb, seq):
    x2 = _merge(x1, o_scan, bonus, g, o_b, o_c, lw, nb, seq)
    return _ffn(x2, lw["ffn2_norm"], lw["ffn2_wg"], lw["ffn2_wu"], lw["ffn2_wd"],
                ple=(p_emb, lw["ple_norm"], lw["ple_gate"], lw["ple_proj"]))


def _prompt_layer(x, p_emb, nb, seq, lw, tabs):
    x1 = _ffn(x, lw["ffn1_norm"], lw["ffn1_wg"], lw["ffn1_wu"], lw["ffn1_wd"])
    wkv0 = jnp.zeros((nb, A_HEADS, HEAD_DIM, HEAD_DIM), F32)

    def other_mixers(xin):
        o_b, conv1 = _conv(xin, nb, seq, lw)
        q, k, qi, kw, kt, vt, kit = _attn_proj(xin, nb, seq, lw, tabs, False)
        return o_b, conv1, kt, vt, kit, _prompt_attention(q, k, vt, qi, kw, nb, seq)

    o_scan, bonus, g, shift1, wkv1, other = _rwkv_branch(x1, nb, seq, lw, wkv0, None, other_mixers)
    o_b, conv1, kt, vt, kit, o_c = other
    x4 = _layer_tail(x1, o_scan, bonus, g, o_b, o_c, p_emb, lw, nb, seq)
    heads = lambda z: z.reshape(nb, C_KV_HEADS, HEAD_DIM, seq).transpose(0, 3, 1, 2)
    st = (heads(kt), heads(vt), kit.transpose(0, 2, 1), wkv1, shift1, conv1)
    return x4, st


def _sample_layer(x, p_emb, nb, seq, lw, tabs, layer, cache_k, cache_v, cache_kidx, page_table,
                  wkv0, shift0, conv0):
    assert seq <= SROWS
    x1 = _ffn(x, lw["ffn1_norm"], lw["ffn1_wg"], lw["ffn1_wu"], lw["ffn1_wd"])
    rep_rows = lambda z: jnp.repeat(z, seq, axis=0)
    o_scan, bonus, g, shift1, wkv1, _ = _rwkv_branch(x1, nb, seq, lw, wkv0, rep_rows(shift0))
    o_b, z_all = _conv(x1, nb, seq, lw, init=(rep_rows(conv0[:, 0]), rep_rows(conv0[:, 1])))
    conv1 = z_all.reshape(nb, seq, B_DIM)[:, seq - (CONV_W - 1):]
    q, k, qi, kw, v, ki = _attn_proj(x1, nb, seq, lw, tabs, True)

    pad_t = lambda z: jnp.pad(z, ((0, 0), (0, SROWS - seq)) + ((0, 0),) * (z.ndim - 2))
    qh = pad_t(qi.reshape(nb, seq, IDX_HEADS, IDX_DIM)).transpose(0, 2, 1, 3)
    qh = qh.reshape(nb, IDX_HEADS * SROWS, IDX_DIM)
    wi = kw[:, IDX_DIM:IDX_DIM + IDX_HEADS].reshape(nb, seq, IDX_HEADS)
    wh = pad_t(wi).transpose(0, 2, 1).reshape(nb, IDX_HEADS * SROWS, 1)
    pad_keys = lambda z: jnp.pad(z.reshape(nb, seq, -1), ((0, 0), (0, LANES - seq), (0, 0)))
    sc_past, sc_new = _sample_index_scores(cache_kidx, page_table, layer, qh, wh, pad_keys(ki), seq)
    past = sc_past.shape[-1]
    b_past, b_new = _sample_select(sc_past.reshape(nb * seq, past), sc_new.reshape(nb * seq, LANES), seq)
    b_past = pad_t(b_past.reshape(nb, seq, past))
    b_new = pad_t(b_new.reshape(nb, seq, LANES))

    rep = C_HEADS // C_KV_HEADS
    qg = pad_t(q.reshape(nb, seq, C_KV_HEADS, rep, HEAD_DIM)).transpose(0, 2, 3, 1, 4)
    qg = qg * (HEAD_DIM ** -0.5)
    eye = jnp.eye(C_KV_HEADS, dtype=F32)
    qbd = jnp.einsum("bgrtd,gh->bgrthd", qg, eye).reshape(nb, C_HEADS * SROWS, C_KV_DIM)
    o = _sample_attend(cache_k, cache_v, page_table, layer, qbd,
                       b_past, b_new, pad_keys(k), pad_keys(v))
    o = o.reshape(nb, C_KV_HEADS, rep, SROWS, C_KV_HEADS, HEAD_DIM)
    o = jnp.einsum("bgrthd,gh->bgrtd", o, eye)[:, :, :, :seq]
    o_c = o.transpose(0, 3, 1, 2, 4).reshape(nb * seq, C_DIM)

    x4 = _layer_tail(x1, o_scan, bonus, g, o_b, o_c, p_emb, lw, 1, nb * seq)
    st = (k.reshape(nb, seq, C_KV_HEADS, HEAD_DIM), v.reshape(nb, seq, C_KV_HEADS, HEAD_DIM),
          ki.reshape(nb, seq, IDX_DIM), wkv1, shift1, conv1)
    return x4, st


def kernel(x_prompt, x_sample, cache_k, cache_v, cache_kidx, state_wkv, state_shift, state_conv, page_table, p_prompt, p_sample, ffn1_norm, ffn1_wg, ffn1_wu, ffn1_wd, mix_norm, w_in, rwkv_mu, rwkv_w0, rwkv_w2, rwkv_a0, rwkv_a2, rwkv_g2, rwkv_kk, rwkv_ka, rwkv_rk, rwkv_lnx_w, rwkv_lnx_b, conv_w, q_norm, k_norm, idx_k_norm, w_pa, w_pb, w_pc, w_out, ffn2_norm, ffn2_wg, ffn2_wu, ffn2_wd, ple_norm, ple_gate, ple_proj):
    params = dict(ffn1_norm=ffn1_norm, ffn1_wg=ffn1_wg, ffn1_wu=ffn1_wu, ffn1_wd=ffn1_wd,
                  mix_norm=mix_norm, w_in=w_in, rwkv_mu=rwkv_mu, rwkv_w0=rwkv_w0, rwkv_w2=rwkv_w2,
                  rwkv_a0=rwkv_a0, rwkv_a2=rwkv_a2, rwkv_g2=rwkv_g2, rwkv_kk=rwkv_kk, rwkv_ka=rwkv_ka,
                  rwkv_rk=rwkv_rk.reshape(rwkv_rk.shape[0], -1), rwkv_lnx_w=rwkv_lnx_w,
                  rwkv_lnx_b=rwkv_lnx_b, conv_w=conv_w, q_norm=q_norm, k_norm=k_norm,
                  idx_k_norm=idx_k_norm, w_pa=w_pa, w_pb=w_pb, w_pc=w_pc, w_out=w_out,
                  ffn2_norm=ffn2_norm, ffn2_wg=ffn2_wg, ffn2_wu=ffn2_wu, ffn2_wd=ffn2_wd,
                  ple_norm=ple_norm, ple_gate=ple_gate, ple_proj=ple_proj)
    nb, seq, d = x_prompt.shape
    db, dseq, _ = x_sample.shape
    depth = w_in.shape[0]
    past = page_table.shape[1] * cache_k.shape[2]
    tabs_p = _rope_tables(jnp.arange(seq, dtype=jnp.int32))
    tabs_s = tuple(jnp.tile(z, (db, 1)) for z in _rope_tables(past + jnp.arange(dseq, dtype=jnp.int32)))
    xp = x_prompt.reshape(nb * seq, d)
    xs = x_sample.reshape(db * dseq, d)
    outs_p, outs_s = [], []
    for i in range(depth):
        lw = _layer_weights(i, params)
        xp, st_p = _prompt_layer(xp, (p_prompt.reshape(depth * nb * seq, -1), i), nb, seq, lw, tabs_p)
        xs, st_s = _sample_layer(xs, (p_sample.reshape(depth * db * dseq, -1), i), db, dseq, lw, tabs_s, i,
                                 cache_k, cache_v, cache_kidx, page_table,
                                 state_wkv[i], state_shift[i], state_conv[i])
        outs_p.append(st_p)
        outs_s.append(st_s)
    k_p, v_p, kidx_p, wkv_p, shift_p, conv_p = [jnp.stack(z) for z in zip(*outs_p)]
    k_s, v_s, kidx_s, wkv_s, shift_s, conv_s = [jnp.stack(z) for z in zip(*outs_s)]
    return (xp.reshape(nb, seq, d), xs.reshape(db, dseq, d), k_p, v_p, kidx_p, wkv_p, shift_p, conv_p,
            k_s, v_s, kidx_s, wkv_s, shift_s, conv_s)
```

```python
import functools

import jax
import jax.numpy as jnp
import numpy as np
from jax import lax
from jax.experimental import pallas as pl
from jax.experimental.pallas import tpu as pltpu

F32 = jnp.float32
BF16 = jnp.bfloat16
I32 = jnp.int32

HEAD_DIM = 64
A_HEADS = 8
A_DIM = A_HEADS * HEAD_DIM
LORA_W = 64
LORA_A = 64
LORA_G = 128
B_DIM = 512
CONV_W = 3
C_HEADS = 8
C_KV_HEADS = 4
C_DIM = C_HEADS * HEAD_DIM
C_KV_DIM = C_KV_HEADS * HEAD_DIM
IDX_HEADS = 8
IDX_DIM = 64
TOPK_MAX = 256
ROT_DIM = HEAD_DIM // 4
ROT_HALF = ROT_DIM // 2
ROPE_THETA = 500000.0
N_BRANCH = 3
RMS_EPS = 1e-6
GN_EPS = 64e-5
RWKV_COLS = 3 * A_DIM + LORA_W + LORA_A + LORA_G
CONV_COLS = 3 * B_DIM
ATTN_COLS = C_DIM + 2 * C_KV_DIM + IDX_HEADS * IDX_DIM + IDX_DIM + IDX_HEADS
ATTN_COLS_PAD = 1664

LANES = 128
INT_MIN = -2 ** 31
NEG_BIG = -1e30
VMEM_LIMIT = 56 * 1024 * 1024
PAGES_PER_STEP = 32


def _pages_per_step(n_pages, cap=PAGES_PER_STEP):
    npg = min(cap, n_pages)
    while n_pages % npg:
        npg -= 1
    return npg


def _cparams(n_axes):
    return pltpu.CompilerParams(dimension_semantics=("arbitrary",) * n_axes,
                                vmem_limit_bytes=VMEM_LIMIT)


def _const_spec(shape):
    nd = len(shape)
    return pl.BlockSpec(shape, lambda *_: (0,) * nd, pipeline_mode=pl.Buffered(1))


def _tile(m, pref):
    t = min(m, pref)
    while m % t:
        t -= 8
    return t


def _rms(x, g):
    ms = jnp.mean(x * x, axis=-1, keepdims=True)
    return x * lax.rsqrt(ms + RMS_EPS) * g


def _dot(a, b):
    return jnp.dot(a, b, preferred_element_type=F32)


def _dot_nt(a, b):
    return lax.dot_general(a, b, (((1,), (1,)), ((), ())), preferred_element_type=F32)


def _segsum(x, bd, split=False):
    hi = x.astype(BF16)
    if not split:
        return _dot(hi, bd)
    lo = (x - hi.astype(F32)).astype(BF16)
    return _dot(hi, bd) + _dot(lo, bd)


def _rope(x, cos, sa, sb):
    n = x.shape[-1]
    return x * cos + pltpu.roll(x, n - ROT_HALF, 1) * sa + pltpu.roll(x, ROT_HALF, 1) * sb


def _rows_before(x, k, fills, *, seg_len, first_chunk_rows=None):
    rows = x.shape[0]
    y = pltpu.roll(x, k, 0)
    ridx = lax.broadcasted_iota(I32, (rows, 1), 0)
    t = ridx if seg_len is None else ridx % seg_len
    for r in range(k):
        y = jnp.where(t == r, fills[k - 1 - r], y)
    return y


ROW_BLOCK = 256


def _row_blocks(tm, seg_len=None, blk=ROW_BLOCK):
    if seg_len is not None or tm % blk:
        return [slice(0, tm)]
    return [slice(i, i + blk) for i in range(0, tm, blk)]


def _softplus(y):
    return jnp.maximum(y, 0.0) + jnp.log(1.0 + jnp.exp(-jnp.abs(y)))


def _ffn_kernel(*refs, chunks, ple):
    if ple:
        x_ref, g_ref, wg_ref, wu_ref, wd_ref, p_ref, pn_ref, pg_ref, pp_ref, o_ref = refs
    else:
        x_ref, g_ref, wg_ref, wu_ref, wd_ref, o_ref = refs
    x = x_ref[...]
    h = _rms(x, g_ref[...]).astype(BF16)
    acc = None
    for lo, hi in chunks:
        gt = _dot(h, wg_ref[:, lo:hi])
        ut = _dot(h, wu_ref[:, lo:hi])
        act = (gt * jax.nn.sigmoid(gt) * ut).astype(BF16)
        d = _dot(act, wd_ref[lo:hi, :])
        acc = d if acc is None else acc + d
    y = x + 0.5 * acc
    if ple:
        hg = _rms(y, pn_ref[...]).astype(BF16)
        gate = jax.nn.sigmoid(_dot(hg, pg_ref[...]))
        y = y + gate * _dot(p_ref[...].astype(BF16), pp_ref[...])
    o_ref[...] = y


def _ffn(x, g, wg, wu, wd, ple=None):
    m, d = x.shape
    f = wg.shape[1]
    tm = _tile(m, 512)
    step = 1024
    chunks = tuple((lo, min(lo + step, f)) for lo in range(0, f, step))
    row = lambda i: (i, 0)
    in_specs = [pl.BlockSpec((tm, d), row), _const_spec((1, d)), _const_spec((d, f)),
                _const_spec((d, f)), _const_spec((f, d))]
    args = [x, g.reshape(1, d), wg, wu, wd]
    if ple is not None:
        (p, layer), pn, pg, pp = ple
        off = layer * (m // tm)
        in_specs += [pl.BlockSpec((tm, p.shape[1]), lambda i: (i + off, 0)), _const_spec((1, d)),
                     _const_spec(pg.shape), _const_spec(pp.shape)]
        args += [p, pn.reshape(1, d), pg, pp]
    return pl.pallas_call(
        functools.partial(_ffn_kernel, chunks=chunks, ple=ple is not None),
        grid=(m // tm,), in_specs=in_specs, out_specs=pl.BlockSpec((tm, d), row),
        out_shape=jax.ShapeDtypeStruct((m, d), F32), compiler_params=_cparams(1),
        name="ffn_ple" if ple is not None else "ffn")(*args)


def _rwkv_pre_kernel(*refs, seg_len):
    (x_ref, g_ref, wa_ref, mu_ref, w0_ref, a0_ref, wwa_ref, g2_ref, kkw_ref, ka_ref,
     rk_ref, bd_ref) = refs[:12]
    rest = refs[12:]
    if seg_len is None:
        r_o, w_o, k_o, v_o, na_o, nb_o, g_o, bonus_o, shift_o, carry_ref = rest
    else:
        init_ref, r_o, w_o, k_o, v_o, na_o, nb_o, g_o, bonus_o, shift_o = rest
    tm = x_ref.shape[0]
    bd = bd_ref[...]
    if seg_len is None:
        @pl.when(pl.program_id(1) == 0)
        def _():
            carry_ref[...] = jnp.zeros_like(carry_ref)
        prev_row = carry_ref[0:1, :]
    for rows in _row_blocks(tm, seg_len):
        rb = rows.stop - rows.start
        h = _rms(x_ref[rows, :], g_ref[...]).astype(BF16)
        u = _dot(h, wa_ref[...])
        if seg_len is None:
            u_prev = _rows_before(u, 1, [prev_row], seg_len=None)
            prev_row = u[rb - 1:rb, :]
        else:
            u_prev = _rows_before(u, 1, [init_ref[rows, :]], seg_len=seg_len)
            shift_o[rows, :] = u
        us = u + (u_prev - u) * mu_ref[...]
        r = us[:, 0:A_DIM]
        k = us[:, A_DIM:2 * A_DIM]
        v = us[:, 2 * A_DIM:3 * A_DIM]
        o3 = 3 * A_DIM
        xwa = us[:, o3:o3 + LORA_W + LORA_A]
        lane = lax.broadcasted_iota(I32, xwa.shape, 1)
        xwa = jnp.where(lane < LORA_W, jnp.tanh(xwa), xwa)
        lo = _dot(xwa.astype(BF16), wwa_ref[...])
        w_log = -_softplus(-(w0_ref[...] + lo[:, :A_DIM])) - 0.5
        log_decay = -jnp.exp(w_log)
        a = jax.nn.sigmoid(a0_ref[...] + lo[:, A_DIM:])
        xg = us[:, o3 + LORA_W + LORA_A:]
        g = _dot(jax.nn.sigmoid(xg).astype(BF16), g2_ref[...])
        kk = k * kkw_ref[...]
        kk = kk / jnp.maximum(jnp.sqrt(_segsum(kk * kk, bd)), 1e-12)
        k2 = k * (1.0 + (a - 1.0) * ka_ref[...])
        seq_outs = zip((r_o, w_o, k_o, v_o, na_o, nb_o), (r, log_decay, k2, v, -kk, kk * a))
        for o_ref, val in seq_outs:
            if seg_len is None:
                o_ref[:, rows] = jnp.transpose(val)
            else:
                o_ref[rows, :] = val
        g_o[rows, :] = g
        bonus_o[rows, :] = _segsum(r * k2 * rk_ref[...], bd) * v
    if seg_len is None:
        carry_ref[0:1, :] = prev_row
        shift_o[0] = prev_row


def _rwkv_pre(x, nb, seq, lw, init_rows=None):
    m, d = x.shape
    consts = [lw["mix_norm"], lw["w_a"], lw["rwkv_mu"], lw["rwkv_w0"], lw["rwkv_a0"], lw["w_wa"],
              lw["rwkv_g2"], lw["rwkv_kk"], lw["rwkv_ka"], lw["rwkv_rk"], lw["bd"]]
    const_specs = [_const_spec(c.shape) for c in consts]
    outs = [jax.ShapeDtypeStruct((m, A_DIM), F32)] * 8
    if init_rows is None:
        tm = _tile(seq, 512)
        nc = seq // tm
        row = lambda b, c: (b * nc + c, 0)
        grid = (nb, nc)
        in_specs = [pl.BlockSpec((tm, d), row)] + const_specs
        out_specs = ([pl.BlockSpec((A_DIM, tm), lambda b, c: (b, c))] * 6 + [pl.BlockSpec((tm, A_DIM), row)] * 2
                     + [pl.BlockSpec((1, 1, RWKV_COLS), lambda b, c: (b, 0, 0))])
        outs = ([jax.ShapeDtypeStruct((nb * A_DIM, seq), F32)] * 6 + outs[6:]
                + [jax.ShapeDtypeStruct((nb, 1, RWKV_COLS), F32)])
        scratch = [pltpu.VMEM((8, RWKV_COLS), F32)]
        args = [x] + consts
        seg_len = None
    else:
        tm = m
        row = lambda i: (0, 0)
        grid = (1,)
        in_specs = [pl.BlockSpec((tm, d), row)] + const_specs + [pl.BlockSpec((tm, RWKV_COLS), row)]
        out_specs = [pl.BlockSpec((tm, A_DIM), row)] * 8 + [pl.BlockSpec((tm, RWKV_COLS), row)]
        outs = outs + [jax.ShapeDtypeStruct((m, RWKV_COLS), F32)]
        scratch = []
        args = [x] + consts + [init_rows]
        seg_len = seq
    return pl.pallas_call(
        functools.partial(_rwkv_pre_kernel, seg_len=seg_len), grid=grid, in_specs=in_specs,
        out_specs=out_specs, out_shape=outs, scratch_shapes=scratch,
        compiler_params=_cparams(len(grid)), name="rwkv_pre")(*args)


def _scan_kernel(r_ref, lw_ref, k_ref, v_ref, a_ref, b_ref, s0_ref, o_ref, st_ref, s_ref, vec_ref, *, tc):
    c = pl.program_id(1)

    @pl.when(c == 0)
    def _():
        s_ref[...] = s0_ref[...]

    jb = 32
    ln = s_ref.shape[-1]

    def step(t, carry):
        cum, gam_prev = carry
        cum = cum + lw_ref[t]
        gam = jnp.exp(cum)
        ginv = jnp.exp(-cum)
        vec_ref[0] = a_ref[t] * gam_prev
        vec_ref[1] = b_ref[t] * ginv
        vec_ref[2] = k_ref[t] * ginv
        vec_ref[3] = r_ref[t] * gam

        def sa_body(q, sa):
            for jj in range(jb):
                j = q * jb + jj
                sa = sa + s_ref[j] * vec_ref[0, pl.ds(j, 1), :]
            return sa

        sa = lax.fori_loop(0, HEAD_DIM // jb, sa_body, jnp.zeros((HEAD_DIM, ln), F32))
        vt = v_ref[t]

        def up_body(q, o):
            for jj in range(jb):
                j = q * jb + jj
                sn = s_ref[j] + sa * vec_ref[1, pl.ds(j, 1), :] + vt * vec_ref[2, pl.ds(j, 1), :]
                s_ref[j] = sn
                o = o + sn * vec_ref[3, pl.ds(j, 1), :]
            return o

        o_ref[t] = lax.fori_loop(0, HEAD_DIM // jb, up_body, jnp.zeros((HEAD_DIM, ln), F32))
        return cum, gam

    zeros = jnp.zeros((HEAD_DIM, ln), F32)
    _, gam_end = lax.fori_loop(0, tc, step, (zeros, zeros + 1.0))
    vec_ref[0] = gam_end
    for j in range(HEAD_DIM):
        s_ref[j] = s_ref[j] * vec_ref[0, j:j + 1, :]

    @pl.when(c == pl.num_programs(1) - 1)
    def _():
        st_ref[...] = s_ref[...]


def _scan(r, w, k, v, a, b, s0):
    t_len, hd, nbh = r.shape
    ln = min(LANES, nbh)
    tc = _tile(t_len, 64) if t_len % 8 == 0 else t_len
    grid = (nbh // ln, t_len // tc)
    seq_spec = pl.BlockSpec((tc, hd, ln), lambda l, c: (c, 0, l))
    st_spec = pl.BlockSpec((hd, hd, ln), lambda l, c: (0, 0, l))
    return pl.pallas_call(
        functools.partial(_scan_kernel, tc=tc), grid=grid,
        in_specs=[seq_spec] * 6 + [st_spec], out_specs=[seq_spec, st_spec],
        out_shape=[jax.ShapeDtypeStruct((t_len, hd, nbh), F32),
                   jax.ShapeDtypeStruct((hd, hd, nbh), F32)],
        scratch_shapes=[pltpu.VMEM((hd, hd, ln), F32), pltpu.VMEM((4, hd, ln), F32)],
        compiler_params=_cparams(2), name="rwkv_scan")(r, w, k, v, a, b, s0)


def _conv_kernel(*refs, seg_len):
    x_ref, g_ref, wb_ref, cw_ref = refs[:4]
    rest = refs[4:]
    if seg_len is None:
        o_ref, st_ref, carry_ref = rest
    else:
        i0_ref, i1_ref, o_ref, st_ref = rest
    tm = x_ref.shape[0]
    cw = cw_ref[...]
    if seg_len is None:
        @pl.when(pl.program_id(1) == 0)
        def _():
            carry_ref[...] = jnp.zeros_like(carry_ref)
        hist = [carry_ref[1:2, :], carry_ref[0:1, :]]
    for rows in _row_blocks(tm, seg_len, ROW_BLOCK // 2):
        rb = rows.stop - rows.start
        h = _rms(x_ref[rows, :], g_ref[...]).astype(BF16)
        u = _dot(h, wb_ref[...])
        bg = u[:, :B_DIM]
        z = u[:, B_DIM:2 * B_DIM] * u[:, 2 * B_DIM:]
        if seg_len is None:
            z1 = _rows_before(z, 1, hist[:1], seg_len=None)
            z2 = _rows_before(z, 2, hist, seg_len=None)
            hist = [z[rb - 1:rb, :], z[rb - 2:rb - 1, :]]
        else:
            init = [i1_ref[rows, :], i0_ref[rows, :]]
            z1 = _rows_before(z, 1, init[:1], seg_len=seg_len)
            z2 = _rows_before(z, 2, init, seg_len=seg_len)
            st_ref[rows, :] = z
        o_ref[rows, :] = bg * (z2 * cw[0:1, :] + z1 * cw[1:2, :] + z * cw[2:3, :])
    if seg_len is None:
        last2 = jnp.concatenate([hist[1], hist[0]], axis=0)
        carry_ref[0:2, :] = last2
        st_ref[0] = last2


def _conv(x, nb, seq, lw, init=None):
    m, d = x.shape
    assert seq >= CONV_W - 1
    consts = [lw["mix_norm"], lw["w_b"], lw["conv_w"]]
    const_specs = [_const_spec(c.shape) for c in consts]
    if init is None:
        tm = _tile(seq, 512)
        nc = seq // tm
        row = lambda b, c: (b * nc + c, 0)
        grid = (nb, nc)
        in_specs = [pl.BlockSpec((tm, d), row)] + const_specs
        out_specs = [pl.BlockSpec((tm, B_DIM), row),
                     pl.BlockSpec((1, CONV_W - 1, B_DIM), lambda b, c: (b, 0, 0))]
        outs = [jax.ShapeDtypeStruct((m, B_DIM), F32), jax.ShapeDtypeStruct((nb, CONV_W - 1, B_DIM), F32)]
        scratch = [pltpu.VMEM((8, B_DIM), F32)]
        args = [x] + consts
        seg_len = None
    else:
        tm = m
        row = lambda i: (0, 0)
        grid = (1,)
        in_specs = [pl.BlockSpec((tm, d), row)] + const_specs + [pl.BlockSpec((tm, B_DIM), row)] * 2
        out_specs = [pl.BlockSpec((tm, B_DIM), row)] * 2
        outs = [jax.ShapeDtypeStruct((m, B_DIM), F32)] * 2
        scratch = []
        args = [x] + consts + list(init)
        seg_len = seq
    return pl.pallas_call(
        functools.partial(_conv_kernel, seg_len=seg_len), grid=grid, in_specs=in_specs,
        out_specs=out_specs, out_shape=outs, scratch_shapes=scratch,
        compiler_params=_cparams(len(grid)), name="shortconv")(*args)


def _attn_proj_kernel(x_ref, g_ref, wc_ref, qn_ref, kn_ref, in_ref, bd_ref, cos_ref, sa_ref, sb_ref,
                      *outs, transposed):
    bd = bd_ref[...]
    inv_hd = 1.0 / HEAD_DIM
    if transposed:
        q_o, k_o, qi_o, kw_o, kt_o, vt_o, kit_o = outs
    else:
        q_o, k_o, qi_o, kw_o, v_o, ki_o = outs
    for rows in _row_blocks(x_ref.shape[0], None if transposed else 1):
        h = _rms(x_ref[rows, :], g_ref[...]).astype(BF16)
        u = _dot(h, wc_ref[...])
        cos, sa, sb = cos_ref[rows, :], sa_ref[rows, :], sb_ref[rows, :]
        q = u[:, :C_DIM]
        q = q * lax.rsqrt(_segsum(q * q, bd) * inv_hd + RMS_EPS) * qn_ref[...]
        q = _rope(q, cos, sa, sb)
        k = u[:, C_DIM:C_DIM + C_KV_DIM]
        k = k * lax.rsqrt(_segsum(k * k, bd[:C_KV_DIM, :C_KV_DIM]) * inv_hd + RMS_EPS) * kn_ref[...]
        k = _rope(k, cos[:, :C_KV_DIM], sa[:, :C_KV_DIM], sb[:, :C_KV_DIM])
        o = C_DIM + C_KV_DIM
        v = u[:, o:o + C_KV_DIM]
        o += C_KV_DIM
        qi = _rope(u[:, o:o + IDX_HEADS * IDX_DIM], cos, sa, sb)
        o += IDX_HEADS * IDX_DIM
        kw = u[:, o:o + LANES]
        lane = lax.broadcasted_iota(I32, kw.shape, 1)
        is_ki = lane < IDX_DIM
        ms = jnp.sum(jnp.where(is_ki, kw * kw, 0.0), axis=-1, keepdims=True) * (1.0 / IDX_DIM)
        kin = _rope(kw * lax.rsqrt(ms + RMS_EPS) * in_ref[...], cos[:, :LANES], sa[:, :LANES], sb[:, :LANES])
        kw = jnp.where(is_ki, kin, kw)
        if transposed:
            kt_o[0, :, rows] = jnp.transpose(k)
            vt_o[0, :, rows] = jnp.transpose(v)
            kit_o[0, :, rows] = jnp.transpose(kw)[:IDX_DIM, :]
        else:
            v_o[rows, :] = v
            ki_o[rows, :] = kw[:, :IDX_DIM]
        q_o[rows, :] = q
        k_o[rows, :] = k
        qi_o[rows, :] = qi
        kw_o[rows, :] = kw


def _attn_proj(x, nb, seq, lw, tabs, sample):
    m, d = x.shape
    consts = [lw["mix_norm"], lw["w_c"], lw["q_norm"], lw["k_norm"], lw["idx_k_norm"], lw["bd"]]
    const_specs = [_const_spec(c.shape) for c in consts]
    if sample:
        tm, nc = m, 1
        grid = (1, 1)
    else:
        tm = _tile(seq, 512)
        nc = seq // tm
        grid = (nb, nc)
    row = lambda b, c: (b * nc + c, 0)
    tab = lambda b, c: (c, 0)
    widths = [C_DIM, C_KV_DIM, IDX_HEADS * IDX_DIM, LANES]
    out_specs = [pl.BlockSpec((tm, w), row) for w in widths]
    out_shape = [jax.ShapeDtypeStruct((m, w), F32) for w in widths]
    if sample:
        for w in (C_KV_DIM, IDX_DIM):
            out_specs.append(pl.BlockSpec((tm, w), row))
            out_shape.append(jax.ShapeDtypeStruct((m, w), F32))
    else:
        for w in (C_KV_DIM, C_KV_DIM, IDX_DIM):
            out_specs.append(pl.BlockSpec((1, w, tm), lambda b, c: (b, 0, c)))
            out_shape.append(jax.ShapeDtypeStruct((nb, w, seq), F32))
    return pl.pallas_call(
        functools.partial(_attn_proj_kernel, transposed=not sample), grid=grid,
        in_specs=[pl.BlockSpec((tm, d), row)] + const_specs + [pl.BlockSpec((tm, C_DIM), tab)] * 3,
        out_specs=out_specs, out_shape=out_shape,
        compiler_params=_cparams(2), name="attn_proj")(x, *consts, *tabs)


def _score_keys(score):
    bits = pltpu.bitcast(score, I32)
    return jnp.where(bits < 0, bits ^ 0x7FFFFFFF, bits)


def _lane_total(acc):
    return jnp.broadcast_to(jnp.sum(acc, axis=-1, keepdims=True), acc.shape)


def _tree(parts, op):
    while len(parts) > 1:
        nxt = [op(parts[i], parts[i + 1]) for i in range(0, len(parts) - 1, 2)]
        parts = nxt + (parts[-1:] if len(parts) % 2 else [])
    return parts[0]


def _fold8(x, op):
    return _tree([x[i * 8:(i + 1) * 8] for i in range(x.shape[0] // 8)], op)


def _kth_key(count_fn, topk, shape):
    def vbit(it, acc):
        cand = acc | jnp.left_shift(jnp.int32(1), 31 - it)
        cmp = cand ^ INT_MIN
        cnt = count_fn(lambda kk, pos: jnp.where(kk >= cmp, 1, 0))
        return jnp.where(cnt >= topk, cand, acc)

    return lax.fori_loop(0, 32, vbit, jnp.zeros(shape, I32)) ^ INT_MIN


def _select_topk(count_fn, topk, pos_bits, shape, lp_ref):
    zeros = jnp.zeros(shape, I32)
    thr = _kth_key(count_fn, topk, shape)
    n_ge = count_fn(lambda kk, pos: jnp.where(kk >= thr, 1, 0))
    excess = jnp.where(thr == INT_MIN, 0, n_ge - topk)
    lp_ref[...] = jnp.full(shape, 2 ** 31 - 1, I32)

    @pl.when(jnp.max(excess) > 0)
    def _():
        need = topk - count_fn(lambda kk, pos: jnp.where(kk > thr, 1, 0))

        def pbit(it, acc):
            cand = acc | jnp.left_shift(jnp.int32(1), pos_bits - 1 - it)
            cnt = count_fn(lambda kk, pos: jnp.where(kk == thr, jnp.where(pos < cand, 1, 0), 0))
            return jnp.where(cnt < need, cand, acc)

        lp_ref[...] = lax.fori_loop(0, pos_bits, pbit, zeros)

    return thr, lp_ref[...]


def _select_bias(kk, pos, thr, last_pos):
    tie = jnp.where(pos <= last_pos, 0.0, NEG_BIG)
    bias = jnp.where(kk == thr, tie, jnp.where(kk > thr, 0.0, NEG_BIG))
    return jnp.where(kk == INT_MIN, NEG_BIG, bias)


def _pattn_kernel(qi_ref, kwq_ref, q_ref, kwk_ref, k_ref, vt_ref, o_ref,
                  keys_ref, s_ref, qip_ref, qg_ref, acc_ref, m_ref, l_ref, ties_ref, *, tq, topk):
    qb = pl.program_id(1)
    nch = qb + 1
    nslab = tq // 8
    shape8 = (8, tq)
    sub8 = lax.broadcasted_iota(I32, shape8, 0)
    lane_q = lax.broadcasted_iota(I32, (tq, LANES), 1)
    key_in = lax.broadcasted_iota(I32, (tq, tq), 0)
    qry_in = lax.broadcasted_iota(I32, (tq, tq), 1)

    wt = jnp.transpose(kwq_ref[...]) * ((IDX_DIM ** -0.5) * (IDX_HEADS ** -0.5))
    for hh in range(IDX_HEADS):
        slab = qi_ref[:, (hh // 2) * LANES:(hh // 2 + 1) * LANES]
        if hh % 2:
            slab = pltpu.roll(slab, IDX_DIM, 1)
        qip_ref[hh] = jnp.where(lane_q < IDX_DIM, slab, 0.0).astype(BF16)

    def for_chunks(body):
        odd = nch % 2

        @pl.when(odd == 1)
        def _():
            body(0)

        def pair(p, carry):
            kc = odd + 2 * p
            body(kc)
            body(kc + 1)
            return carry

        lax.fori_loop(0, nch // 2, pair, 0)

    def idx_body(kc):
        start = pl.multiple_of(kc * tq, tq)
        ks = kwk_ref[pl.ds(start, tq), :].astype(BF16)
        acc = None
        for hh in range(IDX_HEADS):
            s = _dot_nt(ks, qip_ref[hh])
            term = jnp.maximum(s, 0.0) * wt[IDX_DIM + hh:IDX_DIM + hh + 1, :]
            acc = term if acc is None else acc + term
        kk = jnp.where(acc == 0.0, 0, _score_keys(acc))
        keys_ref[kc] = jnp.where(kc * tq + key_in <= qb * tq + qry_in, kk, INT_MIN)

    for_chunks(idx_body)

    def count_fn(ind):
        def body(kc, acc):
            base = kc * tq
            parts = [ind(keys_ref[kc, i * 8:(i + 1) * 8, :], base + i * 8 + sub8) for i in range(nslab)]
            return acc + _tree(parts, jnp.add)
        acc = lax.fori_loop(0, nch, body, jnp.zeros(shape8, I32))
        return jnp.broadcast_to(jnp.sum(acc, axis=0, keepdims=True), shape8)

    thr = _kth_key(count_fn, topk, shape8)
    need = topk - count_fn(lambda kk, pos: jnp.where(kk > thr, 1, 0))

    def chunk_bias(kc, ties_before):
        slabs = []
        for i in range(nslab):
            kk = keys_ref[kc, i * 8:(i + 1) * 8, :]
            tie = jnp.where(kk == thr, 1, 0)
            incl = tie
            for sh in (1, 2, 4):
                incl = incl + jnp.where(sub8 >= sh, pltpu.roll(incl, sh, 0), 0)
            slabs.append((kk, incl - tie, jnp.broadcast_to(incl[7:8, :], shape8)))
        out = []
        for kk, excl, total in slabs:
            rank = ties_before + excl
            bias = jnp.where(kk > thr, 0.0, NEG_BIG)
            bias = jnp.where(kk == thr, jnp.where(rank < need, 0.0, NEG_BIG), bias)
            out.append(jnp.where(kk == INT_MIN, NEG_BIG, bias))
            ties_before = ties_before + total
        return jnp.concatenate(out, axis=0), ties_before

    scale = (HEAD_DIM ** -0.5) * 1.4426950408889634
    for g in range(C_KV_HEADS):
        qslab = q_ref[:, g * LANES:(g + 1) * LANES] * scale
        rolled = pltpu.roll(qslab, HEAD_DIM, 1)
        koff = g % 2
        in_half = (lane_q >= HEAD_DIM) if koff else (lane_q < HEAD_DIM)
        qg_ref[g] = jnp.concatenate(
            [jnp.where(in_half, qslab if r == koff else rolled, 0.0) for r in range(2)],
            axis=0).astype(BF16)

    m_ref[...] = jnp.full(m_ref.shape, NEG_BIG, F32)
    ties_ref[...] = jnp.zeros(shape8, I32)

    def max_body(kc):
        start = pl.multiple_of(kc * tq, tq)
        bias, ties = chunk_bias(kc, ties_ref[...])
        ties_ref[...] = ties
        bias2 = jnp.concatenate([bias, bias], axis=1)
        for g in range(C_KV_HEADS):
            kcol = (g // 2) * LANES
            kch = k_ref[pl.ds(start, tq), kcol:kcol + LANES].astype(BF16)
            s = _dot_nt(kch, qg_ref[g]) + bias2
            s_ref[kc, :, g * 2 * tq:(g + 1) * 2 * tq] = s
            m_ref[g] = jnp.maximum(m_ref[g], _fold8(s, jnp.maximum))

    for_chunks(max_body)
    m_row = [jnp.max(m_ref[g], axis=0, keepdims=True) for g in range(C_KV_HEADS)]

    acc_ref[...] = jnp.zeros_like(acc_ref)
    l_ref[...] = jnp.zeros_like(l_ref)

    def sum_body(kc):
        start = pl.multiple_of(kc * tq, tq)
        for g in range(C_KV_HEADS):
            p = jnp.exp2(s_ref[kc, :, g * 2 * tq:(g + 1) * 2 * tq] - m_row[g])
            vt = vt_ref[0, g * HEAD_DIM:(g + 1) * HEAD_DIM, pl.ds(start, tq)].astype(BF16)
            acc_ref[g] += _dot(vt, p.astype(BF16))
            l_ref[g] += _fold8(p, jnp.add)

    for_chunks(sum_body)
    for g in range(C_KV_HEADS):
        out_t = acc_ref[g] / jnp.sum(l_ref[g], axis=0, keepdims=True)
        o_ref[:, g * LANES:(g + 1) * LANES] = jnp.transpose(
            jnp.concatenate([out_t[:, :tq], out_t[:, tq:]], axis=0))


def _prompt_attention(q, k, vt, qi, kw, nb, seq):
    assert C_HEADS == 2 * C_KV_HEADS
    topk = min(TOPK_MAX, seq // 4)
    tq = _tile(seq, 256)
    assert tq % LANES == 0
    nq = seq // tq
    qrow = lambda b, c: (b * nq + c, 0)
    krow = lambda b, c: (b, 0)
    return pl.pallas_call(
        functools.partial(_pattn_kernel, tq=tq, topk=topk),
        grid=(nb, nq),
        in_specs=[pl.BlockSpec((tq, IDX_HEADS * IDX_DIM), qrow), pl.BlockSpec((tq, LANES), qrow),
                  pl.BlockSpec((tq, C_DIM), qrow), pl.BlockSpec((seq, LANES), krow),
                  pl.BlockSpec((seq, C_KV_DIM), krow),
                  pl.BlockSpec((1, C_KV_DIM, seq), lambda b, c: (b, 0, 0))],
        out_specs=pl.BlockSpec((tq, C_DIM), qrow),
        out_shape=jax.ShapeDtypeStruct((nb * seq, C_DIM), F32),
        scratch_shapes=[pltpu.VMEM((nq, tq, tq), I32),
                        pltpu.VMEM((nq, tq, C_KV_HEADS * 2 * tq), F32),
                        pltpu.VMEM((IDX_HEADS, tq, LANES), BF16),
                        pltpu.VMEM((C_KV_HEADS, 2 * tq, LANES), BF16),
                        pltpu.VMEM((C_KV_HEADS, HEAD_DIM, 2 * tq), F32),
                        pltpu.VMEM((C_KV_HEADS, 8, 2 * tq), F32),
                        pltpu.VMEM((C_KV_HEADS, 8, 2 * tq), F32), pltpu.VMEM((8, tq), I32)],
        compiler_params=_cparams(2), name="prompt_attn")(qi, kw, q, kw, k, vt)


SROWS = 8


def _sidx_kernel(pt_ref, qh_ref, wh_ref, kn_ref, *rest, npg, page):
    pages = rest[:npg]
    past_o, new_o = rest[npg:]
    qh = qh_ref[0].astype(BF16)
    wh = wh_ref[0] * ((IDX_DIM ** -0.5) * (IDX_HEADS ** -0.5))

    n_out = past_o.shape[1]

    def head_sum(s):
        s = jnp.maximum(s, 0.0) * wh
        return _tree([s[hh * SROWS:(hh + 1) * SROWS] for hh in range(IDX_HEADS)], jnp.add)[:n_out]

    keys_t = jnp.concatenate([pg[0] for pg in pages], axis=1).astype(BF16)
    past_o[0] = head_sum(_dot(qh, keys_t))

    @pl.when(pl.program_id(1) == 0)
    def _():
        new_o[0] = head_sum(_dot_nt(qh, kn_ref[0].astype(BF16)))


def _sample_index_scores(cache_kidx, page_table, layer, qh, wh, ki_new, seq):
    depth, n_phys, page, _ = cache_kidx.shape
    db, n_pages = page_table.shape
    npg = _pages_per_step(n_pages, 4 * PAGES_PER_STEP)
    cache = cache_kidx.transpose(0, 1, 3, 2).reshape(depth * n_phys, IDX_DIM, page)
    base = layer * n_phys

    def page_spec(i):
        return pl.BlockSpec((1, IDX_DIM, page), lambda b, s, pt: (base + pt[b, s * npg + i], 0, 0))

    per_b = lambda b, s, pt: (b, 0, 0)
    grid_spec = pltpu.PrefetchScalarGridSpec(
        num_scalar_prefetch=1, grid=(db, n_pages // npg),
        in_specs=[pl.BlockSpec((1, IDX_HEADS * SROWS, IDX_DIM), per_b),
                  pl.BlockSpec((1, IDX_HEADS * SROWS, 1), per_b),
                  pl.BlockSpec((1, LANES, IDX_DIM), per_b)] + [page_spec(i) for i in range(npg)],
        out_specs=[pl.BlockSpec((1, seq, npg * page), lambda b, s, pt: (b, 0, s)),
                   pl.BlockSpec((1, seq, LANES), per_b)])
    return pl.pallas_call(
        functools.partial(_sidx_kernel, npg=npg, page=page), grid_spec=grid_spec,
        out_shape=[jax.ShapeDtypeStruct((db, seq, n_pages * page), F32),
                   jax.ShapeDtypeStruct((db, seq, LANES), F32)],
        compiler_params=_cparams(2), name="sample_index")(page_table, qh, wh, ki_new, *([cache] * npg))


def _ssel_kernel(past_ref, new_ref, bpast_o, bnew_o, keys_ref, lp_ref, *, topk, pos_bits, n_new):
    rows, past = past_ref.shape
    nch = past // LANES
    lane = lax.broadcasted_iota(I32, (rows, LANES), 1)
    t_row = lax.broadcasted_iota(I32, (rows, LANES), 0) % n_new

    def fill(c, carry):
        start = pl.multiple_of(c * LANES, LANES)
        keys_ref[c] = _score_keys(past_ref[:, pl.ds(start, LANES)])
        return carry

    lax.fori_loop(0, nch, fill, 0)
    new_ok = lane <= jnp.minimum(t_row, n_new - 1)
    keys_ref[nch] = jnp.where(new_ok, _score_keys(new_ref[...]), INT_MIN)

    grp = 8
    assert nch % grp == 0

    def count_fn(pred):
        def body(q, acc):
            c0 = q * grp
            return acc + _tree([pred(keys_ref[c0 + i], (c0 + i) * LANES + lane) for i in range(grp)], jnp.add)
        acc = lax.fori_loop(0, nch // grp, body, jnp.zeros((rows, LANES), I32))
        return _lane_total(acc + pred(keys_ref[nch], nch * LANES + lane))

    thr, last_pos = _select_topk(count_fn, topk, pos_bits, (rows, LANES), lp_ref)

    def emit(c, carry):
        start = pl.multiple_of(c * LANES, LANES)
        bpast_o[:, pl.ds(start, LANES)] = _select_bias(keys_ref[c], c * LANES + lane, thr, last_pos)
        return carry

    lax.fori_loop(0, nch, emit, 0)
    bnew_o[...] = _select_bias(keys_ref[nch], nch * LANES + lane, thr, last_pos)


def _sample_select(sc_past, sc_new, n_new):
    rows, past = sc_past.shape
    topk = min(TOPK_MAX, (past + n_new) // 4)
    tr = _tile(rows, 64)
    assert tr % n_new == 0
    pos_bits = int(past + LANES - 1).bit_length()
    row = lambda i: (i, 0)
    return pl.pallas_call(
        functools.partial(_ssel_kernel, topk=topk, pos_bits=pos_bits, n_new=n_new),
        grid=(rows // tr,),
        in_specs=[pl.BlockSpec((tr, past), row), pl.BlockSpec((tr, LANES), row)],
        out_specs=[pl.BlockSpec((tr, past), row), pl.BlockSpec((tr, LANES), row)],
        out_shape=[jax.ShapeDtypeStruct((rows, past), F32), jax.ShapeDtypeStruct((rows, LANES), F32)],
        scratch_shapes=[pltpu.VMEM((past // LANES + 1, tr, LANES), I32), pltpu.VMEM((tr, LANES), I32)],
        compiler_params=_cparams(1), name="sample_select")(sc_past, sc_new)


def _satt_kernel(pt_ref, qbd_ref, bpast_ref, bnew_ref, kn_ref, vn_ref, *rest, npg, page):
    kpages = rest[:npg]
    vpages = rest[npg:2 * npg]
    o_ref, m_ref, l_ref, acc_ref = rest[2 * npg:]
    s_id = pl.program_id(1)
    nrow = qbd_ref.shape[1]
    reps = nrow // SROWS

    @pl.when(s_id == 0)
    def _():
        m_ref[...] = jnp.full_like(m_ref, NEG_BIG)
        l_ref[...] = jnp.zeros_like(l_ref)
        acc_ref[...] = jnp.zeros_like(acc_ref)

    qbd = qbd_ref[0].astype(BF16)

    def update(s, bias, pv):
        s = s + jnp.concatenate([bias] * reps, axis=0)
        m_old = m_ref[...]
        m_new = jnp.maximum(m_old, jnp.max(s, axis=-1, keepdims=True))
        p = jnp.exp(s - m_new)
        alpha = jnp.exp(m_old - m_new)
        l_ref[...] = alpha * l_ref[...] + jnp.sum(p, axis=-1, keepdims=True)
        acc_ref[...] = alpha * acc_ref[...] + pv(p.astype(BF16))
        m_ref[...] = m_new

    keys_t = jnp.concatenate([kp[0] for kp in kpages], axis=1).astype(BF16)
    vals_t = jnp.concatenate([vp[0] for vp in vpages], axis=1).astype(BF16)
    update(_dot(qbd, keys_t), bpast_ref[0], lambda p: _dot_nt(p, vals_t))

    @pl.when(s_id == pl.num_programs(1) - 1)
    def _():
        vn = vn_ref[0].astype(BF16)
        update(_dot_nt(qbd, kn_ref[0].astype(BF16)), bnew_ref[0], lambda p: _dot(p, vn))
        o_ref[0] = acc_ref[...] / l_ref[...]


def _sample_attend(cache_k, cache_v, page_table, layer, qbd, bias_past, bias_new, k_new, v_new):
    depth, n_phys, page = cache_k.shape[:3]
    db, n_pages = page_table.shape
    npg = _pages_per_step(n_pages)
    ck = cache_k.transpose(0, 1, 3, 4, 2).reshape(depth * n_phys, C_KV_DIM, page)
    cv = cache_v.transpose(0, 1, 3, 4, 2).reshape(depth * n_phys, C_KV_DIM, page)
    base = layer * n_phys
    nrow = qbd.shape[1]

    def page_spec(i):
        return pl.BlockSpec((1, C_KV_DIM, page), lambda b, s, pt: (base + pt[b, s * npg + i], 0, 0))

    per_b = lambda b, s, pt: (b, 0, 0)
    grid_spec = pltpu.PrefetchScalarGridSpec(
        num_scalar_prefetch=1, grid=(db, n_pages // npg),
        in_specs=[pl.BlockSpec((1, nrow, C_KV_DIM), per_b),
                  pl.BlockSpec((1, SROWS, npg * page), lambda b, s, pt: (b, 0, s)),
                  pl.BlockSpec((1, SROWS, LANES), per_b),
                  pl.BlockSpec((1, LANES, C_KV_DIM), per_b),
                  pl.BlockSpec((1, LANES, C_KV_DIM), per_b)]
                 + [page_spec(i) for i in range(npg)] * 2,
        out_specs=pl.BlockSpec((1, nrow, C_KV_DIM), per_b),
        scratch_shapes=[pltpu.VMEM((nrow, 1), F32), pltpu.VMEM((nrow, 1), F32),
                        pltpu.VMEM((nrow, C_KV_DIM), F32)])
    return pl.pallas_call(
        functools.partial(_satt_kernel, npg=npg, page=page), grid_spec=grid_spec,
        out_shape=jax.ShapeDtypeStruct((db, nrow, C_KV_DIM), F32),
        compiler_params=_cparams(2), name="sample_attn")(
            page_table, qbd, bias_past, bias_new, k_new, v_new, *([ck] * npg), *([cv] * npg))


def _merge_kernel(x_ref, g_ref, wg_ref, os_ref, bon_ref, gg_ref, lw_ref, lb_ref, bd_ref,
                  ob_ref, oc_ref, wpa_ref, wpb_ref, wpc_ref, wo_ref, o_ref):
    x = x_ref[...]
    d = x.shape[1]
    h = _rms(x, g_ref[...]).astype(BF16)
    gates = jax.nn.sigmoid(_dot(h, wg_ref[...]))
    bd = bd_ref[...]
    o = os_ref[...]
    mean = _segsum(o, bd, split=True) * (1.0 / HEAD_DIM)
    cen = o - mean
    var = _segsum(cen * cen, bd) * (1.0 / HEAD_DIM)
    on = cen * lax.rsqrt(var + GN_EPS) * lw_ref[...] + lb_ref[...]
    oa = ((on + bon_ref[...]) * gg_ref[...]).astype(BF16)
    merged = (gates[:, :d] * _dot(oa, wpa_ref[...])
              + gates[:, d:2 * d] * _dot(ob_ref[...].astype(BF16), wpb_ref[...])
              + gates[:, 2 * d:] * _dot(oc_ref[...].astype(BF16), wpc_ref[...]))
    o_ref[...] = x + _dot(merged.astype(BF16), wo_ref[...])


def _merge(x, o_scan, bonus, g, o_b, o_c, lw, nb, seq):
    m, d = x.shape
    tm = _tile(seq, 512)
    nc = seq // tm
    row = lambda b, c: (b * nc + c, 0)
    tok = lambda w: pl.BlockSpec((tm, w), row)
    consts_a = [lw["mix_norm"], lw["w_g"]]
    consts_b = [lw["rwkv_lnx_w"], lw["rwkv_lnx_b"], lw["bd"]]
    consts_c = [lw["w_pa"], lw["w_pb"], lw["w_pc"], lw["w_out"]]
    in_specs = ([tok(d)] + [_const_spec(c.shape) for c in consts_a]
                + [pl.BlockSpec((tm, A_DIM), lambda b, c: (c, b))] + [tok(A_DIM)] * 2
                + [_const_spec(c.shape) for c in consts_b] + [tok(B_DIM), tok(C_DIM)]
                + [_const_spec(c.shape) for c in consts_c])
    return pl.pallas_call(
        _merge_kernel, grid=(nb, nc), in_specs=in_specs, out_specs=tok(d),
        out_shape=jax.ShapeDtypeStruct((m, d), F32), compiler_params=_cparams(2), name="merge")(
            x, *consts_a, o_scan, bonus, g, *consts_b, o_b, o_c, *consts_c)


def _rope_tables(pos):
    inv = ROPE_THETA ** (-jnp.arange(ROT_HALF, dtype=F32) / ROT_HALF)
    ang = pos.astype(F32)[:, None] * inv[None, :]
    c, s = jnp.cos(ang), jnp.sin(ang)
    t = pos.shape[0]
    pad = jnp.zeros((t, HEAD_DIM - ROT_DIM), F32)
    zer = jnp.zeros((t, ROT_HALF), F32)
    cos = jnp.concatenate([c, c, pad + 1.0], axis=1)
    sa = jnp.concatenate([-s, zer, pad], axis=1)
    sb = jnp.concatenate([zer, s, pad], axis=1)
    return tuple(jnp.tile(z, (1, C_HEADS)) for z in (cos, sa, sb))


def _layer_weights(i, p):
    d = p["w_in"].shape[1]
    w_in = p["w_in"][i]
    o1 = RWKV_COLS
    o2 = o1 + CONV_COLS
    o3 = o2 + ATTN_COLS
    row = lambda v: v.reshape(1, -1)
    tile_h = lambda v, n: jnp.tile(v, n).reshape(1, -1)
    z = jnp.zeros((LORA_W, A_DIM), F32)
    w_wa = jnp.concatenate([jnp.concatenate([p["rwkv_w2"][i], z], axis=1),
                            jnp.concatenate([z, p["rwkv_a2"][i]], axis=1)], axis=0)
    head = np.arange(A_DIM) // HEAD_DIM
    bd = jnp.asarray(head[:, None] == head[None, :], BF16)
    idx_norm = jnp.concatenate([p["idx_k_norm"][i], jnp.zeros((LANES - IDX_DIM,), F32)])
    bf = lambda w: w.astype(BF16)
    return dict(
        ffn1_norm=p["ffn1_norm"][i], ffn1_wg=bf(p["ffn1_wg"][i]), ffn1_wu=bf(p["ffn1_wu"][i]),
        ffn1_wd=bf(p["ffn1_wd"][i]),
        ffn2_norm=p["ffn2_norm"][i], ffn2_wg=bf(p["ffn2_wg"][i]), ffn2_wu=bf(p["ffn2_wu"][i]),
        ffn2_wd=bf(p["ffn2_wd"][i]),
        mix_norm=row(p["mix_norm"][i]),
        w_a=bf(w_in[:, :o1]), w_b=bf(w_in[:, o1:o2]),
        w_c=bf(jnp.pad(w_in[:, o2:o3], ((0, 0), (0, ATTN_COLS_PAD - ATTN_COLS)))),
        w_g=bf(w_in[:, o3:]),
        rwkv_mu=row(p["rwkv_mu"][i]), rwkv_w0=row(p["rwkv_w0"][i]), rwkv_a0=row(p["rwkv_a0"][i]),
        w_wa=bf(w_wa), rwkv_g2=bf(p["rwkv_g2"][i]), rwkv_kk=row(p["rwkv_kk"][i]),
        rwkv_ka=row(p["rwkv_ka"][i]), rwkv_rk=row(p["rwkv_rk"][i]),
        rwkv_lnx_w=row(p["rwkv_lnx_w"][i]), rwkv_lnx_b=row(p["rwkv_lnx_b"][i]),
        conv_w=p["conv_w"][i], bd=bd,
        q_norm=tile_h(p["q_norm"][i], C_HEADS), k_norm=tile_h(p["k_norm"][i], C_KV_HEADS),
        idx_k_norm=row(idx_norm),
        w_pa=bf(p["w_pa"][i]), w_pb=bf(p["w_pb"][i]), w_pc=bf(p["w_pc"][i]), w_out=bf(p["w_out"][i]),
        ple_norm=p["ple_norm"][i], ple_gate=bf(p["ple_gate"][i]), ple_proj=bf(p["ple_proj"][i]),
    )


def _rwkv_branch(x, nb, seq, lw, wkv0, shift_rows, between=None):
    pre = _rwkv_pre(x, nb, seq, lw, shift_rows)
    s0 = wkv0.transpose(3, 2, 0, 1).reshape(HEAD_DIM, HEAD_DIM, nb * A_HEADS)
    other = None
    if between is not None:
        x, pre = lax.optimization_barrier((x, pre))
        other = between(x)
    r, w, k, v, na, nb_, g, bonus, shift_o = pre
    nbh = nb * A_HEADS
    if shift_rows is None:
        ts = lambda z: z.reshape(nbh, HEAD_DIM, seq).transpose(2, 1, 0)
    else:
        ts = lambda z: z.reshape(nb, seq, A_HEADS, HEAD_DIM).transpose(1, 3, 0, 2).reshape(seq, HEAD_DIM, nbh)
    seqs = [ts(z) for z in (r, w, k, v, na, nb_)]
    if between is not None:
        s0, other = lax.optimization_barrier((s0, other))
    o, s_fin = _scan(*seqs, s0)
    wkv1 = s_fin.reshape(HEAD_DIM, HEAD_DIM, nb, A_HEADS).transpose(2, 3, 1, 0)
    if shift_rows is None:
        shift1 = shift_o.reshape(nb, RWKV_COLS)
        o_tok = o.transpose(0, 2, 1).reshape(seq, nb * A_DIM)
    else:
        shift1 = shift_o.reshape(nb, seq, RWKV_COLS)[:, -1]
        o_tok = o.reshape(seq, HEAD_DIM, nb, A_HEADS).transpose(2, 0, 3, 1).reshape(nb * seq, A_DIM)
    return o_tok, bonus, g, shift1, wkv1, other


def _layer_tail(x1, o_scan, bonus, g, o_b, o_c, p_emb, lw, nb, seq):
    x2 = _merge(x1, o_scan, bonus, g, o_b, o_c, lw, nb, seq)
    return _ffn(x2, lw["ffn2_norm"], lw["ffn2_wg"], lw["ffn2_wu"], lw["ffn2_wd"],
                ple=(p_emb, lw["ple_norm"], lw["ple_gate"], lw["ple_proj"]))


def _prompt_layer(x, p_emb, nb, seq, lw, tabs):
    x1 = _ffn(x, lw["ffn1_norm"], lw["ffn1_wg"], lw["ffn1_wu"], lw["ffn1_wd"])
    wkv0 = jnp.zeros((nb, A_HEADS, HEAD_DIM, HEAD_DIM), F32)

    def other_mixers(xin):
        o_b, conv1 = _conv(xin, nb, seq, lw)
        q, k, qi, kw, kt, vt, kit = _attn_proj(xin, nb, seq, lw, tabs, False)
        return o_b, conv1, kt, vt, kit, _prompt_attention(q, k, vt, qi, kw, nb, seq)

    o_scan, bonus, g, shift1, wkv1, other = _rwkv_branch(x1, nb, seq, lw, wkv0, None, other_mixers)
    o_b, conv1, kt, vt, kit, o_c = other
    x4 = _layer_tail(x1, o_scan, bonus, g, o_b, o_c, p_emb, lw, nb, seq)
    heads = lambda z: z.reshape(nb, C_KV_HEADS, HEAD_DIM, seq).transpose(0, 3, 1, 2)
    st = (heads(kt), heads(vt), kit.transpose(0, 2, 1), wkv1, shift1, conv1)
    return x4, st


def _sample_layer(x, p_emb, nb, seq, lw, tabs, layer, cache_k, cache_v, cache_kidx, page_table,
                  wkv0, shift0, conv0):
    assert seq <= SROWS
    x1 = _ffn(x, lw["ffn1_norm"], lw["ffn1_wg"], lw["ffn1_wu"], lw["ffn1_wd"])
    rep_rows = lambda z: jnp.repeat(z, seq, axis=0)
    o_scan, bonus, g, shift1, wkv1, _ = _rwkv_branch(x1, nb, seq, lw, wkv0, rep_rows(shift0))
    o_b, z_all = _conv(x1, nb, seq, lw, init=(rep_rows(conv0[:, 0]), rep_rows(conv0[:, 1])))
    conv1 = z_all.reshape(nb, seq, B_DIM)[:, seq - (CONV_W - 1):]
    q, k, qi, kw, v, ki = _attn_proj(x1, nb, seq, lw, tabs, True)

    pad_t = lambda z: jnp.pad(z, ((0, 0), (0, SROWS - seq)) + ((0, 0),) * (z.ndim - 2))
    qh = pad_t(qi.reshape(nb, seq, IDX_HEADS, IDX_DIM)).transpose(0, 2, 1, 3)
    qh = qh.reshape(nb, IDX_HEADS * SROWS, IDX_DIM)
    wi = kw[:, IDX_DIM:IDX_DIM + IDX_HEADS].reshape(nb, seq, IDX_HEADS)
    wh = pad_t(wi).transpose(0, 2, 1).reshape(nb, IDX_HEADS * SROWS, 1)
    pad_keys = lambda z: jnp.pad(z.reshape(nb, seq, -1), ((0, 0), (0, LANES - seq), (0, 0)))
    sc_past, sc_new = _sample_index_scores(cache_kidx, page_table, layer, qh, wh, pad_keys(ki), seq)
    past = sc_past.shape[-1]
    b_past, b_new = _sample_select(sc_past.reshape(nb * seq, past), sc_new.reshape(nb * seq, LANES), seq)
    b_past = pad_t(b_past.reshape(nb, seq, past))
    b_new = pad_t(b_new.reshape(nb, seq, LANES))

    rep = C_HEADS // C_KV_HEADS
    qg = pad_t(q.reshape(nb, seq, C_KV_HEADS, rep, HEAD_DIM)).transpose(0, 2, 3, 1, 4)
    qg = qg * (HEAD_DIM ** -0.5)
    eye = jnp.eye(C_KV_HEADS, dtype=F32)
    qbd = jnp.einsum("bgrtd,gh->bgrthd", qg, eye).reshape(nb, C_HEADS * SROWS, C_KV_DIM)
    o = _sample_attend(cache_k, cache_v, page_table, layer, qbd,
                       b_past, b_new, pad_keys(k), pad_keys(v))
    o = o.reshape(nb, C_KV_HEADS, rep, SROWS, C_KV_HEADS, HEAD_DIM)
    o = jnp.einsum("bgrthd,gh->bgrtd", o, eye)[:, :, :, :seq]
    o_c = o.transpose(0, 3, 1, 2, 4).reshape(nb * seq, C_DIM)

    x4 = _layer_tail(x1, o_scan, bonus, g, o_b, o_c, p_emb, lw, 1, nb * seq)
    st = (k.reshape(nb, seq, C_KV_HEADS, HEAD_DIM), v.reshape(nb, seq, C_KV_HEADS, HEAD_DIM),
          ki.reshape(nb, seq, IDX_DIM), wkv1, shift1, conv1)
    return x4, st


def kernel(x_prompt, x_sample, cache_k, cache_v, cache_kidx, state_wkv, state_shift, state_conv, page_table, p_prompt, p_sample, ffn1_norm, ffn1_wg, ffn1_wu, ffn1_wd, mix_norm, w_in, rwkv_mu, rwkv_w0, rwkv_w2, rwkv_a0, rwkv_a2, rwkv_g2, rwkv_kk, rwkv_ka, rwkv_rk, rwkv_lnx_w, rwkv_lnx_b, conv_w, q_norm, k_norm, idx_k_norm, w_pa, w_pb, w_pc, w_out, ffn2_norm, ffn2_wg, ffn2_wu, ffn2_wd, ple_norm, ple_gate, ple_proj):
    params = dict(ffn1_norm=ffn1_norm, ffn1_wg=ffn1_wg, ffn1_wu=ffn1_wu, ffn1_wd=ffn1_wd,
                  mix_norm=mix_norm, w_in=w_in, rwkv_mu=rwkv_mu, rwkv_w0=rwkv_w0, rwkv_w2=rwkv_w2,
                  rwkv_a0=rwkv_a0, rwkv_a2=rwkv_a2, rwkv_g2=rwkv_g2, rwkv_kk=rwkv_kk, rwkv_ka=rwkv_ka,
                  rwkv_rk=rwkv_rk.reshape(rwkv_rk.shape[0], -1), rwkv_lnx_w=rwkv_lnx_w,
                  rwkv_lnx_b=rwkv_lnx_b, conv_w=conv_w, q_norm=q_norm, k_norm=k_norm,
                  idx_k_norm=idx_k_norm, w_pa=w_pa, w_pb=w_pb, w_pc=w_pc, w_out=w_out,
                  ffn2_norm=ffn2_norm, ffn2_wg=ffn2_wg, ffn2_wu=ffn2_wu, ffn2_wd=ffn2_wd,
                  ple_norm=ple_norm, ple_gate=ple_gate, ple_proj=ple_proj)
    nb, seq, d = x_prompt.shape
    db, dseq, _ = x_sample.shape
    depth = w_in.shape[0]
    past = page_table.shape[1] * cache_k.shape[2]
    tabs_p = _rope_tables(jnp.arange(seq, dtype=jnp.int32))
    tabs_s = tuple(jnp.tile(z, (db, 1)) for z in _rope_tables(past + jnp.arange(dseq, dtype=jnp.int32)))
    xp = x_prompt.reshape(nb * seq, d)
    xs = x_sample.reshape(db * dseq, d)
    outs_p, outs_s = [], []
    for i in range(depth):
        lw = _layer_weights(i, params)
        xp, st_p = _prompt_layer(xp, (p_prompt.reshape(depth * nb * seq, -1), i), nb, seq, lw, tabs_p)
        xs, st_s = _sample_layer(xs, (p_sample.reshape(depth * db * dseq, -1), i), db, dseq, lw, tabs_s, i,
                                 cache_k, cache_v, cache_kidx, page_table,
                                 state_wkv[i], state_shift[i], state_conv[i])
        outs_p.append(st_p)
        outs_s.append(st_s)
    k_p, v_p, kidx_p, wkv_p, shift_p, conv_p = [jnp.stack(z) for z in zip(*outs_p)]
    k_s, v_s, kidx_s, wkv_s, shift_s, conv_s = [jnp.stack(z) for z in zip(*outs_s)]
    return (xp.reshape(nb, seq, d), xs.reshape(db, dseq, d), k_p, v_p, kidx_p, wkv_p, shift_p, conv_p,
            k_s, v_s, kidx_s, wkv_s, shift_s, conv_s)
```

```python
import functools

import jax
import jax.numpy as jnp
import numpy as np
from jax import lax
from jax.experimental import pallas as pl
from jax.experimental.pallas import tpu as pltpu

F32 = jnp.float32
BF16 = jnp.bfloat16
I32 = jnp.int32

HEAD_DIM = 64
A_HEADS = 8
A_DIM = A_HEADS * HEAD_DIM
LORA_W = 64
LORA_A = 64
LORA_G = 128
B_DIM = 512
CONV_W = 3
C_HEADS = 8
C_KV_HEADS = 4
C_DIM = C_HEADS * HEAD_DIM
C_KV_DIM = C_KV_HEADS * HEAD_DIM
IDX_HEADS = 8
IDX_DIM = 64
TOPK_MAX = 256
ROT_DIM = HEAD_DIM // 4
ROT_HALF = ROT_DIM // 2
ROPE_THETA = 500000.0
N_BRANCH = 3
RMS_EPS = 1e-6
GN_EPS = 64e-5
RWKV_COLS = 3 * A_DIM + LORA_W + LORA_A + LORA_G
CONV_COLS = 3 * B_DIM
ATTN_COLS = C_DIM + 2 * C_KV_DIM + IDX_HEADS * IDX_DIM + IDX_DIM + IDX_HEADS
ATTN_COLS_PAD = 1664

LANES = 128
INT_MIN = -2 ** 31
NEG_BIG = -1e30
VMEM_LIMIT = 56 * 1024 * 1024
PAGES_PER_STEP = 64


def _pages_per_step(n_pages, cap=PAGES_PER_STEP):
    npg = min(cap, n_pages)
    while n_pages % npg:
        npg -= 1
    return npg


def _cparams(n_axes):
    return pltpu.CompilerParams(dimension_semantics=("arbitrary",) * n_axes,
                                vmem_limit_bytes=VMEM_LIMIT)


def _const_spec(shape):
    nd = len(shape)
    return pl.BlockSpec(shape, lambda *_: (0,) * nd, pipeline_mode=pl.Buffered(1))


def _tile(m, pref):
    t = min(m, pref)
    while m % t:
        t -= 8
    return t


def _rms(x, g):
    ms = jnp.mean(x * x, axis=-1, keepdims=True)
    return x * lax.rsqrt(ms + RMS_EPS) * g


def _dot(a, b):
    return jnp.dot(a, b, preferred_element_type=F32)


def _dot_nt(a, b):
    return lax.dot_general(a, b, (((1,), (1,)), ((), ())), preferred_element_type=F32)


def _segsum(x, bd, split=False):
    hi = x.astype(BF16)
    if not split:
        return _dot(hi, bd)
    lo = (x - hi.astype(F32)).astype(BF16)
    return _dot(hi, bd) + _dot(lo, bd)


def _rope(x, cos, sa, sb):
    n = x.shape[-1]
    return x * cos + pltpu.roll(x, n - ROT_HALF, 1) * sa + pltpu.roll(x, ROT_HALF, 1) * sb


def _rows_before(x, k, fills, *, seg_len, first_chunk_rows=None):
    rows = x.shape[0]
    y = pltpu.roll(x, k, 0)
    ridx = lax.broadcasted_iota(I32, (rows, 1), 0)
    t = ridx if seg_len is None else ridx % seg_len
    for r in range(k):
        y = jnp.where(t == r, fills[k - 1 - r], y)
    return y


ROW_BLOCK = 256


def _row_blocks(tm, seg_len=None, blk=ROW_BLOCK):
    if seg_len is not None or tm % blk:
        return [slice(0, tm)]
    return [slice(i, i + blk) for i in range(0, tm, blk)]


def _softplus(y):
    return jnp.maximum(y, 0.0) + jnp.log(1.0 + jnp.exp(-jnp.abs(y)))


def _ffn_kernel(*refs, chunks, ple):
    if ple:
        x_ref, g_ref, wg_ref, wu_ref, wd_ref, p_ref, pn_ref, pg_ref, pp_ref, o_ref = refs
    else:
        x_ref, g_ref, wg_ref, wu_ref, wd_ref, o_ref = refs
    x = x_ref[...]
    h = _rms(x, g_ref[...]).astype(BF16)
    acc = None
    for lo, hi in chunks:
        gt = _dot(h, wg_ref[:, lo:hi])
        ut = _dot(h, wu_ref[:, lo:hi])
        act = (gt * jax.nn.sigmoid(gt) * ut).astype(BF16)
        d = _dot(act, wd_ref[lo:hi, :])
        acc = d if acc is None else acc + d
    y = x + 0.5 * acc
    if ple:
        hg = _rms(y, pn_ref[...]).astype(BF16)
        gate = jax.nn.sigmoid(_dot(hg, pg_ref[...]))
        y = y + gate * _dot(p_ref[...].astype(BF16), pp_ref[...])
    o_ref[...] = y


def _ffn(x, g, wg, wu, wd, ple=None):
    m, d = x.shape
    f = wg.shape[1]
    tm = _tile(m, 512)
    step = 1024
    chunks = tuple((lo, min(lo + step, f)) for lo in range(0, f, step))
    row = lambda i: (i, 0)
    in_specs = [pl.BlockSpec((tm, d), row), _const_spec((1, d)), _const_spec((d, f)),
                _const_spec((d, f)), _const_spec((f, d))]
    args = [x, g.reshape(1, d), wg, wu, wd]
    if ple is not None:
        (p, layer), pn, pg, pp = ple
        off = layer * (m // tm)
        in_specs += [pl.BlockSpec((tm, p.shape[1]), lambda i: (i + off, 0)), _const_spec((1, d)),
                     _const_spec(pg.shape), _const_spec(pp.shape)]
        args += [p, pn.reshape(1, d), pg, pp]
    return pl.pallas_call(
        functools.partial(_ffn_kernel, chunks=chunks, ple=ple is not None),
        grid=(m // tm,), in_specs=in_specs, out_specs=pl.BlockSpec((tm, d), row),
        out_shape=jax.ShapeDtypeStruct((m, d), F32), compiler_params=_cparams(1),
        name="ffn_ple" if ple is not None else "ffn")(*args)


def _rwkv_pre_kernel(*refs, seg_len):
    (x_ref, g_ref, wa_ref, mu_ref, w0_ref, a0_ref, wwa_ref, g2_ref, kkw_ref, ka_ref,
     rk_ref, bd_ref) = refs[:12]
    rest = refs[12:]
    if seg_len is None:
        r_o, w_o, k_o, v_o, na_o, nb_o, g_o, bonus_o, shift_o, carry_ref = rest
    else:
        init_ref, r_o, w_o, k_o, v_o, na_o, nb_o, g_o, bonus_o, shift_o = rest
    tm = x_ref.shape[0]
    bd = bd_ref[...]
    if seg_len is None:
        @pl.when(pl.program_id(1) == 0)
        def _():
            carry_ref[...] = jnp.zeros_like(carry_ref)
        prev_row = carry_ref[0:1, :]
    for rows in _row_blocks(tm, seg_len):
        rb = rows.stop - rows.start
        h = _rms(x_ref[rows, :], g_ref[...]).astype(BF16)
        u = _dot(h, wa_ref[...])
        if seg_len is None:
            u_prev = _rows_before(u, 1, [prev_row], seg_len=None)
            prev_row = u[rb - 1:rb, :]
        else:
            u_prev = _rows_before(u, 1, [init_ref[rows, :]], seg_len=seg_len)
            shift_o[rows, :] = u
        us = u + (u_prev - u) * mu_ref[...]
        r = us[:, 0:A_DIM]
        k = us[:, A_DIM:2 * A_DIM]
        v = us[:, 2 * A_DIM:3 * A_DIM]
        o3 = 3 * A_DIM
        xwa = us[:, o3:o3 + LORA_W + LORA_A]
        lane = lax.broadcasted_iota(I32, xwa.shape, 1)
        xwa = jnp.where(lane < LORA_W, jnp.tanh(xwa), xwa)
        lo = _dot(xwa.astype(BF16), wwa_ref[...])
        w_log = -_softplus(-(w0_ref[...] + lo[:, :A_DIM])) - 0.5
        log_decay = -jnp.exp(w_log)
        a = jax.nn.sigmoid(a0_ref[...] + lo[:, A_DIM:])
        xg = us[:, o3 + LORA_W + LORA_A:]
        g = _dot(jax.nn.sigmoid(xg).astype(BF16), g2_ref[...])
        kk = k * kkw_ref[...]
        kk = kk / jnp.maximum(jnp.sqrt(_segsum(kk * kk, bd)), 1e-12)
        k2 = k * (1.0 + (a - 1.0) * ka_ref[...])
        seq_outs = zip((r_o, w_o, k_o, v_o, na_o, nb_o), (r, log_decay, k2, v, -kk, kk * a))
        for o_ref, val in seq_outs:
            if seg_len is None:
                o_ref[:, rows] = jnp.transpose(val)
            else:
                o_ref[rows, :] = val
        g_o[rows, :] = g
        bonus_o[rows, :] = _segsum(r * k2 * rk_ref[...], bd) * v
    if seg_len is None:
        carry_ref[0:1, :] = prev_row
        shift_o[0] = prev_row


def _rwkv_pre(x, nb, seq, lw, init_rows=None):
    m, d = x.shape
    consts = [lw["mix_norm"], lw["w_a"], lw["rwkv_mu"], lw["rwkv_w0"], lw["rwkv_a0"], lw["w_wa"],
              lw["rwkv_g2"], lw["rwkv_kk"], lw["rwkv_ka"], lw["rwkv_rk"], lw["bd"]]
    const_specs = [_const_spec(c.shape) for c in consts]
    outs = [jax.ShapeDtypeStruct((m, A_DIM), F32)] * 8
    if init_rows is None:
        tm = _tile(seq, 512)
        nc = seq // tm
        row = lambda b, c: (b * nc + c, 0)
        grid = (nb, nc)
        in_specs = [pl.BlockSpec((tm, d), row)] + const_specs
        out_specs = ([pl.BlockSpec((A_DIM, tm), lambda b, c: (b, c))] * 6 + [pl.BlockSpec((tm, A_DIM), row)] * 2
                     + [pl.BlockSpec((1, 1, RWKV_COLS), lambda b, c: (b, 0, 0))])
        outs = ([jax.ShapeDtypeStruct((nb * A_DIM, seq), F32)] * 6 + outs[6:]
                + [jax.ShapeDtypeStruct((nb, 1, RWKV_COLS), F32)])
        scratch = [pltpu.VMEM((8, RWKV_COLS), F32)]
        args = [x] + consts
        seg_len = None
    else:
        tm = m
        row = lambda i: (0, 0)
        grid = (1,)
        in_specs = [pl.BlockSpec((tm, d), row)] + const_specs + [pl.BlockSpec((tm, RWKV_COLS), row)]
        out_specs = [pl.BlockSpec((tm, A_DIM), row)] * 8 + [pl.BlockSpec((tm, RWKV_COLS), row)]
        outs = outs + [jax.ShapeDtypeStruct((m, RWKV_COLS), F32)]
        scratch = []
        args = [x] + consts + [init_rows]
        seg_len = seq
    return pl.pallas_call(
        functools.partial(_rwkv_pre_kernel, seg_len=seg_len), grid=grid, in_specs=in_specs,
        out_specs=out_specs, out_shape=outs, scratch_shapes=scratch,
        compiler_params=_cparams(len(grid)), name="rwkv_pre")(*args)


def _scan_kernel(r_ref, lw_ref, k_ref, v_ref, a_ref, b_ref, s0_ref, o_ref, st_ref, s_ref, vec_ref, *, tc):
    c = pl.program_id(1)

    @pl.when(c == 0)
    def _():
        s_ref[...] = s0_ref[...]

    jb = 32
    ln = s_ref.shape[-1]

    def step(t, carry):
        cum, gam_prev = carry
        cum = cum + lw_ref[t]
        gam = jnp.exp(cum)
        ginv = jnp.exp(-cum)
        vec_ref[0] = a_ref[t] * gam_prev
        vec_ref[1] = b_ref[t] * ginv
        vec_ref[2] = k_ref[t] * ginv
        vec_ref[3] = r_ref[t] * gam

        def sa_body(q, sa):
            for jj in range(jb):
                j = q * jb + jj
                sa = sa + s_ref[j] * vec_ref[0, pl.ds(j, 1), :]
            return sa

        sa = lax.fori_loop(0, HEAD_DIM // jb, sa_body, jnp.zeros((HEAD_DIM, ln), F32))
        vt = v_ref[t]

        def up_body(q, o):
            for jj in range(jb):
                j = q * jb + jj
                sn = s_ref[j] + sa * vec_ref[1, pl.ds(j, 1), :] + vt * vec_ref[2, pl.ds(j, 1), :]
                s_ref[j] = sn
                o = o + sn * vec_ref[3, pl.ds(j, 1), :]
            return o

        o_ref[t] = lax.fori_loop(0, HEAD_DIM // jb, up_body, jnp.zeros((HEAD_DIM, ln), F32))
        return cum, gam

    zeros = jnp.zeros((HEAD_DIM, ln), F32)
    _, gam_end = lax.fori_loop(0, tc, step, (zeros, zeros + 1.0))
    vec_ref[0] = gam_end
    for j in range(HEAD_DIM):
        s_ref[j] = s_ref[j] * vec_ref[0, j:j + 1, :]

    @pl.when(c == pl.num_programs(1) - 1)
    def _():
        st_ref[...] = s_ref[...]


def _scan(r, w, k, v, a, b, s0):
    t_len, hd, nbh = r.shape
    ln = min(LANES, nbh)
    tc = _tile(t_len, 64) if t_len % 8 == 0 else t_len
    grid = (nbh // ln, t_len // tc)
    seq_spec = pl.BlockSpec((tc, hd, ln), lambda l, c: (c, 0, l))
    st_spec = pl.BlockSpec((hd, hd, ln), lambda l, c: (0, 0, l))
    return pl.pallas_call(
        functools.partial(_scan_kernel, tc=tc), grid=grid,
        in_specs=[seq_spec] * 6 + [st_spec], out_specs=[seq_spec, st_spec],
        out_shape=[jax.ShapeDtypeStruct((t_len, hd, nbh), F32),
                   jax.ShapeDtypeStruct((hd, hd, nbh), F32)],
        scratch_shapes=[pltpu.VMEM((hd, hd, ln), F32), pltpu.VMEM((4, hd, ln), F32)],
        compiler_params=_cparams(2), name="rwkv_scan")(r, w, k, v, a, b, s0)


def _conv_kernel(*refs, seg_len):
    x_ref, g_ref, wb_ref, cw_ref = refs[:4]
    rest = refs[4:]
    if seg_len is None:
        o_ref, st_ref, carry_ref = rest
    else:
        i0_ref, i1_ref, o_ref, st_ref = rest
    tm = x_ref.shape[0]
    cw = cw_ref[...]
    if seg_len is None:
        @pl.when(pl.program_id(1) == 0)
        def _():
            carry_ref[...] = jnp.zeros_like(carry_ref)
        hist = [carry_ref[1:2, :], carry_ref[0:1, :]]
    for rows in _row_blocks(tm, seg_len, ROW_BLOCK // 2):
        rb = rows.stop - rows.start
        h = _rms(x_ref[rows, :], g_ref[...]).astype(BF16)
        u = _dot(h, wb_ref[...])
        bg = u[:, :B_DIM]
        z = u[:, B_DIM:2 * B_DIM] * u[:, 2 * B_DIM:]
        if seg_len is None:
            z1 = _rows_before(z, 1, hist[:1], seg_len=None)
            z2 = _rows_before(z, 2, hist, seg_len=None)
            hist = [z[rb - 1:rb, :], z[rb - 2:rb - 1, :]]
        else:
            init = [i1_ref[rows, :], i0_ref[rows, :]]
            z1 = _rows_before(z, 1, init[:1], seg_len=seg_len)
            z2 = _rows_before(z, 2, init, seg_len=seg_len)
            st_ref[rows, :] = z
        o_ref[rows, :] = bg * (z2 * cw[0:1, :] + z1 * cw[1:2, :] + z * cw[2:3, :])
    if seg_len is None:
        last2 = jnp.concatenate([hist[1], hist[0]], axis=0)
        carry_ref[0:2, :] = last2
        st_ref[0] = last2


def _conv(x, nb, seq, lw, init=None):
    m, d = x.shape
    assert seq >= CONV_W - 1
    consts = [lw["mix_norm"], lw["w_b"], lw["conv_w"]]
    const_specs = [_const_spec(c.shape) for c in consts]
    if init is None:
        tm = _tile(seq, 512)
        nc = seq // tm
        row = lambda b, c: (b * nc + c, 0)
        grid = (nb, nc)
        in_specs = [pl.BlockSpec((tm, d), row)] + const_specs
        out_specs = [pl.BlockSpec((tm, B_DIM), row),
                     pl.BlockSpec((1, CONV_W - 1, B_DIM), lambda b, c: (b, 0, 0))]
        outs = [jax.ShapeDtypeStruct((m, B_DIM), F32), jax.ShapeDtypeStruct((nb, CONV_W - 1, B_DIM), F32)]
        scratch = [pltpu.VMEM((8, B_DIM), F32)]
        args = [x] + consts
        seg_len = None
    else:
        tm = m
        row = lambda i: (0, 0)
        grid = (1,)
        in_specs = [pl.BlockSpec((tm, d), row)] + const_specs + [pl.BlockSpec((tm, B_DIM), row)] * 2
        out_specs = [pl.BlockSpec((tm, B_DIM), row)] * 2
        outs = [jax.ShapeDtypeStruct((m, B_DIM), F32)] * 2
        scratch = []
        args = [x] + consts + list(init)
        seg_len = seq
    return pl.pallas_call(
        functools.partial(_conv_kernel, seg_len=seg_len), grid=grid, in_specs=in_specs,
        out_specs=out_specs, out_shape=outs, scratch_shapes=scratch,
        compiler_params=_cparams(len(grid)), name="shortconv")(*args)


def _attn_proj_kernel(x_ref, g_ref, wc_ref, qn_ref, kn_ref, in_ref, bd_ref, cos_ref, sa_ref, sb_ref,
                      *outs, transposed):
    bd = bd_ref[...]
    inv_hd = 1.0 / HEAD_DIM
    if transposed:
        q_o, k_o, qi_o, kw_o, kt_o, vt_o, kit_o = outs
    else:
        q_o, k_o, qi_o, kw_o, v_o, ki_o = outs
    for rows in _row_blocks(x_ref.shape[0], None if transposed else 1):
        h = _rms(x_ref[rows, :], g_ref[...]).astype(BF16)
        u = _dot(h, wc_ref[...])
        cos, sa, sb = cos_ref[rows, :], sa_ref[rows, :], sb_ref[rows, :]
        q = u[:, :C_DIM]
        q = q * lax.rsqrt(_segsum(q * q, bd) * inv_hd + RMS_EPS) * qn_ref[...]
        q = _rope(q, cos, sa, sb)
        k = u[:, C_DIM:C_DIM + C_KV_DIM]
        k = k * lax.rsqrt(_segsum(k * k, bd[:C_KV_DIM, :C_KV_DIM]) * inv_hd + RMS_EPS) * kn_ref[...]
        k = _rope(k, cos[:, :C_KV_DIM], sa[:, :C_KV_DIM], sb[:, :C_KV_DIM])
        o = C_DIM + C_KV_DIM
        v = u[:, o:o + C_KV_DIM]
        o += C_KV_DIM
        qi = _rope(u[:, o:o + IDX_HEADS * IDX_DIM], cos, sa, sb)
        o += IDX_HEADS * IDX_DIM
        kw = u[:, o:o + LANES]
        lane = lax.broadcasted_iota(I32, kw.shape, 1)
        is_ki = lane < IDX_DIM
        ms = jnp.sum(jnp.where(is_ki, kw * kw, 0.0), axis=-1, keepdims=True) * (1.0 / IDX_DIM)
        kin = _rope(kw * lax.rsqrt(ms + RMS_EPS) * in_ref[...], cos[:, :LANES], sa[:, :LANES], sb[:, :LANES])
        kw = jnp.where(is_ki, kin, kw)
        if transposed:
            kt_o[0, :, rows] = jnp.transpose(k)
            vt_o[0, :, rows] = jnp.transpose(v)
            kit_o[0, :, rows] = jnp.transpose(kw)[:IDX_DIM, :]
        else:
            v_o[rows, :] = v
            ki_o[rows, :] = kw[:, :IDX_DIM]
        q_o[rows, :] = q
        k_o[rows, :] = k
        qi_o[rows, :] = qi
        kw_o[rows, :] = kw


def _attn_proj(x, nb, seq, lw, tabs, sample):
    m, d = x.shape
    consts = [lw["mix_norm"], lw["w_c"], lw["q_norm"], lw["k_norm"], lw["idx_k_norm"], lw["bd"]]
    const_specs = [_const_spec(c.shape) for c in consts]
    if sample:
        tm, nc = m, 1
        grid = (1, 1)
    else:
        tm = _tile(seq, 512)
        nc = seq // tm
        grid = (nb, nc)
    row = lambda b, c: (b * nc + c, 0)
    tab = lambda b, c: (c, 0)
    widths = [C_DIM, C_KV_DIM, IDX_HEADS * IDX_DIM, LANES]
    out_specs = [pl.BlockSpec((tm, w), row) for w in widths]
    out_shape = [jax.ShapeDtypeStruct((m, w), F32) for w in widths]
    if sample:
        for w in (C_KV_DIM, IDX_DIM):
            out_specs.append(pl.BlockSpec((tm, w), row))
            out_shape.append(jax.ShapeDtypeStruct((m, w), F32))
    else:
        for w in (C_KV_DIM, C_KV_DIM, IDX_DIM):
            out_specs.append(pl.BlockSpec((1, w, tm), lambda b, c: (b, 0, c)))
            out_shape.append(jax.ShapeDtypeStruct((nb, w, seq), F32))
    return pl.pallas_call(
        functools.partial(_attn_proj_kernel, transposed=not sample), grid=grid,
        in_specs=[pl.BlockSpec((tm, d), row)] + const_specs + [pl.BlockSpec((tm, C_DIM), tab)] * 3,
        out_specs=out_specs, out_shape=out_shape,
        compiler_params=_cparams(2), name="attn_proj")(x, *consts, *tabs)


def _score_keys(score):
    bits = pltpu.bitcast(score, I32)
    return jnp.where(bits < 0, bits ^ 0x7FFFFFFF, bits)


def _lane_total(acc):
    return jnp.broadcast_to(jnp.sum(acc, axis=-1, keepdims=True), acc.shape)


def _tree(parts, op):
    while len(parts) > 1:
        nxt = [op(parts[i], parts[i + 1]) for i in range(0, len(parts) - 1, 2)]
        parts = nxt + (parts[-1:] if len(parts) % 2 else [])
    return parts[0]


def _fold8(x, op):
    return _tree([x[i * 8:(i + 1) * 8] for i in range(x.shape[0] // 8)], op)


def _kth_key(count_fn, topk, shape):
    def vbit(it, acc):
        cand = acc | jnp.left_shift(jnp.int32(1), 31 - it)
        cmp = cand ^ INT_MIN
        cnt = count_fn(lambda kk, pos: jnp.where(kk >= cmp, 1, 0))
        return jnp.where(cnt >= topk, cand, acc)

    return lax.fori_loop(0, 32, vbit, jnp.zeros(shape, I32)) ^ INT_MIN


def _select_topk(count_fn, topk, pos_bits, shape, lp_ref):
    zeros = jnp.zeros(shape, I32)
    thr = _kth_key(count_fn, topk, shape)
    n_ge = count_fn(lambda kk, pos: jnp.where(kk >= thr, 1, 0))
    excess = jnp.where(thr == INT_MIN, 0, n_ge - topk)
    lp_ref[...] = jnp.full(shape, 2 ** 31 - 1, I32)

    @pl.when(jnp.max(excess) > 0)
    def _():
        need = topk - count_fn(lambda kk, pos: jnp.where(kk > thr, 1, 0))

        def pbit(it, acc):
            cand = acc | jnp.left_shift(jnp.int32(1), pos_bits - 1 - it)
            cnt = count_fn(lambda kk, pos: jnp.where(kk == thr, jnp.where(pos < cand, 1, 0), 0))
            return jnp.where(cnt < need, cand, acc)

        lp_ref[...] = lax.fori_loop(0, pos_bits, pbit, zeros)

    return thr, lp_ref[...]


def _select_bias(kk, pos, thr, last_pos):
    tie = jnp.where(pos <= last_pos, 0.0, NEG_BIG)
    bias = jnp.where(kk == thr, tie, jnp.where(kk > thr, 0.0, NEG_BIG))
    return jnp.where(kk == INT_MIN, NEG_BIG, bias)


def _pattn_kernel(qi_ref, kwq_ref, q_ref, kwk_ref, k_ref, vt_ref, o_ref,
                  keys_ref, s_ref, qip_ref, qg_ref, acc_ref, m_ref, l_ref, ties_ref, *, tq, topk):
    qb = pl.program_id(1)
    nch = qb + 1
    nslab = tq // 8
    shape8 = (8, tq)
    sub8 = lax.broadcasted_iota(I32, shape8, 0)
    lane_q = lax.broadcasted_iota(I32, (tq, LANES), 1)
    key_in = lax.broadcasted_iota(I32, (tq, tq), 0)
    qry_in = lax.broadcasted_iota(I32, (tq, tq), 1)

    wt = jnp.transpose(kwq_ref[...]) * ((IDX_DIM ** -0.5) * (IDX_HEADS ** -0.5))
    for hh in range(IDX_HEADS):
        slab = qi_ref[:, (hh // 2) * LANES:(hh // 2 + 1) * LANES]
        if hh % 2:
            slab = pltpu.roll(slab, IDX_DIM, 1)
        qip_ref[hh] = jnp.where(lane_q < IDX_DIM, slab, 0.0).astype(BF16)

    def for_chunks(body):
        odd = nch % 2

        @pl.when(odd == 1)
        def _():
            body(0)

        def pair(p, carry):
            kc = odd + 2 * p
            body(kc)
            body(kc + 1)
            return carry

        lax.fori_loop(0, nch // 2, pair, 0)

    def idx_body(kc):
        start = pl.multiple_of(kc * tq, tq)
        ks = kwk_ref[pl.ds(start, tq), :].astype(BF16)
        acc = None
        for hh in range(IDX_HEADS):
            s = _dot_nt(ks, qip_ref[hh])
            term = jnp.maximum(s, 0.0) * wt[IDX_DIM + hh:IDX_DIM + hh + 1, :]
            acc = term if acc is None else acc + term
        kk = jnp.where(acc == 0.0, 0, _score_keys(acc))
        keys_ref[kc] = jnp.where(kc * tq + key_in <= qb * tq + qry_in, kk, INT_MIN)

    for_chunks(idx_body)

    def count_fn(ind):
        def body(kc, acc):
            base = kc * tq
            parts = [ind(keys_ref[kc, i * 8:(i + 1) * 8, :], base + i * 8 + sub8) for i in range(nslab)]
            return acc + _tree(parts, jnp.add)
        acc = lax.fori_loop(0, nch, body, jnp.zeros(shape8, I32))
        return jnp.broadcast_to(jnp.sum(acc, axis=0, keepdims=True), shape8)

    thr = _kth_key(count_fn, topk, shape8)
    need = topk - count_fn(lambda kk, pos: jnp.where(kk > thr, 1, 0))

    def chunk_bias(kc, ties_before):
        slabs = []
        for i in range(nslab):
            kk = keys_ref[kc, i * 8:(i + 1) * 8, :]
            tie = jnp.where(kk == thr, 1, 0)
            incl = tie
            for sh in (1, 2, 4):
                incl = incl + jnp.where(sub8 >= sh, pltpu.roll(incl, sh, 0), 0)
            slabs.append((kk, incl - tie, jnp.broadcast_to(incl[7:8, :], shape8)))
        out = []
        for kk, excl, total in slabs:
            rank = ties_before + excl
            bias = jnp.where(kk > thr, 0.0, NEG_BIG)
            bias = jnp.where(kk == thr, jnp.where(rank < need, 0.0, NEG_BIG), bias)
            out.append(jnp.where(kk == INT_MIN, NEG_BIG, bias))
            ties_before = ties_before + total
        return jnp.concatenate(out, axis=0), ties_before

    scale = (HEAD_DIM ** -0.5) * 1.4426950408889634
    for g in range(C_KV_HEADS):
        qslab = q_ref[:, g * LANES:(g + 1) * LANES] * scale
        rolled = pltpu.roll(qslab, HEAD_DIM, 1)
        koff = g % 2
        in_half = (lane_q >= HEAD_DIM) if koff else (lane_q < HEAD_DIM)
        qg_ref[g] = jnp.concatenate(
            [jnp.where(in_half, qslab if r == koff else rolled, 0.0) for r in range(2)],
            axis=0).astype(BF16)

    m_ref[...] = jnp.full(m_ref.shape, NEG_BIG, F32)
    ties_ref[...] = jnp.zeros(shape8, I32)

    def max_body(kc):
        start = pl.multiple_of(kc * tq, tq)
        bias, ties = chunk_bias(kc, ties_ref[...])
        ties_ref[...] = ties
        bias2 = jnp.concatenate([bias, bias], axis=1)
        for g in range(C_KV_HEADS):
            kcol = (g // 2) * LANES
            kch = k_ref[pl.ds(start, tq), kcol:kcol + LANES].astype(BF16)
            s = _dot_nt(kch, qg_ref[g]) + bias2
            s_ref[kc, :, g * 2 * tq:(g + 1) * 2 * tq] = s
            m_ref[g] = jnp.maximum(m_ref[g], _fold8(s, jnp.maximum))

    for_chunks(max_body)
    m_row = [jnp.max(m_ref[g], axis=0, keepdims=True) for g in range(C_KV_HEADS)]

    acc_ref[...] = jnp.zeros_like(acc_ref)
    l_ref[...] = jnp.zeros_like(l_ref)

    def sum_body(kc):
        start = pl.multiple_of(kc * tq, tq)
        for g in range(C_KV_HEADS):
            p = jnp.exp2(s_ref[kc, :, g * 2 * tq:(g + 1) * 2 * tq] - m_row[g])
            vt = vt_ref[0, g * HEAD_DIM:(g + 1) * HEAD_DIM, pl.ds(start, tq)].astype(BF16)
            acc_ref[g] += _dot(vt, p.astype(BF16))
            l_ref[g] += _fold8(p, jnp.add)

    for_chunks(sum_body)
    for g in range(C_KV_HEADS):
        out_t = acc_ref[g] / jnp.sum(l_ref[g], axis=0, keepdims=True)
        o_ref[:, g * LANES:(g + 1) * LANES] = jnp.transpose(
            jnp.concatenate([out_t[:, :tq], out_t[:, tq:]], axis=0))


def _prompt_attention(q, k, vt, qi, kw, nb, seq):
    assert C_HEADS == 2 * C_KV_HEADS
    topk = min(TOPK_MAX, seq // 4)
    tq = _tile(seq, 256)
    assert tq % LANES == 0
    nq = seq // tq
    qrow = lambda b, c: (b * nq + c, 0)
    krow = lambda b, c: (b, 0)
    return pl.pallas_call(
        functools.partial(_pattn_kernel, tq=tq, topk=topk),
        grid=(nb, nq),
        in_specs=[pl.BlockSpec((tq, IDX_HEADS * IDX_DIM), qrow), pl.BlockSpec((tq, LANES), qrow),
                  pl.BlockSpec((tq, C_DIM), qrow), pl.BlockSpec((seq, LANES), krow),
                  pl.BlockSpec((seq, C_KV_DIM), krow),
                  pl.BlockSpec((1, C_KV_DIM, seq), lambda b, c: (b, 0, 0))],
        out_specs=pl.BlockSpec((tq, C_DIM), qrow),
        out_shape=jax.ShapeDtypeStruct((nb * seq, C_DIM), F32),
        scratch_shapes=[pltpu.VMEM((nq, tq, tq), I32),
                        pltpu.VMEM((nq, tq, C_KV_HEADS * 2 * tq), F32),
                        pltpu.VMEM((IDX_HEADS, tq, LANES), BF16),
                        pltpu.VMEM((C_KV_HEADS, 2 * tq, LANES), BF16),
                        pltpu.VMEM((C_KV_HEADS, HEAD_DIM, 2 * tq), F32),
                        pltpu.VMEM((C_KV_HEADS, 8, 2 * tq), F32),
                        pltpu.VMEM((C_KV_HEADS, 8, 2 * tq), F32), pltpu.VMEM((8, tq), I32)],
        compiler_params=_cparams(2), name="prompt_attn")(qi, kw, q, kw, k, vt)


SROWS = 8


def _sidx_kernel(pt_ref, qh_ref, wh_ref, kn_ref, *rest, npg, page):
    pages = rest[:npg]
    past_o, new_o = rest[npg:]
    qh = qh_ref[0].astype(BF16)
    wh = wh_ref[0] * ((IDX_DIM ** -0.5) * (IDX_HEADS ** -0.5))

    n_out = past_o.shape[1]

    def head_sum(s):
        s = jnp.maximum(s, 0.0) * wh
        return _tree([s[hh * SROWS:(hh + 1) * SROWS] for hh in range(IDX_HEADS)], jnp.add)[:n_out]

    keys_t = jnp.concatenate([pg[0] for pg in pages], axis=1).astype(BF16)
    past_o[0] = head_sum(_dot(qh, keys_t))

    @pl.when(pl.program_id(1) == 0)
    def _():
        new_o[0] = head_sum(_dot_nt(qh, kn_ref[0].astype(BF16)))


def _sample_index_scores(cache_kidx, page_table, layer, qh, wh, ki_new, seq):
    depth, n_phys, page, _ = cache_kidx.shape
    db, n_pages = page_table.shape
    npg = _pages_per_step(n_pages, 4 * PAGES_PER_STEP)
    cache = cache_kidx.transpose(0, 1, 3, 2).reshape(depth * n_phys, IDX_DIM, page)
    base = layer * n_phys

    def page_spec(i):
        return pl.BlockSpec((1, IDX_DIM, page), lambda b, s, pt: (base + pt[b, s * npg + i], 0, 0))

    per_b = lambda b, s, pt: (b, 0, 0)
    grid_spec = pltpu.PrefetchScalarGridSpec(
        num_scalar_prefetch=1, grid=(db, n_pages // npg),
        in_specs=[pl.BlockSpec((1, IDX_HEADS * SROWS, IDX_DIM), per_b),
                  pl.BlockSpec((1, IDX_HEADS * SROWS, 1), per_b),
                  pl.BlockSpec((1, LANES, IDX_DIM), per_b)] + [page_spec(i) for i in range(npg)],
        out_specs=[pl.BlockSpec((1, seq, npg * page), lambda b, s, pt: (b, 0, s)),
                   pl.BlockSpec((1, seq, LANES), per_b)])
    return pl.pallas_call(
        functools.partial(_sidx_kernel, npg=npg, page=page), grid_spec=grid_spec,
        out_shape=[jax.ShapeDtypeStruct((db, seq, n_pages * page), F32),
                   jax.ShapeDtypeStruct((db, seq, LANES), F32)],
        compiler_params=_cparams(2), name="sample_index")(page_table, qh, wh, ki_new, *([cache] * npg))


def _ssel_kernel(past_ref, new_ref, bpast_o, bnew_o, keys_ref, lp_ref, *, topk, pos_bits, n_new):
    rows, past = past_ref.shape
    nch = past // LANES
    lane = lax.broadcasted_iota(I32, (rows, LANES), 1)
    t_row = lax.broadcasted_iota(I32, (rows, LANES), 0) % n_new

    def fill(c, carry):
        start = pl.multiple_of(c * LANES, LANES)
        keys_ref[c] = _score_keys(past_ref[:, pl.ds(start, LANES)])
        return carry

    lax.fori_loop(0, nch, fill, 0)
    new_ok = lane <= jnp.minimum(t_row, n_new - 1)
    keys_ref[nch] = jnp.where(new_ok, _score_keys(new_ref[...]), INT_MIN)

    grp = 8
    assert nch % grp == 0

    def count_fn(pred):
        def body(q, acc):
            c0 = q * grp
            return acc + _tree([pred(keys_ref[c0 + i], (c0 + i) * LANES + lane) for i in range(grp)], jnp.add)
        acc = lax.fori_loop(0, nch // grp, body, jnp.zeros((rows, LANES), I32))
        return _lane_total(acc + pred(keys_ref[nch], nch * LANES + lane))

    thr, last_pos = _select_topk(count_fn, topk, pos_bits, (rows, LANES), lp_ref)

    def emit(c, carry):
        start = pl.multiple_of(c * LANES, LANES)
        bpast_o[:, pl.ds(start, LANES)] = _select_bias(keys_ref[c], c * LANES + lane, thr, last_pos)
        return carry

    lax.fori_loop(0, nch, emit, 0)
    bnew_o[...] = _select_bias(keys_ref[nch], nch * LANES + lane, thr, last_pos)


def _sample_select(sc_past, sc_new, n_new):
    rows, past = sc_past.shape
    topk = min(TOPK_MAX, (past + n_new) // 4)
    tr = _tile(rows, 64)
    assert tr % n_new == 0
    pos_bits = int(past + LANES - 1).bit_length()
    row = lambda i: (i, 0)
    return pl.pallas_call(
        functools.partial(_ssel_kernel, topk=topk, pos_bits=pos_bits, n_new=n_new),
        grid=(rows // tr,),
        in_specs=[pl.BlockSpec((tr, past), row), pl.BlockSpec((tr, LANES), row)],
        out_specs=[pl.BlockSpec((tr, past), row), pl.BlockSpec((tr, LANES), row)],
        out_shape=[jax.ShapeDtypeStruct((rows, past), F32), jax.ShapeDtypeStruct((rows, LANES), F32)],
        scratch_shapes=[pltpu.VMEM((past // LANES + 1, tr, LANES), I32), pltpu.VMEM((tr, LANES), I32)],
        compiler_params=_cparams(1), name="sample_select")(sc_past, sc_new)


def _satt_kernel(pt_ref, qbd_ref, bpast_ref, bnew_ref, kn_ref, vn_ref, *rest, npg, page):
    kpages = rest[:npg]
    vpages = rest[npg:2 * npg]
    o_ref, m_ref, l_ref, acc_ref = rest[2 * npg:]
    s_id = pl.program_id(1)
    nrow = qbd_ref.shape[1]
    reps = nrow // SROWS

    @pl.when(s_id == 0)
    def _():
        m_ref[...] = jnp.full_like(m_ref, NEG_BIG)
        l_ref[...] = jnp.zeros_like(l_ref)
        acc_ref[...] = jnp.zeros_like(acc_ref)

    qbd = qbd_ref[0].astype(BF16)

    def update(s, bias, pv):
        s = s + jnp.concatenate([bias] * reps, axis=0)
        m_old = m_ref[...]
        m_new = jnp.maximum(m_old, jnp.max(s, axis=-1, keepdims=True))
        p = jnp.exp(s - m_new)
        alpha = jnp.exp(m_old - m_new)
        l_ref[...] = alpha * l_ref[...] + jnp.sum(p, axis=-1, keepdims=True)
        acc_ref[...] = alpha * acc_ref[...] + pv(p.astype(BF16))
        m_ref[...] = m_new

    keys_t = jnp.concatenate([kp[0] for kp in kpages], axis=1).astype(BF16)
    vals_t = jnp.concatenate([vp[0] for vp in vpages], axis=1).astype(BF16)
    update(_dot(qbd, keys_t), bpast_ref[0], lambda p: _dot_nt(p, vals_t))

    @pl.when(s_id == pl.num_programs(1) - 1)
    def _():
        vn = vn_ref[0].astype(BF16)
        update(_dot_nt(qbd, kn_ref[0].astype(BF16)), bnew_ref[0], lambda p: _dot(p, vn))
        o_ref[0] = acc_ref[...] / l_ref[...]


def _sample_attend(cache_k, cache_v, page_table, layer, qbd, bias_past, bias_new, k_new, v_new):
    depth, n_phys, page = cache_k.shape[:3]
    db, n_pages = page_table.shape
    npg = _pages_per_step(n_pages)
    ck = cache_k.transpose(0, 1, 3, 4, 2).reshape(depth * n_phys, C_KV_DIM, page)
    cv = cache_v.transpose(0, 1, 3, 4, 2).reshape(depth * n_phys, C_KV_DIM, page)
    base = layer * n_phys
    nrow = qbd.shape[1]

    def page_spec(i):
        return pl.BlockSpec((1, C_KV_DIM, page), lambda b, s, pt: (base + pt[b, s * npg + i], 0, 0))

    per_b = lambda b, s, pt: (b, 0, 0)
    grid_spec = pltpu.PrefetchScalarGridSpec(
        num_scalar_prefetch=1, grid=(db, n_pages // npg),
        in_specs=[pl.BlockSpec((1, nrow, C_KV_DIM), per_b),
                  pl.BlockSpec((1, SROWS, npg * page), lambda b, s, pt: (b, 0, s)),
                  pl.BlockSpec((1, SROWS, LANES), per_b),
                  pl.BlockSpec((1, LANES, C_KV_DIM), per_b),
                  pl.BlockSpec((1, LANES, C_KV_DIM), per_b)]
                 + [page_spec(i) for i in range(npg)] * 2,
        out_specs=pl.BlockSpec((1, nrow, C_KV_DIM), per_b),
        scratch_shapes=[pltpu.VMEM((nrow, 1), F32), pltpu.VMEM((nrow, 1), F32),
                        pltpu.VMEM((nrow, C_KV_DIM), F32)])
    return pl.pallas_call(
        functools.partial(_satt_kernel, npg=npg, page=page), grid_spec=grid_spec,
        out_shape=jax.ShapeDtypeStruct((db, nrow, C_KV_DIM), F32),
        compiler_params=_cparams(2), name="sample_attn")(
            page_table, qbd, bias_past, bias_new, k_new, v_new, *([ck] * npg), *([cv] * npg))


def _merge_kernel(x_ref, g_ref, wg_ref, os_ref, bon_ref, gg_ref, lw_ref, lb_ref, bd_ref,
                  ob_ref, oc_ref, wpa_ref, wpb_ref, wpc_ref, wo_ref, o_ref):
    x = x_ref[...]
    d = x.shape[1]
    h = _rms(x, g_ref[...]).astype(BF16)
    gates = jax.nn.sigmoid(_dot(h, wg_ref[...]))
    bd = bd_ref[...]
    o = os_ref[...]
    mean = _segsum(o, bd, split=True) * (1.0 / HEAD_DIM)
    cen = o - mean
    var = _segsum(cen * cen, bd) * (1.0 / HEAD_DIM)
    on = cen * lax.rsqrt(var + GN_EPS) * lw_ref[...] + lb_ref[...]
    oa = ((on + bon_ref[...]) * gg_ref[...]).astype(BF16)
    merged = (gates[:, :d] * _dot(oa, wpa_ref[...])
              + gates[:, d:2 * d] * _dot(ob_ref[...].astype(BF16), wpb_ref[...])
              + gates[:, 2 * d:] * _dot(oc_ref[...].astype(BF16), wpc_ref[...]))
    o_ref[...] = x + _dot(merged.astype(BF16), wo_ref[...])


def _merge(x, o_scan, bonus, g, o_b, o_c, lw, nb, seq):
    m, d = x.shape
    tm = _tile(seq, 512)
    nc = seq // tm
    row = lambda b, c: (b * nc + c, 0)
    tok = lambda w: pl.BlockSpec((tm, w), row)
    consts_a = [lw["mix_norm"], lw["w_g"]]
    consts_b = [lw["rwkv_lnx_w"], lw["rwkv_lnx_b"], lw["bd"]]
    consts_c = [lw["w_pa"], lw["w_pb"], lw["w_pc"], lw["w_out"]]
    in_specs = ([tok(d)] + [_const_spec(c.shape) for c in consts_a]
                + [pl.BlockSpec((tm, A_DIM), lambda b, c: (c, b))] + [tok(A_DIM)] * 2
                + [_const_spec(c.shape) for c in consts_b] + [tok(B_DIM), tok(C_DIM)]
                + [_const_spec(c.shape) for c in consts_c])
    return pl.pallas_call(
        _merge_kernel, grid=(nb, nc), in_specs=in_specs, out_specs=tok(d),
        out_shape=jax.ShapeDtypeStruct((m, d), F32), compiler_params=_cparams(2), name="merge")(
            x, *consts_a, o_scan, bonus, g, *consts_b, o_b, o_c, *consts_c)


def _rope_tables(pos):
    inv = ROPE_THETA ** (-jnp.arange(ROT_HALF, dtype=F32) / ROT_HALF)
    ang = pos.astype(F32)[:, None] * inv[None, :]
    c, s = jnp.cos(ang), jnp.sin(ang)
    t = pos.shape[0]
    pad = jnp.zeros((t, HEAD_DIM - ROT_DIM), F32)
    zer = jnp.zeros((t, ROT_HALF), F32)
    cos = jnp.concatenate([c, c, pad + 1.0], axis=1)
    sa = jnp.concatenate([-s, zer, pad], axis=1)
    sb = jnp.concatenate([zer, s, pad], axis=1)
    return tuple(jnp.tile(z, (1, C_HEADS)) for z in (cos, sa, sb))


def _layer_weights(i, p):
    d = p["w_in"].shape[1]
    w_in = p["w_in"][i]
    o1 = RWKV_COLS
    o2 = o1 + CONV_COLS
    o3 = o2 + ATTN_COLS
    row = lambda v: v.reshape(1, -1)
    tile_h = lambda v, n: jnp.tile(v, n).reshape(1, -1)
    z = jnp.zeros((LORA_W, A_DIM), F32)
    w_wa = jnp.concatenate([jnp.concatenate([p["rwkv_w2"][i], z], axis=1),
                            jnp.concatenate([z, p["rwkv_a2"][i]], axis=1)], axis=0)
    head = np.arange(A_DIM) // HEAD_DIM
    bd = jnp.asarray(head[:, None] == head[None, :], BF16)
    idx_norm = jnp.concatenate([p["idx_k_norm"][i], jnp.zeros((LANES - IDX_DIM,), F32)])
    bf = lambda w: w.astype(BF16)
    return dict(
        ffn1_norm=p["ffn1_norm"][i], ffn1_wg=bf(p["ffn1_wg"][i]), ffn1_wu=bf(p["ffn1_wu"][i]),
        ffn1_wd=bf(p["ffn1_wd"][i]),
        ffn2_norm=p["ffn2_norm"][i], ffn2_wg=bf(p["ffn2_wg"][i]), ffn2_wu=bf(p["ffn2_wu"][i]),
        ffn2_wd=bf(p["ffn2_wd"][i]),
        mix_norm=row(p["mix_norm"][i]),
        w_a=bf(w_in[:, :o1]), w_b=bf(w_in[:, o1:o2]),
        w_c=bf(jnp.pad(w_in[:, o2:o3], ((0, 0), (0, ATTN_COLS_PAD - ATTN_COLS)))),
        w_g=bf(w_in[:, o3:]),
        rwkv_mu=row(p["rwkv_mu"][i]), rwkv_w0=row(p["rwkv_w0"][i]), rwkv_a0=row(p["rwkv_a0"][i]),
        w_wa=bf(w_wa), rwkv_g2=bf(p["rwkv_g2"][i]), rwkv_kk=row(p["rwkv_kk"][i]),
        rwkv_ka=row(p["rwkv_ka"][i]), rwkv_rk=row(p["rwkv_rk"][i]),
        rwkv_lnx_w=row(p["rwkv_lnx_w"][i]), rwkv_lnx_b=row(p["rwkv_lnx_b"][i]),
        conv_w=p["conv_w"][i], bd=bd,
        q_norm=tile_h(p["q_norm"][i], C_HEADS), k_norm=tile_h(p["k_norm"][i], C_KV_HEADS),
        idx_k_norm=row(idx_norm),
        w_pa=bf(p["w_pa"][i]), w_pb=bf(p["w_pb"][i]), w_pc=bf(p["w_pc"][i]), w_out=bf(p["w_out"][i]),
        ple_norm=p["ple_norm"][i], ple_gate=bf(p["ple_gate"][i]), ple_proj=bf(p["ple_proj"][i]),
    )


def _rwkv_branch(x, nb, seq, lw, wkv0, shift_rows, between=None):
    pre = _rwkv_pre(x, nb, seq, lw, shift_rows)
    s0 = wkv0.transpose(3, 2, 0, 1).reshape(HEAD_DIM, HEAD_DIM, nb * A_HEADS)
    other = None
    if between is not None:
        x, pre = lax.optimization_barrier((x, pre))
        other = between(x)
    r, w, k, v, na, nb_, g, bonus, shift_o = pre
    nbh = nb * A_HEADS
    if shift_rows is None:
        ts = lambda z: z.reshape(nbh, HEAD_DIM, seq).transpose(2, 1, 0)
    else:
        ts = lambda z: z.reshape(nb, seq, A_HEADS, HEAD_DIM).transpose(1, 3, 0, 2).reshape(seq, HEAD_DIM, nbh)
    seqs = [ts(z) for z in (r, w, k, v, na, nb_)]
    if between is not None:
        s0, other = lax.optimization_barrier((s0, other))
    o, s_fin = _scan(*seqs, s0)
    wkv1 = s_fin.reshape(HEAD_DIM, HEAD_DIM, nb, A_HEADS).transpose(2, 3, 1, 0)
    if shift_rows is None:
        shift1 = shift_o.reshape(nb, RWKV_COLS)
        o_tok = o.transpose(0, 2, 1).reshape(seq, nb * A_DIM)
    else:
        shift1 = shift_o.reshape(nb, seq, RWKV_COLS)[:, -1]
        o_tok = o.reshape(seq, HEAD_DIM, nb, A_HEADS).transpose(2, 0, 3, 1).reshape(nb * seq, A_DIM)
    return o_tok, bonus, g, shift1, wkv1, other


def _layer_tail(x1, o_scan, bonus, g, o_b, o_c, p_emb, lw, nb, seq):
    x2 = _merge(x1, o_scan, bonus, g, o_b, o_c, lw, nb, seq)
    return _ffn(x2, lw["ffn2_norm"], lw["ffn2_wg"], lw["ffn2_wu"], lw["ffn2_wd"],
                ple=(p_emb, lw["ple_norm"], lw["ple_gate"], lw["ple_proj"]))


def _prompt_layer(x, p_emb, nb, seq, lw, tabs):
    x1 = _ffn(x, lw["ffn1_norm"], lw["ffn1_wg"], lw["ffn1_wu"], lw["ffn1_wd"])
    wkv0 = jnp.zeros((nb, A_HEADS, HEAD_DIM, HEAD_DIM), F32)

    def other_mixers(xin):
        o_b, conv1 = _conv(xin, nb, seq, lw)
        q, k, qi, kw, kt, vt, kit = _attn_proj(xin, nb, seq, lw, tabs, False)
        return o_b, conv1, kt, vt, kit, _prompt_attention(q, k, vt, qi, kw, nb, seq)

    o_scan, bonus, g, shift1, wkv1, other = _rwkv_branch(x1, nb, seq, lw, wkv0, None, other_mixers)
    o_b, conv1, kt, vt, kit, o_c = other
    x4 = _layer_tail(x1, o_scan, bonus, g, o_b, o_c, p_emb, lw, nb, seq)
    heads = lambda z: z.reshape(nb, C_KV_HEADS, HEAD_DIM, seq).transpose(0, 3, 1, 2)
    st = (heads(kt), heads(vt), kit.transpose(0, 2, 1), wkv1, shift1, conv1)
    return x4, st


def _sample_layer(x, p_emb, nb, seq, lw, tabs, layer, cache_k, cache_v, cache_kidx, page_table,
                  wkv0, shift0, conv0):
    assert seq <= SROWS
    x1 = _ffn(x, lw["ffn1_norm"], lw["ffn1_wg"], lw["ffn1_wu"], lw["ffn1_wd"])
    rep_rows = lambda z: jnp.repeat(z, seq, axis=0)
    o_scan, bonus, g, shift1, wkv1, _ = _rwkv_branch(x1, nb, seq, lw, wkv0, rep_rows(shift0))
    o_b, z_all = _conv(x1, nb, seq, lw, init=(rep_rows(conv0[:, 0]), rep_rows(conv0[:, 1])))
    conv1 = z_all.reshape(nb, seq, B_DIM)[:, seq - (CONV_W - 1):]
    q, k, qi, kw, v, ki = _attn_proj(x1, nb, seq, lw, tabs, True)

    pad_t = lambda z: jnp.pad(z, ((0, 0), (0, SROWS - seq)) + ((0, 0),) * (z.ndim - 2))
    qh = pad_t(qi.reshape(nb, seq, IDX_HEADS, IDX_DIM)).transpose(0, 2, 1, 3)
    qh = qh.reshape(nb, IDX_HEADS * SROWS, IDX_DIM)
    wi = kw[:, IDX_DIM:IDX_DIM + IDX_HEADS].reshape(nb, seq, IDX_HEADS)
    wh = pad_t(wi).transpose(0, 2, 1).reshape(nb, IDX_HEADS * SROWS, 1)
    pad_keys = lambda z: jnp.pad(z.reshape(nb, seq, -1), ((0, 0), (0, LANES - seq), (0, 0)))
    sc_past, sc_new = _sample_index_scores(cache_kidx, page_table, layer, qh, wh, pad_keys(ki), seq)
    past = sc_past.shape[-1]
    b_past, b_new = _sample_select(sc_past.reshape(nb * seq, past), sc_new.reshape(nb * seq, LANES), seq)
    b_past = pad_t(b_past.reshape(nb, seq, past))
    b_new = pad_t(b_new.reshape(nb, seq, LANES))

    rep = C_HEADS // C_KV_HEADS
    qg = pad_t(q.reshape(nb, seq, C_KV_HEADS, rep, HEAD_DIM)).transpose(0, 2, 3, 1, 4)
    qg = qg * (HEAD_DIM ** -0.5)
    eye = jnp.eye(C_KV_HEADS, dtype=F32)
    qbd = jnp.einsum("bgrtd,gh->bgrthd", qg, eye).reshape(nb, C_HEADS * SROWS, C_KV_DIM)
    o = _sample_attend(cache_k, cache_v, page_table, layer, qbd,
                       b_past, b_new, pad_keys(k), pad_keys(v))
    o = o.reshape(nb, C_KV_HEADS, rep, SROWS, C_KV_HEADS, HEAD_DIM)
    o = jnp.einsum("bgrthd,gh->bgrtd", o, eye)[:, :, :, :seq]
    o_c = o.transpose(0, 3, 1, 2, 4).reshape(nb * seq, C_DIM)

    x4 = _layer_tail(x1, o_scan, bonus, g, o_b, o_c, p_emb, lw, 1, nb * seq)
    st = (k.reshape(nb, seq, C_KV_HEADS, HEAD_DIM), v.reshape(nb, seq, C_KV_HEADS, HEAD_DIM),
          ki.reshape(nb, seq, IDX_DIM), wkv1, shift1, conv1)
    return x4, st


def kernel(x_prompt, x_sample, cache_k, cache_v, cache_kidx, state_wkv, state_shift, state_conv, page_table, p_prompt, p_sample, ffn1_norm, ffn1_wg, ffn1_wu, ffn1_wd, mix_norm, w_in, rwkv_mu, rwkv_w0, rwkv_w2, rwkv_a0, rwkv_a2, rwkv_g2, rwkv_kk, rwkv_ka, rwkv_rk, rwkv_lnx_w, rwkv_lnx_b, conv_w, q_norm, k_norm, idx_k_norm, w_pa, w_pb, w_pc, w_out, ffn2_norm, ffn2_wg, ffn2_wu, ffn2_wd, ple_norm, ple_gate, ple_proj):
    params = dict(ffn1_norm=ffn1_norm, ffn1_wg=ffn1_wg, ffn1_wu=ffn1_wu, ffn1_wd=ffn1_wd,
                  mix_norm=mix_norm, w_in=w_in, rwkv_mu=rwkv_mu, rwkv_w0=rwkv_w0, rwkv_w2=rwkv_w2,
                  rwkv_a0=rwkv_a0, rwkv_a2=rwkv_a2, rwkv_g2=rwkv_g2, rwkv_kk=rwkv_kk, rwkv_ka=rwkv_ka,
                  rwkv_rk=rwkv_rk.reshape(rwkv_rk.shape[0], -1), rwkv_lnx_w=rwkv_lnx_w,
                  rwkv_lnx_b=rwkv_lnx_b, conv_w=conv_w, q_norm=q_norm, k_norm=k_norm,
                  idx_k_norm=idx_k_norm, w_pa=w_pa, w_pb=w_pb, w_pc=w_pc, w_out=w_out,
                  ffn2_norm=ffn2_norm, ffn2_wg=ffn2_wg, ffn2_wu=ffn2_wu, ffn2_wd=ffn2_wd,
                  ple_norm=ple_norm, ple_gate=ple_gate, ple_proj=ple_proj)
    nb, seq, d = x_prompt.shape
    db, dseq, _ = x_sample.shape
    depth = w_in.shape[0]
    past = page_table.shape[1] * cache_k.shape[2]
    tabs_p = _rope_tables(jnp.arange(seq, dtype=jnp.int32))
    tabs_s = tuple(jnp.tile(z, (db, 1)) for z in _rope_tables(past + jnp.arange(dseq, dtype=jnp.int32)))
    xp = x_prompt.reshape(nb * seq, d)
    xs = x_sample.reshape(db * dseq, d)
    outs_p, outs_s = [], []
    for i in range(depth):
        lw = _layer_weights(i, params)
        xp, st_p = _prompt_layer(xp, (p_prompt.reshape(depth * nb * seq, -1), i), nb, seq, lw, tabs_p)
        xs, st_s = _sample_layer(xs, (p_sample.reshape(depth * db * dseq, -1), i), db, dseq, lw, tabs_s, i,
                                 cache_k, cache_v, cache_kidx, page_table,
                                 state_wkv[i], state_shift[i], state_conv[i])
        outs_p.append(st_p)
        outs_s.append(st_s)
    k_p, v_p, kidx_p, wkv_p, shift_p, conv_p = [jnp.stack(z) for z in zip(*outs_p)]
    k_s, v_s, kidx_s, wkv_s, shift_s, conv_s = [jnp.stack(z) for z in zip(*outs_s)]
    return (xp.reshape(nb, seq, d), xs.reshape(db, dseq, d), k_p, v_p, kidx_p, wkv_p, shift_p, conv_p,
            k_s, v_s, kidx_s, wkv_s, shift_s, conv_s)
```
